```python
import math
import numpy as np
import jax
import jax.numpy as jnp
from jax import lax

D_MODEL = 1024
BATCH = 2
SEQ = 8192
DEPTH = 2
DEC_BATCH = 128
DEC_SEQ = 1
PAST_LEN = 2048
PAGE_SIZE = 128

HEAD_DIM = 64
N_FOX_HEADS = 6
W_FOX = N_FOX_HEADS * HEAD_DIM
SSM_GROUP = 16
N_SSM_GROUPS = 24
W_SSM = N_SSM_GROUPS * SSM_GROUP
SSM_STATE = 64
N_MEM_HEADS = 4
W_MEM = N_MEM_HEADS * HEAD_DIM
N_MEM = 256
W_MIX = W_FOX + W_SSM + W_MEM
IN_SIZES = (W_FOX, W_FOX, W_FOX, N_FOX_HEADS, W_FOX, W_SSM, W_SSM, W_MEM, W_MEM)
D_IN = sum(IN_SIZES)
Q_BLOCK = 128
RMS_EPS = 1e-6
POOL_NUM = 5
POOL_DEN = 4
DT_MIN = 1e-3
DT_MAX = 1e-1
FORGET_BIAS_INIT = 2.0

kernel_name = 'hymba_fox_s5_memory_decoder_step'


def rmsnorm(x, g):
    x32 = x.astype(jnp.float32)
    y = x32 * lax.rsqrt(jnp.mean(x32 * x32, axis=-1, keepdims=True) + RMS_EPS)
    return (y * g.astype(jnp.float32)).astype(x.dtype)


def split_heads(t, n_heads):
    return t.reshape(t.shape[:-1] + (n_heads, HEAD_DIM))


def in_projection(x, g_norm, w_in, b_f):
    xn = rmsnorm(x, g_norm)
    z = jnp.einsum('bld,de->ble', xn, w_in)
    cuts = np.cumsum(IN_SIZES)[:-1].tolist()
    q, k, v, f, gate_a, u, gate_s, q_m, gate_m = jnp.split(z, cuts, axis=-1)
    log_f = jax.nn.log_sigmoid((f + b_f).astype(jnp.float32))
    u = u.reshape(u.shape[:2] + (N_SSM_GROUPS, SSM_GROUP))
    return (split_heads(q, N_FOX_HEADS), split_heads(k, N_FOX_HEADS), split_heads(v, N_FOX_HEADS),
            log_f, gate_a, u, gate_s, split_heads(q_m, N_MEM_HEADS), gate_m)


def fox_attend(q, c_q, q_pos, k, v, c_k):
    s = jnp.einsum('bqhd,bkhd->bhqk', q, k).astype(jnp.float32) * (HEAD_DIM ** -0.5)
    s = s + jnp.swapaxes(c_q, 1, 2)[..., :, None] - jnp.swapaxes(c_k, 1, 2)[..., None, :]
    causal = jnp.arange(k.shape[1])[None, :] <= q_pos[:, None]
    s = jnp.where(causal, s, -jnp.inf)
    p = jax.nn.softmax(s, axis=-1)
    return jnp.einsum('bhqk,bkhd->bqhd', p.astype(v.dtype), v)


def fox_prompt(q, k, v, log_f):
    b, l, h, dh = q.shape
    n_blocks = l // Q_BLOCK
    c = jnp.cumsum(log_f, axis=1)
    q_blocks = jnp.moveaxis(q.reshape(b, n_blocks, Q_BLOCK, h, dh), 1, 0)
    c_blocks = jnp.moveaxis(c.reshape(b, n_blocks, Q_BLOCK, h), 1, 0)

    def one_block(args):
        i, q_i, c_i = args
        q_pos = i * Q_BLOCK + jnp.arange(Q_BLOCK)
        return fox_attend(q_i, c_i, q_pos, k, v, c)

    out = lax.map(one_block, (jnp.arange(n_blocks), q_blocks, c_blocks))
    return jnp.moveaxis(out, 0, 1).reshape(b, l, h, dh)


def fox_sample(q, k_new, v_new, log_f_new, k_past, v_past, log_f_past):
    past = k_past.shape[1]
    k = jnp.concatenate([k_past, k_new.astype(k_past.dtype)], axis=1)
    v = jnp.concatenate([v_past, v_new.astype(v_past.dtype)], axis=1)
    c = jnp.cumsum(jnp.concatenate([log_f_past.astype(jnp.float32), log_f_new], axis=1), axis=1)
    q_pos = past + jnp.arange(q.shape[1])
    return fox_attend(q, c[:, past:], q_pos, k, v, c)


def s5_scan(u, h0, lam_re, lam_im, log_dt, b_re, b_im, c_re, c_im, d_skip):
    f32 = jnp.float32
    lam = lax.complex(lam_re.astype(f32), lam_im.astype(f32))
    dt = jnp.exp(log_dt.astype(f32))[:, None]
    lam_bar = jnp.exp(lam * dt)
    b_bar = ((lam_bar - 1.0) / lam)[..., None] * lax.complex(b_re.astype(f32), b_im.astype(f32))
    c_mat = lax.complex(c_re.astype(f32), c_im.astype(f32))
    u32 = u.astype(f32)
    bu = jnp.einsum('gpc,blgc->blgp', b_bar, u32.astype(jnp.complex64))
    bu = bu.at[:, 0].add(lam_bar * h0)
    a = jnp.broadcast_to(lam_bar, bu.shape)

    def combine(e1, e2):
        a1, b1 = e1
        a2, b2 = e2
        return a1 * a2, a2 * b1 + b2

    _, h = lax.associative_scan(combine, (a, bu), axis=1)
    y = jnp.einsum('gcp,blgp->blgc', c_mat, h).real + d_skip.astype(f32) * u32
    return y, h[:, -1]


def memory_kv(mem, g_mem, w_mem_kv):
    m = rmsnorm(mem, g_mem)
    kv = jnp.einsum('bnd,de->bne', m, w_mem_kv)
    k, v = jnp.split(kv, 2, axis=-1)
    return split_heads(k, N_MEM_HEADS), split_heads(v, N_MEM_HEADS)


def memory_attend(q, k, v):
    s = jnp.einsum('bqhd,bnhd->bhqn', q, k.astype(q.dtype)).astype(jnp.float32) * (HEAD_DIM ** -0.5)
    p = jax.nn.softmax(s, axis=-1)
    return jnp.einsum('bhqn,bnhd->bqhd', p.astype(q.dtype), v.astype(q.dtype))


def out_projection(x, fox_out, gate_a, ssm_y, gate_s, w_glu, b_glu, mem_out, gate_m, w_out):
    b, l = x.shape[:2]
    dt = x.dtype
    a = fox_out.reshape(b, l, W_FOX).astype(dt) * jax.nn.silu(gate_a)
    s = jax.nn.gelu(ssm_y.reshape(b, l, W_SSM), approximate=False)
    s = s * jax.nn.sigmoid(jnp.einsum('blc,ce->ble', s, w_glu.astype(jnp.float32)) + b_glu.astype(jnp.float32))
    s = s.astype(dt) * jax.nn.silu(gate_s)
    m = mem_out.reshape(b, l, W_MEM).astype(dt) * jax.nn.silu(gate_m)
    mix = jnp.concatenate([a, s, m], axis=-1)
    return x + jnp.einsum('ble,ed->bld', mix, w_out)


def setup_inputs(seed: int = 0) -> dict:
    key = jax.random.key(seed)
    ks = jax.random.split(key, 28)
    f32 = jnp.float32
    n_pages = PAST_LEN // PAGE_SIZE
    n_pool = (DEC_BATCH * n_pages * POOL_NUM) // POOL_DEN

    def nrm(k, shape, scale=1.0):
        return jax.random.normal(k, shape, f32) * scale

    G, P = N_SSM_GROUPS, SSM_STATE
    x_prompt = nrm(ks[0], (BATCH, SEQ, D_MODEL))
    x_sample = nrm(ks[1], (DEC_BATCH, DEC_SEQ, D_MODEL))
    mem_prompt = nrm(ks[2], (BATCH, N_MEM, D_MODEL))
    cache_k = nrm(ks[3], (DEPTH, n_pool, PAGE_SIZE, N_FOX_HEADS, HEAD_DIM))
    cache_v = nrm(ks[4], (DEPTH, n_pool, PAGE_SIZE, N_FOX_HEADS, HEAD_DIM))
    cache_logf = jax.nn.log_sigmoid(nrm(ks[5], (DEPTH, n_pool, PAGE_SIZE, N_FOX_HEADS)) + FORGET_BIAS_INIT)
    state_ssm = nrm(ks[6], (DEPTH, DEC_BATCH, G, P, 2), 0.2)
    cache_mem_k = nrm(ks[7], (DEPTH, DEC_BATCH, N_MEM, N_MEM_HEADS, HEAD_DIM))
    cache_mem_v = nrm(ks[8], (DEPTH, DEC_BATCH, N_MEM, N_MEM_HEADS, HEAD_DIM))
    page_table = jax.random.permutation(ks[9], n_pool)[: DEC_BATCH * n_pages].reshape(
        DEC_BATCH, n_pages).astype(jnp.int32)
    g_norm = 1.0 + nrm(ks[10], (DEPTH, D_MODEL), 0.02)
    w_in = nrm(ks[11], (DEPTH, D_MODEL, D_IN), D_MODEL ** -0.5)
    b_f = FORGET_BIAS_INIT + nrm(ks[12], (DEPTH, N_FOX_HEADS), 0.5)
    lam_re = -0.5 + nrm(ks[13], (DEPTH, G, P), 0.01)
    lam_im = math.pi * jnp.arange(P, dtype=f32) + nrm(ks[14], (DEPTH, G, P), 0.01)
    log_dt = jax.random.uniform(ks[15], (DEPTH, G), f32, math.log(DT_MIN), math.log(DT_MAX))
    b_re = nrm(ks[16], (DEPTH, G, P, SSM_GROUP), (2 * SSM_GROUP) ** -0.5)
    b_im = nrm(ks[17], (DEPTH, G, P, SSM_GROUP), (2 * SSM_GROUP) ** -0.5)
    c_re = nrm(ks[18], (DEPTH, G, SSM_GROUP, P), SSM_STATE ** -0.5)
    c_im = nrm(ks[19], (DEPTH, G, SSM_GROUP, P), SSM_STATE ** -0.5)
    d_skip = nrm(ks[20], (DEPTH, G, SSM_GROUP))
    w_glu = nrm(ks[21], (DEPTH, W_SSM, W_SSM), W_SSM ** -0.5)
    b_glu = nrm(ks[22], (DEPTH, W_SSM), 0.02)
    g_mem = 1.0 + nrm(ks[23], (DEPTH, D_MODEL), 0.02)
    w_mem_kv = nrm(ks[24], (DEPTH, D_MODEL, 2 * W_MEM), D_MODEL ** -0.5)
    w_out = nrm(ks[25], (DEPTH, W_MIX, D_MODEL), W_MIX ** -0.5)
    g_final = 1.0 + nrm(ks[26], (D_MODEL,), 0.02)
    return {'x_prompt': x_prompt, 'x_sample': x_sample, 'mem_prompt': mem_prompt,
            'cache_k': cache_k, 'cache_v': cache_v, 'cache_logf': cache_logf, 'state_ssm': state_ssm,
            'cache_mem_k': cache_mem_k, 'cache_mem_v': cache_mem_v, 'page_table': page_table,
            'g_norm': g_norm, 'w_in': w_in, 'b_f': b_f, 'lam_re': lam_re, 'lam_im': lam_im,
            'log_dt': log_dt, 'b_re': b_re, 'b_im': b_im, 'c_re': c_re, 'c_im': c_im,
            'd_skip': d_skip, 'w_glu': w_glu, 'b_glu': b_glu, 'g_mem': g_mem,
            'w_mem_kv': w_mem_kv, 'w_out': w_out, 'g_final': g_final}


def reference(x_prompt, x_sample, mem_prompt, cache_k, cache_v, cache_logf, state_ssm,
              cache_mem_k, cache_mem_v, page_table, g_norm, w_in, b_f, lam_re, lam_im,
              log_dt, b_re, b_im, c_re, c_im, d_skip, w_glu, b_glu, g_mem, w_mem_kv,
              w_out, g_final):
    f32 = jnp.float32
    dec_b = x_sample.shape[0]
    past_len = page_table.shape[1] * cache_k.shape[2]
    xp, xs = x_prompt, x_sample
    kp, vp, lfp, hp, mkp, mvp = [], [], [], [], [], []
    ks_, vs_, lfs, hs = [], [], [], []
    for l in range(DEPTH):
        ssm_params = (lam_re[l], lam_im[l], log_dt[l], b_re[l], b_im[l], c_re[l], c_im[l], d_skip[l])

        q, k, v, log_f, ga, u, gs, qm, gm = in_projection(xp, g_norm[l], w_in[l], b_f[l])
        fox_o = fox_prompt(q, k, v, log_f)
        h0 = jnp.zeros((xp.shape[0], N_SSM_GROUPS, SSM_STATE), jnp.complex64)
        ssm_y, h_last = s5_scan(u, h0, *ssm_params)
        mk, mv = memory_kv(mem_prompt, g_mem[l], w_mem_kv[l])
        mem_o = memory_attend(qm, mk, mv)
        xp = out_projection(xp, fox_o, ga, ssm_y, gs, w_glu[l], b_glu[l], mem_o, gm, w_out[l])
        kp.append(k)
        vp.append(v)
        lfp.append(log_f)
        hp.append(jnp.stack([h_last.real, h_last.imag], axis=-1).astype(f32))
        mkp.append(mk)
        mvp.append(mv)

        q, k, v, log_f, ga, u, gs, qm, gm = in_projection(xs, g_norm[l], w_in[l], b_f[l])
        k_past = cache_k[l, page_table].reshape(dec_b, past_len, N_FOX_HEADS, HEAD_DIM)
        v_past = cache_v[l, page_table].reshape(dec_b, past_len, N_FOX_HEADS, HEAD_DIM)
        lf_past = cache_logf[l, page_table].reshape(dec_b, past_len, N_FOX_HEADS)
        fox_o = fox_sample(q, k, v, log_f, k_past, v_past, lf_past)
        h0 = lax.complex(state_ssm[l, ..., 0].astype(f32), state_ssm[l, ..., 1].astype(f32))
        ssm_y, h_last = s5_scan(u, h0, *ssm_params)
        mem_o = memory_attend(qm, cache_mem_k[l], cache_mem_v[l])
        xs = out_projection(xs, fox_o, ga, ssm_y, gs, w_glu[l], b_glu[l], mem_o, gm, w_out[l])
        ks_.append(k)
        vs_.append(v)
        lfs.append(log_f)
        hs.append(jnp.stack([h_last.real, h_last.imag], axis=-1).astype(f32))

    y_prompt = rmsnorm(xp, g_final)
    y_sample = rmsnorm(xs, g_final)
    return (y_prompt, y_sample,
            jnp.stack(kp), jnp.stack(vp), jnp.stack(lfp), jnp.stack(hp), jnp.stack(mkp), jnp.stack(mvp),
            jnp.stack(ks_), jnp.stack(vs_), jnp.stack(lfs), jnp.stack(hs))
```

```python
import functools
import math

import jax
import jax.numpy as jnp
from jax import lax
from jax.experimental import pallas as pl
from jax.experimental.pallas import tpu as pltpu

f32 = jnp.float32
bf16 = jnp.bfloat16

HEAD_DIM = 64
N_FOX_HEADS = 6
W_FOX = N_FOX_HEADS * HEAD_DIM
SSM_GROUP = 16
N_SSM_GROUPS = 24
W_SSM = N_SSM_GROUPS * SSM_GROUP
SSM_STATE = 64
N_MEM_HEADS = 4
W_MEM = N_MEM_HEADS * HEAD_DIM
RMS_EPS = 1e-6
QK_SCALE = HEAD_DIM ** -0.5

LANES = 128
SUBLANES = 8
GROUPS_PER_BLOCK = LANES // SSM_GROUP
N_SSM_BLOCKS = N_SSM_GROUPS // GROUPS_PER_BLOCK
HALF = GROUPS_PER_BLOCK * SSM_STATE
STATE_W = 2 * HALF
N_SEG = SUBLANES

OFF_Q, OFF_K, OFF_V, OFF_U, OFF_QM, OFF_G = 0, 384, 768, 1152, 1536, 1792
W_MAIN = OFF_G + W_FOX + W_SSM + W_MEM
OFF_F = W_MAIN
W_CAT = 3072

VMEM_LIMIT = 56 * 1024 * 1024


def _cparams(sem):
    return pltpu.CompilerParams(dimension_semantics=sem, vmem_limit_bytes=VMEM_LIMIT)


def _dot(a, b):
    return jnp.dot(a, b, preferred_element_type=f32)


def _dot_nt(a, b):
    return lax.dot_general(a, b, (((1,), (1,)), ((), ())), preferred_element_type=f32)


def _split2(a):
    hi = a.astype(bf16)
    lo = (a - hi.astype(f32)).astype(bf16)
    return hi, lo


def _split3(a):
    hi = a.astype(bf16)
    r = a - hi.astype(f32)
    mid = r.astype(bf16)
    lo = (r - mid.astype(f32)).astype(bf16)
    return hi, mid, lo


def _dot3(a, b):
    ah, al = _split2(a)
    bh, bl = _split2(b)
    return _dot(ah, bh) + _dot(al, bh) + _dot(ah, bl)


def _rms(x, g):
    ms = jnp.mean(x * x, axis=-1, keepdims=True)
    return x * lax.rsqrt(ms + RMS_EPS) * g


def _log_sigmoid(x):
    return jnp.minimum(x, 0.0) - jnp.log1p(jnp.exp(-jnp.abs(x)))


def _silu(x):
    return x * jax.nn.sigmoid(x)


def _gelu(x):
    return 0.5 * x * (1.0 + lax.erf(x * (2.0 ** -0.5)))


def _tri(n):
    r = lax.broadcasted_iota(jnp.int32, (n, n), 0)
    c = lax.broadcasted_iota(jnp.int32, (n, n), 1)
    return jnp.where(c <= r, 1.0, 0.0).astype(bf16)


def _cumsum_rows(x):
    tri = _tri(x.shape[0])
    h1, h2, h3 = _split3(x)
    return _dot(tri, h1) + _dot(tri, h2) + _dot(tri, h3)


def _head_mask(n_heads, width):
    col = lax.broadcasted_iota(jnp.int32, (n_heads, width), 1) // HEAD_DIM
    row = lax.broadcasted_iota(jnp.int32, (n_heads, width), 0)
    return col == row


def _inproj_kernel(x_ref, g_ref, wm_ref, wfh_ref, wfl_ref, bf_ref,
                   qs_ref, k_ref, kb_ref, v_ref, vb_ref, lf_ref, ct_ref, sg_ref, u_ref, qm_ref,
                   carry_ref):
    i = pl.program_id(1)

    @pl.when(i == 0)
    def _():
        carry_ref[...] = jnp.zeros_like(carry_ref)

    xn = _rms(x_ref[...], g_ref[...])
    xh = xn.astype(bf16)
    xl = (xn - xh.astype(f32)).astype(bf16)

    q = _dot(xh, wm_ref[:, OFF_Q:OFF_Q + W_FOX])
    qs_ref[...] = (q * QK_SCALE).astype(bf16)
    k = _dot(xh, wm_ref[:, OFF_K:OFF_K + W_FOX])
    k_ref[...] = k
    kb_ref[...] = k.astype(bf16)
    v = _dot(xh, wm_ref[:, OFF_V:OFF_V + W_FOX])
    v_ref[...] = v
    vb_ref[...] = v.astype(bf16)
    u_ref[...] = _dot(xh, wm_ref[:, OFF_U:OFF_U + W_SSM])
    qm = _dot(xh, wm_ref[:, OFF_QM:OFF_QM + W_MEM])
    qm_ref[...] = (qm * QK_SCALE).astype(bf16)
    gates = _dot(xh, wm_ref[:, OFF_G:W_MAIN])
    sg_ref[...] = _silu(gates).astype(bf16)

    f = _dot(xh, wfh_ref[...]) + _dot(xl, wfh_ref[...]) + _dot(xh, wfl_ref[...]) + bf_ref[...]
    lane = lax.broadcasted_iota(jnp.int32, f.shape, 1)
    lf = jnp.where(lane < N_FOX_HEADS, _log_sigmoid(f), 0.0)
    lf_ref[...] = lf[:, :N_FOX_HEADS]
    cs = _cumsum_rows(lf) + carry_ref[0:1, :]
    carry_ref[...] = jnp.broadcast_to(cs[cs.shape[0] - 1:, :], carry_ref.shape)
    ct_ref[...] = cs.T[0:SUBLANES, :]


def _inproj(x, g, wm, wfh, wfl, bfp, tm):
    B, L, D = x.shape
    row = lambda w: pl.BlockSpec((None, tm, w), lambda b, i: (b, i, 0))
    const = lambda shape: pl.BlockSpec(shape, lambda b, i: (0,) * len(shape))
    out_shape = (
        jax.ShapeDtypeStruct((B, L, W_FOX), bf16),
        jax.ShapeDtypeStruct((B, L, W_FOX), f32),
        jax.ShapeDtypeStruct((B, L, W_FOX), bf16),
        jax.ShapeDtypeStruct((B, L, W_FOX), f32),
        jax.ShapeDtypeStruct((B, L, W_FOX), bf16),
        jax.ShapeDtypeStruct((B, L, N_FOX_HEADS), f32),
        jax.ShapeDtypeStruct((B, SUBLANES, L), f32),
        jax.ShapeDtypeStruct((B, L, W_FOX + W_SSM + W_MEM), bf16),
        jax.ShapeDtypeStruct((B, L, W_SSM), f32),
        jax.ShapeDtypeStruct((B, L, W_MEM), bf16),
    )
    out_specs = (row(W_FOX), row(W_FOX), row(W_FOX), row(W_FOX), row(W_FOX), row(N_FOX_HEADS),
                 pl.BlockSpec((None, SUBLANES, tm), lambda b, i: (b, 0, i)),
                 row(W_FOX + W_SSM + W_MEM), row(W_SSM), row(W_MEM))
    return pl.pallas_call(
        _inproj_kernel,
        grid=(B, L // tm),
        in_specs=[row(D), const((1, D)), const(wm.shape), const(wfh.shape), const(wfl.shape), const((1, LANES))],
        out_specs=out_specs,
        out_shape=out_shape,
        scratch_shapes=[pltpu.VMEM((SUBLANES, LANES), f32)],
        compiler_params=_cparams(("parallel", "arbitrary")),
        name="prompt_inproj",
    )(x, g, wm, wfh, wfl, bfp)


def _memkv_kernel(m_ref, g_ref, w_ref, k_ref, v_ref):
    mn = _rms(m_ref[...], g_ref[...]).astype(bf16)
    kv = _dot(mn, w_ref[...])
    k_ref[...] = kv[:, :W_MEM]
    v_ref[...] = kv[:, W_MEM:]


def _memkv(mem, g, w):
    B, N, D = mem.shape
    return pl.pallas_call(
        _memkv_kernel,
        grid=(B,),
        in_specs=[pl.BlockSpec((None, N, D), lambda b: (b, 0, 0)),
                  pl.BlockSpec((1, D), lambda b: (0, 0)),
                  pl.BlockSpec(w.shape, lambda b: (0, 0))],
        out_specs=(pl.BlockSpec((None, N, W_MEM), lambda b: (b, 0, 0)),) * 2,
        out_shape=(jax.ShapeDtypeStruct((B, N, W_MEM), f32),) * 2,
        compiler_params=_cparams(("parallel",)),
        name="prompt_memkv",
    )(mem, g, w)


def _fox_kernel(q_ref, k_ref, v_ref, c_ref, o_ref, acc_ref, m_ref, l_ref, *, tq, tk):
    j = pl.program_id(1)
    i = pl.program_id(2)
    q = q_ref[...]
    lane = lax.broadcasted_iota(jnp.int32, (tq, LANES), 1)
    zero = jnp.zeros_like(q)
    qe = (jnp.where(lane < HEAD_DIM, q, zero), jnp.where(lane >= HEAD_DIM, q, zero))
    acc_ref[...] = jnp.zeros_like(acc_ref)
    m_ref[...] = jnp.full_like(m_ref, -jnp.inf)
    l_ref[...] = jnp.zeros_like(l_ref)

    def step(kj, masked):
        k0 = pl.multiple_of(kj * tk, tk)
        kblk = k_ref[pl.ds(k0, tk), :]
        vblk = v_ref[pl.ds(k0, tk), :]
        for e in range(2):
            s = _dot_nt(qe[e], kblk) - c_ref[pl.ds(2 * j + e, 1), pl.ds(k0, tk)]
            if masked:
                row = i * tq + lax.broadcasted_iota(jnp.int32, (tq, tk), 0)
                col = k0 + lax.broadcasted_iota(jnp.int32, (tq, tk), 1)
                s = jnp.where(col <= row, s, -jnp.inf)
            m_old = m_ref[e]
            m_new = jnp.maximum(m_old, jnp.max(s, axis=1, keepdims=True))
            alpha = jnp.exp(m_old - m_new)
            p = jnp.exp(s - m_new)
            l_ref[e] = alpha * l_ref[e] + jnp.sum(p, axis=1, keepdims=True)
            acc_ref[e] = alpha * acc_ref[e] + _dot(p.astype(bf16), vblk)
            m_ref[e] = m_new

    n_full = i * (tq // tk)

    def body(kj, carry):
        step(kj, False)
        return carry

    lax.fori_loop(0, n_full, body, 0)
    for d in range(tq // tk):
        step(n_full + d, True)
    o_ref[...] = jnp.where(lane < HEAD_DIM, acc_ref[0] / l_ref[0], acc_ref[1] / l_ref[1])


def _fox(qs, kb, vb, ct, tq, tk):
    B, L, _ = qs.shape
    n_pairs = W_FOX // LANES
    return pl.pallas_call(
        functools.partial(_fox_kernel, tq=tq, tk=tk),
        grid=(B, n_pairs, L // tq),
        in_specs=[pl.BlockSpec((None, tq, LANES), lambda b, j, i: (b, i, j)),
                  pl.BlockSpec((None, L, LANES), lambda b, j, i: (b, 0, j)),
                  pl.BlockSpec((None, L, LANES), lambda b, j, i: (b, 0, j)),
                  pl.BlockSpec((None, SUBLANES, L), lambda b, j, i: (b, 0, 0))],
        out_specs=pl.BlockSpec((None, tq, LANES), lambda b, j, i: (b, i, j)),
        out_shape=jax.ShapeDtypeStruct((B, L, W_FOX), f32),
        scratch_shapes=[pltpu.VMEM((2, tq, LANES), f32), pltpu.VMEM((2, tq, 1), f32), pltpu.VMEM((2, tq, 1), f32)],
        compiler_params=_cparams(("parallel", "parallel", "arbitrary")),
        name="prompt_fox",
    )(qs, kb, vb, ct)


def _s5prep_kernel(lr_ref, li_ref, ldt_ref, bre_ref, bim_ref, cre_ref, cim_ref,
                   lamb_ref, bmat_ref, cmat_ref, lpow_ref, lams_ref, *, seg_len):
    lr = lr_ref[...]
    li = li_ref[...]
    dt = jnp.exp(ldt_ref[...])
    ar = lr * dt
    ai = li * dt
    mag = jnp.exp(ar)
    lbr = mag * jnp.cos(ai)
    lbi = mag * jnp.sin(ai)
    lamb_ref[:, :HALF] = jnp.broadcast_to(lbr, (SUBLANES, HALF))
    lamb_ref[:, HALF:] = jnp.broadcast_to(lbi, (SUBLANES, HALF))
    den = lr * lr + li * li
    zr = ((lbr - 1.0) * lr + lbi * li) / den
    zi = (lbi * lr - (lbr - 1.0) * li) / den
    bre = bre_ref[...]
    bim = bim_ref[...]
    bmat_ref[:, :HALF] = zr * bre - zi * bim
    bmat_ref[:, HALF:] = zr * bim + zi * bre
    cmat_ref[:HALF, :] = cre_ref[...]
    cmat_ref[HALF:, :] = -cim_ref[...]
    jj = (lax.broadcasted_iota(jnp.int32, (seg_len, HALF), 0) + 1).astype(f32)
    magp = jnp.exp(ar * jj)
    ang = ai * jj
    pr = magp * jnp.cos(ang)
    pi = magp * jnp.sin(ang)
    lpow_ref[:, :HALF] = pr.astype(bf16)
    lpow_ref[:, HALF:] = pi.astype(bf16)
    lams_ref[:, :HALF] = pr[seg_len - SUBLANES:, :]
    lams_ref[:, HALF:] = pi[seg_len - SUBLANES:, :]


def _s5prep(lam_re, lam_im, log_dt, b_re, b_im, c_re, c_im, seg_len):
    nb, gb = N_SSM_BLOCKS, GROUPS_PER_BLOCK
    eye = jnp.eye(gb, dtype=f32)
    half = lambda a: a.reshape(nb, 1, HALF)
    ldt = jnp.broadcast_to(log_dt[:, None], (N_SSM_GROUPS, SSM_STATE))

    def embed_b(b):
        t = b.reshape(nb, gb, SSM_STATE, SSM_GROUP).transpose(0, 1, 3, 2)
        return (t[:, :, :, None, :] * eye[None, :, None, :, None]).reshape(nb, LANES, HALF)

    def embed_c(c):
        t = c.reshape(nb, gb, SSM_GROUP, SSM_STATE).transpose(0, 1, 3, 2)
        return (t[:, :, :, None, :] * eye[None, :, None, :, None]).reshape(nb, HALF, LANES)

    blk = lambda shape: pl.BlockSpec((None,) + shape, lambda cb: (cb,) + (0,) * len(shape))
    return pl.pallas_call(
        functools.partial(_s5prep_kernel, seg_len=seg_len),
        grid=(nb,),
        in_specs=[blk((1, HALF))] * 3 + [blk((LANES, HALF))] * 2 + [blk((HALF, LANES))] * 2,
        out_specs=(blk((SUBLANES, STATE_W)), blk((LANES, STATE_W)), blk((STATE_W, LANES)),
                   blk((seg_len, STATE_W)), blk((SUBLANES, STATE_W))),
        out_shape=(jax.ShapeDtypeStruct((nb, SUBLANES, STATE_W), f32),
                   jax.ShapeDtypeStruct((nb, LANES, STATE_W), f32),
                   jax.ShapeDtypeStruct((nb, STATE_W, LANES), f32),
                   jax.ShapeDtypeStruct((nb, seg_len, STATE_W), bf16),
                   jax.ShapeDtypeStruct((nb, SUBLANES, STATE_W), f32)),
        compiler_params=_cparams(("parallel",)),
        name="s5_tables",
    )(half(lam_re), half(lam_im), half(ldt), embed_b(b_re), embed_b(b_im), embed_c(c_re), embed_c(c_im))


def _s5scan_kernel(u_ref, bmat_ref, cmat_ref, lamb_ref, dsk_ref, y_ref, hend_ref,
                   up_ref, hb_ref, hcar_ref, *, tj):
    j = pl.program_id(2)

    @pl.when(j == 0)
    def _():
        hcar_ref[...] = jnp.zeros_like(hcar_ref)

    def gather(t, c):
        up_ref[pl.ds(pl.multiple_of(t * N_SEG, N_SEG), N_SEG), :] = u_ref[:, t, :]
        return c

    lax.fori_loop(0, tj, gather, 0)
    hb_ref[...] = _dot(up_ref[...].astype(bf16), bmat_ref[...])
    lam = lamb_ref[...]
    lr = lam[:, :HALF]
    li = lam[:, HALF:]

    def rec(t, h):
        hr, hi = h
        r0 = pl.multiple_of(t * N_SEG, N_SEG)
        nr = lr * hr - li * hi + hb_ref[pl.ds(r0, N_SEG), :HALF]
        ni = lr * hi + li * hr + hb_ref[pl.ds(r0, N_SEG), HALF:]
        hb_ref[pl.ds(r0, N_SEG), :HALF] = nr
        hb_ref[pl.ds(r0, N_SEG), HALF:] = ni
        return nr, ni

    hr, hi = lax.fori_loop(0, tj, rec, (hcar_ref[:, :HALF], hcar_ref[:, HALF:]))
    hcar_ref[:, :HALF] = hr
    hcar_ref[:, HALF:] = hi
    up_ref[...] = _dot(hb_ref[...].astype(bf16), cmat_ref[...]) + dsk_ref[...] * up_ref[...]

    def scatter(t, c):
        y_ref[:, t, :] = up_ref[pl.ds(pl.multiple_of(t * N_SEG, N_SEG), N_SEG), :]
        return c

    lax.fori_loop(0, tj, scatter, 0)

    @pl.when(j == pl.num_programs(2) - 1)
    def _():
        hend_ref[...] = hcar_ref[...]


def _s5scan(u, bmat, cmat, lamb, dsk, tj):
    B, L, _ = u.shape
    seg_len = L // N_SEG
    u4 = u.reshape(B, N_SEG, seg_len, W_SSM)
    nb = N_SSM_BLOCKS
    tile = pl.BlockSpec((None, N_SEG, tj, LANES), lambda b, cb, j: (b, 0, j, cb))
    per_cb = lambda shape: pl.BlockSpec((None,) + shape, lambda b, cb, j: (cb,) + (0,) * len(shape))
    y4, hend = pl.pallas_call(
        functools.partial(_s5scan_kernel, tj=tj),
        grid=(B, nb, seg_len // tj),
        in_specs=[tile, per_cb((LANES, STATE_W)), per_cb((STATE_W, LANES)), per_cb((SUBLANES, STATE_W)),
                  per_cb((1, LANES))],
        out_specs=(tile, pl.BlockSpec((None, None, N_SEG, STATE_W), lambda b, cb, j: (b, cb, 0, 0))),
        out_shape=(jax.ShapeDtypeStruct((B, N_SEG, seg_len, W_SSM), f32),
                   jax.ShapeDtypeStruct((B, nb, N_SEG, STATE_W), f32)),
        scratch_shapes=[pltpu.VMEM((N_SEG * tj, LANES), f32), pltpu.VMEM((N_SEG * tj, STATE_W), f32),
                        pltpu.VMEM((N_SEG, STATE_W), f32)],
        compiler_params=_cparams(("parallel", "parallel", "arbitrary")),
        name="prompt_s5_scan",
    )(u4, bmat, cmat, lamb, dsk)
    return y4.reshape(B, L, W_SSM), hend


def _s5fix_kernel(hend_ref, lams_ref, cmat_ref, hlast_ref, cp_ref, hs_ref):
    he = hend_ref[...]
    lsr = lams_ref[SUBLANES - 1:, :HALF]
    lsi = lams_ref[SUBLANES - 1:, HALF:]
    hs_ref[...] = jnp.zeros_like(hs_ref)
    hr = jnp.zeros((1, HALF), f32)
    hi = jnp.zeros((1, HALF), f32)
    for s in range(N_SEG):
        hs_ref[s:s + 1, :HALF] = hr
        hs_ref[s:s + 1, HALF:] = hi
        nr = lsr * hr - lsi * hi + he[s:s + 1, :HALF]
        ni = lsr * hi + lsi * hr + he[s:s + 1, HALF:]
        hr, hi = nr, ni
    hlast_ref[:, :HALF] = hr
    hlast_ref[:, HALF:] = hi
    hst_r = hs_ref[:, :HALF].T
    hst_i = hs_ref[:, HALF:].T
    cre = cmat_ref[:HALF, :]
    mci = cmat_ref[HALF:, :]
    for s in range(N_SEG):
        col_r = hst_r[:, s:s + 1]
        col_i = hst_i[:, s:s + 1]
        cp_ref[s, :HALF, :] = (cre * col_r + mci * col_i).astype(bf16)
        cp_ref[s, HALF:, :] = (mci * col_r - cre * col_i).astype(bf16)


def _s5fix(hend, lams, cmat):
    B, nb = hend.shape[:2]
    return pl.pallas_call(
        _s5fix_kernel,
        grid=(B, nb),
        in_specs=[pl.BlockSpec((None, None, N_SEG, STATE_W), lambda b, cb: (b, cb, 0, 0)),
                  pl.BlockSpec((None, SUBLANES, STATE_W), lambda b, cb: (cb, 0, 0)),
                  pl.BlockSpec((None, STATE_W, LANES), lambda b, cb: (cb, 0, 0))],
        out_specs=(pl.BlockSpec((None, None, 1, STATE_W), lambda b, cb: (b, cb, 0, 0)),
                   pl.BlockSpec((None, N_SEG, None, STATE_W, LANES), lambda b, cb: (b, 0, cb, 0, 0))),
        out_shape=(jax.ShapeDtypeStruct((B, nb, 1, STATE_W), f32),
                   jax.ShapeDtypeStruct((B, N_SEG, nb, STATE_W, LANES), bf16)),
        scratch_shapes=[pltpu.VMEM((LANES, STATE_W), f32)],
        compiler_params=_cparams(("parallel", "parallel")),
        name="prompt_s5_fix",
    )(hend, lams, cmat)


def _pair_attend(q, k, v):
    lane = lax.broadcasted_iota(jnp.int32, q.shape, 1)
    zero = jnp.zeros_like(q)
    outs = []
    for e in range(2):
        qe = jnp.where((lane < HEAD_DIM) == (e == 0), q, zero)
        s = _dot_nt(qe, k)
        p = jnp.exp(s - jnp.max(s, axis=1, keepdims=True))
        outs.append(_dot(p.astype(bf16), v) / jnp.sum(p, axis=1, keepdims=True))
    return jnp.where(lane < HEAD_DIM, outs[0], outs[1])


def _outproj_kernel(x_ref, fox_ref, y_ref, lp_ref, cp_ref, qm_ref, mk_ref, mv_ref, sg_ref,
                    wglu_ref, bglu_ref, wout_ref, gfin_ref, o_ref, *, final):
    corr = [_dot(lp_ref[cb], cp_ref[cb]) for cb in range(N_SSM_BLOCKS)]
    ssm_y = y_ref[...] + jnp.concatenate(corr, axis=1)
    s = _gelu(ssm_y)
    s = s * jax.nn.sigmoid(_dot(s.astype(bf16), wglu_ref[...]) + bglu_ref[...])
    qm = qm_ref[...]
    mk = mk_ref[...].astype(bf16)
    mv = mv_ref[...].astype(bf16)
    mem = [_pair_attend(qm[:, LANES * jp:LANES * (jp + 1)], mk[:, LANES * jp:LANES * (jp + 1)],
                        mv[:, LANES * jp:LANES * (jp + 1)]) for jp in range(W_MEM // LANES)]
    sg = sg_ref[...].astype(f32)
    mix = jnp.concatenate([fox_ref[...] * sg[:, :W_FOX],
                           s * sg[:, W_FOX:W_FOX + W_SSM],
                           jnp.concatenate(mem, axis=1) * sg[:, W_FOX + W_SSM:]], axis=1)
    xo = x_ref[...] + _dot(mix.astype(bf16), wout_ref[...])
    if final:
        xo = _rms(xo, gfin_ref[...])
    o_ref[...] = xo


def _outproj(x, fox_o, y, lpow, cp, qm, mk, mv, sg, wglu, bglu, wout, gfin, tm, final):
    B, L, D = x.shape
    seg_len = L // N_SEG
    tiles_per_seg = seg_len // tm
    row = lambda w: pl.BlockSpec((None, tm, w), lambda b, i: (b, i, 0))
    const = lambda shape: pl.BlockSpec(shape, lambda b, i: (0,) * len(shape))
    nb = N_SSM_BLOCKS
    return pl.pallas_call(
        functools.partial(_outproj_kernel, final=final),
        grid=(B, L // tm),
        in_specs=[row(D), row(W_FOX), row(W_SSM),
                  pl.BlockSpec((nb, tm, STATE_W), lambda b, i: (0, i % tiles_per_seg, 0)),
                  pl.BlockSpec((None, None, nb, STATE_W, LANES), lambda b, i: (b, i // tiles_per_seg, 0, 0, 0)),
                  row(W_MEM),
                  pl.BlockSpec((None,) + mk.shape[1:], lambda b, i: (b, 0, 0)),
                  pl.BlockSpec((None,) + mv.shape[1:], lambda b, i: (b, 0, 0)),
                  row(D), const(wglu.shape), const((1, W_SSM)), const(wout.shape), const((1, D))],
        out_specs=row(D),
        out_shape=jax.ShapeDtypeStruct((B, L, D), f32),
        compiler_params=_cparams(("parallel", "parallel")),
        name="prompt_outproj",
    )(x, fox_o, y, lpow, cp, qm, mk, mv, sg, wglu, bglu, wout, gfin)


def _s_inproj_kernel(x_ref, g_ref, w_ref, z_ref):
    xn = _rms(x_ref[...], g_ref[...])
    z_ref[...] = _dot3(xn, w_ref[...])


def _s_inproj(xs, g, wcat, tn):
    R, D = xs.shape
    return pl.pallas_call(
        _s_inproj_kernel,
        grid=(W_CAT // tn,),
        in_specs=[pl.BlockSpec((R, D), lambda n: (0, 0)), pl.BlockSpec((1, D), lambda n: (0, 0)),
                  pl.BlockSpec((D, tn), lambda n: (0, n))],
        out_specs=pl.BlockSpec((R, tn), lambda n: (0, n)),
        out_shape=jax.ShapeDtypeStruct((R, W_CAT), f32),
        compiler_params=_cparams(("parallel",)),
        name="sample_inproj",
    )(xs, g, wcat)


def _head_sum_attend(q, n_heads, k_blocks, bias_blocks, v_blocks, extra):
    width = n_heads * HEAD_DIM
    hm = _head_mask(n_heads, width)
    qrows = jnp.where(hm, jnp.broadcast_to(q, (n_heads, width)), 0.0)
    expand = jnp.where(hm, 1.0, 0.0).astype(bf16)
    scores = []
    for kb, bb in zip(k_blocks, bias_blocks):
        s = _dot_nt(kb, qrows)
        scores.append(s if bb is None else s - bb)
    m = functools.reduce(jnp.maximum, [jnp.max(s, axis=0, keepdims=True) for s in scores])
    if extra is not None:
        ek, ev, eb = extra
        s_new = _dot_nt(jnp.broadcast_to(ek, (SUBLANES, width)), qrows)[0:1, :] - eb
        m = jnp.maximum(m, s_new)
    l = jnp.zeros_like(m)
    acc = jnp.zeros((SUBLANES, width), f32)
    for s, vb in zip(scores, v_blocks):
        p = jnp.exp(s - m)
        l = l + jnp.sum(p, axis=0, keepdims=True)
        pe = _dot(p.astype(bf16), expand)
        acc = acc + (pe * vb).reshape(-1, SUBLANES, width).sum(axis=0)
    out = jnp.sum(acc, axis=0, keepdims=True)
    if extra is not None:
        p_new = jnp.exp(s_new - m)
        l = l + p_new
        out = out + _dot(jnp.broadcast_to(p_new, (SUBLANES, n_heads)).astype(bf16), expand)[0:1, :] * ev
    l_exp = _dot3(jnp.broadcast_to(l, (SUBLANES, n_heads)), jnp.where(hm, 1.0, 0.0))[0:1, :]
    return out / l_exp


def _s_attn_kernel(pt_ref, z_ref, bf_ref, *refs, n_pages):
    ck = refs[:n_pages]
    cv = refs[n_pages:2 * n_pages]
    clf = refs[2 * n_pages:3 * n_pages]
    mk_ref, mv_ref, fox_ref, mem_ref, lf_ref = refs[3 * n_pages:]
    z = z_ref[...]
    q = z[:, OFF_Q:OFF_Q + W_FOX] * QK_SCALE
    k_new = z[:, OFF_K:OFF_K + W_FOX]
    v_new = z[:, OFF_V:OFF_V + W_FOX]
    lf_new = _log_sigmoid(z[:, OFF_F:OFF_F + LANES] + bf_ref[...])
    lf_ref[...] = lf_new
    carry = jnp.zeros((1, N_FOX_HEADS), f32)
    cs = []
    for p in range(n_pages):
        c = _cumsum_rows(clf[p][...]) + carry
        carry = c[c.shape[0] - 1:, :]
        cs.append(c)
    c_new = carry + lf_new[:, :N_FOX_HEADS]
    fox_ref[...] = _head_sum_attend(q, N_FOX_HEADS, [r[...] for r in ck], cs, [r[...] for r in cv],
                                    (k_new, v_new, c_new))
    qm = z[:, OFF_QM:OFF_QM + W_MEM] * QK_SCALE
    mem_ref[...] = _head_sum_attend(qm, N_MEM_HEADS, [mk_ref[...]], [None], [mv_ref[...]], None)


def _s_attn(layer, page_table, z, bfp, cache_k, cache_v, cache_logf, cache_mem_k, cache_mem_v):
    R, n_pages = page_table.shape
    page = cache_k.shape[2]
    ck = cache_k.reshape(cache_k.shape[:3] + (W_FOX,))
    cv = cache_v.reshape(cache_v.shape[:3] + (W_FOX,))
    n_mem = cache_mem_k.shape[2]
    cmk = cache_mem_k.reshape(cache_mem_k.shape[:3] + (W_MEM,))
    cmv = cache_mem_v.reshape(cache_mem_v.shape[:3] + (W_MEM,))
    z3 = z.reshape(R, 1, W_CAT)

    def paged(width):
        return [pl.BlockSpec((None, None, page, width), lambda r, pt, p=p: (layer, pt[r, p], 0, 0))
                for p in range(n_pages)]

    per_row = lambda w: pl.BlockSpec((None, 1, w), lambda r, pt: (r, 0, 0))
    mem_spec = pl.BlockSpec((None, None, n_mem, W_MEM), lambda r, pt: (layer, r, 0, 0))
    fox, mem, lf = pl.pallas_call(
        functools.partial(_s_attn_kernel, n_pages=n_pages),
        grid_spec=pltpu.PrefetchScalarGridSpec(
            num_scalar_prefetch=1,
            grid=(R,),
            in_specs=[per_row(W_CAT), pl.BlockSpec((1, LANES), lambda r, pt: (0, 0))]
            + paged(W_FOX) + paged(W_FOX) + paged(N_FOX_HEADS) + [mem_spec, mem_spec],
            out_specs=(per_row(W_FOX), per_row(W_MEM), per_row(LANES))),
        out_shape=(jax.ShapeDtypeStruct((R, 1, W_FOX), f32), jax.ShapeDtypeStruct((R, 1, W_MEM), f32),
                   jax.ShapeDtypeStruct((R, 1, LANES), f32)),
        compiler_params=_cparams(("parallel",)),
        name="sample_attn",
    )(page_table, z3, bfp, *([ck] * n_pages), *([cv] * n_pages), *([cache_logf] * n_pages), cmk, cmv)
    return fox.reshape(R, W_FOX), mem.reshape(R, W_MEM), lf.reshape(R, LANES)


def _s_out_kernel(x_ref, z_ref, fox_ref, mem_ref, h0_ref, bmat_ref, cmat_ref, lamb_ref, dsk_ref,
                  wglu_ref, bglu_ref, wout_ref, gfin_ref, o_ref, h_ref, *, final):
    z = z_ref[...]
    u = z[:, OFF_U:OFF_U + W_SSM]
    ys = []
    for cb in range(N_SSM_BLOCKS):
        ucb = u[:, LANES * cb:LANES * (cb + 1)]
        bu = _dot3(ucb, bmat_ref[cb])
        lr = lamb_ref[cb, 0:1, :HALF]
        li = lamb_ref[cb, 0:1, HALF:]
        h0r = h0_ref[:, STATE_W * cb:STATE_W * cb + HALF]
        h0i = h0_ref[:, STATE_W * cb + HALF:STATE_W * (cb + 1)]
        hr = lr * h0r - li * h0i + bu[:, :HALF]
        hi = lr * h0i + li * h0r + bu[:, HALF:]
        h_ref[:, STATE_W * cb:STATE_W * cb + HALF] = hr
        h_ref[:, STATE_W * cb + HALF:STATE_W * (cb + 1)] = hi
        ys.append(_dot3(jnp.concatenate([hr, hi], axis=1), cmat_ref[cb]) + dsk_ref[cb] * ucb)
    s = _gelu(jnp.concatenate(ys, axis=1))
    s = s * jax.nn.sigmoid(_dot3(s, wglu_ref[...]) + bglu_ref[...])
    sg = _silu(z[:, OFF_G:W_MAIN])
    mix = jnp.concatenate([fox_ref[...] * sg[:, :W_FOX],
                           s * sg[:, W_FOX:W_FOX + W_SSM],
                           mem_ref[...] * sg[:, W_FOX + W_SSM:]], axis=1)
    xo = x_ref[...] + _dot3(mix, wout_ref[...])
    if final:
        xo = _rms(xo, gfin_ref[...])
    o_ref[...] = xo


def _s_out(xs, z, fox, mem, h0, bmat, cmat, lamb, dsk, wglu, bglu, wout, gfin, final):
    R, D = xs.shape
    return pl.pallas_call(
        functools.partial(_s_out_kernel, final=final),
        out_shape=(jax.ShapeDtypeStruct((R, D), f32), jax.ShapeDtypeStruct(h0.shape, f32)),
        compiler_params=pltpu.CompilerParams(vmem_limit_bytes=VMEM_LIMIT),
        name="sample_out",
    )(xs, z, fox, mem, h0, bmat, cmat, lamb, dsk, wglu, bglu, wout, gfin)


def _state_to_cols(h):
    R = h.shape[0]
    t = h.reshape(R, N_SSM_BLOCKS, GROUPS_PER_BLOCK, SSM_STATE, 2).transpose(0, 1, 4, 2, 3)
    return t.reshape(R, N_SSM_BLOCKS * STATE_W)


def _cols_to_state(c):
    R = c.shape[0]
    t = c.reshape(R, N_SSM_BLOCKS, 2, GROUPS_PER_BLOCK, SSM_STATE).transpose(0, 1, 3, 4, 2)
    return t.reshape(R, N_SSM_GROUPS, SSM_STATE, 2)


def _tile(n, pref):
    t = min(n, pref)
    assert n % t == 0
    return t


def kernel(x_prompt, x_sample, mem_prompt, cache_k, cache_v, cache_logf, state_ssm, cache_mem_k, cache_mem_v,
           page_table, g_norm, w_in, b_f, lam_re, lam_im, log_dt, b_re, b_im, c_re, c_im, d_skip, w_glu, b_glu,
           g_mem, w_mem_kv, w_out, g_final):
    depth = w_in.shape[0]
    B, L, D = x_prompt.shape
    R = x_sample.shape[0]
    assert x_sample.shape[1] == 1 and L % N_SEG == 0
    seg_len = L // N_SEG
    tm = _tile(seg_len, 512)
    tq = _tile(L, 512)
    tj = _tile(seg_len, 128)

    xp = x_prompt
    xs = x_sample.reshape(R, D)
    gfin = g_final.reshape(1, D)
    outs = {n: [] for n in ("kp", "vp", "lfp", "hp", "mkp", "mvp", "ks", "vs", "lfs", "hs")}
    cuts = [0, 384, 768, 1152, 1158, 1542, 1926, 2310, 2566, 2822]
    for l in range(depth):
        final = l == depth - 1
        w = w_in[l]
        seg = {n: w[:, a:b] for n, a, b in zip(("q", "k", "v", "f", "ga", "u", "gs", "qm", "gm"), cuts[:-1], cuts[1:])}
        wmain = jnp.concatenate([seg[n] for n in ("q", "k", "v", "u", "qm", "ga", "gs", "gm")], axis=1)
        wf = jnp.pad(seg["f"], ((0, 0), (0, LANES - N_FOX_HEADS)))
        wfh = wf.astype(bf16)
        wfl = (wf - wfh.astype(f32)).astype(bf16)
        wcat = jnp.concatenate([wmain, wf, jnp.zeros((D, W_CAT - W_MAIN - LANES), f32)], axis=1)
        bfp = jnp.pad(b_f[l], (0, LANES - N_FOX_HEADS)).reshape(1, LANES)
        gn = g_norm[l].reshape(1, D)
        wglu = w_glu[l]
        bglu = b_glu[l].reshape(1, W_SSM)
        wout = w_out[l]

        lamb, bmat, cmat, lpow, lams = _s5prep(lam_re[l], lam_im[l], log_dt[l], b_re[l], b_im[l],
                                               c_re[l], c_im[l], seg_len)
        dsk = d_skip[l].reshape(N_SSM_BLOCKS, 1, LANES)

        qs, k, kb, v, vb, lf, ct, sg, u, qm = _inproj(xp, gn, wmain.astype(bf16), wfh, wfl, bfp, tm)
        mk, mv = _memkv(mem_prompt, g_mem[l].reshape(1, D), w_mem_kv[l].astype(bf16))
        fox_o = _fox(qs, kb, vb, ct, tq, tq)
        y, hend = _s5scan(u, bmat.astype(bf16), cmat.astype(bf16), lamb, dsk, tj)
        hlast, cp = _s5fix(hend, lams, cmat)
        xp = _outproj(xp, fox_o, y, lpow, cp, qm, mk, mv, sg, wglu.astype(bf16), bglu, wout.astype(bf16), gfin,
                      tm, final)
        outs["kp"].append(k.reshape(B, L, N_FOX_HEADS, HEAD_DIM))
        outs["vp"].append(v.reshape(B, L, N_FOX_HEADS, HEAD_DIM))
        outs["lfp"].append(lf)
        outs["hp"].append(_cols_to_state(hlast.reshape(B, N_SSM_BLOCKS * STATE_W)))
        outs["mkp"].append(mk.reshape(B, -1, N_MEM_HEADS, HEAD_DIM))
        outs["mvp"].append(mv.reshape(B, -1, N_MEM_HEADS, HEAD_DIM))

        z = _s_inproj(xs, gn, wcat, 256)
        fox_s, mem_s, lf_s = _s_attn(l, page_table, z, bfp, cache_k, cache_v, cache_logf, cache_mem_k, cache_mem_v)
        xs, h_new = _s_out(xs, z, fox_s, mem_s, _state_to_cols(state_ssm[l]), bmat, cmat, lamb, dsk,
                           wglu, bglu, wout, gfin, final)
        outs["ks"].append(z[:, OFF_K:OFF_K + W_FOX].reshape(R, 1, N_FOX_HEADS, HEAD_DIM))
        outs["vs"].append(z[:, OFF_V:OFF_V + W_FOX].reshape(R, 1, N_FOX_HEADS, HEAD_DIM))
        outs["lfs"].append(lf_s[:, :N_FOX_HEADS].reshape(R, 1, N_FOX_HEADS))
        outs["hs"].append(_cols_to_state(h_new))

    st = lambda n: jnp.stack(outs[n])
    return (xp, xs.reshape(R, 1, D), st("kp"), st("vp"), st("lfp"), st("hp"), st("mkp"), st("mvp"),
            st("ks"), st("vs"), st("lfs"), st("hs"))
```

```python
import functools
import math

import jax
import jax.numpy as jnp
from jax import lax
from jax.experimental import pallas as pl
from jax.experimental.pallas import tpu as pltpu

f32 = jnp.float32
bf16 = jnp.bfloat16

HEAD_DIM = 64
N_FOX_HEADS = 6
W_FOX = N_FOX_HEADS * HEAD_DIM
SSM_GROUP = 16
N_SSM_GROUPS = 24
W_SSM = N_SSM_GROUPS * SSM_GROUP
SSM_STATE = 64
N_MEM_HEADS = 4
W_MEM = N_MEM_HEADS * HEAD_DIM
RMS_EPS = 1e-6
QK_SCALE = HEAD_DIM ** -0.5

LANES = 128
SUBLANES = 8
GROUPS_PER_BLOCK = LANES // SSM_GROUP
N_SSM_BLOCKS = N_SSM_GROUPS // GROUPS_PER_BLOCK
HALF = GROUPS_PER_BLOCK * SSM_STATE
STATE_W = 2 * HALF
N_SEG = SUBLANES

OFF_Q, OFF_K, OFF_V, OFF_U, OFF_QM, OFF_G = 0, 384, 768, 1152, 1536, 1792
W_MAIN = OFF_G + W_FOX + W_SSM + W_MEM
OFF_F = W_MAIN
W_CAT = 3072

VMEM_LIMIT = 56 * 1024 * 1024


def _cparams(sem):
    return pltpu.CompilerParams(dimension_semantics=sem, vmem_limit_bytes=VMEM_LIMIT)


def _dot(a, b):
    return jnp.dot(a, b, preferred_element_type=f32)


def _dot_nt(a, b):
    return lax.dot_general(a, b, (((1,), (1,)), ((), ())), preferred_element_type=f32)


def _split2(a):
    hi = a.astype(bf16)
    lo = (a - hi.astype(f32)).astype(bf16)
    return hi, lo


def _split3(a):
    hi = a.astype(bf16)
    r = a - hi.astype(f32)
    mid = r.astype(bf16)
    lo = (r - mid.astype(f32)).astype(bf16)
    return hi, mid, lo


def _dot3(a, b):
    ah, al = _split2(a)
    bh, bl = _split2(b)
    return _dot(ah, bh) + _dot(al, bh) + _dot(ah, bl)


def _rms(x, g):
    ms = jnp.mean(x * x, axis=-1, keepdims=True)
    return x * lax.rsqrt(ms + RMS_EPS) * g


def _log_sigmoid(x):
    return jnp.minimum(x, 0.0) - jnp.log1p(jnp.exp(-jnp.abs(x)))


def _silu(x):
    return x * jax.nn.sigmoid(x)


def _gelu(x):
    return 0.5 * x * (1.0 + lax.erf(x * (2.0 ** -0.5)))


def _tri(n):
    r = lax.broadcasted_iota(jnp.int32, (n, n), 0)
    c = lax.broadcasted_iota(jnp.int32, (n, n), 1)
    return jnp.where(c <= r, 1.0, 0.0).astype(bf16)


def _cumsum_rows(x):
    tri = _tri(x.shape[0])
    h1, h2, h3 = _split3(x)
    return _dot(tri, h1) + _dot(tri, h2) + _dot(tri, h3)


def _head_mask(n_heads, width):
    col = lax.broadcasted_iota(jnp.int32, (n_heads, width), 1) // HEAD_DIM
    row = lax.broadcasted_iota(jnp.int32, (n_heads, width), 0)
    return col == row


def _head_block(x, h):
    blk = x[:, LANES * (h // 2):LANES * (h // 2 + 1)]
    return blk if h % 2 == 0 else pltpu.roll(blk, HEAD_DIM, 1)


def _inproj_kernel(x_ref, g_ref, wm_ref, wfh_ref, wfl_ref, bf_ref,
                   qa_ref, k_ref, ka_ref, v_ref, va_ref, lf_ref, sg_ref, u_ref, qm_ref,
                   carry_ref):
    i = pl.program_id(1)

    @pl.when(i == 0)
    def _():
        carry_ref[...] = jnp.zeros_like(carry_ref)

    xn = _rms(x_ref[...], g_ref[...])
    xh = xn.astype(bf16)
    xl = (xn - xh.astype(f32)).astype(bf16)
    tm = xn.shape[0]
    lane = lax.broadcasted_iota(jnp.int32, (tm, LANES), 1)
    low = lane < HEAD_DIM

    f = _dot(xh, wfh_ref[...]) + _dot(xl, wfh_ref[...]) + _dot(xh, wfl_ref[...]) + bf_ref[...]
    lf = jnp.where(lane < N_FOX_HEADS, _log_sigmoid(f), 0.0)
    lf_ref[...] = lf[:, :N_FOX_HEADS]
    cs = _cumsum_rows(lf) + carry_ref[0:1, :]
    carry_ref[...] = jnp.broadcast_to(cs[tm - 1:, :], carry_ref.shape)
    pieces = _split3(-cs)
    prow = lax.broadcasted_iota(jnp.int32, (LANES, N_FOX_HEADS * LANES), 0)
    pcol = lax.broadcasted_iota(jnp.int32, (LANES, N_FOX_HEADS * LANES), 1)
    caug = sum(_dot(pc, jnp.where(pcol == prow * LANES + HEAD_DIM + t, 1.0, 0.0).astype(bf16))
               for t, pc in enumerate(pieces))
    ones3 = jnp.where((lane >= HEAD_DIM) & (lane < HEAD_DIM + len(pieces)), 1.0, 0.0)

    q = _dot(xh, wm_ref[:, OFF_Q:OFF_Q + W_FOX]) * QK_SCALE
    k = _dot(xh, wm_ref[:, OFF_K:OFF_K + W_FOX])
    k_ref[...] = k
    v = _dot(xh, wm_ref[:, OFF_V:OFF_V + W_FOX])
    v_ref[...] = v
    for h in range(N_FOX_HEADS):
        qa_ref[h] = jnp.where(low, _head_block(q, h), ones3).astype(bf16)
        ka_ref[h] = jnp.where(low, _head_block(k, h), caug[:, LANES * h:LANES * (h + 1)]).astype(bf16)
        va_ref[h] = jnp.where(low, _head_block(v, h), 1.0).astype(bf16)
    u_ref[...] = _dot(xh, wm_ref[:, OFF_U:OFF_U + W_SSM])
    qm = _dot(xh, wm_ref[:, OFF_QM:OFF_QM + W_MEM])
    qm_ref[...] = (qm * QK_SCALE).astype(bf16)
    gates = _dot(xh, wm_ref[:, OFF_G:W_MAIN])
    sg_ref[...] = _silu(gates).astype(bf16)


def _inproj(x, g, wm, wfh, wfl, bfp, tm):
    B, L, D = x.shape
    row = lambda w: pl.BlockSpec((None, tm, w), lambda b, i: (b, i, 0))
    const = lambda shape: pl.BlockSpec(shape, lambda b, i: (0,) * len(shape))
    heads = pl.BlockSpec((None, N_FOX_HEADS, tm, LANES), lambda b, i: (b, 0, i, 0))
    aug = jax.ShapeDtypeStruct((B, N_FOX_HEADS, L, LANES), bf16)
    out_shape = (
        aug,
        jax.ShapeDtypeStruct((B, L, W_FOX), f32),
        aug,
        jax.ShapeDtypeStruct((B, L, W_FOX), f32),
        aug,
        jax.ShapeDtypeStruct((B, L, N_FOX_HEADS), f32),
        jax.ShapeDtypeStruct((B, L, W_FOX + W_SSM + W_MEM), bf16),
        jax.ShapeDtypeStruct((B, L, W_SSM), f32),
        jax.ShapeDtypeStruct((B, L, W_MEM), bf16),
    )
    out_specs = (heads, row(W_FOX), heads, row(W_FOX), heads, row(N_FOX_HEADS),
                 row(W_FOX + W_SSM + W_MEM), row(W_SSM), row(W_MEM))
    return pl.pallas_call(
        _inproj_kernel,
        grid=(B, L // tm),
        in_specs=[row(D), const((1, D)), const(wm.shape), const(wfh.shape), const(wfl.shape), const((1, LANES))],
        out_specs=out_specs,
        out_shape=out_shape,
        scratch_shapes=[pltpu.VMEM((SUBLANES, LANES), f32)],
        compiler_params=_cparams(("parallel", "arbitrary")),
        name="prompt_inproj",
    )(x, g, wm, wfh, wfl, bfp)


def _memkv_kernel(m_ref, g_ref, w_ref, k_ref, v_ref):
    mn = _rms(m_ref[...], g_ref[...]).astype(bf16)
    kv = _dot(mn, w_ref[...])
    k_ref[...] = kv[:, :W_MEM]
    v_ref[...] = kv[:, W_MEM:]


def _memkv(mem, g, w):
    B, N, D = mem.shape
    return pl.pallas_call(
        _memkv_kernel,
        grid=(B,),
        in_specs=[pl.BlockSpec((None, N, D), lambda b: (b, 0, 0)),
                  pl.BlockSpec((1, D), lambda b: (0, 0)),
                  pl.BlockSpec(w.shape, lambda b: (0, 0))],
        out_specs=(pl.BlockSpec((None, N, W_MEM), lambda b: (b, 0, 0)),) * 2,
        out_shape=(jax.ShapeDtypeStruct((B, N, W_MEM), f32),) * 2,
        compiler_params=_cparams(("parallel",)),
        name="prompt_memkv",
    )(mem, g, w)


def _fox_kernel(q_ref, k_ref, v_ref, o_ref, acc_ref, m_ref, *, tq, tk):
    i = pl.program_id(2)
    acc_ref[...] = jnp.zeros_like(acc_ref)
    m_ref[...] = jnp.full_like(m_ref, -jnp.inf)

    def step(kj, r0):
        k0 = pl.multiple_of(kj * tk, tk)
        rows = slice(0, tq) if r0 is None else slice(r0, tq)
        n = rows.stop - rows.start
        for e in range(2):
            s = _dot_nt(q_ref[e, rows, :], k_ref[e, pl.ds(k0, tk), :])
            if r0 is not None:
                row = r0 + lax.broadcasted_iota(jnp.int32, (n, tk), 0)
                col = r0 + lax.broadcasted_iota(jnp.int32, (n, tk), 1)
                s = jnp.where(col <= row, s, -jnp.inf)
            m_old = m_ref[e, rows, :]
            m_new = jnp.maximum(m_old, jnp.max(s, axis=1, keepdims=True))
            p = jnp.exp((s - jnp.tile(m_new, (1, tk // LANES))).astype(bf16))
            acc_ref[e, rows, :] = (jnp.exp(m_old - m_new) * acc_ref[e, rows, :]
                                   + _dot(p, v_ref[e, pl.ds(k0, tk), :]))
            m_ref[e, rows, :] = m_new

    n_full = i * (tq // tk)

    def body(kj, carry):
        step(kj, None)
        return carry

    lax.fori_loop(0, n_full, body, 0)
    for d in range(tq // tk):
        step(n_full + d, d * tk)
    lane = lax.broadcasted_iota(jnp.int32, (tq, LANES), 1)
    a0 = acc_ref[0]
    a1 = acc_ref[1]
    o_ref[...] = jnp.where(lane < HEAD_DIM, a0 / pltpu.roll(a0, HEAD_DIM, 1), pltpu.roll(a1, HEAD_DIM, 1) / a1)


def _fox(qa, ka, va, tq, tk):
    B, _, L, _ = qa.shape
    n_pairs = W_FOX // LANES
    return pl.pallas_call(
        functools.partial(_fox_kernel, tq=tq, tk=tk),
        grid=(B, n_pairs, L // tq),
        in_specs=[pl.BlockSpec((None, 2, tq, LANES), lambda b, j, i: (b, j, i, 0)),
                  pl.BlockSpec((None, 2, L, LANES), lambda b, j, i: (b, j, 0, 0)),
                  pl.BlockSpec((None, 2, L, LANES), lambda b, j, i: (b, j, 0, 0))],
        out_specs=pl.BlockSpec((None, tq, LANES), lambda b, j, i: (b, i, j)),
        out_shape=jax.ShapeDtypeStruct((B, L, W_FOX), f32),
        scratch_shapes=[pltpu.VMEM((2, tq, LANES), f32), pltpu.VMEM((2, tq, LANES), f32)],
        compiler_params=_cparams(("parallel", "parallel", "arbitrary")),
        name="prompt_fox",
    )(qa, ka, va)


def _s5prep_kernel(lr_ref, li_ref, ldt_ref, bre_ref, bim_ref, cre_ref, cim_ref,
                   lamb_ref, bmat_ref, cmat_ref, lpow_ref, lams_ref, *, seg_len):
    lr = lr_ref[...]
    li = li_ref[...]
    dt = jnp.exp(ldt_ref[...])
    ar = lr * dt
    ai = li * dt
    mag = jnp.exp(ar)
    lbr = mag * jnp.cos(ai)
    lbi = mag * jnp.sin(ai)
    lamb_ref[:, :HALF] = jnp.broadcast_to(lbr, (SUBLANES, HALF))
    lamb_ref[:, HALF:] = jnp.broadcast_to(lbi, (SUBLANES, HALF))
    den = lr * lr + li * li
    zr = ((lbr - 1.0) * lr + lbi * li) / den
    zi = (lbi * lr - (lbr - 1.0) * li) / den
    bre = bre_ref[...]
    bim = bim_ref[...]
    bmat_ref[:, :HALF] = zr * bre - zi * bim
    bmat_ref[:, HALF:] = zr * bim + zi * bre
    cmat_ref[:HALF, :] = cre_ref[...]
    cmat_ref[HALF:, :] = -cim_ref[...]
    jj = (lax.broadcasted_iota(jnp.int32, (seg_len, HALF), 0) + 1).astype(f32)
    magp = jnp.exp(ar * jj)
    ang = ai * jj
    pr = magp * jnp.cos(ang)
    pi = magp * jnp.sin(ang)
    lpow_ref[:, :HALF] = pr.astype(bf16)
    lpow_ref[:, HALF:] = pi.astype(bf16)
    lams_ref[:, :HALF] = pr[seg_len - SUBLANES:, :]
    lams_ref[:, HALF:] = pi[seg_len - SUBLANES:, :]


def _s5prep(lam_re, lam_im, log_dt, b_re, b_im, c_re, c_im, seg_len):
    nb, gb = N_SSM_BLOCKS, GROUPS_PER_BLOCK
    eye = jnp.eye(gb, dtype=f32)
    half = lambda a: a.reshape(nb, 1, HALF)
    ldt = jnp.broadcast_to(log_dt[:, None], (N_SSM_GROUPS, SSM_STATE))

    def embed_b(b):
        t = b.reshape(nb, gb, SSM_STATE, SSM_GROUP).transpose(0, 1, 3, 2)
        return (t[:, :, :, None, :] * eye[None, :, None, :, None]).reshape(nb, LANES, HALF)

    def embed_c(c):
        t = c.reshape(nb, gb, SSM_GROUP, SSM_STATE).transpose(0, 1, 3, 2)
        return (t[:, :, :, None, :] * eye[None, :, None, :, None]).reshape(nb, HALF, LANES)

    blk = lambda shape: pl.BlockSpec((None,) + shape, lambda cb: (cb,) + (0,) * len(shape))
    return pl.pallas_call(
        functools.partial(_s5prep_kernel, seg_len=seg_len),
        grid=(nb,),
        in_specs=[blk((1, HALF))] * 3 + [blk((LANES, HALF))] * 2 + [blk((HALF, LANES))] * 2,
        out_specs=(blk((SUBLANES, STATE_W)), blk((LANES, STATE_W)), blk((STATE_W, LANES)),
                   blk((seg_len, STATE_W)), blk((SUBLANES, STATE_W))),
        out_shape=(jax.ShapeDtypeStruct((nb, SUBLANES, STATE_W), f32),
                   jax.ShapeDtypeStruct((nb, LANES, STATE_W), f32),
                   jax.ShapeDtypeStruct((nb, STATE_W, LANES), f32),
                   jax.ShapeDtypeStruct((nb, seg_len, STATE_W), bf16),
                   jax.ShapeDtypeStruct((nb, SUBLANES, STATE_W), f32)),
        compiler_params=_cparams(("parallel",)),
        name="s5_tables",
    )(half(lam_re), half(lam_im), half(ldt), embed_b(b_re), embed_b(b_im), embed_c(c_re), embed_c(c_im))


def _s5scan_kernel(u_ref, bmat_ref, cmat_ref, lamb_ref, dsk_ref, y_ref, hend_ref,
                   up_ref, hb_ref, hcar_ref, *, tj):
    j = pl.program_id(2)

    @pl.when(j == 0)
    def _():
        hcar_ref[...] = jnp.zeros_like(hcar_ref)

    def gather(t, c):
        up_ref[pl.ds(pl.multiple_of(t * N_SEG, N_SEG), N_SEG), :] = u_ref[:, t, :]
        return c

    lax.fori_loop(0, tj, gather, 0)
    hb_ref[...] = _dot(up_ref[...].astype(bf16), bmat_ref[...])
    lam = lamb_ref[...]
    lr = lam[:, :HALF]
    li = lam[:, HALF:]

    def rec(t, h):
        hr, hi = h
        r0 = pl.multiple_of(t * N_SEG, N_SEG)
        nr = lr * hr - li * hi + hb_ref[pl.ds(r0, N_SEG), :HALF]
        ni = lr * hi + li * hr + hb_ref[pl.ds(r0, N_SEG), HALF:]
        hb_ref[pl.ds(r0, N_SEG), :HALF] = nr
        hb_ref[pl.ds(r0, N_SEG), HALF:] = ni
        return nr, ni

    hr, hi = lax.fori_loop(0, tj, rec, (hcar_ref[:, :HALF], hcar_ref[:, HALF:]))
    hcar_ref[:, :HALF] = hr
    hcar_ref[:, HALF:] = hi
    up_ref[...] = _dot(hb_ref[...].astype(bf16), cmat_ref[...]) + dsk_ref[...] * up_ref[...]

    def scatter(t, c):
        y_ref[:, t, :] = up_ref[pl.ds(pl.multiple_of(t * N_SEG, N_SEG), N_SEG), :]
        return c

    lax.fori_loop(0, tj, scatter, 0)

    @pl.when(j == pl.num_programs(2) - 1)
    def _():
        hend_ref[...] = hcar_ref[...]


def _s5scan(u, bmat, cmat, lamb, dsk, tj):
    B, L, _ = u.shape
    seg_len = L // N_SEG
    u4 = u.reshape(B, N_SEG, seg_len, W_SSM)
    nb = N_SSM_BLOCKS
    tile = pl.BlockSpec((None, N_SEG, tj, LANES), lambda b, cb, j: (b, 0, j, cb))
    per_cb = lambda shape: pl.BlockSpec((None,) + shape, lambda b, cb, j: (cb,) + (0,) * len(shape))
    y4, hend = pl.pallas_call(
        functools.partial(_s5scan_kernel, tj=tj),
        grid=(B, nb, seg_len // tj),
        in_specs=[tile, per_cb((LANES, STATE_W)), per_cb((STATE_W, LANES)), per_cb((SUBLANES, STATE_W)),
                  per_cb((1, LANES))],
        out_specs=(tile, pl.BlockSpec((None, None, N_SEG, STATE_W), lambda b, cb, j: (b, cb, 0, 0))),
        out_shape=(jax.ShapeDtypeStruct((B, N_SEG, seg_len, W_SSM), f32),
                   jax.ShapeDtypeStruct((B, nb, N_SEG, STATE_W), f32)),
        scratch_shapes=[pltpu.VMEM((N_SEG * tj, LANES), f32), pltpu.VMEM((N_SEG * tj, STATE_W), f32),
                        pltpu.VMEM((N_SEG, STATE_W), f32)],
        compiler_params=_cparams(("parallel", "parallel", "arbitrary")),
        name="prompt_s5_scan",
    )(u4, bmat, cmat, lamb, dsk)
    return y4.reshape(B, L, W_SSM), hend


def _s5fix_kernel(hend_ref, lams_ref, cmat_ref, hlast_ref, cp_ref, hs_ref):
    he = hend_ref[...]
    lsr = lams_ref[SUBLANES - 1:, :HALF]
    lsi = lams_ref[SUBLANES - 1:, HALF:]
    hs_ref[...] = jnp.zeros_like(hs_ref)
    hr = jnp.zeros((1, HALF), f32)
    hi = jnp.zeros((1, HALF), f32)
    for s in range(N_SEG):
        hs_ref[s:s + 1, :HALF] = hr
        hs_ref[s:s + 1, HALF:] = hi
        nr = lsr * hr - lsi * hi + he[s:s + 1, :HALF]
        ni = lsr * hi + lsi * hr + he[s:s + 1, HALF:]
        hr, hi = nr, ni
    hlast_ref[:, :HALF] = hr
    hlast_ref[:, HALF:] = hi
    hst_r = hs_ref[:, :HALF].T
    hst_i = hs_ref[:, HALF:].T
    cre = cmat_ref[:HALF, :]
    mci = cmat_ref[HALF:, :]
    for s in range(N_SEG):
        col_r = hst_r[:, s:s + 1]
        col_i = hst_i[:, s:s + 1]
        cp_ref[s, :HALF, :] = (cre * col_r + mci * col_i).astype(bf16)
        cp_ref[s, HALF:, :] = (mci * col_r - cre * col_i).astype(bf16)


def _s5fix(hend, lams, cmat):
    B, nb = hend.shape[:2]
    return pl.pallas_call(
        _s5fix_kernel,
        grid=(B, nb),
        in_specs=[pl.BlockSpec((None, None, N_SEG, STATE_W), lambda b, cb: (b, cb, 0, 0)),
                  pl.BlockSpec((None, SUBLANES, STATE_W), lambda b, cb: (cb, 0, 0)),
                  pl.BlockSpec((None, STATE_W, LANES), lambda b, cb: (cb, 0, 0))],
        out_specs=(pl.BlockSpec((None, None, 1, STATE_W), lambda b, cb: (b, cb, 0, 0)),
                   pl.BlockSpec((None, N_SEG, None, STATE_W, LANES), lambda b, cb: (b, 0, cb, 0, 0))),
        out_shape=(jax.ShapeDtypeStruct((B, nb, 1, STATE_W), f32),
                   jax.ShapeDtypeStruct((B, N_SEG, nb, STATE_W, LANES), bf16)),
        scratch_shapes=[pltpu.VMEM((LANES, STATE_W), f32)],
        compiler_params=_cparams(("parallel", "parallel")),
        name="prompt_s5_fix",
    )(hend, lams, cmat)


def _pair_attend(q, k, v):
    lane = lax.broadcasted_iota(jnp.int32, q.shape, 1)
    zero = jnp.zeros_like(q)
    outs = []
    for e in range(2):
        qe = jnp.where((lane < HEAD_DIM) == (e == 0), q, zero)
        s = _dot_nt(qe, k)
        p = jnp.exp(s - jnp.max(s, axis=1, keepdims=True))
        outs.append(_dot(p.astype(bf16), v) / jnp.sum(p, axis=1, keepdims=True))
    return jnp.where(lane < HEAD_DIM, outs[0], outs[1])


def _outproj_kernel(x_ref, fox_ref, y_ref, lp_ref, cp_ref, qm_ref, mk_ref, mv_ref, sg_ref,
                    wglu_ref, bglu_ref, wout_ref, gfin_ref, o_ref, *, final):
    corr = [_dot(lp_ref[cb], cp_ref[cb]) for cb in range(N_SSM_BLOCKS)]
    ssm_y = y_ref[...] + jnp.concatenate(corr, axis=1)
    s = _gelu(ssm_y)
    s = s * jax.nn.sigmoid(_dot(s.astype(bf16), wglu_ref[...]) + bglu_ref[...])
    qm = qm_ref[...]
    mk = mk_ref[...].astype(bf16)
    mv = mv_ref[...].astype(bf16)
    mem = [_pair_attend(qm[:, LANES * jp:LANES * (jp + 1)], mk[:, LANES * jp:LANES * (jp + 1)],
                        mv[:, LANES * jp:LANES * (jp + 1)]) for jp in range(W_MEM // LANES)]
    sg = sg_ref[...].astype(f32)
    mix = jnp.concatenate([fox_ref[...] * sg[:, :W_FOX],
                           s * sg[:, W_FOX:W_FOX + W_SSM],
                           jnp.concatenate(mem, axis=1) * sg[:, W_FOX + W_SSM:]], axis=1)
    xo = x_ref[...] + _dot(mix.astype(bf16), wout_ref[...])
    if final:
        xo = _rms(xo, gfin_ref[...])
    o_ref[...] = xo


def _outproj(x, fox_o, y, lpow, cp, qm, mk, mv, sg, wglu, bglu, wout, gfin, tm, final):
    B, L, D = x.shape
    seg_len = L // N_SEG
    tiles_per_seg = seg_len // tm
    row = lambda w: pl.BlockSpec((None, tm, w), lambda b, i: (b, i, 0))
    const = lambda shape: pl.BlockSpec(shape, lambda b, i: (0,) * len(shape))
    nb = N_SSM_BLOCKS
    return pl.pallas_call(
        functools.partial(_outproj_kernel, final=final),
        grid=(B, L // tm),
        in_specs=[row(D), row(W_FOX), row(W_SSM),
                  pl.BlockSpec((nb, tm, STATE_W), lambda b, i: (0, i % tiles_per_seg, 0)),
                  pl.BlockSpec((None, None, nb, STATE_W, LANES), lambda b, i: (b, i // tiles_per_seg, 0, 0, 0)),
                  row(W_MEM),
                  pl.BlockSpec((None,) + mk.shape[1:], lambda b, i: (b, 0, 0)),
                  pl.BlockSpec((None,) + mv.shape[1:], lambda b, i: (b, 0, 0)),
                  row(D), const(wglu.shape), const((1, W_SSM)), const(wout.shape), const((1, D))],
        out_specs=row(D),
        out_shape=jax.ShapeDtypeStruct((B, L, D), f32),
        compiler_params=_cparams(("parallel", "parallel")),
        name="prompt_outproj",
    )(x, fox_o, y, lpow, cp, qm, mk, mv, sg, wglu, bglu, wout, gfin)


def _s_inproj_kernel(x_ref, g_ref, w_ref, z_ref):
    xn = _rms(x_ref[...], g_ref[...])
    z_ref[...] = _dot3(xn, w_ref[...])


def _s_inproj(xs, g, wcat, tn):
    R, D = xs.shape
    return pl.pallas_call(
        _s_inproj_kernel,
        grid=(W_CAT // tn,),
        in_specs=[pl.BlockSpec((R, D), lambda n: (0, 0)), pl.BlockSpec((1, D), lambda n: (0, 0)),
                  pl.BlockSpec((D, tn), lambda n: (0, n))],
        out_specs=pl.BlockSpec((R, tn), lambda n: (0, n)),
        out_shape=jax.ShapeDtypeStruct((R, W_CAT), f32),
        compiler_params=_cparams(("parallel",)),
        name="sample_inproj",
    )(xs, g, wcat)


def _full_max(x):
    return jnp.max(jnp.max(x, axis=0, keepdims=True), axis=1, keepdims=True)


def _full_sum(x):
    return jnp.sum(jnp.sum(x, axis=0, keepdims=True), axis=1, keepdims=True)


def _lane_sums(acc):
    ones = jnp.ones((SUBLANES, LANES), bf16)
    hi, lo = _split2(acc)
    return (_dot_nt(ones, hi) + _dot_nt(ones, lo))[0:1, :]


def _col_bcast(row):
    return jnp.broadcast_to(row, (LANES, row.shape[1])).T


def _s_attn_kernel(pt_ref, z_ref, bf_ref, lfc_ref, *refs, n_pages):
    ck = refs[:n_pages]
    cv = refs[n_pages:2 * n_pages]
    mk_ref, mv_ref, fox_ref, mem_ref, lf_ref, s_scr, lfs_scr, p_scr, acc_scr, accm_scr = refs[2 * n_pages:]
    r = pl.program_id(0)
    z = z_ref[...]
    q = z[:, OFF_Q:OFF_Q + W_FOX] * QK_SCALE
    k_new = z[:, OFF_K:OFF_K + W_FOX]
    v_new = z[:, OFF_V:OFF_V + W_FOX]
    lf_new = _log_sigmoid(z[:, OFF_F:OFF_F + LANES] + bf_ref[...])
    lf_ref[...] = lf_new

    qt = _col_bcast(q)
    for h in range(N_FOX_HEADS):
        qh = qt[HEAD_DIM * h:HEAD_DIM * (h + 1), :]
        for p in range(n_pages):
            row = h * n_pages + p
            s_scr[row:row + 1, :] = jnp.sum(ck[p][h] * qh, axis=0, keepdims=True)
            lfs_scr[row:row + 1, :] = lfc_ref[h, pl.ds(pt_ref[r, p], 1), :]

    n_rows = N_FOX_HEADS * n_pages
    ur = lax.broadcasted_iota(jnp.int32, (LANES, LANES), 0)
    uc = lax.broadcasted_iota(jnp.int32, (LANES, LANES), 1)
    upper = jnp.where(ur <= uc, 1.0, 0.0).astype(bf16)
    cs = sum(_dot(pc, upper) for pc in _split3(lfs_scr[...]))
    tot = jnp.broadcast_to(cs[:, LANES - 1:], (n_rows, LANES))
    mr = lax.broadcasted_iota(jnp.int32, (n_rows, n_rows), 0)
    mc = lax.broadcasted_iota(jnp.int32, (n_rows, n_rows), 1)
    before = jnp.where((mc < mr) & (mc // n_pages == mr // n_pages), 1.0, 0.0).astype(bf16)
    c = cs + sum(_dot(before, pc) for pc in _split3(tot))
    s_all = s_scr[...] - c

    qk_new = q * k_new
    lane_head = lax.broadcasted_iota(jnp.int32, (1, W_FOX), 1) // HEAD_DIM
    pn_exp = jnp.zeros((1, W_FOX), f32)
    l_exp = jnp.zeros((1, W_FOX), f32)
    for h in range(N_FOX_HEADS):
        rows = slice(h * n_pages, (h + 1) * n_pages)
        s_h = s_all[rows, :]
        c_new = c[(h + 1) * n_pages - 1:(h + 1) * n_pages, LANES - 1:] + lf_new[:, h:h + 1]
        s_new = jnp.sum(qk_new[:, HEAD_DIM * h:HEAD_DIM * (h + 1)], axis=1, keepdims=True) - c_new
        m = jnp.maximum(_full_max(s_h), s_new)
        p_h = jnp.exp(s_h - m)
        pn = jnp.exp(s_new - m)
        p_scr[rows, :] = p_h
        pn_exp = jnp.where(lane_head == h, pn, pn_exp)
        l_exp = jnp.where(lane_head == h, _full_sum(p_h) + pn, l_exp)
        acc = jnp.zeros((HEAD_DIM, LANES), f32)
        for p in range(n_pages):
            acc = acc + cv[p][h] * p_scr[h * n_pages + p:h * n_pages + p + 1, :]
        acc_scr[HEAD_DIM * h:HEAD_DIM * (h + 1), :] = acc
    fox_ref[...] = (_lane_sums(acc_scr[...]) + pn_exp * v_new) / l_exp

    qm = z[:, OFF_QM:OFF_QM + W_MEM] * QK_SCALE
    qmt = _col_bcast(qm)
    n_halves = mk_ref.shape[2] // LANES
    mem_head = lax.broadcasted_iota(jnp.int32, (1, W_MEM), 1) // HEAD_DIM
    lm_exp = jnp.zeros((1, W_MEM), f32)
    for h in range(N_MEM_HEADS):
        qh = qmt[HEAD_DIM * h:HEAD_DIM * (h + 1), :]
        ss = [jnp.sum(mk_ref[h, :, LANES * t:LANES * (t + 1)] * qh, axis=0, keepdims=True) for t in range(n_halves)]
        m = functools.reduce(jnp.maximum, [jnp.max(s, axis=1, keepdims=True) for s in ss])
        ps = [jnp.exp(s - m) for s in ss]
        lm = sum(jnp.sum(p, axis=1, keepdims=True) for p in ps)
        lm_exp = jnp.where(mem_head == h, lm, lm_exp)
        accm_scr[HEAD_DIM * h:HEAD_DIM * (h + 1), :] = sum(
            mv_ref[h, :, LANES * t:LANES * (t + 1)] * ps[t] for t in range(n_halves))
    mem_ref[...] = _lane_sums(accm_scr[...]) / lm_exp


def _s_attn(layer, page_table, z, bfp, cache_k, cache_v, cache_logf, cache_mem_k, cache_mem_v):
    R, n_pages = page_table.shape
    n_pool, page = cache_k.shape[1:3]
    n_mem = cache_mem_k.shape[2]
    assert page == LANES and n_mem % LANES == 0
    ckt = jnp.transpose(cache_k, (0, 1, 3, 4, 2))
    cvt = jnp.transpose(cache_v, (0, 1, 3, 4, 2))
    lft = jnp.transpose(cache_logf, (0, 3, 1, 2))
    mkt = jnp.transpose(cache_mem_k, (0, 1, 3, 4, 2))
    mvt = jnp.transpose(cache_mem_v, (0, 1, 3, 4, 2))
    z3 = z.reshape(R, 1, W_CAT)

    paged = [pl.BlockSpec((None, None, N_FOX_HEADS, HEAD_DIM, page), lambda r, pt, p=p: (layer, pt[r, p], 0, 0, 0))
             for p in range(n_pages)]
    per_row = lambda w: pl.BlockSpec((None, 1, w), lambda r, pt: (r, 0, 0))
    mem_spec = pl.BlockSpec((None, None, N_MEM_HEADS, HEAD_DIM, n_mem), lambda r, pt: (layer, r, 0, 0, 0))
    n_rows = N_FOX_HEADS * n_pages
    fox, mem, lf = pl.pallas_call(
        functools.partial(_s_attn_kernel, n_pages=n_pages),
        grid_spec=pltpu.PrefetchScalarGridSpec(
            num_scalar_prefetch=1,
            grid=(R,),
            in_specs=[per_row(W_CAT), pl.BlockSpec((1, LANES), lambda r, pt: (0, 0)),
                      pl.BlockSpec((None, N_FOX_HEADS, n_pool, page), lambda r, pt: (layer, 0, 0, 0))]
            + paged + paged + [mem_spec, mem_spec],
            out_specs=(per_row(W_FOX), per_row(W_MEM), per_row(LANES)),
            scratch_shapes=[pltpu.VMEM((n_rows, LANES), f32), pltpu.VMEM((n_rows, LANES), f32),
                            pltpu.VMEM((n_rows, LANES), f32), pltpu.VMEM((W_FOX, LANES), f32),
                            pltpu.VMEM((W_MEM, LANES), f32)]),
        out_shape=(jax.ShapeDtypeStruct((R, 1, W_FOX), f32), jax.ShapeDtypeStruct((R, 1, W_MEM), f32),
                   jax.ShapeDtypeStruct((R, 1, LANES), f32)),
        compiler_params=_cparams(("parallel",)),
        name="sample_attn",
    )(page_table, z3, bfp, lft, *([ckt] * n_pages), *([cvt] * n_pages), mkt, mvt)
    return fox.reshape(R, W_FOX), mem.reshape(R, W_MEM), lf.reshape(R, LANES)


def _s_out_kernel(x_ref, z_ref, fox_ref, mem_ref, h0_ref, bmat_ref, cmat_ref, lamb_ref, dsk_ref,
                  wglu_ref, bglu_ref, wout_ref, gfin_ref, o_ref, h_ref, *, final):
    z = z_ref[...]
    u = z[:, OFF_U:OFF_U + W_SSM]
    ys = []
    for cb in range(N_SSM_BLOCKS):
        ucb = u[:, LANES * cb:LANES * (cb + 1)]
        bu = _dot3(ucb, bmat_ref[cb])
        lr = lamb_ref[cb, 0:1, :HALF]
        li = lamb_ref[cb, 0:1, HALF:]
        h0r = h0_ref[:, STATE_W * cb:STATE_W * cb + HALF]
        h0i = h0_ref[:, STATE_W * cb + HALF:STATE_W * (cb + 1)]
        hr = lr * h0r - li * h0i + bu[:, :HALF]
        hi = lr * h0i + li * h0r + bu[:, HALF:]
        h_ref[:, STATE_W * cb:STATE_W * cb + HALF] = hr
        h_ref[:, STATE_W * cb + HALF:STATE_W * (cb + 1)] = hi
        ys.append(_dot3(jnp.concatenate([hr, hi], axis=1), cmat_ref[cb]) + dsk_ref[cb] * ucb)
    s = _gelu(jnp.concatenate(ys, axis=1))
    s = s * jax.nn.sigmoid(_dot3(s, wglu_ref[...]) + bglu_ref[...])
    sg = _silu(z[:, OFF_G:W_MAIN])
    mix = jnp.concatenate([fox_ref[...] * sg[:, :W_FOX],
                           s * sg[:, W_FOX:W_FOX + W_SSM],
                           mem_ref[...] * sg[:, W_FOX + W_SSM:]], axis=1)
    xo = x_ref[...] + _dot3(mix, wout_ref[...])
    if final:
        xo = _rms(xo, gfin_ref[...])
    o_ref[...] = xo


def _s_out(xs, z, fox, mem, h0, bmat, cmat, lamb, dsk, wglu, bglu, wout, gfin, final):
    R, D = xs.shape
    return pl.pallas_call(
        functools.partial(_s_out_kernel, final=final),
        out_shape=(jax.ShapeDtypeStruct((R, D), f32), jax.ShapeDtypeStruct(h0.shape, f32)),
        compiler_params=pltpu.CompilerParams(vmem_limit_bytes=VMEM_LIMIT),
        name="sample_out",
    )(xs, z, fox, mem, h0, bmat, cmat, lamb, dsk, wglu, bglu, wout, gfin)


def _state_to_cols(h):
    R = h.shape[0]
    t = h.reshape(R, N_SSM_BLOCKS, GROUPS_PER_BLOCK, SSM_STATE, 2).transpose(0, 1, 4, 2, 3)
    return t.reshape(R, N_SSM_BLOCKS * STATE_W)


def _cols_to_state(c):
    R = c.shape[0]
    t = c.reshape(R, N_SSM_BLOCKS, 2, GROUPS_PER_BLOCK, SSM_STATE).transpose(0, 1, 3, 4, 2)
    return t.reshape(R, N_SSM_GROUPS, SSM_STATE, 2)


def _tile(n, pref):
    t = min(n, pref)
    assert n % t == 0
    return t


def kernel(x_prompt, x_sample, mem_prompt, cache_k, cache_v, cache_logf, state_ssm, cache_mem_k, cache_mem_v,
           page_table, g_norm, w_in, b_f, lam_re, lam_im, log_dt, b_re, b_im, c_re, c_im, d_skip, w_glu, b_glu,
           g_mem, w_mem_kv, w_out, g_final):
    depth = w_in.shape[0]
    B, L, D = x_prompt.shape
    R = x_sample.shape[0]
    assert x_sample.shape[1] == 1 and L % N_SEG == 0
    seg_len = L // N_SEG
    tm = _tile(seg_len, 512)
    tq = _tile(L, 1024)
    tk = _tile(tq, 512)
    tj = _tile(seg_len, 128)

    xp = x_prompt
    xs = x_sample.reshape(R, D)
    gfin = g_final.reshape(1, D)
    outs = {n: [] for n in ("kp", "vp", "lfp", "hp", "mkp", "mvp", "ks", "vs", "lfs", "hs")}
    cuts = [0, 384, 768, 1152, 1158, 1542, 1926, 2310, 2566, 2822]
    for l in range(depth):
        final = l == depth - 1
        w = w_in[l]
        seg = {n: w[:, a:b] for n, a, b in zip(("q", "k", "v", "f", "ga", "u", "gs", "qm", "gm"), cuts[:-1], cuts[1:])}
        wmain = jnp.concatenate([seg[n] for n in ("q", "k", "v", "u", "qm", "ga", "gs", "gm")], axis=1)
        wf = jnp.pad(seg["f"], ((0, 0), (0, LANES - N_FOX_HEADS)))
        wfh = wf.astype(bf16)
        wfl = (wf - wfh.astype(f32)).astype(bf16)
        wcat = jnp.concatenate([wmain, wf, jnp.zeros((D, W_CAT - W_MAIN - LANES), f32)], axis=1)
        bfp = jnp.pad(b_f[l], (0, LANES - N_FOX_HEADS)).reshape(1, LANES)
        gn = g_norm[l].reshape(1, D)
        wglu = w_glu[l]
        bglu = b_glu[l].reshape(1, W_SSM)
        wout = w_out[l]

        lamb, bmat, cmat, lpow, lams = _s5prep(lam_re[l], lam_im[l], log_dt[l], b_re[l], b_im[l],
                                               c_re[l], c_im[l], seg_len)
        dsk = d_skip[l].reshape(N_SSM_BLOCKS, 1, LANES)

        qa, k, ka, v, va, lf, sg, u, qm = _inproj(xp, gn, wmain.astype(bf16), wfh, wfl, bfp, tm)
        mk, mv = _memkv(mem_prompt, g_mem[l].reshape(1, D), w_mem_kv[l].astype(bf16))
        fox_o = _fox(qa, ka, va, tq, tk)
        y, hend = _s5scan(u, bmat.astype(bf16), cmat.astype(bf16), lamb, dsk, tj)
        hlast, cp = _s5fix(hend, lams, cmat)
        xp = _outproj(xp, fox_o, y, lpow, cp, qm, mk, mv, sg, wglu.astype(bf16), bglu, wout.astype(bf16), gfin,
                      tm, final)
        outs["kp"].append(k.reshape(B, L, N_FOX_HEADS, HEAD_DIM))
        outs["vp"].append(v.reshape(B, L, N_FOX_HEADS, HEAD_DIM))
        outs["lfp"].append(lf)
        outs["hp"].append(_cols_to_state(hlast.reshape(B, N_SSM_BLOCKS * STATE_W)))
        outs["mkp"].append(mk.reshape(B, -1, N_MEM_HEADS, HEAD_DIM))
        outs["mvp"].append(mv.reshape(B, -1, N_MEM_HEADS, HEAD_DIM))

        z = _s_inproj(xs, gn, wcat, 256)
        fox_s, mem_s, lf_s = _s_attn(l, page_table, z, bfp, cache_k, cache_v, cache_logf, cache_mem_k, cache_mem_v)
        xs, h_new = _s_out(xs, z, fox_s, mem_s, _state_to_cols(state_ssm[l]), bmat, cmat, lamb, dsk,
                           wglu, bglu, wout, gfin, final)
        outs["ks"].append(z[:, OFF_K:OFF_K + W_FOX].reshape(R, 1, N_FOX_HEADS, HEAD_DIM))
        outs["vs"].append(z[:, OFF_V:OFF_V + W_FOX].reshape(R, 1, N_FOX_HEADS, HEAD_DIM))
        outs["lfs"].append(lf_s[:, :N_FOX_HEADS].reshape(R, 1, N_FOX_HEADS))
        outs["hs"].append(_cols_to_state(h_new))

    st = lambda n: jnp.stack(outs[n])
    return (xp, xs.reshape(R, 1, D), st("kp"), st("vp"), st("lfp"), st("hp"), st("mkp"), st("mvp"),
            st("ks"), st("vs"), st("lfs"), st("hs"))
```

```python
import functools

import jax
import jax.numpy as jnp
from jax import lax
from jax.experimental import pallas as pl
from jax.experimental.pallas import tpu as pltpu

f32 = jnp.float32
bf16 = jnp.bfloat16

HEAD_DIM = 64
N_FOX_HEADS = 6
W_FOX = N_FOX_HEADS * HEAD_DIM
SSM_GROUP = 16
N_SSM_GROUPS = 24
W_SSM = N_SSM_GROUPS * SSM_GROUP
SSM_STATE = 64
N_MEM_HEADS = 4
W_MEM = N_MEM_HEADS * HEAD_DIM
W_MIX = W_FOX + W_SSM + W_MEM
RMS_EPS = 1e-6
QK_SCALE = HEAD_DIM ** -0.5

LANES = 128
SUBLANES = 8
GROUPS_PER_BLOCK = LANES // SSM_GROUP
N_SSM_BLOCKS = N_SSM_GROUPS // GROUPS_PER_BLOCK
HALF = GROUPS_PER_BLOCK * SSM_STATE
STATE_W = 2 * HALF
N_SEG = SUBLANES
PIECE_PITCH = SUBLANES

OFF_Q, OFF_K, OFF_V, OFF_U, OFF_QM, OFF_G = 0, 384, 768, 1152, 1536, 1792
W_MAIN = OFF_G + W_MIX
OFF_F = W_MAIN
W_CAT = 3072
IN_CUTS = (0, 384, 768, 1152, 1158, 1542, 1926, 2310, 2566, 2822)

VMEM_LIMIT = 56 * 1024 * 1024


def _cparams(sem):
    return pltpu.CompilerParams(dimension_semantics=sem, vmem_limit_bytes=VMEM_LIMIT)


def _dot(a, b):
    return jnp.dot(a, b, preferred_element_type=f32)


def _dot_nt(a, b):
    return lax.dot_general(a, b, (((1,), (1,)), ((), ())), preferred_element_type=f32)


def _split2(a):
    hi = a.astype(bf16)
    lo = (a - hi.astype(f32)).astype(bf16)
    return hi, lo


def _split3(a):
    hi = a.astype(bf16)
    r = a - hi.astype(f32)
    mid = r.astype(bf16)
    lo = (r - mid.astype(f32)).astype(bf16)
    return hi, mid, lo


def _dot3(a, b):
    ah, al = _split2(a)
    bh, bl = _split2(b)
    return _dot(ah, bh) + _dot(al, bh) + _dot(ah, bl)


def _rms(x, g):
    ms = jnp.mean(x * x, axis=-1, keepdims=True)
    return x * lax.rsqrt(ms + RMS_EPS) * g


def _log_sigmoid(x):
    return jnp.minimum(x, 0.0) - jnp.log1p(jnp.exp(-jnp.abs(x)))


def _silu(x):
    return x * jax.nn.sigmoid(x)


def _gelu(x):
    return 0.5 * x * (1.0 + lax.erf(x * (2.0 ** -0.5)))


def _tri(n):
    r = lax.broadcasted_iota(jnp.int32, (n, n), 0)
    c = lax.broadcasted_iota(jnp.int32, (n, n), 1)
    return jnp.where(c <= r, 1.0, 0.0).astype(bf16)


def _pack_pieces(x):
    packed = None
    for i, pc in enumerate(_split3(x)):
        pc = pc.astype(f32)
        if i:
            pc = pltpu.roll(pc, PIECE_PITCH * i, 1)
        packed = pc if packed is None else packed + pc
    return packed.astype(bf16)


def _unpack_pieces(y):
    return y + pltpu.roll(y, LANES - PIECE_PITCH, 1) + pltpu.roll(y, LANES - 2 * PIECE_PITCH, 1)


def _layer(l, shape):
    return pl.BlockSpec((None,) + tuple(shape), lambda *_: (l,) + (0,) * len(shape),
                        pipeline_mode=pl.Buffered(1))


def _head_block(x, h):
    blk = x[:, LANES * (h // 2):LANES * (h // 2 + 1)]
    return blk if h % 2 == 0 else pltpu.roll(blk, HEAD_DIM, 1)


def _inproj_kernel(x_ref, g_ref, wm_ref, wf_ref, bf_ref,
                   qa_ref, k_ref, ka_ref, v_ref, va_ref, lf_ref, sg_ref, u_ref, qm_ref,
                   carry_ref):
    i = pl.program_id(1)

    @pl.when(i == 0)
    def _():
        carry_ref[...] = jnp.zeros_like(carry_ref)

    xn = _rms(x_ref[...], g_ref[...])
    xh = xn.astype(bf16)
    xl = (xn - xh.astype(f32)).astype(bf16)
    tm = xn.shape[0]
    lane = lax.broadcasted_iota(jnp.int32, (tm, LANES), 1)
    low = lane < HEAD_DIM

    ff = _dot(xh, wf_ref[...])
    f = ff[:, :LANES] + ff[:, LANES:] + _dot(xl, wf_ref[:, :LANES]) + bf_ref[...]
    lf = jnp.where(lane < N_FOX_HEADS, _log_sigmoid(f), 0.0)
    lf_ref[...] = lf[:, :N_FOX_HEADS]
    cs = _unpack_pieces(_dot(_tri(tm), _pack_pieces(lf)))
    cs = jnp.where(lane < N_FOX_HEADS, cs, 0.0) + carry_ref[0:1, :]
    carry_ref[...] = jnp.broadcast_to(cs[tm - 1:, :], carry_ref.shape)
    prow = lax.broadcasted_iota(jnp.int32, (LANES, N_FOX_HEADS * LANES), 0)
    pcol = lax.broadcasted_iota(jnp.int32, (LANES, N_FOX_HEADS * LANES), 1)
    place = (pcol == (prow % PIECE_PITCH) * LANES + HEAD_DIM + prow // PIECE_PITCH) & (prow % PIECE_PITCH < N_FOX_HEADS)
    caug = _dot(_pack_pieces(-cs), jnp.where(place, 1.0, 0.0).astype(bf16))
    ones3 = jnp.where((lane >= HEAD_DIM) & (lane < HEAD_DIM + 3), 1.0, 0.0)

    qk = _dot(xh, wm_ref[:, OFF_Q:OFF_V])
    q = qk[:, :W_FOX] * QK_SCALE
    k = qk[:, W_FOX:]
    k_ref[...] = k
    vu = _dot(xh, wm_ref[:, OFF_V:OFF_QM])
    v = vu[:, :W_FOX]
    v_ref[...] = v
    u_ref[...] = vu[:, W_FOX:]
    for h in range(N_FOX_HEADS):
        qa_ref[h] = jnp.where(low, _head_block(q, h), ones3).astype(bf16)
        ka_ref[h] = jnp.where(low, _head_block(k, h), caug[:, LANES * h:LANES * (h + 1)]).astype(bf16)
        va_ref[h] = jnp.where(low, _head_block(v, h), 1.0).astype(bf16)
    rest = _dot(xh, wm_ref[:, OFF_QM:W_MAIN])
    qm_ref[...] = (rest[:, :W_MEM] * QK_SCALE).astype(bf16)
    sg_ref[...] = _silu(rest[:, W_MEM:]).astype(bf16)


def _inproj(l, x, g_all, wm_all, wf_all, bf_all, tm):
    B, L, D = x.shape
    row = lambda w: pl.BlockSpec((None, tm, w), lambda b, i: (b, i, 0))
    heads = pl.BlockSpec((None, N_FOX_HEADS, tm, LANES), lambda b, i: (b, 0, i, 0))
    aug = jax.ShapeDtypeStruct((B, N_FOX_HEADS, L, LANES), bf16)
    out_shape = (
        aug,
        jax.ShapeDtypeStruct((B, L, W_FOX), f32),
        aug,
        jax.ShapeDtypeStruct((B, L, W_FOX), f32),
        aug,
        jax.ShapeDtypeStruct((B, L, N_FOX_HEADS), f32),
        jax.ShapeDtypeStruct((B, L, W_MIX), bf16),
        jax.ShapeDtypeStruct((B, L, W_SSM), f32),
        jax.ShapeDtypeStruct((B, L, W_MEM), bf16),
    )
    out_specs = (heads, row(W_FOX), heads, row(W_FOX), heads, row(N_FOX_HEADS),
                 row(W_MIX), row(W_SSM), row(W_MEM))
    return pl.pallas_call(
        _inproj_kernel,
        grid=(B, L // tm),
        in_specs=[row(D), _layer(l, (1, D)), _layer(l, wm_all.shape[1:]), _layer(l, wf_all.shape[1:]),
                  _layer(l, (1, LANES))],
        out_specs=out_specs,
        out_shape=out_shape,
        scratch_shapes=[pltpu.VMEM((SUBLANES, LANES), f32)],
        compiler_params=_cparams(("parallel", "arbitrary")),
        name="prompt_inproj",
    )(x, g_all, wm_all, wf_all, bf_all)


def _memkv_kernel(m_ref, g_ref, w_ref, k_ref, v_ref):
    mn = _rms(m_ref[...], g_ref[...]).astype(bf16)
    kv = _dot(mn, w_ref[...])
    k_ref[...] = kv[:, :W_MEM]
    v_ref[...] = kv[:, W_MEM:]


def _memkv(mem, g_all, w_all):
    B, N, D = mem.shape
    depth = w_all.shape[0]
    out = pl.BlockSpec((None, None, N, W_MEM), lambda l, b: (l, b, 0, 0))
    return pl.pallas_call(
        _memkv_kernel,
        grid=(depth, B),
        in_specs=[pl.BlockSpec((None, N, D), lambda l, b: (b, 0, 0)),
                  pl.BlockSpec((None, 1, D), lambda l, b: (l, 0, 0)),
                  pl.BlockSpec((None,) + w_all.shape[1:], lambda l, b: (l, 0, 0))],
        out_specs=(out, out),
        out_shape=(jax.ShapeDtypeStruct((depth, B, N, W_MEM), f32),) * 2,
        compiler_params=_cparams(("parallel", "parallel")),
        name="prompt_memkv",
    )(mem, g_all, w_all)


def _fox_kernel(q_ref, k_ref, v_ref, o_ref, acc_ref, m_ref, *, tq, tk):
    i = pl.program_id(2)
    acc_ref[...] = jnp.zeros_like(acc_ref)
    m_ref[...] = jnp.full_like(m_ref, -jnp.inf)

    def step(kj, r0):
        k0 = pl.multiple_of(kj * tk, tk)
        rows = slice(0, tq) if r0 is None else slice(r0, tq)
        n = rows.stop - rows.start
        for e in range(2):
            s = _dot_nt(q_ref[e, rows, :], k_ref[e, pl.ds(k0, tk), :])
            if r0 is not None:
                row = r0 + lax.broadcasted_iota(jnp.int32, (n, tk), 0)
                col = r0 + lax.broadcasted_iota(jnp.int32, (n, tk), 1)
                s = jnp.where(col <= row, s, -jnp.inf)
            m_old = m_ref[e, rows, :]
            m_new = jnp.maximum(m_old, jnp.max(s, axis=1, keepdims=True))
            p = jnp.exp((s - jnp.tile(m_new, (1, tk // LANES))).astype(bf16))
            acc_ref[e, rows, :] = (jnp.exp(m_old - m_new) * acc_ref[e, rows, :]
                                   + _dot(p, v_ref[e, pl.ds(k0, tk), :]))
            m_ref[e, rows, :] = m_new

    n_full = i * (tq // tk)

    def body(kj, carry):
        step(kj, None)
        return carry

    lax.fori_loop(0, n_full, body, 0)
    for d in range(tq // tk):
        step(n_full + d, d * tk)
    lane = lax.broadcasted_iota(jnp.int32, (tq, LANES), 1)
    a0 = acc_ref[0]
    a1 = acc_ref[1]
    o_ref[...] = jnp.where(lane < HEAD_DIM, a0 / pltpu.roll(a0, HEAD_DIM, 1), pltpu.roll(a1, HEAD_DIM, 1) / a1)


def _fox(qa, ka, va, tq, tk):
    B, _, L, _ = qa.shape
    n_pairs = W_FOX // LANES
    return pl.pallas_call(
        functools.partial(_fox_kernel, tq=tq, tk=tk),
        grid=(B, n_pairs, L // tq),
        in_specs=[pl.BlockSpec((None, 2, tq, LANES), lambda b, j, i: (b, j, i, 0)),
                  pl.BlockSpec((None, 2, L, LANES), lambda b, j, i: (b, j, 0, 0)),
                  pl.BlockSpec((None, 2, L, LANES), lambda b, j, i: (b, j, 0, 0))],
        out_specs=pl.BlockSpec((None, tq, LANES), lambda b, j, i: (b, i, j)),
        out_shape=jax.ShapeDtypeStruct((B, L, W_FOX), f32),
        scratch_shapes=[pltpu.VMEM((2, tq, LANES), f32), pltpu.VMEM((2, tq, LANES), f32)],
        compiler_params=_cparams(("parallel", "parallel", "arbitrary")),
        name="prompt_fox",
    )(qa, ka, va)


def _s5prep_kernel(lr_ref, li_ref, ldt_ref, bre_ref, bim_ref, cre_ref, cim_ref,
                   lamb_ref, bmat_ref, cmat_ref, lpow_ref, lams_ref, *, seg_len):
    lr = lr_ref[...]
    li = li_ref[...]
    dt = jnp.exp(ldt_ref[...])
    ar = lr * dt
    ai = li * dt
    mag = jnp.exp(ar)
    lbr = mag * jnp.cos(ai)
    lbi = mag * jnp.sin(ai)
    lamb_ref[:, :HALF] = jnp.broadcast_to(lbr, (SUBLANES, HALF))
    lamb_ref[:, HALF:] = jnp.broadcast_to(lbi, (SUBLANES, HALF))
    den = lr * lr + li * li
    zr = ((lbr - 1.0) * lr + lbi * li) / den
    zi = (lbi * lr - (lbr - 1.0) * li) / den
    bre = bre_ref[...]
    bim = bim_ref[...]
    bmat_ref[:, :HALF] = zr * bre - zi * bim
    bmat_ref[:, HALF:] = zr * bim + zi * bre
    cmat_ref[:HALF, :] = cre_ref[...]
    cmat_ref[HALF:, :] = -cim_ref[...]
    jj = (lax.broadcasted_iota(jnp.int32, (seg_len, HALF), 0) + 1).astype(f32)
    magp = jnp.exp(ar * jj)
    ang = ai * jj
    pr = magp * jnp.cos(ang)
    pi = magp * jnp.sin(ang)
    lpow_ref[:, :HALF] = pr.astype(bf16)
    lpow_ref[:, HALF:] = pi.astype(bf16)
    lams_ref[:, :HALF] = pr[seg_len - SUBLANES:, :]
    lams_ref[:, HALF:] = pi[seg_len - SUBLANES:, :]


def _s5prep(lam_re, lam_im, log_dt, b_re, b_im, c_re, c_im, seg_len):
    depth = lam_re.shape[0]
    nb, gb = N_SSM_BLOCKS, GROUPS_PER_BLOCK
    eye = jnp.eye(gb, dtype=f32)
    half = lambda a: a.reshape(depth, nb, 1, HALF)
    ldt = jnp.broadcast_to(log_dt[:, :, None], (depth, N_SSM_GROUPS, SSM_STATE))

    def embed_b(b):
        t = b.reshape(depth, nb, gb, SSM_STATE, SSM_GROUP).transpose(0, 1, 2, 4, 3)
        return (t[:, :, :, :, None, :] * eye[None, None, :, None, :, None]).reshape(depth, nb, LANES, HALF)

    def embed_c(c):
        t = c.reshape(depth, nb, gb, SSM_GROUP, SSM_STATE).transpose(0, 1, 2, 4, 3)
        return (t[:, :, :, :, None, :] * eye[None, None, :, None, :, None]).reshape(depth, nb, HALF, LANES)

    blk = lambda shape: pl.BlockSpec((None, None) + shape, lambda l, cb: (l, cb) + (0,) * len(shape))
    arr = lambda shape, dt: jax.ShapeDtypeStruct((depth, nb) + shape, dt)
    return pl.pallas_call(
        functools.partial(_s5prep_kernel, seg_len=seg_len),
        grid=(depth, nb),
        in_specs=[blk((1, HALF))] * 3 + [blk((LANES, HALF))] * 2 + [blk((HALF, LANES))] * 2,
        out_specs=(blk((SUBLANES, STATE_W)), blk((LANES, STATE_W)), blk((STATE_W, LANES)),
                   blk((seg_len, STATE_W)), blk((SUBLANES, STATE_W))),
        out_shape=(arr((SUBLANES, STATE_W), f32),
                   arr((LANES, STATE_W), f32),
                   arr((STATE_W, LANES), f32),
                   arr((seg_len, STATE_W), bf16),
                   arr((SUBLANES, STATE_W), f32)),
        compiler_params=_cparams(("parallel", "parallel")),
        name="s5_tables",
    )(half(lam_re), half(lam_im), half(ldt), embed_b(b_re), embed_b(b_im), embed_c(c_re), embed_c(c_im))


def _s5scan_kernel(u_ref, bmat_ref, cmat_ref, lamb_ref, dsk_ref, y_ref, hend_ref,
                   up_ref, hb_ref, hq_ref, hcar_ref, *, tj):
    j = pl.program_id(1)
    nb = N_SSM_BLOCKS

    @pl.when(j == 0)
    def _():
        hcar_ref[...] = jnp.zeros_like(hcar_ref)

    for cb in range(nb):
        for s in range(N_SEG):
            up_ref[cb, pl.ds(s, tj, stride=N_SEG), :] = u_ref[s, :, LANES * cb:LANES * (cb + 1)]
        hb_ref[cb] = _dot(up_ref[cb].astype(bf16), bmat_ref[cb])

    def rec(t2, h):
        r0 = pl.multiple_of(t2 * 2 * N_SEG, 2 * N_SEG)
        new = []
        for cb in range(nb):
            hr, hi = h[2 * cb], h[2 * cb + 1]
            lr = lamb_ref[cb, :, :HALF]
            li = lamb_ref[cb, :, HALF:]
            rs, is_ = [], []
            for d in range(2):
                bu = hb_ref[cb, pl.ds(r0 + d * N_SEG, N_SEG), :]
                hr, hi = lr * hr - li * hi + bu[:, :HALF], lr * hi + li * hr + bu[:, HALF:]
                rs.append(hr)
                is_.append(hi)
            hq_ref[cb, pl.ds(r0, 2 * N_SEG), :HALF] = jnp.concatenate(rs, axis=0).astype(bf16)
            hq_ref[cb, pl.ds(r0, 2 * N_SEG), HALF:] = jnp.concatenate(is_, axis=0).astype(bf16)
            new += [hr, hi]
        return tuple(new)

    h0 = tuple(hcar_ref[cb, :, HALF * part:HALF * (part + 1)] for cb in range(nb) for part in range(2))
    h = lax.fori_loop(0, tj // 2, rec, h0)
    for cb in range(nb):
        hcar_ref[cb, :, :HALF] = h[2 * cb]
        hcar_ref[cb, :, HALF:] = h[2 * cb + 1]
        up_ref[cb] = _dot(hq_ref[cb], cmat_ref[cb]) + dsk_ref[cb] * up_ref[cb]
        for s in range(N_SEG):
            y_ref[s, :, LANES * cb:LANES * (cb + 1)] = up_ref[cb, pl.ds(s, tj, stride=N_SEG), :]

    @pl.when(j == pl.num_programs(1) - 1)
    def _():
        hend_ref[...] = hcar_ref[...]


def _s5scan(l, u, bmat_all, cmat_all, lamb_all, dsk_all, tj):
    B, L, _ = u.shape
    seg_len = L // N_SEG
    u4 = u.reshape(B, N_SEG, seg_len, W_SSM)
    nb = N_SSM_BLOCKS
    tile = pl.BlockSpec((None, N_SEG, tj, W_SSM), lambda b, j: (b, 0, j, 0))
    y4, hend = pl.pallas_call(
        functools.partial(_s5scan_kernel, tj=tj),
        grid=(B, seg_len // tj),
        in_specs=[tile, _layer(l, bmat_all.shape[1:]), _layer(l, cmat_all.shape[1:]),
                  _layer(l, lamb_all.shape[1:]), _layer(l, dsk_all.shape[1:])],
        out_specs=(tile, pl.BlockSpec((None, nb, N_SEG, STATE_W), lambda b, j: (b, 0, 0, 0))),
        out_shape=(jax.ShapeDtypeStruct((B, N_SEG, seg_len, W_SSM), f32),
                   jax.ShapeDtypeStruct((B, nb, N_SEG, STATE_W), f32)),
        scratch_shapes=[pltpu.VMEM((nb, N_SEG * tj, LANES), f32), pltpu.VMEM((nb, N_SEG * tj, STATE_W), f32),
                        pltpu.VMEM((nb, N_SEG * tj, STATE_W), bf16), pltpu.VMEM((nb, N_SEG, STATE_W), f32)],
        compiler_params=_cparams(("parallel", "arbitrary")),
        name="prompt_s5_scan",
    )(u4, bmat_all, cmat_all, lamb_all, dsk_all)
    return y4.reshape(B, L, W_SSM), hend


def _s5fix_kernel(hend_ref, lams_ref, cmat_ref, hlast_ref, cp_ref, hs_ref):
    he = hend_ref[...]
    lsr = lams_ref[SUBLANES - 1:, :HALF]
    lsi = lams_ref[SUBLANES - 1:, HALF:]
    hs_ref[...] = jnp.zeros_like(hs_ref)
    hr = jnp.zeros((1, HALF), f32)
    hi = jnp.zeros((1, HALF), f32)
    for s in range(N_SEG):
        hs_ref[s:s + 1, :HALF] = hr
        hs_ref[s:s + 1, HALF:] = hi
        nr = lsr * hr - lsi * hi + he[s:s + 1, :HALF]
        ni = lsr * hi + lsi * hr + he[s:s + 1, HALF:]
        hr, hi = nr, ni
    hlast_ref[:, :HALF] = hr
    hlast_ref[:, HALF:] = hi
    hst_r = hs_ref[:, :HALF].T
    hst_i = hs_ref[:, HALF:].T
    cre = cmat_ref[:HALF, :]
    mci = cmat_ref[HALF:, :]
    for s in range(N_SEG):
        col_r = hst_r[:, s:s + 1]
        col_i = hst_i[:, s:s + 1]
        cp_ref[s, :HALF, :] = (cre * col_r + mci * col_i).astype(bf16)
        cp_ref[s, HALF:, :] = (mci * col_r - cre * col_i).astype(bf16)


def _s5fix(l, hend, lams_all, cmat_all):
    B, nb = hend.shape[:2]
    return pl.pallas_call(
        _s5fix_kernel,
        grid=(B, nb),
        in_specs=[pl.BlockSpec((None, None, N_SEG, STATE_W), lambda b, cb: (b, cb, 0, 0)),
                  pl.BlockSpec((None, None, SUBLANES, STATE_W), lambda b, cb: (l, cb, 0, 0)),
                  pl.BlockSpec((None, None, STATE_W, LANES), lambda b, cb: (l, cb, 0, 0))],
        out_specs=(pl.BlockSpec((None, None, 1, STATE_W), lambda b, cb: (b, cb, 0, 0)),
                   pl.BlockSpec((None, N_SEG, None, STATE_W, LANES), lambda b, cb: (b, 0, cb, 0, 0))),
        out_shape=(jax.ShapeDtypeStruct((B, nb, 1, STATE_W), f32),
                   jax.ShapeDtypeStruct((B, N_SEG, nb, STATE_W, LANES), bf16)),
        scratch_shapes=[pltpu.VMEM((LANES, STATE_W), f32)],
        compiler_params=_cparams(("parallel", "parallel")),
        name="prompt_s5_fix",
    )(hend, lams_all, cmat_all)


def _pair_attend(q, k, v):
    lane = lax.broadcasted_iota(jnp.int32, q.shape, 1)
    zero = jnp.zeros_like(q)
    outs = []
    for e in range(2):
        qe = jnp.where((lane < HEAD_DIM) == (e == 0), q, zero)
        s = _dot_nt(qe, k)
        p = jnp.exp(s - jnp.max(s, axis=1, keepdims=True))
        outs.append(_dot(p.astype(bf16), v) / jnp.sum(p, axis=1, keepdims=True))
    return jnp.where(lane < HEAD_DIM, outs[0], outs[1])


def _outproj_kernel(x_ref, fox_ref, y_ref, lp_ref, cp_ref, qm_ref, mk_ref, mv_ref, sg_ref,
                    wglu_ref, bglu_ref, wout_ref, gfin_ref, o_ref, *, final):
    corr = [_dot(lp_ref[cb], cp_ref[cb]) for cb in range(N_SSM_BLOCKS)]
    ssm_y = y_ref[...] + jnp.concatenate(corr, axis=1)
    s = _gelu(ssm_y)
    s = s * jax.nn.sigmoid(_dot(s.astype(bf16), wglu_ref[...]) + bglu_ref[...])
    qm = qm_ref[...]
    mk = mk_ref[...].astype(bf16)
    mv = mv_ref[...].astype(bf16)
    mem = [_pair_attend(qm[:, LANES * jp:LANES * (jp + 1)], mk[:, LANES * jp:LANES * (jp + 1)],
                        mv[:, LANES * jp:LANES * (jp + 1)]) for jp in range(W_MEM // LANES)]
    sg = sg_ref[...].astype(f32)
    mix = jnp.concatenate([fox_ref[...] * sg[:, :W_FOX],
                           s * sg[:, W_FOX:W_FOX + W_SSM],
                           jnp.concatenate(mem, axis=1) * sg[:, W_FOX + W_SSM:]], axis=1)
    xo = x_ref[...] + _dot(mix.astype(bf16), wout_ref[...])
    if final:
        xo = _rms(xo, gfin_ref[...])
    o_ref[...] = xo


def _outproj(l, x, fox_o, y, lpow_all, cp, qm, mk_all, mv_all, sg, wglu_all, bglu_all, wout_all, gfin, tm, final):
    B, L, D = x.shape
    seg_len = L // N_SEG
    tiles_per_seg = seg_len // tm
    row = lambda w: pl.BlockSpec((None, tm, w), lambda b, i: (b, i, 0))
    nb = N_SSM_BLOCKS
    mem_spec = pl.BlockSpec((None, None) + mk_all.shape[2:], lambda b, i: (l, b, 0, 0))
    return pl.pallas_call(
        functools.partial(_outproj_kernel, final=final),
        grid=(B, L // tm),
        in_specs=[row(D), row(W_FOX), row(W_SSM),
                  pl.BlockSpec((None, nb, tm, STATE_W), lambda b, i: (l, 0, i % tiles_per_seg, 0)),
                  pl.BlockSpec((None, None, nb, STATE_W, LANES), lambda b, i: (b, i // tiles_per_seg, 0, 0, 0)),
                  row(W_MEM), mem_spec, mem_spec, row(D),
                  _layer(l, wglu_all.shape[1:]), _layer(l, (1, W_SSM)), _layer(l, wout_all.shape[1:]),
                  pl.BlockSpec((1, D), lambda b, i: (0, 0))],
        out_specs=row(D),
        out_shape=jax.ShapeDtypeStruct((B, L, D), f32),
        compiler_params=_cparams(("parallel", "parallel")),
        name="prompt_outproj",
    )(x, fox_o, y, lpow_all, cp, qm, mk_all, mv_all, sg, wglu_all, bglu_all, wout_all, gfin)


def _s_inproj_kernel(x_ref, g_ref, w_ref, z_ref):
    xn = _rms(x_ref[...], g_ref[...])
    z_ref[...] = _dot3(xn, w_ref[...])


def _s_inproj(l, xs, g_all, wcat_all, tn):
    R, D = xs.shape
    return pl.pallas_call(
        _s_inproj_kernel,
        grid=(W_CAT // tn,),
        in_specs=[pl.BlockSpec((R, D), lambda n: (0, 0)), _layer(l, (1, D)),
                  pl.BlockSpec((None, D, tn), lambda n: (l, 0, n))],
        out_specs=pl.BlockSpec((R, tn), lambda n: (0, n)),
        out_shape=jax.ShapeDtypeStruct((R, W_CAT), f32),
        compiler_params=_cparams(("parallel",)),
        name="sample_inproj",
    )(xs, g_all, wcat_all)


def _full_max(x):
    return jnp.max(jnp.max(x, axis=0, keepdims=True), axis=1, keepdims=True)


def _full_sum(x):
    return jnp.sum(jnp.sum(x, axis=0, keepdims=True), axis=1, keepdims=True)


def _lane_sums(acc):
    ones = jnp.ones((SUBLANES, LANES), bf16)
    hi, lo = _split2(acc)
    return (_dot_nt(ones, hi) + _dot_nt(ones, lo))[0:1, :]


def _col_bcast(row):
    return jnp.broadcast_to(row, (LANES, row.shape[1])).T


def _s_attn_kernel(pt_ref, z_ref, bf_ref, lfc_ref, *refs, n_pages, n_seq):
    per_seq = 2 * n_pages + 2
    ins = refs[:n_seq * per_seq]
    fox_ref, mem_ref, lf_ref, s_scr, lfs_scr, p_scr, acc_scr, accm_scr = refs[n_seq * per_seq:]
    for i in range(n_seq):
        mine = ins[i * per_seq:(i + 1) * per_seq]
        _decode_one(pt_ref, pl.program_id(0) * n_seq + i, z_ref.at[i], bf_ref, lfc_ref,
                    mine[:n_pages], mine[n_pages:2 * n_pages], mine[2 * n_pages], mine[2 * n_pages + 1],
                    fox_ref.at[i], mem_ref.at[i], lf_ref.at[i],
                    s_scr.at[i], lfs_scr.at[i], p_scr.at[i], acc_scr.at[i], accm_scr.at[i])


def _decode_one(pt_ref, r, z_ref, bf_ref, lfc_ref, ck, cv, mk_ref, mv_ref, fox_ref, mem_ref, lf_ref,
                s_scr, lfs_scr, p_scr, acc_scr, accm_scr):
    n_pages = len(ck)
    z = z_ref[...]
    q = z[:, OFF_Q:OFF_Q + W_FOX] * QK_SCALE
    k_new = z[:, OFF_K:OFF_K + W_FOX]
    v_new = z[:, OFF_V:OFF_V + W_FOX]
    lf_new = _log_sigmoid(z[:, OFF_F:OFF_F + LANES] + bf_ref[...])
    lf_ref[...] = lf_new

    qt = _col_bcast(q)
    for h in range(N_FOX_HEADS):
        qh = qt[HEAD_DIM * h:HEAD_DIM * (h + 1), :]
        for p in range(n_pages):
            row = h * n_pages + p
            s_scr[row:row + 1, :] = jnp.sum(ck[p][h] * qh, axis=0, keepdims=True)
            lfs_scr[row:row + 1, :] = lfc_ref[h, pl.ds(pt_ref[r, p], 1), :]

    n_rows = N_FOX_HEADS * n_pages
    ur = lax.broadcasted_iota(jnp.int32, (LANES, LANES), 0)
    uc = lax.broadcasted_iota(jnp.int32, (LANES, LANES), 1)
    upper = jnp.where(ur <= uc, 1.0, 0.0).astype(bf16)
    cs = sum(_dot(pc, upper) for pc in _split3(lfs_scr[...]))
    tot = jnp.broadcast_to(cs[:, LANES - 1:], (n_rows, LANES))
    mr = lax.broadcasted_iota(jnp.int32, (n_rows, n_rows), 0)
    mc = lax.broadcasted_iota(jnp.int32, (n_rows, n_rows), 1)
    before = jnp.where((mc < mr) & (mc // n_pages == mr // n_pages), 1.0, 0.0).astype(bf16)
    c = cs + sum(_dot(before, pc) for pc in _split3(tot))
    s_all = s_scr[...] - c

    qk_new = q * k_new
    lane_head = lax.broadcasted_iota(jnp.int32, (1, W_FOX), 1) // HEAD_DIM
    pn_exp = jnp.zeros((1, W_FOX), f32)
    l_exp = jnp.zeros((1, W_FOX), f32)
    for h in range(N_FOX_HEADS):
        rows = slice(h * n_pages, (h + 1) * n_pages)
        s_h = s_all[rows, :]
        c_new = c[(h + 1) * n_pages - 1:(h + 1) * n_pages, LANES - 1:] + lf_new[:, h:h + 1]
        s_new = jnp.sum(qk_new[:, HEAD_DIM * h:HEAD_DIM * (h + 1)], axis=1, keepdims=True) - c_new
        m = jnp.maximum(_full_max(s_h), s_new)
        p_h = jnp.exp(s_h - m)
        pn = jnp.exp(s_new - m)
        p_scr[rows, :] = p_h
        pn_exp = jnp.where(lane_head == h, pn, pn_exp)
        l_exp = jnp.where(lane_head == h, _full_sum(p_h) + pn, l_exp)
        acc = jnp.zeros((HEAD_DIM, LANES), f32)
        for p in range(n_pages):
            acc = acc + cv[p][h] * p_scr[h * n_pages + p:h * n_pages + p + 1, :]
        acc_scr[HEAD_DIM * h:HEAD_DIM * (h + 1), :] = acc
    fox_ref[...] = (_lane_sums(acc_scr[...]) + pn_exp * v_new) / l_exp

    qm = z[:, OFF_QM:OFF_QM + W_MEM] * QK_SCALE
    qmt = _col_bcast(qm)
    n_halves = mk_ref.shape[2] // LANES
    mem_head = lax.broadcasted_iota(jnp.int32, (1, W_MEM), 1) // HEAD_DIM
    lm_exp = jnp.zeros((1, W_MEM), f32)
    for h in range(N_MEM_HEADS):
        qh = qmt[HEAD_DIM * h:HEAD_DIM * (h + 1), :]
        ss = [jnp.sum(mk_ref[h, :, LANES * t:LANES * (t + 1)] * qh, axis=0, keepdims=True) for t in range(n_halves)]
        m = functools.reduce(jnp.maximum, [jnp.max(s, axis=1, keepdims=True) for s in ss])
        ps = [jnp.exp(s - m) for s in ss]
        lm = sum(jnp.sum(p, axis=1, keepdims=True) for p in ps)
        lm_exp = jnp.where(mem_head == h, lm, lm_exp)
        accm_scr[HEAD_DIM * h:HEAD_DIM * (h + 1), :] = sum(
            mv_ref[h, :, LANES * t:LANES * (t + 1)] * ps[t] for t in range(n_halves))
    mem_ref[...] = _lane_sums(accm_scr[...]) / lm_exp


def _s_attn(l, page_table, z, bf_all, ckt, cvt, lft, mkt, mvt, n_seq):
    R, n_pages = page_table.shape
    n_pool, page = ckt.shape[1], ckt.shape[4]
    n_mem = mkt.shape[4]
    assert page == LANES and n_mem % LANES == 0
    n_seq = _tile(R, n_seq)
    z3 = z.reshape(R, 1, W_CAT)
    in_specs, operands = [], []
    for i in range(n_seq):
        paged = [pl.BlockSpec((None, None, N_FOX_HEADS, HEAD_DIM, page),
                              lambda g, pt, i=i, p=p: (l, pt[g * n_seq + i, p], 0, 0, 0)) for p in range(n_pages)]
        mem_spec = pl.BlockSpec((None, None, N_MEM_HEADS, HEAD_DIM, n_mem),
                                lambda g, pt, i=i: (l, g * n_seq + i, 0, 0, 0))
        in_specs += paged + paged + [mem_spec, mem_spec]
        operands += [ckt] * n_pages + [cvt] * n_pages + [mkt, mvt]
    per_row = lambda w: pl.BlockSpec((n_seq, 1, w), lambda g, pt: (g, 0, 0))
    n_rows = N_FOX_HEADS * n_pages
    scr = lambda rows: pltpu.VMEM((n_seq, rows, LANES), f32)
    fox, mem, lf = pl.pallas_call(
        functools.partial(_s_attn_kernel, n_pages=n_pages, n_seq=n_seq),
        grid_spec=pltpu.PrefetchScalarGridSpec(
            num_scalar_prefetch=1,
            grid=(R // n_seq,),
            in_specs=[per_row(W_CAT), _layer(l, (1, LANES)), _layer(l, (N_FOX_HEADS, n_pool, page))] + in_specs,
            out_specs=(per_row(W_FOX), per_row(W_MEM), per_row(LANES)),
            scratch_shapes=[scr(n_rows), scr(n_rows), scr(n_rows), scr(W_FOX), scr(W_MEM)]),
        out_shape=(jax.ShapeDtypeStruct((R, 1, W_FOX), f32), jax.ShapeDtypeStruct((R, 1, W_MEM), f32),
                   jax.ShapeDtypeStruct((R, 1, LANES), f32)),
        compiler_params=_cparams(("parallel",)),
        name="sample_attn",
    )(page_table, z3, bf_all, lft, *operands)
    return fox.reshape(R, W_FOX), mem.reshape(R, W_MEM), lf.reshape(R, LANES)


def _s_out_kernel(x_ref, z_ref, fox_ref, mem_ref, h0_ref, bmat_ref, cmat_ref, lamb_ref, dsk_ref,
                  wglu_ref, bglu_ref, wout_ref, gfin_ref, o_ref, h_ref, *, final):
    z = z_ref[...]
    u = z[:, OFF_U:OFF_U + W_SSM]
    ys = []
    for cb in range(N_SSM_BLOCKS):
        ucb = u[:, LANES * cb:LANES * (cb + 1)]
        bu = _dot3(ucb, bmat_ref[cb])
        lr = lamb_ref[cb, 0:1, :HALF]
        li = lamb_ref[cb, 0:1, HALF:]
        h0r = h0_ref[:, STATE_W * cb:STATE_W * cb + HALF]
        h0i = h0_ref[:, STATE_W * cb + HALF:STATE_W * (cb + 1)]
        hr = lr * h0r - li * h0i + bu[:, :HALF]
        hi = lr * h0i + li * h0r + bu[:, HALF:]
        h_ref[:, STATE_W * cb:STATE_W * cb + HALF] = hr
        h_ref[:, STATE_W * cb + HALF:STATE_W * (cb + 1)] = hi
        ys.append(_dot3(jnp.concatenate([hr, hi], axis=1), cmat_ref[cb]) + dsk_ref[cb] * ucb)
    s = _gelu(jnp.concatenate(ys, axis=1))
    s = s * jax.nn.sigmoid(_dot3(s, wglu_ref[...]) + bglu_ref[...])
    sg = _silu(z[:, OFF_G:W_MAIN])
    mix = jnp.concatenate([fox_ref[...] * sg[:, :W_FOX],
                           s * sg[:, W_FOX:W_FOX + W_SSM],
                           mem_ref[...] * sg[:, W_FOX + W_SSM:]], axis=1)
    xo = x_ref[...] + _dot3(mix, wout_ref[...])
    if final:
        xo = _rms(xo, gfin_ref[...])
    o_ref[...] = xo


def _s_out(l, xs, z, fox, mem, h0_all, bmat_all, cmat_all, lamb_all, dsk_all, wglu_all, bglu_all, wout_all,
           gfin, final):
    R, D = xs.shape
    whole = lambda a: pl.BlockSpec(a.shape, lambda i: (0,) * a.ndim)
    hspec = pl.BlockSpec((R, h0_all.shape[2]), lambda i: (0, 0))
    return pl.pallas_call(
        functools.partial(_s_out_kernel, final=final),
        grid=(1,),
        in_specs=[whole(xs), whole(z), whole(fox), whole(mem), _layer(l, h0_all.shape[1:]),
                  _layer(l, bmat_all.shape[1:]), _layer(l, cmat_all.shape[1:]), _layer(l, lamb_all.shape[1:]),
                  _layer(l, dsk_all.shape[1:]), _layer(l, wglu_all.shape[1:]), _layer(l, (1, W_SSM)),
                  _layer(l, wout_all.shape[1:]), whole(gfin)],
        out_specs=(whole(xs), hspec),
        out_shape=(jax.ShapeDtypeStruct((R, D), f32), jax.ShapeDtypeStruct(h0_all.shape[1:], f32)),
        compiler_params=_cparams(("arbitrary",)),
        name="sample_out",
    )(xs, z, fox, mem, h0_all, bmat_all, cmat_all, lamb_all, dsk_all, wglu_all, bglu_all, wout_all, gfin)


def _state_to_cols(h):
    lead = h.shape[:-3]
    n = len(lead)
    t = h.reshape(lead + (N_SSM_BLOCKS, GROUPS_PER_BLOCK, SSM_STATE, 2))
    t = t.transpose(tuple(range(n)) + (n, n + 3, n + 1, n + 2))
    return t.reshape(lead + (N_SSM_BLOCKS * STATE_W,))


def _cols_to_state(c):
    lead = c.shape[:-1]
    n = len(lead)
    t = c.reshape(lead + (N_SSM_BLOCKS, 2, GROUPS_PER_BLOCK, SSM_STATE))
    t = t.transpose(tuple(range(n)) + (n, n + 2, n + 3, n + 1))
    return t.reshape(lead + (N_SSM_GROUPS, SSM_STATE, 2))


def _tile(n, pref):
    t = min(n, pref)
    assert n % t == 0
    return t


def kernel(x_prompt, x_sample, mem_prompt, cache_k, cache_v, cache_logf, state_ssm, cache_mem_k, cache_mem_v,
           page_table, g_norm, w_in, b_f, lam_re, lam_im, log_dt, b_re, b_im, c_re, c_im, d_skip, w_glu, b_glu,
           g_mem, w_mem_kv, w_out, g_final):
    depth = w_in.shape[0]
    B, L, D = x_prompt.shape
    R = x_sample.shape[0]
    assert x_sample.shape[1] == 1 and L % N_SEG == 0
    seg_len = L // N_SEG
    tm = _tile(seg_len, 512)
    tm_in = _tile(L, 512)
    tq = _tile(L, 1024)
    tk = _tile(tq, 512)
    tj = _tile(seg_len, 128)

    seg = dict(zip(("q", "k", "v", "f", "ga", "u", "gs", "qm", "gm"),
                   (w_in[:, :, a:b] for a, b in zip(IN_CUTS[:-1], IN_CUTS[1:]))))
    wcat_all = jnp.concatenate(
        [seg[n] for n in ("q", "k", "v", "u", "qm", "ga", "gs", "gm", "f")]
        + [jnp.zeros((depth, D, W_CAT - W_MAIN - N_FOX_HEADS), f32)], axis=2)
    wm_all = wcat_all[:, :, :W_MAIN].astype(bf16)
    wf = wcat_all[:, :, OFF_F:OFF_F + LANES]
    wfh = wf.astype(bf16)
    wf_all = jnp.concatenate([wfh, (wf - wfh.astype(f32)).astype(bf16)], axis=2)
    bf_all = jnp.pad(b_f, ((0, 0), (0, LANES - N_FOX_HEADS))).reshape(depth, 1, LANES)
    gn_all = g_norm.reshape(depth, 1, D)
    gmem_all = g_mem.reshape(depth, 1, D)
    bglu_all = b_glu.reshape(depth, 1, W_SSM)
    gfin = g_final.reshape(1, D)
    dsk_all = d_skip.reshape(depth, N_SSM_BLOCKS, 1, LANES)
    wglu_bf = w_glu.astype(bf16)
    wout_bf = w_out.astype(bf16)
    ckt = jnp.transpose(cache_k, (0, 1, 3, 4, 2))
    cvt = jnp.transpose(cache_v, (0, 1, 3, 4, 2))
    lft = jnp.transpose(cache_logf, (0, 3, 1, 2))
    mkt = jnp.transpose(cache_mem_k, (0, 1, 3, 4, 2))
    mvt = jnp.transpose(cache_mem_v, (0, 1, 3, 4, 2))
    h0_all = _state_to_cols(state_ssm)

    lamb_all, bmat_all, cmat_all, lpow_all, lams_all = _s5prep(lam_re, lam_im, log_dt, b_re, b_im, c_re, c_im, seg_len)
    bmat_bf = bmat_all.astype(bf16)
    cmat_bf = cmat_all.astype(bf16)
    mk_all, mv_all = _memkv(mem_prompt, gmem_all, w_mem_kv.astype(bf16))

    xp = x_prompt
    xs = x_sample.reshape(R, D)
    outs = {n: [] for n in ("kp", "vp", "lfp", "hp", "z", "lfs", "hs")}
    for l in range(depth):
        final = l == depth - 1
        qa, k, ka, v, va, lf, sg, u, qm = _inproj(l, xp, gn_all, wm_all, wf_all, bf_all, tm_in)
        fox_o = _fox(qa, ka, va, tq, tk)
        y, hend = _s5scan(l, u, bmat_bf, cmat_bf, lamb_all, dsk_all, tj)
        hlast, cp = _s5fix(l, hend, lams_all, cmat_all)
        xp = _outproj(l, xp, fox_o, y, lpow_all, cp, qm, mk_all, mv_all, sg, wglu_bf, bglu_all, wout_bf, gfin,
                      tm, final)
        outs["kp"].append(k)
        outs["vp"].append(v)
        outs["lfp"].append(lf)
        outs["hp"].append(hlast.reshape(B, N_SSM_BLOCKS * STATE_W))

        z = _s_inproj(l, xs, gn_all, wcat_all, 256)
        fox_s, mem_s, lf_s = _s_attn(l, page_table, z, bf_all, ckt, cvt, lft, mkt, mvt, n_seq=2)
        xs, h_new = _s_out(l, xs, z, fox_s, mem_s, h0_all, bmat_all, cmat_all, lamb_all, dsk_all,
                           w_glu, bglu_all, w_out, gfin, final)
        outs["z"].append(z)
        outs["lfs"].append(lf_s)
        outs["hs"].append(h_new)

    st = lambda n: jnp.stack(outs[n])
    z_all = st("z")
    n_mem = mk_all.shape[2]
    return (xp, xs.reshape(R, 1, D),
            st("kp").reshape(depth, B, L, N_FOX_HEADS, HEAD_DIM),
            st("vp").reshape(depth, B, L, N_FOX_HEADS, HEAD_DIM),
            st("lfp"),
            _cols_to_state(st("hp")),
            mk_all.reshape(depth, B, n_mem, N_MEM_HEADS, HEAD_DIM),
            mv_all.reshape(depth, B, n_mem, N_MEM_HEADS, HEAD_DIM),
            z_all[:, :, OFF_K:OFF_K + W_FOX].reshape(depth, R, 1, N_FOX_HEADS, HEAD_DIM),
            z_all[:, :, OFF_V:OFF_V + W_FOX].reshape(depth, R, 1, N_FOX_HEADS, HEAD_DIM),
            st("lfs")[:, :, :N_FOX_HEADS].reshape(depth, R, 1, N_FOX_HEADS),
            _cols_to_state(st("hs")))
```

```python
import functools

import jax
import jax.numpy as jnp
from jax import lax
from jax.experimental import pallas as pl
from jax.experimental.pallas import tpu as pltpu

f32 = jnp.float32
bf16 = jnp.bfloat16

HEAD_DIM = 64
N_FOX_HEADS = 6
W_FOX = N_FOX_HEADS * HEAD_DIM
SSM_GROUP = 16
N_SSM_GROUPS = 24
W_SSM = N_SSM_GROUPS * SSM_GROUP
SSM_STATE = 64
N_MEM_HEADS = 4
W_MEM = N_MEM_HEADS * HEAD_DIM
W_MIX = W_FOX + W_SSM + W_MEM
RMS_EPS = 1e-6
QK_SCALE = HEAD_DIM ** -0.5

LANES = 128
SUBLANES = 8
GROUPS_PER_BLOCK = LANES // SSM_GROUP
N_SSM_BLOCKS = N_SSM_GROUPS // GROUPS_PER_BLOCK
HALF = GROUPS_PER_BLOCK * SSM_STATE
STATE_W = 2 * HALF
N_SEG = SUBLANES
PIECE_PITCH = SUBLANES

OFF_Q, OFF_K, OFF_V, OFF_U, OFF_QM, OFF_G = 0, 384, 768, 1152, 1536, 1792
W_MAIN = OFF_G + W_MIX
OFF_F = W_MAIN
W_CAT = 3072
IN_CUTS = (0, 384, 768, 1152, 1158, 1542, 1926, 2310, 2566, 2822)

VMEM_LIMIT = 56 * 1024 * 1024


def _cparams(sem):
    return pltpu.CompilerParams(dimension_semantics=sem, vmem_limit_bytes=VMEM_LIMIT)


def _dot(a, b):
    return jnp.dot(a, b, preferred_element_type=f32)


def _dot_nt(a, b):
    return lax.dot_general(a, b, (((1,), (1,)), ((), ())), preferred_element_type=f32)


def _split2(a):
    hi = a.astype(bf16)
    lo = (a - hi.astype(f32)).astype(bf16)
    return hi, lo


def _split3(a):
    hi = a.astype(bf16)
    r = a - hi.astype(f32)
    mid = r.astype(bf16)
    lo = (r - mid.astype(f32)).astype(bf16)
    return hi, mid, lo


def _dot3(a, b):
    ah, al = _split2(a)
    bh, bl = _split2(b)
    return _dot(ah, bh) + _dot(al, bh) + _dot(ah, bl)


def _rms(x, g):
    ms = jnp.mean(x * x, axis=-1, keepdims=True)
    return x * lax.rsqrt(ms + RMS_EPS) * g


def _log_sigmoid(x):
    return jnp.minimum(x, 0.0) - jnp.log1p(jnp.exp(-jnp.abs(x)))


def _silu(x):
    return x * jax.nn.sigmoid(x)


def _gelu(x):
    return 0.5 * x * (1.0 + lax.erf(x * (2.0 ** -0.5)))


def _tri(n):
    r = lax.broadcasted_iota(jnp.int32, (n, n), 0)
    c = lax.broadcasted_iota(jnp.int32, (n, n), 1)
    return jnp.where(c <= r, 1.0, 0.0).astype(bf16)


def _pack_pieces(x):
    packed = None
    for i, pc in enumerate(_split3(x)):
        pc = pc.astype(f32)
        if i:
            pc = pltpu.roll(pc, PIECE_PITCH * i, 1)
        packed = pc if packed is None else packed + pc
    return packed.astype(bf16)


def _unpack_pieces(y):
    return y + pltpu.roll(y, LANES - PIECE_PITCH, 1) + pltpu.roll(y, LANES - 2 * PIECE_PITCH, 1)


def _layer(l, shape):
    return pl.BlockSpec((None,) + tuple(shape), lambda *_: (l,) + (0,) * len(shape),
                        pipeline_mode=pl.Buffered(1))


def _head_block(x, h):
    blk = x[:, LANES * (h // 2):LANES * (h // 2 + 1)]
    return blk if h % 2 == 0 else pltpu.roll(blk, HEAD_DIM, 1)


def _inproj_kernel(x_ref, g_ref, wm_ref, wf_ref, bf_ref, *rest):
    qa_ref, kt_ref, ka_ref, vt_ref, va_ref, lf_ref, sg_ref, u_ref, qm_ref, carry_ref = rest[-10:]
    i = pl.program_id(1)

    @pl.when(i == 0)
    def _():
        carry_ref[...] = jnp.zeros_like(carry_ref)

    xn = _rms(x_ref[...], g_ref[...])
    xh = xn.astype(bf16)
    xl = (xn - xh.astype(f32)).astype(bf16)
    tm = xn.shape[0]
    lane = lax.broadcasted_iota(jnp.int32, (tm, LANES), 1)
    low = lane < HEAD_DIM

    ff = _dot(xh, wf_ref[...])
    f = ff[:, :LANES] + ff[:, LANES:] + _dot(xl, wf_ref[:, :LANES]) + bf_ref[...]
    lf = jnp.where(lane < N_FOX_HEADS, _log_sigmoid(f), 0.0)
    lf_ref[...] = lf[:, :N_FOX_HEADS]
    cs = _unpack_pieces(_dot(_tri(tm), _pack_pieces(lf)))
    cs = jnp.where(lane < N_FOX_HEADS, cs, 0.0) + carry_ref[0:1, :]
    carry_ref[...] = jnp.broadcast_to(cs[tm - 1:, :], carry_ref.shape)
    prow = lax.broadcasted_iota(jnp.int32, (LANES, N_FOX_HEADS * LANES), 0)
    pcol = lax.broadcasted_iota(jnp.int32, (LANES, N_FOX_HEADS * LANES), 1)
    place = (pcol == (prow % PIECE_PITCH) * LANES + HEAD_DIM + prow // PIECE_PITCH) & (prow % PIECE_PITCH < N_FOX_HEADS)
    caug = _dot(_pack_pieces(-cs), jnp.where(place, 1.0, 0.0).astype(bf16))
    ones3 = jnp.where((lane >= HEAD_DIM) & (lane < HEAD_DIM + 3), 1.0, 0.0)

    qk = _dot(xh, wm_ref[:, OFF_Q:OFF_V])
    q = qk[:, :W_FOX] * QK_SCALE
    k = qk[:, W_FOX:]
    kt_ref[...] = k.T
    vu = _dot(xh, wm_ref[:, OFF_V:OFF_QM])
    v = vu[:, :W_FOX]
    vt_ref[...] = v.T
    u_ref[...] = vu[:, W_FOX:]
    for h in range(N_FOX_HEADS):
        qa_ref[h] = jnp.where(low, _head_block(q, h), ones3).astype(bf16)
        ka_ref[h] = jnp.where(low, _head_block(k, h), caug[:, LANES * h:LANES * (h + 1)]).astype(bf16)
        va_ref[h] = jnp.where(low, _head_block(v, h), 1.0).astype(bf16)
    rest = _dot(xh, wm_ref[:, OFF_QM:W_MAIN])
    qm_ref[...] = (rest[:, :W_MEM] * QK_SCALE).astype(bf16)
    sg_ref[...] = _silu(rest[:, W_MEM:]).astype(bf16)


def _inproj(l, x, g_all, wm_all, wf_all, bf_all, kv_prev, tm):
    B, L, D = x.shape
    depth = wm_all.shape[0]
    row = lambda w: pl.BlockSpec((None, tm, w), lambda b, i: (b, i, 0))
    heads = pl.BlockSpec((None, N_FOX_HEADS, tm, LANES), lambda b, i: (b, 0, i, 0))
    slab = pl.BlockSpec((None, None, W_FOX, tm), lambda b, i: (l, b, 0, i))
    aug = jax.ShapeDtypeStruct((B, N_FOX_HEADS, L, LANES), bf16)
    kvt = jax.ShapeDtypeStruct((depth, B, W_FOX, L), f32)
    out_shape = (
        aug,
        kvt,
        aug,
        kvt,
        aug,
        jax.ShapeDtypeStruct((B, L, N_FOX_HEADS), f32),
        jax.ShapeDtypeStruct((B, L, W_MIX), bf16),
        jax.ShapeDtypeStruct((B, L, W_SSM), f32),
        jax.ShapeDtypeStruct((B, L, W_MEM), bf16),
    )
    out_specs = (heads, slab, heads, slab, heads, row(N_FOX_HEADS), row(W_MIX), row(W_SSM), row(W_MEM))
    in_specs = [row(D), _layer(l, (1, D)), _layer(l, wm_all.shape[1:]), _layer(l, wf_all.shape[1:]),
                _layer(l, (1, LANES))]
    n_in = len(in_specs)
    return pl.pallas_call(
        _inproj_kernel,
        grid=(B, L // tm),
        in_specs=in_specs + [pl.BlockSpec(memory_space=pl.ANY)] * len(kv_prev),
        out_specs=out_specs,
        out_shape=out_shape,
        input_output_aliases={n_in + n: out for n, out in zip(range(len(kv_prev)), (1, 3))},
        scratch_shapes=[pltpu.VMEM((SUBLANES, LANES), f32)],
        compiler_params=_cparams(("parallel", "arbitrary")),
        name="prompt_inproj",
    )(x, g_all, wm_all, wf_all, bf_all, *kv_prev)


def _memkv_kernel(m_ref, g_ref, w_ref, k_ref, v_ref):
    mn = _rms(m_ref[...], g_ref[...]).astype(bf16)
    kv = _dot(mn, w_ref[...])
    k_ref[...] = kv[:, :W_MEM]
    v_ref[...] = kv[:, W_MEM:]


def _memkv(mem, g_all, w_all):
    B, N, D = mem.shape
    depth = w_all.shape[0]
    out = pl.BlockSpec((None, None, N, W_MEM), lambda l, b: (l, b, 0, 0))
    return pl.pallas_call(
        _memkv_kernel,
        grid=(depth, B),
        in_specs=[pl.BlockSpec((None, N, D), lambda l, b: (b, 0, 0)),
                  pl.BlockSpec((None, 1, D), lambda l, b: (l, 0, 0)),
                  pl.BlockSpec((None,) + w_all.shape[1:], lambda l, b: (l, 0, 0))],
        out_specs=(out, out),
        out_shape=(jax.ShapeDtypeStruct((depth, B, N, W_MEM), f32),) * 2,
        compiler_params=_cparams(("parallel", "parallel")),
        name="prompt_memkv",
    )(mem, g_all, w_all)


def _fox_kernel(q_ref, k_ref, v_ref, o_ref, acc_ref, m_ref, *, tq, tk):
    i = pl.program_id(2)
    acc_ref[...] = jnp.zeros_like(acc_ref)
    m_ref[...] = jnp.full_like(m_ref, -jnp.inf)

    def step(kj, r0):
        k0 = pl.multiple_of(kj * tk, tk)
        rows = slice(0, tq) if r0 is None else slice(r0, tq)
        n = rows.stop - rows.start
        for e in range(2):
            s = _dot_nt(q_ref[e, rows, :], k_ref[e, pl.ds(k0, tk), :])
            if r0 is not None:
                row = r0 + lax.broadcasted_iota(jnp.int32, (n, tk), 0)
                col = r0 + lax.broadcasted_iota(jnp.int32, (n, tk), 1)
                s = jnp.where(col <= row, s, -jnp.inf)
            m_old = m_ref[e, rows, :]
            m_new = jnp.maximum(m_old, jnp.max(s, axis=1, keepdims=True))
            p = jnp.exp((s - jnp.tile(m_new, (1, tk // LANES))).astype(bf16))
            acc_ref[e, rows, :] = (jnp.exp(m_old - m_new) * acc_ref[e, rows, :]
                                   + _dot(p, v_ref[e, pl.ds(k0, tk), :]))
            m_ref[e, rows, :] = m_new

    n_diag = tq // tk

    def body(t, carry):
        for d in range(n_diag):
            step(t * n_diag + d, None)
        return carry

    lax.fori_loop(0, i, body, 0)
    for d in range(n_diag):
        step(i * n_diag + d, d * tk)
    lane = lax.broadcasted_iota(jnp.int32, (tq, LANES), 1)
    a0 = acc_ref[0]
    a1 = acc_ref[1]
    o_ref[...] = jnp.where(lane < HEAD_DIM, a0 / pltpu.roll(a0, HEAD_DIM, 1), pltpu.roll(a1, HEAD_DIM, 1) / a1)


def _fox(qa, ka, va, tq, tk):
    B, _, L, _ = qa.shape
    n_pairs = W_FOX // LANES
    return pl.pallas_call(
        functools.partial(_fox_kernel, tq=tq, tk=tk),
        grid=(B, n_pairs, L // tq),
        in_specs=[pl.BlockSpec((None, 2, tq, LANES), lambda b, j, i: (b, j, i, 0)),
                  pl.BlockSpec((None, 2, L, LANES), lambda b, j, i: (b, j, 0, 0)),
                  pl.BlockSpec((None, 2, L, LANES), lambda b, j, i: (b, j, 0, 0))],
        out_specs=pl.BlockSpec((None, tq, LANES), lambda b, j, i: (b, i, j)),
        out_shape=jax.ShapeDtypeStruct((B, L, W_FOX), f32),
        scratch_shapes=[pltpu.VMEM((2, tq, LANES), f32), pltpu.VMEM((2, tq, LANES), f32)],
        compiler_params=_cparams(("parallel", "parallel", "arbitrary")),
        name="prompt_fox",
    )(qa, ka, va)


def _s5prep_kernel(lr_ref, li_ref, ldt_ref, bre_ref, bim_ref, cre_ref, cim_ref,
                   lamb_ref, bmat_ref, cmat_ref, lpow_ref, lams_ref, pw_ref, *, seg_len):
    lr = lr_ref[...]
    li = li_ref[...]
    dt = jnp.exp(ldt_ref[...])
    ar = lr * dt
    ai = li * dt
    mag = jnp.exp(ar)
    lbr = mag * jnp.cos(ai)
    lbi = mag * jnp.sin(ai)
    lamb_ref[:, :HALF] = jnp.broadcast_to(lbr, (SUBLANES, HALF))
    lamb_ref[:, HALF:] = jnp.broadcast_to(lbi, (SUBLANES, HALF))
    den = lr * lr + li * li
    zr = ((lbr - 1.0) * lr + lbi * li) / den
    zi = (lbi * lr - (lbr - 1.0) * li) / den
    bre = bre_ref[...]
    bim = bim_ref[...]
    bmat_ref[:, :HALF] = zr * bre - zi * bim
    bmat_ref[:, HALF:] = zr * bim + zi * bre
    cmat_ref[:HALF, :] = cre_ref[...]
    cmat_ref[HALF:, :] = -cim_ref[...]
    jj = (lax.broadcasted_iota(jnp.int32, (SUBLANES, HALF), 0) + 1).astype(f32)
    magp = jnp.exp(ar * jj)
    pw_ref[:SUBLANES, :HALF] = magp * jnp.cos(ai * jj)
    pw_ref[:SUBLANES, HALF:] = magp * jnp.sin(ai * jj)
    n = SUBLANES
    while n < seg_len:
        sr = pw_ref[n - 1:n, :HALF]
        si = pw_ref[n - 1:n, HALF:]
        pr = pw_ref[:n, :HALF]
        pi = pw_ref[:n, HALF:]
        pw_ref[n:2 * n, :HALF] = pr * sr - pi * si
        pw_ref[n:2 * n, HALF:] = pr * si + pi * sr
        n *= 2
    lpow_ref[...] = pw_ref[...].astype(bf16)
    lams_ref[...] = pw_ref[seg_len - SUBLANES:, :]


def _s5prep(lam_re, lam_im, log_dt, b_re, b_im, c_re, c_im, seg_len):
    depth = lam_re.shape[0]
    assert seg_len >= SUBLANES and seg_len & (seg_len - 1) == 0
    nb, gb = N_SSM_BLOCKS, GROUPS_PER_BLOCK
    eye = jnp.eye(gb, dtype=f32)
    half = lambda a: a.reshape(depth, nb, 1, HALF)
    ldt = jnp.broadcast_to(log_dt[:, :, None], (depth, N_SSM_GROUPS, SSM_STATE))

    def embed_b(b):
        t = b.reshape(depth, nb, gb, SSM_STATE, SSM_GROUP).transpose(0, 1, 2, 4, 3)
        return (t[:, :, :, :, None, :] * eye[None, None, :, None, :, None]).reshape(depth, nb, LANES, HALF)

    def embed_c(c):
        t = c.reshape(depth, nb, gb, SSM_GROUP, SSM_STATE).transpose(0, 1, 2, 4, 3)
        return (t[:, :, :, :, None, :] * eye[None, None, :, None, :, None]).reshape(depth, nb, HALF, LANES)

    blk = lambda shape: pl.BlockSpec((None, None) + shape, lambda l, cb: (l, cb) + (0,) * len(shape))
    arr = lambda shape, dt: jax.ShapeDtypeStruct((depth, nb) + shape, dt)
    return pl.pallas_call(
        functools.partial(_s5prep_kernel, seg_len=seg_len),
        grid=(depth, nb),
        in_specs=[blk((1, HALF))] * 3 + [blk((LANES, HALF))] * 2 + [blk((HALF, LANES))] * 2,
        out_specs=(blk((SUBLANES, STATE_W)), blk((LANES, STATE_W)), blk((STATE_W, LANES)),
                   blk((seg_len, STATE_W)), blk((SUBLANES, STATE_W))),
        out_shape=(arr((SUBLANES, STATE_W), f32),
                   arr((LANES, STATE_W), f32),
                   arr((STATE_W, LANES), f32),
                   arr((seg_len, STATE_W), bf16),
                   arr((SUBLANES, STATE_W), f32)),
        scratch_shapes=[pltpu.VMEM((seg_len, STATE_W), f32)],
        compiler_params=_cparams(("parallel", "parallel")),
        name="s5_tables",
    )(half(lam_re), half(lam_im), half(ldt), embed_b(b_re), embed_b(b_im), embed_c(c_re), embed_c(c_im))


def _s5scan_kernel(u_ref, bmat_ref, cmat_ref, lamb_ref, dsk_ref, y_ref, hend_ref,
                   up_ref, hb_ref, hq_ref, hcar_ref, *, tj):
    j = pl.program_id(1)
    nb = N_SSM_BLOCKS

    @pl.when(j == 0)
    def _():
        hcar_ref[...] = jnp.zeros_like(hcar_ref)

    for cb in range(nb):
        for s in range(N_SEG):
            up_ref[cb, pl.ds(s, tj, stride=N_SEG), :] = u_ref[s, :, LANES * cb:LANES * (cb + 1)]
        hb_ref[cb] = _dot(up_ref[cb].astype(bf16), bmat_ref[cb])

    def rec(t2, h):
        r0 = pl.multiple_of(t2 * 2 * N_SEG, 2 * N_SEG)
        new = []
        for cb in range(nb):
            hr, hi = h[2 * cb], h[2 * cb + 1]
            lr = lamb_ref[cb, :, :HALF]
            li = lamb_ref[cb, :, HALF:]
            rs, is_ = [], []
            for d in range(2):
                bu = hb_ref[cb, pl.ds(r0 + d * N_SEG, N_SEG), :]
                hr, hi = lr * hr - li * hi + bu[:, :HALF], lr * hi + li * hr + bu[:, HALF:]
                rs.append(hr)
                is_.append(hi)
            hq_ref[cb, pl.ds(r0, 2 * N_SEG), :HALF] = jnp.concatenate(rs, axis=0).astype(bf16)
            hq_ref[cb, pl.ds(r0, 2 * N_SEG), HALF:] = jnp.concatenate(is_, axis=0).astype(bf16)
            new += [hr, hi]
        return tuple(new)

    h0 = tuple(hcar_ref[cb, :, HALF * part:HALF * (part + 1)] for cb in range(nb) for part in range(2))
    h = lax.fori_loop(0, tj // 2, rec, h0, unroll=True)
    for cb in range(nb):
        hcar_ref[cb, :, :HALF] = h[2 * cb]
        hcar_ref[cb, :, HALF:] = h[2 * cb + 1]
        up_ref[cb] = _dot(hq_ref[cb], cmat_ref[cb]) + dsk_ref[cb] * up_ref[cb]
        for s in range(N_SEG):
            y_ref[s, :, LANES * cb:LANES * (cb + 1)] = up_ref[cb, pl.ds(s, tj, stride=N_SEG), :]

    @pl.when(j == pl.num_programs(1) - 1)
    def _():
        hend_ref[...] = hcar_ref[...]


def _s5scan(l, u, bmat_all, cmat_all, lamb_all, dsk_all, tj):
    B, L, _ = u.shape
    seg_len = L // N_SEG
    u4 = u.reshape(B, N_SEG, seg_len, W_SSM)
    nb = N_SSM_BLOCKS
    tile = pl.BlockSpec((None, N_SEG, tj, W_SSM), lambda b, j: (b, 0, j, 0))
    y4, hend = pl.pallas_call(
        functools.partial(_s5scan_kernel, tj=tj),
        grid=(B, seg_len // tj),
        in_specs=[tile, _layer(l, bmat_all.shape[1:]), _layer(l, cmat_all.shape[1:]),
                  _layer(l, lamb_all.shape[1:]), _layer(l, dsk_all.shape[1:])],
        out_specs=(tile, pl.BlockSpec((None, nb, N_SEG, STATE_W), lambda b, j: (b, 0, 0, 0))),
        out_shape=(jax.ShapeDtypeStruct((B, N_SEG, seg_len, W_SSM), f32),
                   jax.ShapeDtypeStruct((B, nb, N_SEG, STATE_W), f32)),
        scratch_shapes=[pltpu.VMEM((nb, N_SEG * tj, LANES), f32), pltpu.VMEM((nb, N_SEG * tj, STATE_W), f32),
                        pltpu.VMEM((nb, N_SEG * tj, STATE_W), bf16), pltpu.VMEM((nb, N_SEG, STATE_W), f32)],
        compiler_params=_cparams(("parallel", "arbitrary")),
        name="prompt_s5_scan",
    )(u4, bmat_all, cmat_all, lamb_all, dsk_all)
    return y4.reshape(B, L, W_SSM), hend


def _s5fix_kernel(hend_ref, lams_ref, cmat_ref, hlast_ref, cp_ref, hs_ref):
    he = hend_ref[...]
    lsr = lams_ref[SUBLANES - 1:, :HALF]
    lsi = lams_ref[SUBLANES - 1:, HALF:]
    hs_ref[...] = jnp.zeros_like(hs_ref)
    hr = jnp.zeros((1, HALF), f32)
    hi = jnp.zeros((1, HALF), f32)
    for s in range(N_SEG):
        hs_ref[s:s + 1, :HALF] = hr
        hs_ref[s:s + 1, HALF:] = hi
        nr = lsr * hr - lsi * hi + he[s:s + 1, :HALF]
        ni = lsr * hi + lsi * hr + he[s:s + 1, HALF:]
        hr, hi = nr, ni
    hlast_ref[:, :HALF] = hr
    hlast_ref[:, HALF:] = hi
    hst_r = hs_ref[:, :HALF].T
    hst_i = hs_ref[:, HALF:].T
    cre = cmat_ref[:HALF, :]
    mci = cmat_ref[HALF:, :]
    for s in range(N_SEG):
        col_r = hst_r[:, s:s + 1]
        col_i = hst_i[:, s:s + 1]
        cp_ref[s, :HALF, :] = (cre * col_r + mci * col_i).astype(bf16)
        cp_ref[s, HALF:, :] = (mci * col_r - cre * col_i).astype(bf16)


def _s5fix(l, hend, lams_all, cmat_all):
    B, nb = hend.shape[:2]
    return pl.pallas_call(
        _s5fix_kernel,
        grid=(B, nb),
        in_specs=[pl.BlockSpec((None, None, N_SEG, STATE_W), lambda b, cb: (b, cb, 0, 0)),
                  pl.BlockSpec((None, None, SUBLANES, STATE_W), lambda b, cb: (l, cb, 0, 0)),
                  pl.BlockSpec((None, None, STATE_W, LANES), lambda b, cb: (l, cb, 0, 0))],
        out_specs=(pl.BlockSpec((None, None, 1, STATE_W), lambda b, cb: (b, cb, 0, 0)),
                   pl.BlockSpec((None, N_SEG, None, STATE_W, LANES), lambda b, cb: (b, 0, cb, 0, 0))),
        out_shape=(jax.ShapeDtypeStruct((B, nb, 1, STATE_W), f32),
                   jax.ShapeDtypeStruct((B, N_SEG, nb, STATE_W, LANES), bf16)),
        scratch_shapes=[pltpu.VMEM((LANES, STATE_W), f32)],
        compiler_params=_cparams(("parallel", "parallel")),
        name="prompt_s5_fix",
    )(hend, lams_all, cmat_all)


def _pair_attend(q, k, v):
    lane = lax.broadcasted_iota(jnp.int32, q.shape, 1)
    zero = jnp.zeros_like(q)
    outs = []
    for e in range(2):
        qe = jnp.where((lane < HEAD_DIM) == (e == 0), q, zero)
        s = _dot_nt(qe, k)
        p = jnp.exp(s - jnp.max(s, axis=1, keepdims=True))
        outs.append(_dot(p.astype(bf16), v) / jnp.sum(p, axis=1, keepdims=True))
    return jnp.where(lane < HEAD_DIM, outs[0], outs[1])


def _outproj_kernel(x_ref, fox_ref, y_ref, lp_ref, cp_ref, qm_ref, mk_ref, mv_ref, sg_ref,
                    wglu_ref, bglu_ref, wout_ref, gfin_ref, o_ref, *, final):
    corr = [_dot(lp_ref[cb], cp_ref[cb]) for cb in range(N_SSM_BLOCKS)]
    ssm_y = y_ref[...] + jnp.concatenate(corr, axis=1)
    s = _gelu(ssm_y)
    s = s * jax.nn.sigmoid(_dot(s.astype(bf16), wglu_ref[...]) + bglu_ref[...])
    qm = qm_ref[...]
    mk = mk_ref[...].astype(bf16)
    mv = mv_ref[...].astype(bf16)
    mem = [_pair_attend(qm[:, LANES * jp:LANES * (jp + 1)], mk[:, LANES * jp:LANES * (jp + 1)],
                        mv[:, LANES * jp:LANES * (jp + 1)]) for jp in range(W_MEM // LANES)]
    sg = sg_ref[...].astype(f32)
    mix = jnp.concatenate([fox_ref[...] * sg[:, :W_FOX],
                           s * sg[:, W_FOX:W_FOX + W_SSM],
                           jnp.concatenate(mem, axis=1) * sg[:, W_FOX + W_SSM:]], axis=1)
    xo = x_ref[...] + _dot(mix.astype(bf16), wout_ref[...])
    if final:
        xo = _rms(xo, gfin_ref[...])
    o_ref[...] = xo


def _outproj(l, x, fox_o, y, lpow_all, cp, qm, mk_all, mv_all, sg, wglu_all, bglu_all, wout_all, gfin, tm, final):
    B, L, D = x.shape
    seg_len = L // N_SEG
    tiles_per_seg = seg_len // tm
    row = lambda w: pl.BlockSpec((None, tm, w), lambda b, i: (b, i, 0))
    nb = N_SSM_BLOCKS
    mem_spec = pl.BlockSpec((None, None) + mk_all.shape[2:], lambda b, i: (l, b, 0, 0))
    return pl.pallas_call(
        functools.partial(_outproj_kernel, final=final),
        grid=(B, L // tm),
        in_specs=[row(D), row(W_FOX), row(W_SSM),
                  pl.BlockSpec((None, nb, tm, STATE_W), lambda b, i: (l, 0, i % tiles_per_seg, 0)),
                  pl.BlockSpec((None, None, nb, STATE_W, LANES), lambda b, i: (b, i // tiles_per_seg, 0, 0, 0)),
                  row(W_MEM), mem_spec, mem_spec, row(D),
                  _layer(l, wglu_all.shape[1:]), _layer(l, (1, W_SSM)), _layer(l, wout_all.shape[1:]),
                  pl.BlockSpec((1, D), lambda b, i: (0, 0))],
        out_specs=row(D),
        out_shape=jax.ShapeDtypeStruct((B, L, D), f32),
        compiler_params=_cparams(("parallel", "parallel")),
        name="prompt_outproj",
    )(x, fox_o, y, lpow_all, cp, qm, mk_all, mv_all, sg, wglu_all, bglu_all, wout_all, gfin)


def _s_inproj_kernel(x_ref, g_ref, w_ref, z_ref):
    xn = _rms(x_ref[...], g_ref[...])
    z_ref[...] = _dot3(xn, w_ref[...])


def _s_inproj(l, xs, g_all, wcat_all, tn):
    R, D = xs.shape
    return pl.pallas_call(
        _s_inproj_kernel,
        grid=(W_CAT // tn,),
        in_specs=[pl.BlockSpec((R, D), lambda n: (0, 0)), _layer(l, (1, D)),
                  pl.BlockSpec((None, D, tn), lambda n: (l, 0, n))],
        out_specs=pl.BlockSpec((R, tn), lambda n: (0, n)),
        out_shape=jax.ShapeDtypeStruct((R, W_CAT), f32),
        compiler_params=_cparams(("parallel",)),
        name="sample_inproj",
    )(xs, g_all, wcat_all)


def _full_max(x):
    return jnp.max(jnp.max(x, axis=0, keepdims=True), axis=1, keepdims=True)


def _full_sum(x):
    return jnp.sum(jnp.sum(x, axis=0, keepdims=True), axis=1, keepdims=True)


def _lane_sums(acc):
    ones = jnp.ones((SUBLANES, LANES), bf16)
    hi, lo = _split2(acc)
    return (_dot_nt(ones, hi) + _dot_nt(ones, lo))[0:1, :]


def _col_bcast(row):
    return jnp.broadcast_to(row, (LANES, row.shape[1])).T


def _s_attn_kernel(pt_ref, z_ref, bf_ref, lfc_ref, *refs, n_pages, n_seq):
    per_seq = 2 * n_pages + 2
    ins = refs[:n_seq * per_seq]
    fox_ref, mem_ref, lf_ref, s_scr, lfs_scr, p_scr, acc_scr, accm_scr = refs[n_seq * per_seq:]
    for i in range(n_seq):
        mine = ins[i * per_seq:(i + 1) * per_seq]
        _decode_one(pt_ref, pl.program_id(0) * n_seq + i, z_ref.at[i], bf_ref, lfc_ref,
                    mine[:n_pages], mine[n_pages:2 * n_pages], mine[2 * n_pages], mine[2 * n_pages + 1],
                    fox_ref.at[i], mem_ref.at[i], lf_ref.at[i],
                    s_scr.at[i], lfs_scr.at[i], p_scr.at[i], acc_scr.at[i], accm_scr.at[i])


def _decode_one(pt_ref, r, z_ref, bf_ref, lfc_ref, ck, cv, mk_ref, mv_ref, fox_ref, mem_ref, lf_ref,
                s_scr, lfs_scr, p_scr, acc_scr, accm_scr):
    n_pages = len(ck)
    z = z_ref[...]
    q = z[:, OFF_Q:OFF_Q + W_FOX] * QK_SCALE
    k_new = z[:, OFF_K:OFF_K + W_FOX]
    v_new = z[:, OFF_V:OFF_V + W_FOX]
    lf_new = _log_sigmoid(z[:, OFF_F:OFF_F + LANES] + bf_ref[...])
    lf_ref[...] = lf_new

    qt = _col_bcast(q)
    for h in range(N_FOX_HEADS):
        qh = qt[HEAD_DIM * h:HEAD_DIM * (h + 1), :]
        for p in range(n_pages):
            row = h * n_pages + p
            s_scr[row:row + 1, :] = jnp.sum(ck[p][h] * qh, axis=0, keepdims=True)
            lfs_scr[row:row + 1, :] = lfc_ref[h, pl.ds(pt_ref[r, p], 1), :]

    n_rows = N_FOX_HEADS * n_pages
    ur = lax.broadcasted_iota(jnp.int32, (LANES, LANES), 0)
    uc = lax.broadcasted_iota(jnp.int32, (LANES, LANES), 1)
    upper = jnp.where(ur <= uc, 1.0, 0.0).astype(bf16)
    cs = sum(_dot(pc, upper) for pc in _split3(lfs_scr[...]))
    tot = jnp.broadcast_to(cs[:, LANES - 1:], (n_rows, LANES))
    mr = lax.broadcasted_iota(jnp.int32, (n_rows, n_rows), 0)
    mc = lax.broadcasted_iota(jnp.int32, (n_rows, n_rows), 1)
    before = jnp.where((mc < mr) & (mc // n_pages == mr // n_pages), 1.0, 0.0).astype(bf16)
    c = cs + sum(_dot(before, pc) for pc in _split3(tot))
    s_all = s_scr[...] - c

    qk_new = q * k_new
    lane_head = lax.broadcasted_iota(jnp.int32, (1, W_FOX), 1) // HEAD_DIM
    pn_exp = jnp.zeros((1, W_FOX), f32)
    l_exp = jnp.zeros((1, W_FOX), f32)
    for h in range(N_FOX_HEADS):
        rows = slice(h * n_pages, (h + 1) * n_pages)
        s_h = s_all[rows, :]
        c_new = c[(h + 1) * n_pages - 1:(h + 1) * n_pages, LANES - 1:] + lf_new[:, h:h + 1]
        s_new = jnp.sum(qk_new[:, HEAD_DIM * h:HEAD_DIM * (h + 1)], axis=1, keepdims=True) - c_new
        m = jnp.maximum(_full_max(s_h), s_new)
        p_h = jnp.exp(s_h - m)
        pn = jnp.exp(s_new - m)
        p_scr[rows, :] = p_h
        pn_exp = jnp.where(lane_head == h, pn, pn_exp)
        l_exp = jnp.where(lane_head == h, _full_sum(p_h) + pn, l_exp)
        acc = jnp.zeros((HEAD_DIM, LANES), f32)
        for p in range(n_pages):
            acc = acc + cv[p][h] * p_scr[h * n_pages + p:h * n_pages + p + 1, :]
        acc_scr[HEAD_DIM * h:HEAD_DIM * (h + 1), :] = acc
    fox_ref[...] = (_lane_sums(acc_scr[...]) + pn_exp * v_new) / l_exp

    qm = z[:, OFF_QM:OFF_QM + W_MEM] * QK_SCALE
    qmt = _col_bcast(qm)
    n_halves = mk_ref.shape[2] // LANES
    mem_head = lax.broadcasted_iota(jnp.int32, (1, W_MEM), 1) // HEAD_DIM
    lm_exp = jnp.zeros((1, W_MEM), f32)
    for h in range(N_MEM_HEADS):
        qh = qmt[HEAD_DIM * h:HEAD_DIM * (h + 1), :]
        ss = [jnp.sum(mk_ref[h, :, LANES * t:LANES * (t + 1)] * qh, axis=0, keepdims=True) for t in range(n_halves)]
        m = functools.reduce(jnp.maximum, [jnp.max(s, axis=1, keepdims=True) for s in ss])
        ps = [jnp.exp(s - m) for s in ss]
        lm = sum(jnp.sum(p, axis=1, keepdims=True) for p in ps)
        lm_exp = jnp.where(mem_head == h, lm, lm_exp)
        accm_scr[HEAD_DIM * h:HEAD_DIM * (h + 1), :] = sum(
            mv_ref[h, :, LANES * t:LANES * (t + 1)] * ps[t] for t in range(n_halves))
    mem_ref[...] = _lane_sums(accm_scr[...]) / lm_exp


def _s_attn(l, page_table, z, bf_all, ckt, cvt, lft, mkt, mvt, n_seq):
    R, n_pages = page_table.shape
    n_pool, page = ckt.shape[1], ckt.shape[4]
    n_mem = mkt.shape[4]
    assert page == LANES and n_mem % LANES == 0
    n_seq = _tile(R, n_seq)
    z3 = z.reshape(R, 1, W_CAT)
    in_specs, operands = [], []
    for i in range(n_seq):
        paged = [pl.BlockSpec((None, None, N_FOX_HEADS, HEAD_DIM, page),
                              lambda g, pt, i=i, p=p: (l, pt[g * n_seq + i, p], 0, 0, 0)) for p in range(n_pages)]
        mem_spec = pl.BlockSpec((None, None, N_MEM_HEADS, HEAD_DIM, n_mem),
                                lambda g, pt, i=i: (l, g * n_seq + i, 0, 0, 0))
        in_specs += paged + paged + [mem_spec, mem_spec]
        operands += [ckt] * n_pages + [cvt] * n_pages + [mkt, mvt]
    per_row = lambda w: pl.BlockSpec((n_seq, 1, w), lambda g, pt: (g, 0, 0))
    n_rows = N_FOX_HEADS * n_pages
    scr = lambda rows: pltpu.VMEM((n_seq, rows, LANES), f32)
    fox, mem, lf = pl.pallas_call(
        functools.partial(_s_attn_kernel, n_pages=n_pages, n_seq=n_seq),
        grid_spec=pltpu.PrefetchScalarGridSpec(
            num_scalar_prefetch=1,
            grid=(R // n_seq,),
            in_specs=[per_row(W_CAT), _layer(l, (1, LANES)), _layer(l, (N_FOX_HEADS, n_pool, page))] + in_specs,
            out_specs=(per_row(W_FOX), per_row(W_MEM), per_row(LANES)),
            scratch_shapes=[scr(n_rows), scr(n_rows), scr(n_rows), scr(W_FOX), scr(W_MEM)]),
        out_shape=(jax.ShapeDtypeStruct((R, 1, W_FOX), f32), jax.ShapeDtypeStruct((R, 1, W_MEM), f32),
                   jax.ShapeDtypeStruct((R, 1, LANES), f32)),
        compiler_params=_cparams(("parallel",)),
        name="sample_attn",
    )(page_table, z3, bf_all, lft, *operands)
    return fox.reshape(R, W_FOX), mem.reshape(R, W_MEM), lf.reshape(R, LANES)


def _s_out_kernel(x_ref, z_ref, fox_ref, mem_ref, h0_ref, bmat_ref, cmat_ref, lamb_ref, dsk_ref,
                  wglu_ref, bglu_ref, wout_ref, gfin_ref, o_ref, h_ref, *, final):
    z = z_ref[...]
    u = z[:, OFF_U:OFF_U + W_SSM]
    ys = []
    for cb in range(N_SSM_BLOCKS):
        ucb = u[:, LANES * cb:LANES * (cb + 1)]
        bu = _dot3(ucb, bmat_ref[cb])
        lr = lamb_ref[cb, 0:1, :HALF]
        li = lamb_ref[cb, 0:1, HALF:]
        h0r = h0_ref[:, STATE_W * cb:STATE_W * cb + HALF]
        h0i = h0_ref[:, STATE_W * cb + HALF:STATE_W * (cb + 1)]
        hr = lr * h0r - li * h0i + bu[:, :HALF]
        hi = lr * h0i + li * h0r + bu[:, HALF:]
        h_ref[:, STATE_W * cb:STATE_W * cb + HALF] = hr
        h_ref[:, STATE_W * cb + HALF:STATE_W * (cb + 1)] = hi
        ys.append(_dot3(jnp.concatenate([hr, hi], axis=1), cmat_ref[cb]) + dsk_ref[cb] * ucb)
    s = _gelu(jnp.concatenate(ys, axis=1))
    s = s * jax.nn.sigmoid(_dot3(s, wglu_ref[...]) + bglu_ref[...])
    sg = _silu(z[:, OFF_G:W_MAIN])
    mix = jnp.concatenate([fox_ref[...] * sg[:, :W_FOX],
                           s * sg[:, W_FOX:W_FOX + W_SSM],
                           mem_ref[...] * sg[:, W_FOX + W_SSM:]], axis=1)
    xo = x_ref[...] + _dot3(mix, wout_ref[...])
    if final:
        xo = _rms(xo, gfin_ref[...])
    o_ref[...] = xo


def _s_out(l, xs, z, fox, mem, h0_all, bmat_all, cmat_all, lamb_all, dsk_all, wglu_all, bglu_all, wout_all,
           gfin, final):
    R, D = xs.shape
    whole = lambda a: pl.BlockSpec(a.shape, lambda i: (0,) * a.ndim)
    hspec = pl.BlockSpec((R, h0_all.shape[2]), lambda i: (0, 0))
    return pl.pallas_call(
        functools.partial(_s_out_kernel, final=final),
        grid=(1,),
        in_specs=[whole(xs), whole(z), whole(fox), whole(mem), _layer(l, h0_all.shape[1:]),
                  _layer(l, bmat_all.shape[1:]), _layer(l, cmat_all.shape[1:]), _layer(l, lamb_all.shape[1:]),
                  _layer(l, dsk_all.shape[1:]), _layer(l, wglu_all.shape[1:]), _layer(l, (1, W_SSM)),
                  _layer(l, wout_all.shape[1:]), whole(gfin)],
        out_specs=(whole(xs), hspec),
        out_shape=(jax.ShapeDtypeStruct((R, D), f32), jax.ShapeDtypeStruct(h0_all.shape[1:], f32)),
        compiler_params=_cparams(("arbitrary",)),
        name="sample_out",
    )(xs, z, fox, mem, h0_all, bmat_all, cmat_all, lamb_all, dsk_all, wglu_all, bglu_all, wout_all, gfin)


def _state_to_cols(h):
    lead = h.shape[:-3]
    n = len(lead)
    t = h.reshape(lead + (N_SSM_BLOCKS, GROUPS_PER_BLOCK, SSM_STATE, 2))
    t = t.transpose(tuple(range(n)) + (n, n + 3, n + 1, n + 2))
    return t.reshape(lead + (N_SSM_BLOCKS * STATE_W,))


def _cols_to_state(c):
    lead = c.shape[:-1]
    n = len(lead)
    t = c.reshape(lead + (N_SSM_BLOCKS, 2, GROUPS_PER_BLOCK, SSM_STATE))
    t = t.transpose(tuple(range(n)) + (n, n + 2, n + 3, n + 1))
    return t.reshape(lead + (N_SSM_GROUPS, SSM_STATE, 2))


def _tile(n, pref):
    t = min(n, pref)
    assert n % t == 0
    return t


def kernel(x_prompt, x_sample, mem_prompt, cache_k, cache_v, cache_logf, state_ssm, cache_mem_k, cache_mem_v,
           page_table, g_norm, w_in, b_f, lam_re, lam_im, log_dt, b_re, b_im, c_re, c_im, d_skip, w_glu, b_glu,
           g_mem, w_mem_kv, w_out, g_final):
    depth = w_in.shape[0]
    B, L, D = x_prompt.shape
    R = x_sample.shape[0]
    assert x_sample.shape[1] == 1 and L % N_SEG == 0
    seg_len = L // N_SEG
    tm = _tile(seg_len, 512)
    tm_in = _tile(L, 512)
    tq = _tile(L, 1024)
    tk = _tile(tq, 512)
    tj = _tile(seg_len, 128)

    seg = dict(zip(("q", "k", "v", "f", "ga", "u", "gs", "qm", "gm"),
                   (w_in[:, :, a:b] for a, b in zip(IN_CUTS[:-1], IN_CUTS[1:]))))
    wcat_all = jnp.concatenate(
        [seg[n] for n in ("q", "k", "v", "u", "qm", "ga", "gs", "gm", "f")]
        + [jnp.zeros((depth, D, W_CAT - W_MAIN - N_FOX_HEADS), f32)], axis=2)
    wm_all = wcat_all[:, :, :W_MAIN].astype(bf16)
    wf = wcat_all[:, :, OFF_F:OFF_F + LANES]
    wfh = wf.astype(bf16)
    wf_all = jnp.concatenate([wfh, (wf - wfh.astype(f32)).astype(bf16)], axis=2)
    bf_all = jnp.pad(b_f, ((0, 0), (0, LANES - N_FOX_HEADS))).reshape(depth, 1, LANES)
    gn_all = g_norm.reshape(depth, 1, D)
    gmem_all = g_mem.reshape(depth, 1, D)
    bglu_all = b_glu.reshape(depth, 1, W_SSM)
    gfin = g_final.reshape(1, D)
    dsk_all = d_skip.reshape(depth, N_SSM_BLOCKS, 1, LANES)
    wglu_bf = w_glu.astype(bf16)
    wout_bf = w_out.astype(bf16)
    ckt = jnp.transpose(cache_k, (0, 1, 3, 4, 2))
    cvt = jnp.transpose(cache_v, (0, 1, 3, 4, 2))
    lft = jnp.transpose(cache_logf, (0, 3, 1, 2))
    mkt = jnp.transpose(cache_mem_k, (0, 1, 3, 4, 2))
    mvt = jnp.transpose(cache_mem_v, (0, 1, 3, 4, 2))
    h0_all = _state_to_cols(state_ssm)

    lamb_all, bmat_all, cmat_all, lpow_all, lams_all = _s5prep(lam_re, lam_im, log_dt, b_re, b_im, c_re, c_im, seg_len)
    bmat_bf = bmat_all.astype(bf16)
    cmat_bf = cmat_all.astype(bf16)
    mk_all, mv_all = _memkv(mem_prompt, gmem_all, w_mem_kv.astype(bf16))

    xp = x_prompt
    xs = x_sample.reshape(R, D)
    outs = {n: [] for n in ("lfp", "hp", "z", "lfs", "hs")}
    kv_prev = ()
    for l in range(depth):
        final = l == depth - 1
        qa, kt_all, ka, vt_all, va, lf, sg, u, qm = _inproj(l, xp, gn_all, wm_all, wf_all, bf_all, kv_prev, tm_in)
        kv_prev = (kt_all, vt_all)
        fox_o = _fox(qa, ka, va, tq, tk)
        y, hend = _s5scan(l, u, bmat_bf, cmat_bf, lamb_all, dsk_all, tj)
        hlast, cp = _s5fix(l, hend, lams_all, cmat_all)
        xp = _outproj(l, xp, fox_o, y, lpow_all, cp, qm, mk_all, mv_all, sg, wglu_bf, bglu_all, wout_bf, gfin,
                      tm, final)
        outs["lfp"].append(lf)
        outs["hp"].append(hlast.reshape(B, N_SSM_BLOCKS * STATE_W))

        z = _s_inproj(l, xs, gn_all, wcat_all, 256)
        fox_s, mem_s, lf_s = _s_attn(l, page_table, z, bf_all, ckt, cvt, lft, mkt, mvt, n_seq=2)
        xs, h_new = _s_out(l, xs, z, fox_s, mem_s, h0_all, bmat_all, cmat_all, lamb_all, dsk_all,
                           w_glu, bglu_all, w_out, gfin, final)
        outs["z"].append(z)
        outs["lfs"].append(lf_s)
        outs["hs"].append(h_new)

    st = lambda n: jnp.stack(outs[n])
    z_all = st("z")
    n_mem = mk_all.shape[2]
    return (xp, xs.reshape(R, 1, D),
            kt_all.reshape(depth, B, N_FOX_HEADS, HEAD_DIM, L).transpose(0, 1, 4, 2, 3),
            vt_all.reshape(depth, B, N_FOX_HEADS, HEAD_DIM, L).transpose(0, 1, 4, 2, 3),
            st("lfp"),
            _cols_to_state(st("hp")),
            mk_all.reshape(depth, B, n_mem, N_MEM_HEADS, HEAD_DIM),
            mv_all.reshape(depth, B, n_mem, N_MEM_HEADS, HEAD_DIM),
            z_all[:, :, OFF_K:OFF_K + W_FOX].reshape(depth, R, 1, N_FOX_HEADS, HEAD_DIM),
            z_all[:, :, OFF_V:OFF_V + W_FOX].reshape(depth, R, 1, N_FOX_HEADS, HEAD_DIM),
            st("lfs")[:, :, :N_FOX_HEADS].reshape(depth, R, 1, N_FOX_HEADS),
            _cols_to_state(st("hs")))
```

```python
import functools

import jax
import jax.numpy as jnp
from jax import lax
from jax.experimental import pallas as pl
from jax.experimental.pallas import tpu as pltpu

f32 = jnp.float32
bf16 = jnp.bfloat16

HEAD_DIM = 64
N_FOX_HEADS = 6
W_FOX = N_FOX_HEADS * HEAD_DIM
SSM_GROUP = 16
N_SSM_GROUPS = 24
W_SSM = N_SSM_GROUPS * SSM_GROUP
SSM_STATE = 64
N_MEM_HEADS = 4
W_MEM = N_MEM_HEADS * HEAD_DIM
W_MIX = W_FOX + W_SSM + W_MEM
RMS_EPS = 1e-6
QK_SCALE = HEAD_DIM ** -0.5

LANES = 128
SUBLANES = 8
GROUPS_PER_BLOCK = LANES // SSM_GROUP
N_SSM_BLOCKS = N_SSM_GROUPS // GROUPS_PER_BLOCK
HALF = GROUPS_PER_BLOCK * SSM_STATE
STATE_W = 2 * HALF
N_SEG = SUBLANES
PIECE_PITCH = SUBLANES

OFF_Q, OFF_K, OFF_V, OFF_U, OFF_QM, OFF_G = 0, 384, 768, 1152, 1536, 1792
W_MAIN = OFF_G + W_MIX
OFF_F = W_MAIN
W_CAT = 3072
IN_CUTS = (0, 384, 768, 1152, 1158, 1542, 1926, 2310, 2566, 2822)

VMEM_LIMIT = 56 * 1024 * 1024

SKIP_GAP = 110.0
NORM_SLACK = 1.02
KST_KN, KST_QN, KST_CEND, KST_ROWS = 0, SUBLANES, 2 * SUBLANES, 3 * SUBLANES


def _cparams(sem):
    return pltpu.CompilerParams(dimension_semantics=sem, vmem_limit_bytes=VMEM_LIMIT)


def _dot(a, b):
    return jnp.dot(a, b, preferred_element_type=f32)


def _dot_nt(a, b):
    return lax.dot_general(a, b, (((1,), (1,)), ((), ())), preferred_element_type=f32)


def _split2(a):
    hi = a.astype(bf16)
    lo = (a - hi.astype(f32)).astype(bf16)
    return hi, lo


def _split3(a):
    hi = a.astype(bf16)
    r = a - hi.astype(f32)
    mid = r.astype(bf16)
    lo = (r - mid.astype(f32)).astype(bf16)
    return hi, mid, lo


def _dot3(a, b):
    ah, al = _split2(a)
    bh, bl = _split2(b)
    return _dot(ah, bh) + _dot(al, bh) + _dot(ah, bl)


def _rms(x, g):
    ms = jnp.mean(x * x, axis=-1, keepdims=True)
    return x * lax.rsqrt(ms + RMS_EPS) * g


def _log_sigmoid(x):
    return jnp.minimum(x, 0.0) - jnp.log1p(jnp.exp(-jnp.abs(x)))


def _silu(x):
    return x * jax.nn.sigmoid(x)


def _gelu(x):
    return 0.5 * x * (1.0 + lax.erf(x * (2.0 ** -0.5)))


def _tri(n):
    r = lax.broadcasted_iota(jnp.int32, (n, n), 0)
    c = lax.broadcasted_iota(jnp.int32, (n, n), 1)
    return jnp.where(c <= r, 1.0, 0.0).astype(bf16)


def _pack_pieces(x):
    packed = None
    for i, pc in enumerate(_split3(x)):
        pc = pc.astype(f32)
        if i:
            pc = pltpu.roll(pc, PIECE_PITCH * i, 1)
        packed = pc if packed is None else packed + pc
    return packed.astype(bf16)


def _unpack_pieces(y):
    return y + pltpu.roll(y, LANES - PIECE_PITCH, 1) + pltpu.roll(y, LANES - 2 * PIECE_PITCH, 1)


def _layer(l, shape):
    return pl.BlockSpec((None,) + tuple(shape), lambda *_: (l,) + (0,) * len(shape),
                        pipeline_mode=pl.Buffered(1))


def _head_block(x, h):
    blk = x[:, LANES * (h // 2):LANES * (h // 2 + 1)]
    return blk if h % 2 == 0 else pltpu.roll(blk, HEAD_DIM, 1)


def _inproj_kernel(x_ref, g_ref, wm_ref, wf_ref, bf_ref, *rest):
    qa_ref, kt_ref, ka_ref, vt_ref, va_ref, kst_ref, lf_ref, sg_ref, u_ref, qm_ref, carry_ref = rest[-11:]
    i = pl.program_id(1)

    @pl.when(i == 0)
    def _():
        carry_ref[...] = jnp.zeros_like(carry_ref)

    xn = _rms(x_ref[...], g_ref[...])
    xh = xn.astype(bf16)
    xl = (xn - xh.astype(f32)).astype(bf16)
    tm = xn.shape[0]
    lane = lax.broadcasted_iota(jnp.int32, (tm, LANES), 1)
    low = lane < HEAD_DIM

    ff = _dot(xh, wf_ref[...])
    f = ff[:, :LANES] + ff[:, LANES:] + _dot(xl, wf_ref[:, :LANES]) + bf_ref[...]
    lf = jnp.where(lane < N_FOX_HEADS, _log_sigmoid(f), 0.0)
    lf_ref[...] = lf[:, :N_FOX_HEADS]
    cs = _unpack_pieces(_dot(_tri(tm), _pack_pieces(lf)))
    cs = jnp.where(lane < N_FOX_HEADS, cs, 0.0) + carry_ref[0:1, :]
    carry_ref[...] = jnp.broadcast_to(cs[tm - 1:, :], carry_ref.shape)
    prow = lax.broadcasted_iota(jnp.int32, (LANES, N_FOX_HEADS * LANES), 0)
    pcol = lax.broadcasted_iota(jnp.int32, (LANES, N_FOX_HEADS * LANES), 1)
    place = (pcol == (prow % PIECE_PITCH) * LANES + HEAD_DIM + prow // PIECE_PITCH) & (prow % PIECE_PITCH < N_FOX_HEADS)
    caug = _dot(_pack_pieces(-cs), jnp.where(place, 1.0, 0.0).astype(bf16))
    ones3 = jnp.where((lane >= HEAD_DIM) & (lane < HEAD_DIM + 3), 1.0, 0.0)

    qk = _dot(xh, wm_ref[:, OFF_Q:OFF_V])
    q = qk[:, :W_FOX] * QK_SCALE
    k = qk[:, W_FOX:]
    kt_ref[...] = k.T
    kb = k.astype(bf16).astype(f32)
    qb = q.astype(bf16).astype(f32)
    squares = jnp.concatenate([(kb * kb).astype(bf16), (qb * qb).astype(bf16)], axis=1)
    srow = lax.broadcasted_iota(jnp.int32, (2 * W_FOX, LANES), 0)
    scol = lax.broadcasted_iota(jnp.int32, (2 * W_FOX, LANES), 1)
    target = (srow // W_FOX) * SUBLANES + (srow % W_FOX) // HEAD_DIM
    sums = _dot(squares, jnp.where(scol == target, 1.0, 0.0).astype(bf16))
    norms = jnp.sqrt(jnp.max(sums, axis=0, keepdims=True)) * NORM_SLACK
    kst_ref[:KST_CEND, :] = _col_bcast(norms)[:KST_CEND, :]
    kst_ref[KST_CEND:, :] = _col_bcast(cs[tm - 1:, :])[:SUBLANES, :]
    vu = _dot(xh, wm_ref[:, OFF_V:OFF_QM])
    v = vu[:, :W_FOX]
    vt_ref[...] = v.T
    u_ref[...] = vu[:, W_FOX:]
    for h in range(N_FOX_HEADS):
        qa_ref[h] = jnp.where(low, _head_block(q, h), ones3).astype(bf16)
        ka_ref[h] = jnp.where(low, _head_block(k, h), caug[:, LANES * h:LANES * (h + 1)]).astype(bf16)
        va_ref[h] = jnp.where(low, _head_block(v, h), 1.0).astype(bf16)
    rest = _dot(xh, wm_ref[:, OFF_QM:W_MAIN])
    qm_ref[...] = (rest[:, :W_MEM] * QK_SCALE).astype(bf16)
    sg_ref[...] = _silu(rest[:, W_MEM:]).astype(bf16)


def _inproj(l, x, g_all, wm_all, wf_all, bf_all, kv_prev, tm):
    B, L, D = x.shape
    depth = wm_all.shape[0]
    row = lambda w: pl.BlockSpec((None, tm, w), lambda b, i: (b, i, 0))
    heads = pl.BlockSpec((None, N_FOX_HEADS, tm, LANES), lambda b, i: (b, 0, i, 0))
    slab = pl.BlockSpec((None, None, W_FOX, tm), lambda b, i: (l, b, 0, i))
    aug = jax.ShapeDtypeStruct((B, N_FOX_HEADS, L, LANES), bf16)
    kvt = jax.ShapeDtypeStruct((depth, B, W_FOX, L), f32)
    out_shape = (
        aug,
        kvt,
        aug,
        kvt,
        aug,
        jax.ShapeDtypeStruct((B, L // tm, KST_ROWS, LANES), f32),
        jax.ShapeDtypeStruct((B, L, N_FOX_HEADS), f32),
        jax.ShapeDtypeStruct((B, L, W_MIX), bf16),
        jax.ShapeDtypeStruct((B, L, W_SSM), f32),
        jax.ShapeDtypeStruct((B, L, W_MEM), bf16),
    )
    kst = pl.BlockSpec((None, None, KST_ROWS, LANES), lambda b, i: (b, i, 0, 0))
    out_specs = (heads, slab, heads, slab, heads, kst, row(N_FOX_HEADS), row(W_MIX), row(W_SSM), row(W_MEM))
    in_specs = [row(D), _layer(l, (1, D)), _layer(l, wm_all.shape[1:]), _layer(l, wf_all.shape[1:]),
                _layer(l, (1, LANES))]
    n_in = len(in_specs)
    return pl.pallas_call(
        _inproj_kernel,
        grid=(B, L // tm),
        in_specs=in_specs + [pl.BlockSpec(memory_space=pl.ANY)] * len(kv_prev),
        out_specs=out_specs,
        out_shape=out_shape,
        input_output_aliases={n_in + n: out for n, out in zip(range(len(kv_prev)), (1, 3))},
        scratch_shapes=[pltpu.VMEM((SUBLANES, LANES), f32)],
        compiler_params=_cparams(("parallel", "arbitrary")),
        name="prompt_inproj",
    )(x, g_all, wm_all, wf_all, bf_all, *kv_prev)


def _memkv_kernel(m_ref, g_ref, w_ref, k_ref, v_ref):
    mn = _rms(m_ref[...], g_ref[...]).astype(bf16)
    kv = _dot(mn, w_ref[...])
    k_ref[...] = kv[:, :W_MEM]
    v_ref[...] = kv[:, W_MEM:]


def _memkv(mem, g_all, w_all):
    B, N, D = mem.shape
    depth = w_all.shape[0]
    out = pl.BlockSpec((None, None, N, W_MEM), lambda l, b: (l, b, 0, 0))
    return pl.pallas_call(
        _memkv_kernel,
        grid=(depth, B),
        in_specs=[pl.BlockSpec((None, N, D), lambda l, b: (b, 0, 0)),
                  pl.BlockSpec((None, 1, D), lambda l, b: (l, 0, 0)),
                  pl.BlockSpec((None,) + w_all.shape[1:], lambda l, b: (l, 0, 0))],
        out_specs=(out, out),
        out_shape=(jax.ShapeDtypeStruct((depth, B, N, W_MEM), f32),) * 2,
        compiler_params=_cparams(("parallel", "parallel")),
        name="prompt_memkv",
    )(mem, g_all, w_all)


def _first_needed_block(kst_ref, h, i, n_diag):
    n_full = i * n_diag
    mine = pl.ds(n_full, n_diag)
    qn = jnp.max(kst_ref[mine, KST_QN + h, :], axis=0, keepdims=True)
    kn_own = jnp.max(kst_ref[mine, KST_KN + h, :], axis=0, keepdims=True)
    lb = -qn * kn_own - kst_ref[pl.ds(jnp.maximum(n_full - 1, 0), 1), KST_CEND + h, :]
    ub = qn * kst_ref[:, KST_KN + h, :] - kst_ref[:, KST_CEND + h, :]
    kj = lax.broadcasted_iota(jnp.int32, ub.shape, 0)
    needed = (ub - lb >= -SKIP_GAP) & (kj < n_full)
    return _full_min(jnp.where(needed, kj, n_full).astype(f32))


def _fox_kernel(q_ref, k_ref, v_ref, kst_ref, o_ref, acc_ref, m_ref, *, tq, tk):
    j = pl.program_id(1)
    i = pl.program_id(2)
    acc_ref[...] = jnp.zeros_like(acc_ref)
    m_ref[...] = jnp.full_like(m_ref, -jnp.inf)

    def step(kj, r0):
        k0 = pl.multiple_of(kj * tk, tk)
        rows = slice(0, tq) if r0 is None else slice(r0, tq)
        n = rows.stop - rows.start
        for e in range(2):
            s = _dot_nt(q_ref[e, rows, :], k_ref[e, pl.ds(k0, tk), :])
            if r0 is not None:
                row = r0 + lax.broadcasted_iota(jnp.int32, (n, tk), 0)
                col = r0 + lax.broadcasted_iota(jnp.int32, (n, tk), 1)
                s = jnp.where(col <= row, s, -jnp.inf)
            m_old = m_ref[e, rows, :]
            m_new = jnp.maximum(m_old, jnp.max(s, axis=1, keepdims=True))
            p = jnp.exp((s - jnp.tile(m_new, (1, tk // LANES))).astype(bf16))
            acc_ref[e, rows, :] = (jnp.exp(m_old - m_new) * acc_ref[e, rows, :]
                                   + _dot(p, v_ref[e, pl.ds(k0, tk), :]))
            m_ref[e, rows, :] = m_new

    n_diag = tq // tk
    first = jnp.minimum(*[_first_needed_block(kst_ref, 2 * j + e, i, n_diag) for e in range(2)])
    t_first = jnp.min(first).astype(jnp.int32) // n_diag

    def body(t, carry):
        for d in range(n_diag):
            step(t * n_diag + d, None)
        return carry

    lax.fori_loop(t_first, i, body, 0)
    for d in range(n_diag):
        step(i * n_diag + d, d * tk)
    lane = lax.broadcasted_iota(jnp.int32, (tq, LANES), 1)
    a0 = acc_ref[0]
    a1 = acc_ref[1]
    o_ref[...] = jnp.where(lane < HEAD_DIM, a0 / pltpu.roll(a0, HEAD_DIM, 1), pltpu.roll(a1, HEAD_DIM, 1) / a1)


def _fox(qa, ka, va, kst, tq, tk):
    B, _, L, _ = qa.shape
    n_pairs = W_FOX // LANES
    assert kst.shape[1] == L // tk
    return pl.pallas_call(
        functools.partial(_fox_kernel, tq=tq, tk=tk),
        grid=(B, n_pairs, L // tq),
        in_specs=[pl.BlockSpec((None, 2, tq, LANES), lambda b, j, i: (b, j, i, 0)),
                  pl.BlockSpec((None, 2, L, LANES), lambda b, j, i: (b, j, 0, 0)),
                  pl.BlockSpec((None, 2, L, LANES), lambda b, j, i: (b, j, 0, 0)),
                  pl.BlockSpec((None,) + kst.shape[1:], lambda b, j, i: (b, 0, 0, 0))],
        out_specs=pl.BlockSpec((None, tq, LANES), lambda b, j, i: (b, i, j)),
        out_shape=jax.ShapeDtypeStruct((B, L, W_FOX), f32),
        scratch_shapes=[pltpu.VMEM((2, tq, LANES), f32), pltpu.VMEM((2, tq, LANES), f32)],
        compiler_params=_cparams(("parallel", "parallel", "arbitrary")),
        name="prompt_fox",
    )(qa, ka, va, kst)


def _s5prep_kernel(lr_ref, li_ref, ldt_ref, bre_ref, bim_ref, cre_ref, cim_ref,
                   lamb_ref, bmat_ref, cmat_ref, lpow_ref, lams_ref, pw_ref, *, seg_len):
    lr = lr_ref[...]
    li = li_ref[...]
    dt = jnp.exp(ldt_ref[...])
    ar = lr * dt
    ai = li * dt
    mag = jnp.exp(ar)
    lbr = mag * jnp.cos(ai)
    lbi = mag * jnp.sin(ai)
    lamb_ref[:, :HALF] = jnp.broadcast_to(lbr, (SUBLANES, HALF))
    lamb_ref[:, HALF:] = jnp.broadcast_to(lbi, (SUBLANES, HALF))
    den = lr * lr + li * li
    zr = ((lbr - 1.0) * lr + lbi * li) / den
    zi = (lbi * lr - (lbr - 1.0) * li) / den
    bre = bre_ref[...]
    bim = bim_ref[...]
    bmat_ref[:, :HALF] = zr * bre - zi * bim
    bmat_ref[:, HALF:] = zr * bim + zi * bre
    cmat_ref[:HALF, :] = cre_ref[...]
    cmat_ref[HALF:, :] = -cim_ref[...]
    jj = (lax.broadcasted_iota(jnp.int32, (SUBLANES, HALF), 0) + 1).astype(f32)
    magp = jnp.exp(ar * jj)
    pw_ref[:SUBLANES, :HALF] = magp * jnp.cos(ai * jj)
    pw_ref[:SUBLANES, HALF:] = magp * jnp.sin(ai * jj)
    n = SUBLANES
    while n < seg_len:
        sr = pw_ref[n - 1:n, :HALF]
        si = pw_ref[n - 1:n, HALF:]
        pr = pw_ref[:n, :HALF]
        pi = pw_ref[:n, HALF:]
        pw_ref[n:2 * n, :HALF] = pr * sr - pi * si
        pw_ref[n:2 * n, HALF:] = pr * si + pi * sr
        n *= 2
    lpow_ref[...] = pw_ref[...].astype(bf16)
    lams_ref[...] = pw_ref[seg_len - SUBLANES:, :]


def _s5prep(lam_re, lam_im, log_dt, b_re, b_im, c_re, c_im, seg_len):
    depth = lam_re.shape[0]
    assert seg_len >= SUBLANES and seg_len & (seg_len - 1) == 0
    nb, gb = N_SSM_BLOCKS, GROUPS_PER_BLOCK
    eye = jnp.eye(gb, dtype=f32)
    half = lambda a: a.reshape(depth, nb, 1, HALF)
    ldt = jnp.broadcast_to(log_dt[:, :, None], (depth, N_SSM_GROUPS, SSM_STATE))

    def embed_b(b):
        t = b.reshape(depth, nb, gb, SSM_STATE, SSM_GROUP).transpose(0, 1, 2, 4, 3)
        return (t[:, :, :, :, None, :] * eye[None, None, :, None, :, None]).reshape(depth, nb, LANES, HALF)

    def embed_c(c):
        t = c.reshape(depth, nb, gb, SSM_GROUP, SSM_STATE).transpose(0, 1, 2, 4, 3)
        return (t[:, :, :, :, None, :] * eye[None, None, :, None, :, None]).reshape(depth, nb, HALF, LANES)

    blk = lambda shape: pl.BlockSpec((None, None) + shape, lambda l, cb: (l, cb) + (0,) * len(shape))
    arr = lambda shape, dt: jax.ShapeDtypeStruct((depth, nb) + shape, dt)
    return pl.pallas_call(
        functools.partial(_s5prep_kernel, seg_len=seg_len),
        grid=(depth, nb),
        in_specs=[blk((1, HALF))] * 3 + [blk((LANES, HALF))] * 2 + [blk((HALF, LANES))] * 2,
        out_specs=(blk((SUBLANES, STATE_W)), blk((LANES, STATE_W)), blk((STATE_W, LANES)),
                   blk((seg_len, STATE_W)), blk((SUBLANES, STATE_W))),
        out_shape=(arr((SUBLANES, STATE_W), f32),
                   arr((LANES, STATE_W), f32),
                   arr((STATE_W, LANES), f32),
                   arr((seg_len, STATE_W), bf16),
                   arr((SUBLANES, STATE_W), f32)),
        scratch_shapes=[pltpu.VMEM((seg_len, STATE_W), f32)],
        compiler_params=_cparams(("parallel", "parallel")),
        name="s5_tables",
    )(half(lam_re), half(lam_im), half(ldt), embed_b(b_re), embed_b(b_im), embed_c(c_re), embed_c(c_im))


def _s5scan_kernel(u_ref, bmat_ref, cmat_ref, lamb_ref, dsk_ref, y_ref, hend_ref,
                   up_ref, hb_ref, hq_ref, hcar_ref, *, tj):
    j = pl.program_id(1)
    nb = N_SSM_BLOCKS

    @pl.when(j == 0)
    def _():
        hcar_ref[...] = jnp.zeros_like(hcar_ref)

    for cb in range(nb):
        for s in range(N_SEG):
            up_ref[cb, pl.ds(s, tj, stride=N_SEG), :] = u_ref[s, :, LANES * cb:LANES * (cb + 1)]
        hb_ref[cb] = _dot(up_ref[cb].astype(bf16), bmat_ref[cb])

    def rec(t2, h):
        r0 = pl.multiple_of(t2 * 2 * N_SEG, 2 * N_SEG)
        new = []
        for cb in range(nb):
            hr, hi = h[2 * cb], h[2 * cb + 1]
            lr = lamb_ref[cb, :, :HALF]
            li = lamb_ref[cb, :, HALF:]
            rs, is_ = [], []
            for d in range(2):
                bu = hb_ref[cb, pl.ds(r0 + d * N_SEG, N_SEG), :]
                hr, hi = lr * hr - li * hi + bu[:, :HALF], lr * hi + li * hr + bu[:, HALF:]
                rs.append(hr)
                is_.append(hi)
            hq_ref[cb, pl.ds(r0, 2 * N_SEG), :HALF] = jnp.concatenate(rs, axis=0).astype(bf16)
            hq_ref[cb, pl.ds(r0, 2 * N_SEG), HALF:] = jnp.concatenate(is_, axis=0).astype(bf16)
            new += [hr, hi]
        return tuple(new)

    h0 = tuple(hcar_ref[cb, :, HALF * part:HALF * (part + 1)] for cb in range(nb) for part in range(2))
    h = lax.fori_loop(0, tj // 2, rec, h0, unroll=True)
    for cb in range(nb):
        hcar_ref[cb, :, :HALF] = h[2 * cb]
        hcar_ref[cb, :, HALF:] = h[2 * cb + 1]
        up_ref[cb] = _dot(hq_ref[cb], cmat_ref[cb]) + dsk_ref[cb] * up_ref[cb]
        for s in range(N_SEG):
            y_ref[s, :, LANES * cb:LANES * (cb + 1)] = up_ref[cb, pl.ds(s, tj, stride=N_SEG), :]

    @pl.when(j == pl.num_programs(1) - 1)
    def _():
        hend_ref[...] = hcar_ref[...]


def _s5scan(l, u, bmat_all, cmat_all, lamb_all, dsk_all, tj):
    B, L, _ = u.shape
    seg_len = L // N_SEG
    u4 = u.reshape(B, N_SEG, seg_len, W_SSM)
    nb = N_SSM_BLOCKS
    tile = pl.BlockSpec((None, N_SEG, tj, W_SSM), lambda b, j: (b, 0, j, 0))
    y4, hend = pl.pallas_call(
        functools.partial(_s5scan_kernel, tj=tj),
        grid=(B, seg_len // tj),
        in_specs=[tile, _layer(l, bmat_all.shape[1:]), _layer(l, cmat_all.shape[1:]),
                  _layer(l, lamb_all.shape[1:]), _layer(l, dsk_all.shape[1:])],
        out_specs=(tile, pl.BlockSpec((None, nb, N_SEG, STATE_W), lambda b, j: (b, 0, 0, 0))),
        out_shape=(jax.ShapeDtypeStruct((B, N_SEG, seg_len, W_SSM), f32),
                   jax.ShapeDtypeStruct((B, nb, N_SEG, STATE_W), f32)),
        scratch_shapes=[pltpu.VMEM((nb, N_SEG * tj, LANES), f32), pltpu.VMEM((nb, N_SEG * tj, STATE_W), f32),
                        pltpu.VMEM((nb, N_SEG * tj, STATE_W), bf16), pltpu.VMEM((nb, N_SEG, STATE_W), f32)],
        compiler_params=_cparams(("parallel", "arbitrary")),
        name="prompt_s5_scan",
    )(u4, bmat_all, cmat_all, lamb_all, dsk_all)
    return y4.reshape(B, L, W_SSM), hend


def _s5fix_kernel(hend_ref, lams_ref, cmat_ref, hlast_ref, cp_ref, hs_ref):
    he = hend_ref[...]
    lsr = lams_ref[SUBLANES - 1:, :HALF]
    lsi = lams_ref[SUBLANES - 1:, HALF:]
    hs_ref[...] = jnp.zeros_like(hs_ref)
    hr = jnp.zeros((1, HALF), f32)
    hi = jnp.zeros((1, HALF), f32)
    for s in range(N_SEG):
        hs_ref[s:s + 1, :HALF] = hr
        hs_ref[s:s + 1, HALF:] = hi
        nr = lsr * hr - lsi * hi + he[s:s + 1, :HALF]
        ni = lsr * hi + lsi * hr + he[s:s + 1, HALF:]
        hr, hi = nr, ni
    hlast_ref[:, :HALF] = hr
    hlast_ref[:, HALF:] = hi
    hst_r = hs_ref[:, :HALF].T
    hst_i = hs_ref[:, HALF:].T
    cre = cmat_ref[:HALF, :]
    mci = cmat_ref[HALF:, :]
    for s in range(N_SEG):
        col_r = hst_r[:, s:s + 1]
        col_i = hst_i[:, s:s + 1]
        cp_ref[s, :HALF, :] = (cre * col_r + mci * col_i).astype(bf16)
        cp_ref[s, HALF:, :] = (mci * col_r - cre * col_i).astype(bf16)


def _s5fix(l, hend, lams_all, cmat_all):
    B, nb = hend.shape[:2]
    return pl.pallas_call(
        _s5fix_kernel,
        grid=(B, nb),
        in_specs=[pl.BlockSpec((None, None, N_SEG, STATE_W), lambda b, cb: (b, cb, 0, 0)),
                  pl.BlockSpec((None, None, SUBLANES, STATE_W), lambda b, cb: (l, cb, 0, 0)),
                  pl.BlockSpec((None, None, STATE_W, LANES), lambda b, cb: (l, cb, 0, 0))],
        out_specs=(pl.BlockSpec((None, None, 1, STATE_W), lambda b, cb: (b, cb, 0, 0)),
                   pl.BlockSpec((None, N_SEG, None, STATE_W, LANES), lambda b, cb: (b, 0, cb, 0, 0))),
        out_shape=(jax.ShapeDtypeStruct((B, nb, 1, STATE_W), f32),
                   jax.ShapeDtypeStruct((B, N_SEG, nb, STATE_W, LANES), bf16)),
        scratch_shapes=[pltpu.VMEM((LANES, STATE_W), f32)],
        compiler_params=_cparams(("parallel", "parallel")),
        name="prompt_s5_fix",
    )(hend, lams_all, cmat_all)


def _pair_attend(q, k, v):
    lane = lax.broadcasted_iota(jnp.int32, q.shape, 1)
    zero = jnp.zeros_like(q)
    outs = []
    for e in range(2):
        qe = jnp.where((lane < HEAD_DIM) == (e == 0), q, zero)
        s = _dot_nt(qe, k)
        p = jnp.exp(s - jnp.max(s, axis=1, keepdims=True))
        outs.append(_dot(p.astype(bf16), v) / jnp.sum(p, axis=1, keepdims=True))
    return jnp.where(lane < HEAD_DIM, outs[0], outs[1])


def _outproj_kernel(x_ref, fox_ref, y_ref, lp_ref, cp_ref, qm_ref, mk_ref, mv_ref, sg_ref,
                    wglu_ref, bglu_ref, wout_ref, gfin_ref, o_ref, *, final, tiles_per_seg):
    tm = x_ref.shape[0]
    j0 = pl.multiple_of((pl.program_id(1) % tiles_per_seg) * tm, tm)
    corr = [_dot(lp_ref[cb, pl.ds(j0, tm), :], cp_ref[cb]) for cb in range(N_SSM_BLOCKS)]
    ssm_y = y_ref[...] + jnp.concatenate(corr, axis=1)
    s = _gelu(ssm_y)
    s = s * jax.nn.sigmoid(_dot(s.astype(bf16), wglu_ref[...]) + bglu_ref[...])
    qm = qm_ref[...]
    mk = mk_ref[...].astype(bf16)
    mv = mv_ref[...].astype(bf16)
    mem = [_pair_attend(qm[:, LANES * jp:LANES * (jp + 1)], mk[:, LANES * jp:LANES * (jp + 1)],
                        mv[:, LANES * jp:LANES * (jp + 1)]) for jp in range(W_MEM // LANES)]
    sg = sg_ref[...].astype(f32)
    mix = jnp.concatenate([fox_ref[...] * sg[:, :W_FOX],
                           s * sg[:, W_FOX:W_FOX + W_SSM],
                           jnp.concatenate(mem, axis=1) * sg[:, W_FOX + W_SSM:]], axis=1)
    xo = x_ref[...] + _dot(mix.astype(bf16), wout_ref[...])
    if final:
        xo = _rms(xo, gfin_ref[...])
    o_ref[...] = xo


def _outproj(l, x, fox_o, y, lpow_all, cp, qm, mk_all, mv_all, sg, wglu_all, bglu_all, wout_all, gfin, tm, final):
    B, L, D = x.shape
    seg_len = L // N_SEG
    tiles_per_seg = seg_len // tm
    row = lambda w: pl.BlockSpec((None, tm, w), lambda b, i: (b, i, 0))
    nb = N_SSM_BLOCKS
    mem_spec = pl.BlockSpec((None, None) + mk_all.shape[2:], lambda b, i: (l, b, 0, 0))
    return pl.pallas_call(
        functools.partial(_outproj_kernel, final=final, tiles_per_seg=tiles_per_seg),
        grid=(B, L // tm),
        in_specs=[row(D), row(W_FOX), row(W_SSM), _layer(l, lpow_all.shape[1:]),
                  pl.BlockSpec((None, None, nb, STATE_W, LANES), lambda b, i: (b, i // tiles_per_seg, 0, 0, 0)),
                  row(W_MEM), mem_spec, mem_spec, row(D),
                  _layer(l, wglu_all.shape[1:]), _layer(l, (1, W_SSM)), _layer(l, wout_all.shape[1:]),
                  pl.BlockSpec((1, D), lambda b, i: (0, 0))],
        out_specs=row(D),
        out_shape=jax.ShapeDtypeStruct((B, L, D), f32),
        compiler_params=_cparams(("parallel", "parallel")),
        name="prompt_outproj",
    )(x, fox_o, y, lpow_all, cp, qm, mk_all, mv_all, sg, wglu_all, bglu_all, wout_all, gfin)


def _s_inproj_kernel(x_ref, g_ref, w_ref, z_ref):
    xn = _rms(x_ref[...], g_ref[...])
    z_ref[...] = _dot3(xn, w_ref[...])


def _s_inproj(l, xs, g_all, wcat_all, tn):
    R, D = xs.shape
    return pl.pallas_call(
        _s_inproj_kernel,
        grid=(W_CAT // tn,),
        in_specs=[pl.BlockSpec((R, D), lambda n: (0, 0)), _layer(l, (1, D)),
                  pl.BlockSpec((None, D, tn), lambda n: (l, 0, n))],
        out_specs=pl.BlockSpec((R, tn), lambda n: (0, n)),
        out_shape=jax.ShapeDtypeStruct((R, W_CAT), f32),
        compiler_params=_cparams(("parallel",)),
        name="sample_inproj",
    )(xs, g_all, wcat_all)


def _full_max(x):
    return jnp.max(jnp.max(x, axis=0, keepdims=True), axis=1, keepdims=True)


def _full_min(x):
    return jnp.min(jnp.min(x, axis=0, keepdims=True), axis=1, keepdims=True)


def _full_sum(x):
    return jnp.sum(jnp.sum(x, axis=0, keepdims=True), axis=1, keepdims=True)


def _lane_sums(acc):
    ones = jnp.ones((SUBLANES, LANES), bf16)
    hi, lo = _split2(acc)
    return (_dot_nt(ones, hi) + _dot_nt(ones, lo))[0:1, :]


def _col_bcast(row):
    return jnp.broadcast_to(row, (LANES, row.shape[1])).T


def _s_attn_kernel(pt_ref, z_ref, bf_ref, lfc_ref, *refs, n_pages, n_seq):
    per_seq = 2 * n_pages + 2
    ins = refs[:n_seq * per_seq]
    fox_ref, mem_ref, lf_ref, s_scr, lfs_scr, p_scr, acc_scr, accm_scr = refs[n_seq * per_seq:]
    for i in range(n_seq):
        mine = ins[i * per_seq:(i + 1) * per_seq]
        _decode_one(pt_ref, pl.program_id(0) * n_seq + i, z_ref.at[i], bf_ref, lfc_ref,
                    mine[:n_pages], mine[n_pages:2 * n_pages], mine[2 * n_pages], mine[2 * n_pages + 1],
                    fox_ref.at[i], mem_ref.at[i], lf_ref.at[i],
                    s_scr.at[i], lfs_scr.at[i], p_scr.at[i], acc_scr.at[i], accm_scr.at[i])


def _decode_one(pt_ref, r, z_ref, bf_ref, lfc_ref, ck, cv, mk_ref, mv_ref, fox_ref, mem_ref, lf_ref,
                s_scr, lfs_scr, p_scr, acc_scr, accm_scr):
    n_pages = len(ck)
    z = z_ref[...]
    q = z[:, OFF_Q:OFF_Q + W_FOX] * QK_SCALE
    k_new = z[:, OFF_K:OFF_K + W_FOX]
    v_new = z[:, OFF_V:OFF_V + W_FOX]
    lf_new = _log_sigmoid(z[:, OFF_F:OFF_F + LANES] + bf_ref[...])
    lf_ref[...] = lf_new

    qt = _col_bcast(q)
    for h in range(N_FOX_HEADS):
        qh = qt[HEAD_DIM * h:HEAD_DIM * (h + 1), :]
        for p in range(n_pages):
            row = h * n_pages + p
            s_scr[row:row + 1, :] = jnp.sum(ck[p][h] * qh, axis=0, keepdims=True)
            lfs_scr[row:row + 1, :] = lfc_ref[h, pl.ds(pt_ref[r, p], 1), :]

    n_rows = N_FOX_HEADS * n_pages
    ur = lax.broadcasted_iota(jnp.int32, (LANES, LANES), 0)
    uc = lax.broadcasted_iota(jnp.int32, (LANES, LANES), 1)
    upper = jnp.where(ur <= uc, 1.0, 0.0).astype(bf16)
    cs = sum(_dot(pc, upper) for pc in _split3(lfs_scr[...]))
    tot = jnp.broadcast_to(cs[:, LANES - 1:], (n_rows, LANES))
    mr = lax.broadcasted_iota(jnp.int32, (n_rows, n_rows), 0)
    mc = lax.broadcasted_iota(jnp.int32, (n_rows, n_rows), 1)
    before = jnp.where((mc < mr) & (mc // n_pages == mr // n_pages), 1.0, 0.0).astype(bf16)
    c = cs + sum(_dot(before, pc) for pc in _split3(tot))
    s_all = s_scr[...] - c

    qk_new = q * k_new
    lane_head = lax.broadcasted_iota(jnp.int32, (1, W_FOX), 1) // HEAD_DIM
    pn_exp = jnp.zeros((1, W_FOX), f32)
    l_exp = jnp.zeros((1, W_FOX), f32)
    for h in range(N_FOX_HEADS):
        rows = slice(h * n_pages, (h + 1) * n_pages)
        s_h = s_all[rows, :]
        c_new = c[(h + 1) * n_pages - 1:(h + 1) * n_pages, LANES - 1:] + lf_new[:, h:h + 1]
        s_new = jnp.sum(qk_new[:, HEAD_DIM * h:HEAD_DIM * (h + 1)], axis=1, keepdims=True) - c_new
        m = jnp.maximum(_full_max(s_h), s_new)
        p_h = jnp.exp(s_h - m)
        pn = jnp.exp(s_new - m)
        p_scr[rows, :] = p_h
        pn_exp = jnp.where(lane_head == h, pn, pn_exp)
        l_exp = jnp.where(lane_head == h, _full_sum(p_h) + pn, l_exp)
        acc = jnp.zeros((HEAD_DIM, LANES), f32)
        for p in range(n_pages):
            acc = acc + cv[p][h] * p_scr[h * n_pages + p:h * n_pages + p + 1, :]
        acc_scr[HEAD_DIM * h:HEAD_DIM * (h + 1), :] = acc
    fox_ref[...] = (_lane_sums(acc_scr[...]) + pn_exp * v_new) / l_exp

    qm = z[:, OFF_QM:OFF_QM + W_MEM] * QK_SCALE
    qmt = _col_bcast(qm)
    n_halves = mk_ref.shape[2] // LANES
    mem_head = lax.broadcasted_iota(jnp.int32, (1, W_MEM), 1) // HEAD_DIM
    lm_exp = jnp.zeros((1, W_MEM), f32)
    for h in range(N_MEM_HEADS):
        qh = qmt[HEAD_DIM * h:HEAD_DIM * (h + 1), :]
        ss = [jnp.sum(mk_ref[h, :, LANES * t:LANES * (t + 1)] * qh, axis=0, keepdims=True) for t in range(n_halves)]
        m = functools.reduce(jnp.maximum, [jnp.max(s, axis=1, keepdims=True) for s in ss])
        ps = [jnp.exp(s - m) for s in ss]
        lm = sum(jnp.sum(p, axis=1, keepdims=True) for p in ps)
        lm_exp = jnp.where(mem_head == h, lm, lm_exp)
        accm_scr[HEAD_DIM * h:HEAD_DIM * (h + 1), :] = sum(
            mv_ref[h, :, LANES * t:LANES * (t + 1)] * ps[t] for t in range(n_halves))
    mem_ref[...] = _lane_sums(accm_scr[...]) / lm_exp


def _s_attn(l, page_table, z, bf_all, ckt, cvt, lft, mkt, mvt, n_seq):
    R, n_pages = page_table.shape
    n_pool, page = ckt.shape[1], ckt.shape[4]
    n_mem = mkt.shape[4]
    assert page == LANES and n_mem % LANES == 0
    n_seq = _tile(R, n_seq)
    z3 = z.reshape(R, 1, W_CAT)
    in_specs, operands = [], []
    for i in range(n_seq):
        paged = [pl.BlockSpec((None, None, N_FOX_HEADS, HEAD_DIM, page),
                              lambda g, pt, i=i, p=p: (l, pt[g * n_seq + i, p], 0, 0, 0)) for p in range(n_pages)]
        mem_spec = pl.BlockSpec((None, None, N_MEM_HEADS, HEAD_DIM, n_mem),
                                lambda g, pt, i=i: (l, g * n_seq + i, 0, 0, 0))
        in_specs += paged + paged + [mem_spec, mem_spec]
        operands += [ckt] * n_pages + [cvt] * n_pages + [mkt, mvt]
    per_row = lambda w: pl.BlockSpec((n_seq, 1, w), lambda g, pt: (g, 0, 0))
    n_rows = N_FOX_HEADS * n_pages
    scr = lambda rows: pltpu.VMEM((n_seq, rows, LANES), f32)
    fox, mem, lf = pl.pallas_call(
        functools.partial(_s_attn_kernel, n_pages=n_pages, n_seq=n_seq),
        grid_spec=pltpu.PrefetchScalarGridSpec(
            num_scalar_prefetch=1,
            grid=(R // n_seq,),
            in_specs=[per_row(W_CAT), _layer(l, (1, LANES)), _layer(l, (N_FOX_HEADS, n_pool, page))] + in_specs,
            out_specs=(per_row(W_FOX), per_row(W_MEM), per_row(LANES)),
            scratch_shapes=[scr(n_rows), scr(n_rows), scr(n_rows), scr(W_FOX), scr(W_MEM)]),
        out_shape=(jax.ShapeDtypeStruct((R, 1, W_FOX), f32), jax.ShapeDtypeStruct((R, 1, W_MEM), f32),
                   jax.ShapeDtypeStruct((R, 1, LANES), f32)),
        compiler_params=_cparams(("parallel",)),
        name="sample_attn",
    )(page_table, z3, bf_all, lft, *operands)
    return fox.reshape(R, W_FOX), mem.reshape(R, W_MEM), lf.reshape(R, LANES)


def _s_out_kernel(x_ref, z_ref, fox_ref, mem_ref, h0_ref, bmat_ref, cmat_ref, lamb_ref, dsk_ref,
                  wglu_ref, bglu_ref, wout_ref, gfin_ref, o_ref, h_ref, *, final):
    z = z_ref[...]
    u = z[:, OFF_U:OFF_U + W_SSM]
    ys = []
    for cb in range(N_SSM_BLOCKS):
        ucb = u[:, LANES * cb:LANES * (cb + 1)]
        bu = _dot3(ucb, bmat_ref[cb])
        lr = lamb_ref[cb, 0:1, :HALF]
        li = lamb_ref[cb, 0:1, HALF:]
        h0r = h0_ref[:, STATE_W * cb:STATE_W * cb + HALF]
        h0i = h0_ref[:, STATE_W * cb + HALF:STATE_W * (cb + 1)]
        hr = lr * h0r - li * h0i + bu[:, :HALF]
        hi = lr * h0i + li * h0r + bu[:, HALF:]
        h_ref[:, STATE_W * cb:STATE_W * cb + HALF] = hr
        h_ref[:, STATE_W * cb + HALF:STATE_W * (cb + 1)] = hi
        ys.append(_dot3(jnp.concatenate([hr, hi], axis=1), cmat_ref[cb]) + dsk_ref[cb] * ucb)
    s = _gelu(jnp.concatenate(ys, axis=1))
    s = s * jax.nn.sigmoid(_dot3(s, wglu_ref[...]) + bglu_ref[...])
    sg = _silu(z[:, OFF_G:W_MAIN])
    mix = jnp.concatenate([fox_ref[...] * sg[:, :W_FOX],
                           s * sg[:, W_FOX:W_FOX + W_SSM],
                           mem_ref[...] * sg[:, W_FOX + W_SSM:]], axis=1)
    xo = x_ref[...] + _dot3(mix, wout_ref[...])
    if final:
        xo = _rms(xo, gfin_ref[...])
    o_ref[...] = xo


def _s_out(l, xs, z, fox, mem, h0_all, bmat_all, cmat_all, lamb_all, dsk_all, wglu_all, bglu_all, wout_all,
           gfin, final):
    R, D = xs.shape
    whole = lambda a: pl.BlockSpec(a.shape, lambda i: (0,) * a.ndim)
    hspec = pl.BlockSpec((R, h0_all.shape[2]), lambda i: (0, 0))
    return pl.pallas_call(
        functools.partial(_s_out_kernel, final=final),
        grid=(1,),
        in_specs=[whole(xs), whole(z), whole(fox), whole(mem), _layer(l, h0_all.shape[1:]),
                  _layer(l, bmat_all.shape[1:]), _layer(l, cmat_all.shape[1:]), _layer(l, lamb_all.shape[1:]),
                  _layer(l, dsk_all.shape[1:]), _layer(l, wglu_all.shape[1:]), _layer(l, (1, W_SSM)),
                  _layer(l, wout_all.shape[1:]), whole(gfin)],
        out_specs=(whole(xs), hspec),
        out_shape=(jax.ShapeDtypeStruct((R, D), f32), jax.ShapeDtypeStruct(h0_all.shape[1:], f32)),
        compiler_params=_cparams(("arbitrary",)),
        name="sample_out",
    )(xs, z, fox, mem, h0_all, bmat_all, cmat_all, lamb_all, dsk_all, wglu_all, bglu_all, wout_all, gfin)


def _state_to_cols(h):
    lead = h.shape[:-3]
    n = len(lead)
    t = h.reshape(lead + (N_SSM_BLOCKS, GROUPS_PER_BLOCK, SSM_STATE, 2))
    t = t.transpose(tuple(range(n)) + (n, n + 3, n + 1, n + 2))
    return t.reshape(lead + (N_SSM_BLOCKS * STATE_W,))


def _cols_to_state(c):
    lead = c.shape[:-1]
    n = len(lead)
    t = c.reshape(lead + (N_SSM_BLOCKS, 2, GROUPS_PER_BLOCK, SSM_STATE))
    t = t.transpose(tuple(range(n)) + (n, n + 2, n + 3, n + 1))
    return t.reshape(lead + (N_SSM_GROUPS, SSM_STATE, 2))


def _tile(n, pref):
    t = min(n, pref)
    assert n % t == 0
    return t


def kernel(x_prompt, x_sample, mem_prompt, cache_k, cache_v, cache_logf, state_ssm, cache_mem_k, cache_mem_v,
           page_table, g_norm, w_in, b_f, lam_re, lam_im, log_dt, b_re, b_im, c_re, c_im, d_skip, w_glu, b_glu,
           g_mem, w_mem_kv, w_out, g_final):
    depth = w_in.shape[0]
    B, L, D = x_prompt.shape
    R = x_sample.shape[0]
    assert x_sample.shape[1] == 1 and L % N_SEG == 0
    seg_len = L // N_SEG
    tm = _tile(seg_len, 512)
    tq = _tile(L, 1024)
    tk = _tile(tq, 512)
    tj = _tile(seg_len, 128)

    seg = dict(zip(("q", "k", "v", "f", "ga", "u", "gs", "qm", "gm"),
                   (w_in[:, :, a:b] for a, b in zip(IN_CUTS[:-1], IN_CUTS[1:]))))
    wcat_all = jnp.concatenate(
        [seg[n] for n in ("q", "k", "v", "u", "qm", "ga", "gs", "gm", "f")]
        + [jnp.zeros((depth, D, W_CAT - W_MAIN - N_FOX_HEADS), f32)], axis=2)
    wm_all = wcat_all[:, :, :W_MAIN].astype(bf16)
    wf = wcat_all[:, :, OFF_F:OFF_F + LANES]
    wfh = wf.astype(bf16)
    wf_all = jnp.concatenate([wfh, (wf - wfh.astype(f32)).astype(bf16)], axis=2)
    bf_all = jnp.pad(b_f, ((0, 0), (0, LANES - N_FOX_HEADS))).reshape(depth, 1, LANES)
    gn_all = g_norm.reshape(depth, 1, D)
    gmem_all = g_mem.reshape(depth, 1, D)
    bglu_all = b_glu.reshape(depth, 1, W_SSM)
    gfin = g_final.reshape(1, D)
    dsk_all = d_skip.reshape(depth, N_SSM_BLOCKS, 1, LANES)
    wglu_bf = w_glu.astype(bf16)
    wout_bf = w_out.astype(bf16)
    ckt = jnp.transpose(cache_k, (0, 1, 3, 4, 2))
    cvt = jnp.transpose(cache_v, (0, 1, 3, 4, 2))
    lft = jnp.transpose(cache_logf, (0, 3, 1, 2))
    mkt = jnp.transpose(cache_mem_k, (0, 1, 3, 4, 2))
    mvt = jnp.transpose(cache_mem_v, (0, 1, 3, 4, 2))
    h0_all = _state_to_cols(state_ssm)

    lamb_all, bmat_all, cmat_all, lpow_all, lams_all = _s5prep(lam_re, lam_im, log_dt, b_re, b_im, c_re, c_im, seg_len)
    bmat_bf = bmat_all.astype(bf16)
    cmat_bf = cmat_all.astype(bf16)
    mk_all, mv_all = _memkv(mem_prompt, gmem_all, w_mem_kv.astype(bf16))

    xp = x_prompt
    xs = x_sample.reshape(R, D)
    outs = {n: [] for n in ("lfp", "hp", "z", "lfs", "hs")}
    kv_prev = ()
    for l in range(depth):
        final = l == depth - 1
        qa, kt_all, ka, vt_all, va, kst, lf, sg, u, qm = _inproj(l, xp, gn_all, wm_all, wf_all, bf_all, kv_prev, tk)
        kv_prev = (kt_all, vt_all)
        fox_o = _fox(qa, ka, va, kst, tq, tk)
        y, hend = _s5scan(l, u, bmat_bf, cmat_bf, lamb_all, dsk_all, tj)
        hlast, cp = _s5fix(l, hend, lams_all, cmat_all)
        xp = _outproj(l, xp, fox_o, y, lpow_all, cp, qm, mk_all, mv_all, sg, wglu_bf, bglu_all, wout_bf, gfin,
                      tm, final)
        outs["lfp"].append(lf)
        outs["hp"].append(hlast.reshape(B, N_SSM_BLOCKS * STATE_W))

        z = _s_inproj(l, xs, gn_all, wcat_all, 256)
        fox_s, mem_s, lf_s = _s_attn(l, page_table, z, bf_all, ckt, cvt, lft, mkt, mvt, n_seq=2)
        xs, h_new = _s_out(l, xs, z, fox_s, mem_s, h0_all, bmat_all, cmat_all, lamb_all, dsk_all,
                           w_glu, bglu_all, w_out, gfin, final)
        outs["z"].append(z)
        outs["lfs"].append(lf_s)
        outs["hs"].append(h_new)

    st = lambda n: jnp.stack(outs[n])
    z_all = st("z")
    n_mem = mk_all.shape[2]
    return (xp, xs.reshape(R, 1, D),
            kt_all.reshape(depth, B, N_FOX_HEADS, HEAD_DIM, L).transpose(0, 1, 4, 2, 3),
            vt_all.reshape(depth, B, N_FOX_HEADS, HEAD_DIM, L).transpose(0, 1, 4, 2, 3),
            st("lfp"),
            _cols_to_state(st("hp")),
            mk_all.reshape(depth, B, n_mem, N_MEM_HEADS, HEAD_DIM),
            mv_all.reshape(depth, B, n_mem, N_MEM_HEADS, HEAD_DIM),
            z_all[:, :, OFF_K:OFF_K + W_FOX].reshape(depth, R, 1, N_FOX_HEADS, HEAD_DIM),
            z_all[:, :, OFF_V:OFF_V + W_FOX].reshape(depth, R, 1, N_FOX_HEADS, HEAD_DIM),
            st("lfs")[:, :, :N_FOX_HEADS].reshape(depth, R, 1, N_FOX_HEADS),
            _cols_to_state(st("hs")))
```

```python
import functools

import jax
import jax.numpy as jnp
from jax import lax
from jax.experimental import pallas as pl
from jax.experimental.pallas import tpu as pltpu

f32 = jnp.float32
bf16 = jnp.bfloat16

HEAD_DIM = 64
N_FOX_HEADS = 6
W_FOX = N_FOX_HEADS * HEAD_DIM
SSM_GROUP = 16
N_SSM_GROUPS = 24
W_SSM = N_SSM_GROUPS * SSM_GROUP
SSM_STATE = 64
N_MEM_HEADS = 4
W_MEM = N_MEM_HEADS * HEAD_DIM
W_MIX = W_FOX + W_SSM + W_MEM
RMS_EPS = 1e-6
QK_SCALE = HEAD_DIM ** -0.5

LANES = 128
SUBLANES = 8
GROUPS_PER_BLOCK = LANES // SSM_GROUP
N_SSM_BLOCKS = N_SSM_GROUPS // GROUPS_PER_BLOCK
HALF = GROUPS_PER_BLOCK * SSM_STATE
STATE_W = 2 * HALF
N_SEG = SUBLANES
PIECE_PITCH = SUBLANES

OFF_Q, OFF_K, OFF_V, OFF_U, OFF_QM, OFF_G = 0, 384, 768, 1152, 1536, 1792
W_MAIN = OFF_G + W_MIX
OFF_F = W_MAIN
W_CAT = 3072
IN_CUTS = (0, 384, 768, 1152, 1158, 1542, 1926, 2310, 2566, 2822)

VMEM_LIMIT = 56 * 1024 * 1024

SKIP_GAP = 110.0
NORM_SLACK = 1.02
KST_KN, KST_QN, KST_CEND, KST_ROWS = 0, SUBLANES, 2 * SUBLANES, 3 * SUBLANES


def _cparams(sem):
    return pltpu.CompilerParams(dimension_semantics=sem, vmem_limit_bytes=VMEM_LIMIT)


def _dot(a, b):
    return jnp.dot(a, b, preferred_element_type=f32)


def _dot_nt(a, b):
    return lax.dot_general(a, b, (((1,), (1,)), ((), ())), preferred_element_type=f32)


def _split2(a):
    hi = a.astype(bf16)
    lo = (a - hi.astype(f32)).astype(bf16)
    return hi, lo


def _split3(a):
    hi = a.astype(bf16)
    r = a - hi.astype(f32)
    mid = r.astype(bf16)
    lo = (r - mid.astype(f32)).astype(bf16)
    return hi, mid, lo


def _dot3(a, b):
    ah, al = _split2(a)
    bh, bl = _split2(b)
    return _dot(ah, bh) + _dot(al, bh) + _dot(ah, bl)


def _rms(x, g):
    ms = jnp.mean(x * x, axis=-1, keepdims=True)
    return x * lax.rsqrt(ms + RMS_EPS) * g


def _log_sigmoid(x):
    return jnp.minimum(x, 0.0) - jnp.log1p(jnp.exp(-jnp.abs(x)))


def _silu(x):
    return x * jax.nn.sigmoid(x)


def _gelu(x):
    return 0.5 * x * (1.0 + lax.erf(x * (2.0 ** -0.5)))


def _tri(n):
    r = lax.broadcasted_iota(jnp.int32, (n, n), 0)
    c = lax.broadcasted_iota(jnp.int32, (n, n), 1)
    return jnp.where(c <= r, 1.0, 0.0).astype(bf16)


def _pack_pieces(x):
    packed = None
    for i, pc in enumerate(_split3(x)):
        pc = pc.astype(f32)
        if i:
            pc = pltpu.roll(pc, PIECE_PITCH * i, 1)
        packed = pc if packed is None else packed + pc
    return packed.astype(bf16)


def _unpack_pieces(y):
    return y + pltpu.roll(y, LANES - PIECE_PITCH, 1) + pltpu.roll(y, LANES - 2 * PIECE_PITCH, 1)


def _layer(l, shape):
    return pl.BlockSpec((None,) + tuple(shape), lambda *_: (l,) + (0,) * len(shape),
                        pipeline_mode=pl.Buffered(1))


def _head_block(x, h):
    blk = x[:, LANES * (h // 2):LANES * (h // 2 + 1)]
    return blk if h % 2 == 0 else pltpu.roll(blk, HEAD_DIM, 1)


def _inproj_kernel(x_ref, g_ref, wm_ref, wf_ref, bf_ref, *rest):
    qa_ref, kt_ref, ka_ref, vt_ref, va_ref, kst_ref, lf_ref, sg_ref, u_ref, qm_ref, carry_ref = rest[-11:]
    i = pl.program_id(1)

    @pl.when(i == 0)
    def _():
        carry_ref[...] = jnp.zeros_like(carry_ref)

    xn = _rms(x_ref[...], g_ref[...])
    xh = xn.astype(bf16)
    tm = xn.shape[0]
    lane = lax.broadcasted_iota(jnp.int32, (tm, LANES), 1)
    low = lane < HEAD_DIM

    ff = _dot(xh, wf_ref[...])
    f = ff[:, :LANES] + ff[:, LANES:] + bf_ref[...]
    lf = jnp.where(lane < N_FOX_HEADS, _log_sigmoid(f), 0.0)
    lf_ref[...] = lf[:, :N_FOX_HEADS]
    cs = _unpack_pieces(_dot(_tri(tm), _pack_pieces(lf)))
    cs = jnp.where(lane < N_FOX_HEADS, cs, 0.0) + carry_ref[0:1, :]
    carry_ref[...] = jnp.broadcast_to(cs[tm - 1:, :], carry_ref.shape)
    prow = lax.broadcasted_iota(jnp.int32, (LANES, N_FOX_HEADS * LANES), 0)
    pcol = lax.broadcasted_iota(jnp.int32, (LANES, N_FOX_HEADS * LANES), 1)
    place = (pcol == (prow % PIECE_PITCH) * LANES + HEAD_DIM + prow // PIECE_PITCH) & (prow % PIECE_PITCH < N_FOX_HEADS)
    caug = _dot(_pack_pieces(-cs), jnp.where(place, 1.0, 0.0).astype(bf16))
    ones3 = jnp.where((lane >= HEAD_DIM) & (lane < HEAD_DIM + 3), 1.0, 0.0)

    qk = _dot(xh, wm_ref[:, OFF_Q:OFF_V])
    q = qk[:, :W_FOX] * QK_SCALE
    k = qk[:, W_FOX:]
    kt_ref[...] = k.T
    kb = k.astype(bf16).astype(f32)
    qb = q.astype(bf16).astype(f32)
    squares = jnp.concatenate([(kb * kb).astype(bf16), (qb * qb).astype(bf16)], axis=1)
    srow = lax.broadcasted_iota(jnp.int32, (2 * W_FOX, LANES), 0)
    scol = lax.broadcasted_iota(jnp.int32, (2 * W_FOX, LANES), 1)
    target = (srow // W_FOX) * SUBLANES + (srow % W_FOX) // HEAD_DIM
    sums = _dot(squares, jnp.where(scol == target, 1.0, 0.0).astype(bf16))
    norms = jnp.sqrt(jnp.max(sums, axis=0, keepdims=True)) * NORM_SLACK
    kst_ref[:KST_CEND, :] = _col_bcast(norms)[:KST_CEND, :]
    kst_ref[KST_CEND:, :] = _col_bcast(cs[tm - 1:, :])[:SUBLANES, :]
    vu = _dot(xh, wm_ref[:, OFF_V:OFF_QM])
    v = vu[:, :W_FOX]
    vt_ref[...] = v.T
    u_ref[...] = vu[:, W_FOX:]
    for h in range(N_FOX_HEADS):
        qa_ref[h] = jnp.where(low, _head_block(q, h), ones3).astype(bf16)
        ka_ref[h] = jnp.where(low, _head_block(k, h), caug[:, LANES * h:LANES * (h + 1)]).astype(bf16)
        va_ref[h] = jnp.where(low, _head_block(v, h), 1.0).astype(bf16)
    rest = _dot(xh, wm_ref[:, OFF_QM:W_MAIN])
    qm_ref[...] = (rest[:, :W_MEM] * QK_SCALE).astype(bf16)
    sg_ref[...] = _silu(rest[:, W_MEM:]).astype(bf16)


def _inproj(l, x, g_all, wm_all, wf_all, bf_all, kv_prev, tm):
    B, L, D = x.shape
    depth = wm_all.shape[0]
    row = lambda w: pl.BlockSpec((None, tm, w), lambda b, i: (b, i, 0))
    heads = pl.BlockSpec((None, N_FOX_HEADS, tm, LANES), lambda b, i: (b, 0, i, 0))
    slab = pl.BlockSpec((None, None, W_FOX, tm), lambda b, i: (l, b, 0, i))
    aug = jax.ShapeDtypeStruct((B, N_FOX_HEADS, L, LANES), bf16)
    kvt = jax.ShapeDtypeStruct((depth, B, W_FOX, L), f32)
    out_shape = (
        aug,
        kvt,
        aug,
        kvt,
        aug,
        jax.ShapeDtypeStruct((B, L // tm, KST_ROWS, LANES), f32),
        jax.ShapeDtypeStruct((B, L, N_FOX_HEADS), f32),
        jax.ShapeDtypeStruct((B, L, W_MIX), bf16),
        jax.ShapeDtypeStruct((B, L, W_SSM), f32),
        jax.ShapeDtypeStruct((B, L, W_MEM), bf16),
    )
    kst = pl.BlockSpec((None, None, KST_ROWS, LANES), lambda b, i: (b, i, 0, 0))
    out_specs = (heads, slab, heads, slab, heads, kst, row(N_FOX_HEADS), row(W_MIX), row(W_SSM), row(W_MEM))
    in_specs = [row(D), _layer(l, (1, D)), _layer(l, wm_all.shape[1:]), _layer(l, wf_all.shape[1:]),
                _layer(l, (1, LANES))]
    n_in = len(in_specs)
    return pl.pallas_call(
        _inproj_kernel,
        grid=(B, L // tm),
        in_specs=in_specs + [pl.BlockSpec(memory_space=pl.ANY)] * len(kv_prev),
        out_specs=out_specs,
        out_shape=out_shape,
        input_output_aliases={n_in + n: out for n, out in zip(range(len(kv_prev)), (1, 3))},
        scratch_shapes=[pltpu.VMEM((SUBLANES, LANES), f32)],
        compiler_params=_cparams(("parallel", "arbitrary")),
        name="prompt_inproj",
    )(x, g_all, wm_all, wf_all, bf_all, *kv_prev)


def _memkv_kernel(m_ref, g_ref, w_ref, k_ref, v_ref):
    mn = _rms(m_ref[...], g_ref[...]).astype(bf16)
    kv = _dot(mn, w_ref[...])
    k_ref[...] = kv[:, :W_MEM]
    v_ref[...] = kv[:, W_MEM:]


def _memkv(mem, g_all, w_all):
    B, N, D = mem.shape
    depth = w_all.shape[0]
    out = pl.BlockSpec((None, None, N, W_MEM), lambda l, b: (l, b, 0, 0))
    return pl.pallas_call(
        _memkv_kernel,
        grid=(depth, B),
        in_specs=[pl.BlockSpec((None, N, D), lambda l, b: (b, 0, 0)),
                  pl.BlockSpec((None, 1, D), lambda l, b: (l, 0, 0)),
                  pl.BlockSpec((None,) + w_all.shape[1:], lambda l, b: (l, 0, 0))],
        out_specs=(out, out),
        out_shape=(jax.ShapeDtypeStruct((depth, B, N, W_MEM), f32),) * 2,
        compiler_params=_cparams(("parallel", "parallel")),
        name="prompt_memkv",
    )(mem, g_all, w_all)


def _first_needed_block(kst_ref, h, i, n_diag):
    n_full = i * n_diag
    mine = pl.ds(n_full, n_diag)
    qn = jnp.max(kst_ref[mine, KST_QN + h, :], axis=0, keepdims=True)
    kn_own = jnp.max(kst_ref[mine, KST_KN + h, :], axis=0, keepdims=True)
    lb = -qn * kn_own - kst_ref[pl.ds(jnp.maximum(n_full - 1, 0), 1), KST_CEND + h, :]
    ub = qn * kst_ref[:, KST_KN + h, :] - kst_ref[:, KST_CEND + h, :]
    kj = lax.broadcasted_iota(jnp.int32, ub.shape, 0)
    needed = (ub - lb >= -SKIP_GAP) & (kj < n_full)
    return _full_min(jnp.where(needed, kj, n_full).astype(f32))


def _fox_kernel(q_ref, k_ref, v_ref, kst_ref, o_ref, acc_ref, m_ref, *, tq, tk):
    j = pl.program_id(1)
    i = pl.program_id(2)
    acc_ref[...] = jnp.zeros_like(acc_ref)
    m_ref[...] = jnp.full_like(m_ref, -jnp.inf)

    def step(kj, r0):
        k0 = pl.multiple_of(kj * tk, tk)
        rows = slice(0, tq) if r0 is None else slice(r0, tq)
        n = rows.stop - rows.start
        for e in range(2):
            s = _dot_nt(q_ref[e, rows, :], k_ref[e, pl.ds(k0, tk), :])
            if r0 is not None:
                row = r0 + lax.broadcasted_iota(jnp.int32, (n, tk), 0)
                col = r0 + lax.broadcasted_iota(jnp.int32, (n, tk), 1)
                s = jnp.where(col <= row, s, -jnp.inf)
            m_old = m_ref[e, rows, :]
            m_new = jnp.maximum(m_old, jnp.max(s, axis=1, keepdims=True))
            p = jnp.exp((s - jnp.tile(m_new, (1, tk // LANES))).astype(bf16))
            acc_ref[e, rows, :] = (jnp.exp(m_old - m_new) * acc_ref[e, rows, :]
                                   + _dot(p, v_ref[e, pl.ds(k0, tk), :]))
            m_ref[e, rows, :] = m_new

    n_diag = tq // tk
    first = jnp.minimum(*[_first_needed_block(kst_ref, 2 * j + e, i, n_diag) for e in range(2)])
    t_first = jnp.min(first).astype(jnp.int32) // n_diag

    def body(t, carry):
        for d in range(n_diag):
            step(t * n_diag + d, None)
        return carry

    lax.fori_loop(t_first, i, body, 0)
    for d in range(n_diag):
        step(i * n_diag + d, d * tk)
    lane = lax.broadcasted_iota(jnp.int32, (tq, LANES), 1)
    a0 = acc_ref[0]
    a1 = acc_ref[1]
    o_ref[...] = jnp.where(lane < HEAD_DIM, a0 / pltpu.roll(a0, HEAD_DIM, 1), pltpu.roll(a1, HEAD_DIM, 1) / a1)


def _fox(qa, ka, va, kst, tq, tk):
    B, _, L, _ = qa.shape
    n_pairs = W_FOX // LANES
    assert kst.shape[1] == L // tk
    return pl.pallas_call(
        functools.partial(_fox_kernel, tq=tq, tk=tk),
        grid=(B, n_pairs, L // tq),
        in_specs=[pl.BlockSpec((None, 2, tq, LANES), lambda b, j, i: (b, j, i, 0)),
                  pl.BlockSpec((None, 2, L, LANES), lambda b, j, i: (b, j, 0, 0)),
                  pl.BlockSpec((None, 2, L, LANES), lambda b, j, i: (b, j, 0, 0)),
                  pl.BlockSpec((None,) + kst.shape[1:], lambda b, j, i: (b, 0, 0, 0))],
        out_specs=pl.BlockSpec((None, tq, LANES), lambda b, j, i: (b, i, j)),
        out_shape=jax.ShapeDtypeStruct((B, L, W_FOX), f32),
        scratch_shapes=[pltpu.VMEM((2, tq, LANES), f32), pltpu.VMEM((2, tq, LANES), f32)],
        compiler_params=_cparams(("parallel", "parallel", "arbitrary")),
        name="prompt_fox",
    )(qa, ka, va, kst)


def _s5prep_kernel(lr_ref, li_ref, ldt_ref, bre_ref, bim_ref, cre_ref, cim_ref,
                   lamb_ref, bmat_ref, cmat_ref, lpow_ref, lams_ref, pw_ref, *, seg_len):
    lr = lr_ref[...]
    li = li_ref[...]
    dt = jnp.exp(ldt_ref[...])
    ar = lr * dt
    ai = li * dt
    mag = jnp.exp(ar)
    lbr = mag * jnp.cos(ai)
    lbi = mag * jnp.sin(ai)
    lamb_ref[:, :HALF] = jnp.broadcast_to(lbr, (SUBLANES, HALF))
    lamb_ref[:, HALF:] = jnp.broadcast_to(lbi, (SUBLANES, HALF))
    den = lr * lr + li * li
    zr = ((lbr - 1.0) * lr + lbi * li) / den
    zi = (lbi * lr - (lbr - 1.0) * li) / den
    bre = bre_ref[...]
    bim = bim_ref[...]
    bmat_ref[:, :HALF] = zr * bre - zi * bim
    bmat_ref[:, HALF:] = zr * bim + zi * bre
    cmat_ref[:HALF, :] = cre_ref[...]
    cmat_ref[HALF:, :] = -cim_ref[...]
    jj = (lax.broadcasted_iota(jnp.int32, (SUBLANES, HALF), 0) + 1).astype(f32)
    magp = jnp.exp(ar * jj)
    pw_ref[:SUBLANES, :HALF] = magp * jnp.cos(ai * jj)
    pw_ref[:SUBLANES, HALF:] = magp * jnp.sin(ai * jj)
    n = SUBLANES
    while n < seg_len:
        sr = pw_ref[n - 1:n, :HALF]
        si = pw_ref[n - 1:n, HALF:]
        pr = pw_ref[:n, :HALF]
        pi = pw_ref[:n, HALF:]
        pw_ref[n:2 * n, :HALF] = pr * sr - pi * si
        pw_ref[n:2 * n, HALF:] = pr * si + pi * sr
        n *= 2
    lpow_ref[...] = pw_ref[...].astype(bf16)
    lams_ref[...] = pw_ref[seg_len - SUBLANES:, :]


def _s5prep(lam_re, lam_im, log_dt, b_re, b_im, c_re, c_im, seg_len):
    depth = lam_re.shape[0]
    assert seg_len >= SUBLANES and seg_len & (seg_len - 1) == 0
    nb, gb = N_SSM_BLOCKS, GROUPS_PER_BLOCK
    eye = jnp.eye(gb, dtype=f32)
    half = lambda a: a.reshape(depth, nb, 1, HALF)
    ldt = jnp.broadcast_to(log_dt[:, :, None], (depth, N_SSM_GROUPS, SSM_STATE))

    def embed_b(b):
        t = b.reshape(depth, nb, gb, SSM_STATE, SSM_GROUP).transpose(0, 1, 2, 4, 3)
        return (t[:, :, :, :, None, :] * eye[None, None, :, None, :, None]).reshape(depth, nb, LANES, HALF)

    def embed_c(c):
        t = c.reshape(depth, nb, gb, SSM_GROUP, SSM_STATE).transpose(0, 1, 2, 4, 3)
        return (t[:, :, :, :, None, :] * eye[None, None, :, None, :, None]).reshape(depth, nb, HALF, LANES)

    blk = lambda shape: pl.BlockSpec((None, None) + shape, lambda l, cb: (l, cb) + (0,) * len(shape))
    arr = lambda shape, dt: jax.ShapeDtypeStruct((depth, nb) + shape, dt)
    return pl.pallas_call(
        functools.partial(_s5prep_kernel, seg_len=seg_len),
        grid=(depth, nb),
        in_specs=[blk((1, HALF))] * 3 + [blk((LANES, HALF))] * 2 + [blk((HALF, LANES))] * 2,
        out_specs=(blk((SUBLANES, STATE_W)), blk((LANES, STATE_W)), blk((STATE_W, LANES)),
                   blk((seg_len, STATE_W)), blk((SUBLANES, STATE_W))),
        out_shape=(arr((SUBLANES, STATE_W), f32),
                   arr((LANES, STATE_W), f32),
                   arr((STATE_W, LANES), f32),
                   arr((seg_len, STATE_W), bf16),
                   arr((SUBLANES, STATE_W), f32)),
        scratch_shapes=[pltpu.VMEM((seg_len, STATE_W), f32)],
        compiler_params=_cparams(("parallel", "parallel")),
        name="s5_tables",
    )(half(lam_re), half(lam_im), half(ldt), embed_b(b_re), embed_b(b_im), embed_c(c_re), embed_c(c_im))


def _s5scan_kernel(u_ref, bmat_ref, cmat_ref, lamb_ref, dsk_ref, y_ref, hend_ref,
                   up_ref, hb_ref, hq_ref, hcar_ref, *, tj):
    j = pl.program_id(1)
    nb = N_SSM_BLOCKS

    @pl.when(j == 0)
    def _():
        hcar_ref[...] = jnp.zeros_like(hcar_ref)

    for cb in range(nb):
        for s in range(N_SEG):
            up_ref[cb, pl.ds(s, tj, stride=N_SEG), :] = u_ref[s, :, LANES * cb:LANES * (cb + 1)]
        hb_ref[cb] = _dot(up_ref[cb].astype(bf16), bmat_ref[cb])

    def rec(t2, h):
        r0 = pl.multiple_of(t2 * 2 * N_SEG, 2 * N_SEG)
        new = []
        for cb in range(nb):
            hr, hi = h[2 * cb], h[2 * cb + 1]
            lr = lamb_ref[cb, :, :HALF]
            li = lamb_ref[cb, :, HALF:]
            rs, is_ = [], []
            for d in range(2):
                bu = hb_ref[cb, pl.ds(r0 + d * N_SEG, N_SEG), :]
                hr, hi = lr * hr - li * hi + bu[:, :HALF], lr * hi + li * hr + bu[:, HALF:]
                rs.append(hr)
                is_.append(hi)
            hq_ref[cb, pl.ds(r0, 2 * N_SEG), :HALF] = jnp.concatenate(rs, axis=0).astype(bf16)
            hq_ref[cb, pl.ds(r0, 2 * N_SEG), HALF:] = jnp.concatenate(is_, axis=0).astype(bf16)
            new += [hr, hi]
        return tuple(new)

    h0 = tuple(hcar_ref[cb, :, HALF * part:HALF * (part + 1)] for cb in range(nb) for part in range(2))
    h = lax.fori_loop(0, tj // 2, rec, h0, unroll=True)
    for cb in range(nb):
        hcar_ref[cb, :, :HALF] = h[2 * cb]
        hcar_ref[cb, :, HALF:] = h[2 * cb + 1]
        up_ref[cb] = _dot(hq_ref[cb], cmat_ref[cb]) + dsk_ref[cb] * up_ref[cb]
        for s in range(N_SEG):
            y_ref[s, :, LANES * cb:LANES * (cb + 1)] = up_ref[cb, pl.ds(s, tj, stride=N_SEG), :]

    @pl.when(j == pl.num_programs(1) - 1)
    def _():
        hend_ref[...] = hcar_ref[...]


def _s5scan(l, u, bmat_all, cmat_all, lamb_all, dsk_all, tj):
    B, L, _ = u.shape
    seg_len = L // N_SEG
    u4 = u.reshape(B, N_SEG, seg_len, W_SSM)
    nb = N_SSM_BLOCKS
    tile = pl.BlockSpec((None, N_SEG, tj, W_SSM), lambda b, j: (b, 0, j, 0))
    y4, hend = pl.pallas_call(
        functools.partial(_s5scan_kernel, tj=tj),
        grid=(B, seg_len // tj),
        in_specs=[tile, _layer(l, bmat_all.shape[1:]), _layer(l, cmat_all.shape[1:]),
                  _layer(l, lamb_all.shape[1:]), _layer(l, dsk_all.shape[1:])],
        out_specs=(tile, pl.BlockSpec((None, nb, N_SEG, STATE_W), lambda b, j: (b, 0, 0, 0))),
        out_shape=(jax.ShapeDtypeStruct((B, N_SEG, seg_len, W_SSM), f32),
                   jax.ShapeDtypeStruct((B, nb, N_SEG, STATE_W), f32)),
        scratch_shapes=[pltpu.VMEM((nb, N_SEG * tj, LANES), f32), pltpu.VMEM((nb, N_SEG * tj, STATE_W), f32),
                        pltpu.VMEM((nb, N_SEG * tj, STATE_W), bf16), pltpu.VMEM((nb, N_SEG, STATE_W), f32)],
        compiler_params=_cparams(("parallel", "arbitrary")),
        name="prompt_s5_scan",
    )(u4, bmat_all, cmat_all, lamb_all, dsk_all)
    return y4.reshape(B, L, W_SSM), hend


def _s5fix_kernel(hend_ref, lams_ref, cmat_ref, hlast_ref, cp_ref, hs_ref):
    he = hend_ref[...]
    lsr = lams_ref[SUBLANES - 1:, :HALF]
    lsi = lams_ref[SUBLANES - 1:, HALF:]
    hs_ref[...] = jnp.zeros_like(hs_ref)
    hr = jnp.zeros((1, HALF), f32)
    hi = jnp.zeros((1, HALF), f32)
    for s in range(N_SEG):
        hs_ref[s:s + 1, :HALF] = hr
        hs_ref[s:s + 1, HALF:] = hi
        nr = lsr * hr - lsi * hi + he[s:s + 1, :HALF]
        ni = lsr * hi + lsi * hr + he[s:s + 1, HALF:]
        hr, hi = nr, ni
    hlast_ref[:, :HALF] = hr
    hlast_ref[:, HALF:] = hi
    hst_r = hs_ref[:, :HALF].T
    hst_i = hs_ref[:, HALF:].T
    cre = cmat_ref[:HALF, :]
    mci = cmat_ref[HALF:, :]
    for s in range(N_SEG):
        col_r = hst_r[:, s:s + 1]
        col_i = hst_i[:, s:s + 1]
        cp_ref[s, :HALF, :] = (cre * col_r + mci * col_i).astype(bf16)
        cp_ref[s, HALF:, :] = (mci * col_r - cre * col_i).astype(bf16)


def _s5fix(l, hend, lams_all, cmat_all):
    B, nb = hend.shape[:2]
    return pl.pallas_call(
        _s5fix_kernel,
        grid=(B, nb),
        in_specs=[pl.BlockSpec((None, None, N_SEG, STATE_W), lambda b, cb: (b, cb, 0, 0)),
                  pl.BlockSpec((None, None, SUBLANES, STATE_W), lambda b, cb: (l, cb, 0, 0)),
                  pl.BlockSpec((None, None, STATE_W, LANES), lambda b, cb: (l, cb, 0, 0))],
        out_specs=(pl.BlockSpec((None, None, 1, STATE_W), lambda b, cb: (b, cb, 0, 0)),
                   pl.BlockSpec((None, N_SEG, None, STATE_W, LANES), lambda b, cb: (b, 0, cb, 0, 0))),
        out_shape=(jax.ShapeDtypeStruct((B, nb, 1, STATE_W), f32),
                   jax.ShapeDtypeStruct((B, N_SEG, nb, STATE_W, LANES), bf16)),
        scratch_shapes=[pltpu.VMEM((LANES, STATE_W), f32)],
        compiler_params=_cparams(("parallel", "parallel")),
        name="prompt_s5_fix",
    )(hend, lams_all, cmat_all)


def _pair_attend(q, k, v):
    lane = lax.broadcasted_iota(jnp.int32, q.shape, 1)
    zero = jnp.zeros_like(q)
    outs = []
    for e in range(2):
        qe = jnp.where((lane < HEAD_DIM) == (e == 0), q, zero)
        s = _dot_nt(qe, k)
        p = jnp.exp(s - jnp.max(s, axis=1, keepdims=True))
        outs.append(_dot(p.astype(bf16), v) / jnp.sum(p, axis=1, keepdims=True))
    return jnp.where(lane < HEAD_DIM, outs[0], outs[1])


N_OUTPROJ_INPUTS = 13


def _outproj_stages(x_ref, fox_ref, y_ref, lp_ref, cp_ref, qm_ref, mk_ref, mv_ref, sg_ref,
                    wglu_ref, bglu_ref, wout_ref, gfin_ref, o_ref, *, final, tiles_per_seg):
    tm = x_ref.shape[0]
    j0 = pl.multiple_of((pl.program_id(1) % tiles_per_seg) * tm, tm)
    corr = [_dot(lp_ref[cb, pl.ds(j0, tm), :], cp_ref[cb]) for cb in range(N_SSM_BLOCKS)]
    ssm_y = y_ref[...] + jnp.concatenate(corr, axis=1)
    yield
    s = _gelu(ssm_y)
    s = s * jax.nn.sigmoid(_dot(s.astype(bf16), wglu_ref[...]) + bglu_ref[...])
    yield
    qm = qm_ref[...]
    mk = mk_ref[...].astype(bf16)
    mv = mv_ref[...].astype(bf16)
    mem = []
    for jp in range(W_MEM // LANES):
        mem.append(_pair_attend(qm[:, LANES * jp:LANES * (jp + 1)], mk[:, LANES * jp:LANES * (jp + 1)],
                                mv[:, LANES * jp:LANES * (jp + 1)]))
        yield
    sg = sg_ref[...].astype(f32)
    mix = jnp.concatenate([fox_ref[...] * sg[:, :W_FOX],
                           s * sg[:, W_FOX:W_FOX + W_SSM],
                           jnp.concatenate(mem, axis=1) * sg[:, W_FOX + W_SSM:]], axis=1)
    xo = x_ref[...] + _dot(mix.astype(bf16), wout_ref[...])
    yield
    if final:
        xo = _rms(xo, gfin_ref[...])
    o_ref[...] = xo


def _s_inproj_kernel(x_ref, g_ref, w_ref, z_ref):
    xn = _rms(x_ref[...], g_ref[...])
    z_ref[...] = _dot3(xn, w_ref[...])


def _s_inproj(l, xs, g_all, wcat_all, tn):
    R, D = xs.shape
    return pl.pallas_call(
        _s_inproj_kernel,
        grid=(W_CAT // tn,),
        in_specs=[pl.BlockSpec((R, D), lambda n: (0, 0)), _layer(l, (1, D)),
                  pl.BlockSpec((None, D, tn), lambda n: (l, 0, n))],
        out_specs=pl.BlockSpec((R, tn), lambda n: (0, n)),
        out_shape=jax.ShapeDtypeStruct((R, W_CAT), f32),
        compiler_params=_cparams(("parallel",)),
        name="sample_inproj",
    )(xs, g_all, wcat_all)


def _full_max(x):
    return jnp.max(jnp.max(x, axis=0, keepdims=True), axis=1, keepdims=True)


def _full_min(x):
    return jnp.min(jnp.min(x, axis=0, keepdims=True), axis=1, keepdims=True)


def _full_sum(x):
    return jnp.sum(jnp.sum(x, axis=0, keepdims=True), axis=1, keepdims=True)


def _lane_sums(acc):
    ones = jnp.ones((SUBLANES, LANES), bf16)
    hi, lo = _split2(acc)
    return (_dot_nt(ones, hi) + _dot_nt(ones, lo))[0:1, :]


def _col_bcast(row):
    return jnp.broadcast_to(row, (LANES, row.shape[1])).T


def _outproj_decode_kernel(pt_ref, *refs, n_pages, n_seq, final, tiles_per_seg):
    per_seq = 2 * n_pages + 2
    op_in = refs[:N_OUTPROJ_INPUTS]
    z_ref, bf_ref, lfc_ref = refs[N_OUTPROJ_INPUTS:N_OUTPROJ_INPUTS + 3]
    ins = refs[N_OUTPROJ_INPUTS + 3:N_OUTPROJ_INPUTS + 3 + n_seq * per_seq]
    (o_ref, fox_ref, mem_ref, lf_ref,
     s_scr, lfs_scr, p_scr, acc_scr, accm_scr) = refs[N_OUTPROJ_INPUTS + 3 + n_seq * per_seq:]
    step = pl.program_id(0) * pl.num_programs(1) + pl.program_id(1)
    chains = [_outproj_stages(*op_in, o_ref, final=final, tiles_per_seg=tiles_per_seg)]
    for i in range(n_seq):
        mine = ins[i * per_seq:(i + 1) * per_seq]
        chains.append(_decode_one(
            pt_ref, step * n_seq + i, z_ref.at[i], bf_ref, lfc_ref,
            mine[:n_pages], mine[n_pages:2 * n_pages], mine[2 * n_pages], mine[2 * n_pages + 1],
            fox_ref.at[i], mem_ref.at[i], lf_ref.at[i],
            s_scr.at[i], lfs_scr.at[i], p_scr.at[i], acc_scr.at[i], accm_scr.at[i]))
    while chains:
        chains = [c for c in chains if next(c, _DONE) is not _DONE]


_DONE = object()


def _decode_one(pt_ref, r, z_ref, bf_ref, lfc_ref, ck, cv, mk_ref, mv_ref, fox_ref, mem_ref, lf_ref,
                s_scr, lfs_scr, p_scr, acc_scr, accm_scr):
    n_pages = len(ck)
    z = z_ref[...]
    q = z[:, OFF_Q:OFF_Q + W_FOX] * QK_SCALE
    k_new = z[:, OFF_K:OFF_K + W_FOX]
    v_new = z[:, OFF_V:OFF_V + W_FOX]
    lf_new = _log_sigmoid(z[:, OFF_F:OFF_F + LANES] + bf_ref[...])
    lf_ref[...] = lf_new

    qt = _col_bcast(q)
    for h in range(N_FOX_HEADS):
        qh = qt[HEAD_DIM * h:HEAD_DIM * (h + 1), :]
        for p in range(n_pages):
            row = h * n_pages + p
            s_scr[row:row + 1, :] = jnp.sum(ck[p][h] * qh, axis=0, keepdims=True)
            lfs_scr[row:row + 1, :] = lfc_ref[h, pl.ds(pt_ref[r, p], 1), :]
        yield

    n_rows = N_FOX_HEADS * n_pages
    ur = lax.broadcasted_iota(jnp.int32, (LANES, LANES), 0)
    uc = lax.broadcasted_iota(jnp.int32, (LANES, LANES), 1)
    upper = jnp.where(ur <= uc, 1.0, 0.0).astype(bf16)
    cs = sum(_dot(pc, upper) for pc in _split3(lfs_scr[...]))
    tot = jnp.broadcast_to(cs[:, LANES - 1:], (n_rows, LANES))
    mr = lax.broadcasted_iota(jnp.int32, (n_rows, n_rows), 0)
    mc = lax.broadcasted_iota(jnp.int32, (n_rows, n_rows), 1)
    before = jnp.where((mc < mr) & (mc // n_pages == mr // n_pages), 1.0, 0.0).astype(bf16)
    c = cs + sum(_dot(before, pc) for pc in _split3(tot))
    s_all = s_scr[...] - c
    yield

    qk_new = q * k_new
    lane_head = lax.broadcasted_iota(jnp.int32, (1, W_FOX), 1) // HEAD_DIM
    pn_exp = jnp.zeros((1, W_FOX), f32)
    l_exp = jnp.zeros((1, W_FOX), f32)
    for h in range(N_FOX_HEADS):
        rows = slice(h * n_pages, (h + 1) * n_pages)
        s_h = s_all[rows, :]
        c_new = c[(h + 1) * n_pages - 1:(h + 1) * n_pages, LANES - 1:] + lf_new[:, h:h + 1]
        s_new = jnp.sum(qk_new[:, HEAD_DIM * h:HEAD_DIM * (h + 1)], axis=1, keepdims=True) - c_new
        m = jnp.maximum(_full_max(s_h), s_new)
        p_h = jnp.exp(s_h - m)
        pn = jnp.exp(s_new - m)
        p_scr[rows, :] = p_h
        pn_exp = jnp.where(lane_head == h, pn, pn_exp)
        l_exp = jnp.where(lane_head == h, _full_sum(p_h) + pn, l_exp)
        acc = jnp.zeros((HEAD_DIM, LANES), f32)
        for p in range(n_pages):
            acc = acc + cv[p][h] * p_scr[h * n_pages + p:h * n_pages + p + 1, :]
        acc_scr[HEAD_DIM * h:HEAD_DIM * (h + 1), :] = acc
        yield
    fox_ref[...] = (_lane_sums(acc_scr[...]) + pn_exp * v_new) / l_exp
    yield

    qm = z[:, OFF_QM:OFF_QM + W_MEM] * QK_SCALE
    qmt = _col_bcast(qm)
    n_halves = mk_ref.shape[2] // LANES
    mem_head = lax.broadcasted_iota(jnp.int32, (1, W_MEM), 1) // HEAD_DIM
    lm_exp = jnp.zeros((1, W_MEM), f32)
    for h in range(N_MEM_HEADS):
        qh = qmt[HEAD_DIM * h:HEAD_DIM * (h + 1), :]
        ss = [jnp.sum(mk_ref[h, :, LANES * t:LANES * (t + 1)] * qh, axis=0, keepdims=True) for t in range(n_halves)]
        m = functools.reduce(jnp.maximum, [jnp.max(s, axis=1, keepdims=True) for s in ss])
        ps = [jnp.exp(s - m) for s in ss]
        lm = sum(jnp.sum(p, axis=1, keepdims=True) for p in ps)
        lm_exp = jnp.where(mem_head == h, lm, lm_exp)
        accm_scr[HEAD_DIM * h:HEAD_DIM * (h + 1), :] = sum(
            mv_ref[h, :, LANES * t:LANES * (t + 1)] * ps[t] for t in range(n_halves))
    mem_ref[...] = _lane_sums(accm_scr[...]) / lm_exp


def _outproj_decode(l, x, fox_o, y, lpow_all, cp, qm, mk_all, mv_all, sg, wglu_all, bglu_all, wout_all, gfin,
                    page_table, z, bf_all, ckt, cvt, lft, mkt, mvt, tm, final):
    B, L, D = x.shape
    seg_len = L // N_SEG
    tiles_per_seg = seg_len // tm
    n_i = L // tm
    R, n_pages = page_table.shape
    n_pool, page = ckt.shape[1], ckt.shape[4]
    n_mem = mkt.shape[4]
    assert page == LANES and n_mem % LANES == 0 and R % (B * n_i) == 0
    n_seq = R // (B * n_i)
    nb = N_SSM_BLOCKS
    row = lambda w: pl.BlockSpec((None, tm, w), lambda b, i, pt: (b, i, 0))
    mem_kv = pl.BlockSpec((None, None) + mk_all.shape[2:], lambda b, i, pt: (l, b, 0, 0))
    per_seq = lambda w: pl.BlockSpec((n_seq, 1, w), lambda b, i, pt: (b * n_i + i, 0, 0))
    in_specs = [row(D), row(W_FOX), row(W_SSM), _layer(l, lpow_all.shape[1:]),
                pl.BlockSpec((None, None, nb, STATE_W, LANES), lambda b, i, pt: (b, i // tiles_per_seg, 0, 0, 0)),
                row(W_MEM), mem_kv, mem_kv, row(D),
                _layer(l, wglu_all.shape[1:]), _layer(l, (1, W_SSM)), _layer(l, wout_all.shape[1:]),
                pl.BlockSpec((1, D), lambda b, i, pt: (0, 0)),
                per_seq(W_CAT), _layer(l, (1, LANES)), _layer(l, (N_FOX_HEADS, n_pool, page))]
    assert len(in_specs) == N_OUTPROJ_INPUTS + 3
    operands = [x, fox_o, y, lpow_all, cp, qm, mk_all, mv_all, sg, wglu_all, bglu_all, wout_all, gfin,
                z.reshape(R, 1, W_CAT), bf_all, lft]
    for s in range(n_seq):
        seq = lambda b, i, s=s: (b * n_i + i) * n_seq + s
        paged = [pl.BlockSpec((None, None, N_FOX_HEADS, HEAD_DIM, page),
                              lambda b, i, pt, seq=seq, p=p: (l, pt[seq(b, i), p], 0, 0, 0)) for p in range(n_pages)]
        mem_spec = pl.BlockSpec((None, None, N_MEM_HEADS, HEAD_DIM, n_mem),
                                lambda b, i, pt, seq=seq: (l, seq(b, i), 0, 0, 0))
        in_specs += paged + paged + [mem_spec, mem_spec]
        operands += [ckt] * n_pages + [cvt] * n_pages + [mkt, mvt]
    n_rows = N_FOX_HEADS * n_pages
    scr = lambda rows: pltpu.VMEM((n_seq, rows, LANES), f32)
    xo, fox, mem, lf = pl.pallas_call(
        functools.partial(_outproj_decode_kernel, n_pages=n_pages, n_seq=n_seq, final=final,
                          tiles_per_seg=tiles_per_seg),
        grid_spec=pltpu.PrefetchScalarGridSpec(
            num_scalar_prefetch=1,
            grid=(B, n_i),
            in_specs=in_specs,
            out_specs=(row(D), per_seq(W_FOX), per_seq(W_MEM), per_seq(LANES)),
            scratch_shapes=[scr(n_rows), scr(n_rows), scr(n_rows), scr(W_FOX), scr(W_MEM)]),
        out_shape=(jax.ShapeDtypeStruct((B, L, D), f32), jax.ShapeDtypeStruct((R, 1, W_FOX), f32),
                   jax.ShapeDtypeStruct((R, 1, W_MEM), f32), jax.ShapeDtypeStruct((R, 1, LANES), f32)),
        compiler_params=_cparams(("parallel", "parallel")),
        name="outproj_decode",
    )(page_table, *operands)
    return xo, fox.reshape(R, W_FOX), mem.reshape(R, W_MEM), lf.reshape(R, LANES)


def _s_out_kernel(x_ref, z_ref, fox_ref, mem_ref, h0_ref, bmat_ref, cmat_ref, lamb_ref, dsk_ref,
                  wglu_ref, bglu_ref, wout_ref, gfin_ref, o_ref, h_ref, *, final):
    z = z_ref[...]
    u = z[:, OFF_U:OFF_U + W_SSM]
    ys = []
    for cb in range(N_SSM_BLOCKS):
        ucb = u[:, LANES * cb:LANES * (cb + 1)]
        bu = _dot3(ucb, bmat_ref[cb])
        lr = lamb_ref[cb, 0:1, :HALF]
        li = lamb_ref[cb, 0:1, HALF:]
        h0r = h0_ref[:, STATE_W * cb:STATE_W * cb + HALF]
        h0i = h0_ref[:, STATE_W * cb + HALF:STATE_W * (cb + 1)]
        hr = lr * h0r - li * h0i + bu[:, :HALF]
        hi = lr * h0i + li * h0r + bu[:, HALF:]
        h_ref[:, STATE_W * cb:STATE_W * cb + HALF] = hr
        h_ref[:, STATE_W * cb + HALF:STATE_W * (cb + 1)] = hi
        ys.append(_dot3(jnp.concatenate([hr, hi], axis=1), cmat_ref[cb]) + dsk_ref[cb] * ucb)
    s = _gelu(jnp.concatenate(ys, axis=1))
    s = s * jax.nn.sigmoid(_dot3(s, wglu_ref[...]) + bglu_ref[...])
    sg = _silu(z[:, OFF_G:W_MAIN])
    mix = jnp.concatenate([fox_ref[...] * sg[:, :W_FOX],
                           s * sg[:, W_FOX:W_FOX + W_SSM],
                           mem_ref[...] * sg[:, W_FOX + W_SSM:]], axis=1)
    xo = x_ref[...] + _dot3(mix, wout_ref[...])
    if final:
        xo = _rms(xo, gfin_ref[...])
    o_ref[...] = xo


def _s_out(l, xs, z, fox, mem, h0_all, bmat_all, cmat_all, lamb_all, dsk_all, wglu_all, bglu_all, wout_all,
           gfin, final):
    R, D = xs.shape
    whole = lambda a: pl.BlockSpec(a.shape, lambda i: (0,) * a.ndim)
    hspec = pl.BlockSpec((R, h0_all.shape[2]), lambda i: (0, 0))
    return pl.pallas_call(
        functools.partial(_s_out_kernel, final=final),
        grid=(1,),
        in_specs=[whole(xs), whole(z), whole(fox), whole(mem), _layer(l, h0_all.shape[1:]),
                  _layer(l, bmat_all.shape[1:]), _layer(l, cmat_all.shape[1:]), _layer(l, lamb_all.shape[1:]),
                  _layer(l, dsk_all.shape[1:]), _layer(l, wglu_all.shape[1:]), _layer(l, (1, W_SSM)),
                  _layer(l, wout_all.shape[1:]), whole(gfin)],
        out_specs=(whole(xs), hspec),
        out_shape=(jax.ShapeDtypeStruct((R, D), f32), jax.ShapeDtypeStruct(h0_all.shape[1:], f32)),
        compiler_params=_cparams(("arbitrary",)),
        name="sample_out",
    )(xs, z, fox, mem, h0_all, bmat_all, cmat_all, lamb_all, dsk_all, wglu_all, bglu_all, wout_all, gfin)


def _state_to_cols(h):
    lead = h.shape[:-3]
    n = len(lead)
    t = h.reshape(lead + (N_SSM_BLOCKS, GROUPS_PER_BLOCK, SSM_STATE, 2))
    t = t.transpose(tuple(range(n)) + (n, n + 3, n + 1, n + 2))
    return t.reshape(lead + (N_SSM_BLOCKS * STATE_W,))


def _cols_to_state(c):
    lead = c.shape[:-1]
    n = len(lead)
    t = c.reshape(lead + (N_SSM_BLOCKS, 2, GROUPS_PER_BLOCK, SSM_STATE))
    t = t.transpose(tuple(range(n)) + (n, n + 2, n + 3, n + 1))
    return t.reshape(lead + (N_SSM_GROUPS, SSM_STATE, 2))


def _tile(n, pref):
    t = min(n, pref)
    assert n % t == 0
    return t


def kernel(x_prompt, x_sample, mem_prompt, cache_k, cache_v, cache_logf, state_ssm, cache_mem_k, cache_mem_v,
           page_table, g_norm, w_in, b_f, lam_re, lam_im, log_dt, b_re, b_im, c_re, c_im, d_skip, w_glu, b_glu,
           g_mem, w_mem_kv, w_out, g_final):
    depth = w_in.shape[0]
    B, L, D = x_prompt.shape
    R = x_sample.shape[0]
    assert x_sample.shape[1] == 1 and L % N_SEG == 0
    seg_len = L // N_SEG
    tm = _tile(seg_len, 256)
    tq = _tile(L, 1024)
    tk = _tile(tq, 512)
    tj = _tile(seg_len, 128)

    seg = dict(zip(("q", "k", "v", "f", "ga", "u", "gs", "qm", "gm"),
                   (w_in[:, :, a:b] for a, b in zip(IN_CUTS[:-1], IN_CUTS[1:]))))
    wcat_all = jnp.concatenate(
        [seg[n] for n in ("q", "k", "v", "u", "qm", "ga", "gs", "gm", "f")]
        + [jnp.zeros((depth, D, W_CAT - W_MAIN - N_FOX_HEADS), f32)], axis=2)
    wm_all = wcat_all[:, :, :W_MAIN].astype(bf16)
    wf = wcat_all[:, :, OFF_F:OFF_F + LANES]
    wfh = wf.astype(bf16)
    wf_all = jnp.concatenate([wfh, (wf - wfh.astype(f32)).astype(bf16)], axis=2)
    bf_all = jnp.pad(b_f, ((0, 0), (0, LANES - N_FOX_HEADS))).reshape(depth, 1, LANES)
    gn_all = g_norm.reshape(depth, 1, D)
    gmem_all = g_mem.reshape(depth, 1, D)
    bglu_all = b_glu.reshape(depth, 1, W_SSM)
    gfin = g_final.reshape(1, D)
    dsk_all = d_skip.reshape(depth, N_SSM_BLOCKS, 1, LANES)
    wglu_bf = w_glu.astype(bf16)
    wout_bf = w_out.astype(bf16)
    ckt = jnp.transpose(cache_k, (0, 1, 3, 4, 2))
    cvt = jnp.transpose(cache_v, (0, 1, 3, 4, 2))
    lft = jnp.transpose(cache_logf, (0, 3, 1, 2))
    mkt = jnp.transpose(cache_mem_k, (0, 1, 3, 4, 2))
    mvt = jnp.transpose(cache_mem_v, (0, 1, 3, 4, 2))
    h0_all = _state_to_cols(state_ssm)

    lamb_all, bmat_all, cmat_all, lpow_all, lams_all = _s5prep(lam_re, lam_im, log_dt, b_re, b_im, c_re, c_im, seg_len)
    bmat_bf = bmat_all.astype(bf16)
    cmat_bf = cmat_all.astype(bf16)
    mk_all, mv_all = _memkv(mem_prompt, gmem_all, w_mem_kv.astype(bf16))

    xp = x_prompt
    xs = x_sample.reshape(R, D)
    outs = {n: [] for n in ("lfp", "hp", "z", "lfs", "hs")}
    kv_prev = ()
    for l in range(depth):
        final = l == depth - 1
        qa, kt_all, ka, vt_all, va, kst, lf, sg, u, qm = _inproj(l, xp, gn_all, wm_all, wf_all, bf_all, kv_prev, tk)
        kv_prev = (kt_all, vt_all)
        fox_o = _fox(qa, ka, va, kst, tq, tk)
        y, hend = _s5scan(l, u, bmat_bf, cmat_bf, lamb_all, dsk_all, tj)
        hlast, cp = _s5fix(l, hend, lams_all, cmat_all)
        outs["lfp"].append(lf)
        outs["hp"].append(hlast.reshape(B, N_SSM_BLOCKS * STATE_W))

        z = _s_inproj(l, xs, gn_all, wcat_all, 256)
        xp, fox_s, mem_s, lf_s = _outproj_decode(
            l, xp, fox_o, y, lpow_all, cp, qm, mk_all, mv_all, sg, wglu_bf, bglu_all, wout_bf, gfin,
            page_table, z, bf_all, ckt, cvt, lft, mkt, mvt, tm, final)
        xs, h_new = _s_out(l, xs, z, fox_s, mem_s, h0_all, bmat_all, cmat_all, lamb_all, dsk_all,
                           w_glu, bglu_all, w_out, gfin, final)
        outs["z"].append(z)
        outs["lfs"].append(lf_s)
        outs["hs"].append(h_new)

    st = lambda n: jnp.stack(outs[n])
    z_all = st("z")
    n_mem = mk_all.shape[2]
    return (xp, xs.reshape(R, 1, D),
            kt_all.reshape(depth, B, N_FOX_HEADS, HEAD_DIM, L).transpose(0, 1, 4, 2, 3),
            vt_all.reshape(depth, B, N_FOX_HEADS, HEAD_DIM, L).transpose(0, 1, 4, 2, 3),
            st("lfp"),
            _cols_to_state(st("hp")),
            mk_all.reshape(depth, B, n_mem, N_MEM_HEADS, HEAD_DIM),
            mv_all.reshape(depth, B, n_mem, N_MEM_HEADS, HEAD_DIM),
            z_all[:, :, OFF_K:OFF_K + W_FOX].reshape(depth, R, 1, N_FOX_HEADS, HEAD_DIM),
            z_all[:, :, OFF_V:OFF_V + W_FOX].reshape(depth, R, 1, N_FOX_HEADS, HEAD_DIM),
            st("lfs")[:, :, :N_FOX_HEADS].reshape(depth, R, 1, N_FOX_HEADS),
            _cols_to_state(st("hs")))
```

```python
import functools

import jax
import jax.numpy as jnp
from jax import lax
from jax.experimental import pallas as pl
from jax.experimental.pallas import tpu as pltpu

f32 = jnp.float32
bf16 = jnp.bfloat16

HEAD_DIM = 64
N_FOX_HEADS = 6
W_FOX = N_FOX_HEADS * HEAD_DIM
SSM_GROUP = 16
N_SSM_GROUPS = 24
W_SSM = N_SSM_GROUPS * SSM_GROUP
SSM_STATE = 64
N_MEM_HEADS = 4
W_MEM = N_MEM_HEADS * HEAD_DIM
W_MIX = W_FOX + W_SSM + W_MEM
RMS_EPS = 1e-6
QK_SCALE = HEAD_DIM ** -0.5

LANES = 128
SUBLANES = 8
GROUPS_PER_BLOCK = LANES // SSM_GROUP
N_SSM_BLOCKS = N_SSM_GROUPS // GROUPS_PER_BLOCK
HALF = GROUPS_PER_BLOCK * SSM_STATE
STATE_W = 2 * HALF
N_SEG = SUBLANES
PIECE_PITCH = SUBLANES

OFF_Q, OFF_K, OFF_V, OFF_U, OFF_QM, OFF_G = 0, 384, 768, 1152, 1536, 1792
W_MAIN = OFF_G + W_MIX
OFF_F = W_MAIN
W_CAT = 3072
IN_CUTS = (0, 384, 768, 1152, 1158, 1542, 1926, 2310, 2566, 2822)

VMEM_LIMIT = 56 * 1024 * 1024
SEQS_PER_FOX_STEP = 2

SKIP_GAP = 110.0
NORM_SLACK = 1.02
KST_KN, KST_QN, KST_CEND, KST_ROWS = 0, SUBLANES, 2 * SUBLANES, 3 * SUBLANES


def _cparams(sem):
    return pltpu.CompilerParams(dimension_semantics=sem, vmem_limit_bytes=VMEM_LIMIT)


def _dot(a, b):
    return jnp.dot(a, b, preferred_element_type=f32)


def _dot_nt(a, b):
    return lax.dot_general(a, b, (((1,), (1,)), ((), ())), preferred_element_type=f32)


def _split2(a):
    hi = a.astype(bf16)
    lo = (a - hi.astype(f32)).astype(bf16)
    return hi, lo


def _split3(a):
    hi = a.astype(bf16)
    r = a - hi.astype(f32)
    mid = r.astype(bf16)
    lo = (r - mid.astype(f32)).astype(bf16)
    return hi, mid, lo


def _dot3(a, b):
    ah, al = _split2(a)
    bh, bl = _split2(b)
    return _dot(ah, bh) + _dot(al, bh) + _dot(ah, bl)


def _rms(x, g):
    ms = jnp.mean(x * x, axis=-1, keepdims=True)
    return x * lax.rsqrt(ms + RMS_EPS) * g


def _log_sigmoid(x):
    return jnp.minimum(x, 0.0) - jnp.log1p(jnp.exp(-jnp.abs(x)))


def _silu(x):
    return x * jax.nn.sigmoid(x)


def _gelu(x):
    return 0.5 * x * (1.0 + lax.erf(x * (2.0 ** -0.5)))


def _tri(n):
    r = lax.broadcasted_iota(jnp.int32, (n, n), 0)
    c = lax.broadcasted_iota(jnp.int32, (n, n), 1)
    return jnp.where(c <= r, 1.0, 0.0).astype(bf16)


def _pack_pieces(x):
    packed = None
    for i, pc in enumerate(_split3(x)):
        pc = pc.astype(f32)
        if i:
            pc = pltpu.roll(pc, PIECE_PITCH * i, 1)
        packed = pc if packed is None else packed + pc
    return packed.astype(bf16)


def _unpack_pieces(y):
    return y + pltpu.roll(y, LANES - PIECE_PITCH, 1) + pltpu.roll(y, LANES - 2 * PIECE_PITCH, 1)


def _layer(l, shape):
    return pl.BlockSpec((None,) + tuple(shape), lambda *_: (l,) + (0,) * len(shape),
                        pipeline_mode=pl.Buffered(1))


def _head_block(x, h):
    blk = x[:, LANES * (h // 2):LANES * (h // 2 + 1)]
    return blk if h % 2 == 0 else pltpu.roll(blk, HEAD_DIM, 1)


def _inproj_kernel(x_ref, g_ref, wm_ref, wf_ref, bf_ref, *rest):
    qa_ref, kt_ref, ka_ref, vt_ref, va_ref, kst_ref, lf_ref, sg_ref, u_ref, qm_ref, carry_ref = rest[-11:]
    i = pl.program_id(1)

    @pl.when(i == 0)
    def _():
        carry_ref[...] = jnp.zeros_like(carry_ref)

    xn = _rms(x_ref[...], g_ref[...])
    xh = xn.astype(bf16)
    tm = xn.shape[0]
    lane = lax.broadcasted_iota(jnp.int32, (tm, LANES), 1)
    low = lane < HEAD_DIM

    ff = _dot(xh, wf_ref[...])
    f = ff[:, :LANES] + ff[:, LANES:] + bf_ref[...]
    lf = jnp.where(lane < N_FOX_HEADS, _log_sigmoid(f), 0.0)
    lf_ref[...] = lf[:, :N_FOX_HEADS]
    cs = _unpack_pieces(_dot(_tri(tm), _pack_pieces(lf)))
    cs = jnp.where(lane < N_FOX_HEADS, cs, 0.0) + carry_ref[0:1, :]
    carry_ref[...] = jnp.broadcast_to(cs[tm - 1:, :], carry_ref.shape)
    prow = lax.broadcasted_iota(jnp.int32, (LANES, N_FOX_HEADS * LANES), 0)
    pcol = lax.broadcasted_iota(jnp.int32, (LANES, N_FOX_HEADS * LANES), 1)
    place = (pcol == (prow % PIECE_PITCH) * LANES + HEAD_DIM + prow // PIECE_PITCH) & (prow % PIECE_PITCH < N_FOX_HEADS)
    caug = _dot(_pack_pieces(-cs), jnp.where(place, 1.0, 0.0).astype(bf16))
    ones3 = jnp.where((lane >= HEAD_DIM) & (lane < HEAD_DIM + 3), 1.0, 0.0)

    qk = _dot(xh, wm_ref[:, OFF_Q:OFF_V])
    q = qk[:, :W_FOX] * QK_SCALE
    k = qk[:, W_FOX:]
    kt_ref[...] = k.T
    kb = k.astype(bf16).astype(f32)
    qb = q.astype(bf16).astype(f32)
    squares = jnp.concatenate([(kb * kb).astype(bf16), (qb * qb).astype(bf16)], axis=1)
    srow = lax.broadcasted_iota(jnp.int32, (2 * W_FOX, LANES), 0)
    scol = lax.broadcasted_iota(jnp.int32, (2 * W_FOX, LANES), 1)
    target = (srow // W_FOX) * SUBLANES + (srow % W_FOX) // HEAD_DIM
    sums = _dot(squares, jnp.where(scol == target, 1.0, 0.0).astype(bf16))
    norms = jnp.sqrt(jnp.max(sums, axis=0, keepdims=True)) * NORM_SLACK
    kst_ref[:KST_CEND, :] = _col_bcast(norms)[:KST_CEND, :]
    kst_ref[KST_CEND:, :] = _col_bcast(cs[tm - 1:, :])[:SUBLANES, :]
    vu = _dot(xh, wm_ref[:, OFF_V:OFF_QM])
    v = vu[:, :W_FOX]
    vt_ref[...] = v.T
    u_ref[...] = vu[:, W_FOX:]
    for h in range(N_FOX_HEADS):
        qa_ref[h] = jnp.where(low, _head_block(q, h), ones3).astype(bf16)
        ka_ref[h] = jnp.where(low, _head_block(k, h), caug[:, LANES * h:LANES * (h + 1)]).astype(bf16)
        va_ref[h] = jnp.where(low, _head_block(v, h), 1.0).astype(bf16)
    rest = _dot(xh, wm_ref[:, OFF_QM:W_MAIN])
    qm_ref[...] = (rest[:, :W_MEM] * QK_SCALE).astype(bf16)
    sg_ref[...] = _silu(rest[:, W_MEM:]).astype(bf16)


def _inproj(l, x, g_all, wm_all, wf_all, bf_all, kv_prev, tm):
    B, L, D = x.shape
    depth = wm_all.shape[0]
    row = lambda w: pl.BlockSpec((None, tm, w), lambda b, i: (b, i, 0))
    heads = pl.BlockSpec((None, N_FOX_HEADS, tm, LANES), lambda b, i: (b, 0, i, 0))
    slab = pl.BlockSpec((None, None, W_FOX, tm), lambda b, i: (l, b, 0, i))
    aug = jax.ShapeDtypeStruct((B, N_FOX_HEADS, L, LANES), bf16)
    kvt = jax.ShapeDtypeStruct((depth, B, W_FOX, L), f32)
    out_shape = (
        aug,
        kvt,
        aug,
        kvt,
        aug,
        jax.ShapeDtypeStruct((B, L // tm, KST_ROWS, LANES), f32),
        jax.ShapeDtypeStruct((B, L, N_FOX_HEADS), f32),
        jax.ShapeDtypeStruct((B, L, W_MIX), bf16),
        jax.ShapeDtypeStruct((B, L, W_SSM), f32),
        jax.ShapeDtypeStruct((B, L, W_MEM), bf16),
    )
    kst = pl.BlockSpec((None, None, KST_ROWS, LANES), lambda b, i: (b, i, 0, 0))
    out_specs = (heads, slab, heads, slab, heads, kst, row(N_FOX_HEADS), row(W_MIX), row(W_SSM), row(W_MEM))
    in_specs = [row(D), _layer(l, (1, D)), _layer(l, wm_all.shape[1:]), _layer(l, wf_all.shape[1:]),
                _layer(l, (1, LANES))]
    n_in = len(in_specs)
    return pl.pallas_call(
        _inproj_kernel,
        grid=(B, L // tm),
        in_specs=in_specs + [pl.BlockSpec(memory_space=pl.ANY)] * len(kv_prev),
        out_specs=out_specs,
        out_shape=out_shape,
        input_output_aliases={n_in + n: out for n, out in zip(range(len(kv_prev)), (1, 3))},
        scratch_shapes=[pltpu.VMEM((SUBLANES, LANES), f32)],
        compiler_params=_cparams(("parallel", "arbitrary")),
        name="prompt_inproj",
    )(x, g_all, wm_all, wf_all, bf_all, *kv_prev)


def _memkv_kernel(m_ref, g_ref, w_ref, k_ref, v_ref):
    mn = _rms(m_ref[...], g_ref[...]).astype(bf16)
    kv = _dot(mn, w_ref[...])
    k_ref[...] = kv[:, :W_MEM]
    v_ref[...] = kv[:, W_MEM:]


def _memkv(mem, g_all, w_all):
    B, N, D = mem.shape
    depth = w_all.shape[0]
    out = pl.BlockSpec((None, None, N, W_MEM), lambda l, b: (l, b, 0, 0))
    return pl.pallas_call(
        _memkv_kernel,
        grid=(depth, B),
        in_specs=[pl.BlockSpec((None, N, D), lambda l, b: (b, 0, 0)),
                  pl.BlockSpec((None, 1, D), lambda l, b: (l, 0, 0)),
                  pl.BlockSpec((None,) + w_all.shape[1:], lambda l, b: (l, 0, 0))],
        out_specs=(out, out),
        out_shape=(jax.ShapeDtypeStruct((depth, B, N, W_MEM), f32),) * 2,
        compiler_params=_cparams(("parallel", "parallel")),
        name="prompt_memkv",
    )(mem, g_all, w_all)


def _first_needed_block(kst_ref, h, i, n_diag):
    n_full = i * n_diag
    mine = pl.ds(n_full, n_diag)
    qn = jnp.max(kst_ref[mine, KST_QN + h, :], axis=0, keepdims=True)
    kn_own = jnp.max(kst_ref[mine, KST_KN + h, :], axis=0, keepdims=True)
    lb = -qn * kn_own - kst_ref[pl.ds(jnp.maximum(n_full - 1, 0), 1), KST_CEND + h, :]
    ub = qn * kst_ref[:, KST_KN + h, :] - kst_ref[:, KST_CEND + h, :]
    kj = lax.broadcasted_iota(jnp.int32, ub.shape, 0)
    needed = (ub - lb >= -SKIP_GAP) & (kj < n_full)
    return _full_min(jnp.where(needed, kj, n_full).astype(f32))


N_FOX_INPUTS = 4


def _fox_kernel(pt_ref, *refs, guest, tq, tk):
    q_ref, k_ref, v_ref, kst_ref = refs[:N_FOX_INPUTS]
    n_in = N_FOX_INPUTS + guest.n_in
    o_ref = refs[n_in]
    acc_ref, m_ref = refs[n_in + 1 + guest.N_OUT:n_in + 3 + guest.N_OUT]
    j = pl.program_id(1)
    i = pl.program_id(2)
    _run_chains(guest.chains(pt_ref, guest.step_of(pl.program_id(0), j, i), refs[N_FOX_INPUTS:n_in],
                             refs[n_in + 1:n_in + 1 + guest.N_OUT], refs[n_in + 3 + guest.N_OUT:]))
    acc_ref[...] = jnp.zeros_like(acc_ref)
    m_ref[...] = jnp.full_like(m_ref, -jnp.inf)

    def step(kj, r0):
        k0 = pl.multiple_of(kj * tk, tk)
        rows = slice(0, tq) if r0 is None else slice(r0, tq)
        n = rows.stop - rows.start
        for e in range(2):
            s = _dot_nt(q_ref[e, rows, :], k_ref[e, pl.ds(k0, tk), :])
            if r0 is not None:
                row = r0 + lax.broadcasted_iota(jnp.int32, (n, tk), 0)
                col = r0 + lax.broadcasted_iota(jnp.int32, (n, tk), 1)
                s = jnp.where(col <= row, s, -jnp.inf)
            m_old = m_ref[e, rows, :]
            m_new = jnp.maximum(m_old, jnp.max(s, axis=1, keepdims=True))
            p = jnp.exp((s - jnp.tile(m_new, (1, tk // LANES))).astype(bf16))
            acc_ref[e, rows, :] = (jnp.exp(m_old - m_new) * acc_ref[e, rows, :]
                                   + _dot(p, v_ref[e, pl.ds(k0, tk), :]))
            m_ref[e, rows, :] = m_new

    n_diag = tq // tk
    first = jnp.minimum(*[_first_needed_block(kst_ref, 2 * j + e, i, n_diag) for e in range(2)])
    t_first = jnp.min(first).astype(jnp.int32) // n_diag

    def body(t, carry):
        for d in range(n_diag):
            step(t * n_diag + d, None)
        return carry

    lax.fori_loop(t_first, i, body, 0)
    for d in range(n_diag):
        step(i * n_diag + d, d * tk)
    lane = lax.broadcasted_iota(jnp.int32, (tq, LANES), 1)
    a0 = acc_ref[0]
    a1 = acc_ref[1]
    o_ref[...] = jnp.where(lane < HEAD_DIM, a0 / pltpu.roll(a0, HEAD_DIM, 1), pltpu.roll(a1, HEAD_DIM, 1) / a1)


def _fox(l, qa, ka, va, kst, page_table, z, bf_all, ckt, cvt, lft, mkt, mvt, n_seq, tq, tk):
    B, _, L, _ = qa.shape
    n_pairs = W_FOX // LANES
    n_i = L // tq
    assert kst.shape[1] == L // tk
    guest = _DecodeGuest(l, n_seq, 0, B * n_pairs * n_i, lambda b, j, i: (b * n_pairs + j) * n_i + i,
                         page_table, z, bf_all, ckt, cvt, lft, mkt, mvt)
    resident = lambda: pl.BlockSpec((None, 2, L, LANES), lambda b, j, i, pt: (b, j, 0, 0),
                                    pipeline_mode=pl.Buffered(1))
    fox_o, *dec = pl.pallas_call(
        functools.partial(_fox_kernel, guest=guest, tq=tq, tk=tk),
        grid_spec=pltpu.PrefetchScalarGridSpec(
            num_scalar_prefetch=1,
            grid=(B, n_pairs, n_i),
            in_specs=[pl.BlockSpec((None, 2, tq, LANES), lambda b, j, i, pt: (b, j, i, 0)), resident(), resident(),
                      pl.BlockSpec((None,) + kst.shape[1:], lambda b, j, i, pt: (b, 0, 0, 0))] + guest.in_specs,
            out_specs=[pl.BlockSpec((None, tq, LANES), lambda b, j, i, pt: (b, i, j))] + guest.out_specs,
            scratch_shapes=[pltpu.VMEM((2, tq, LANES), f32), pltpu.VMEM((2, tq, LANES), f32)]
            + guest.scratch_shapes),
        out_shape=[jax.ShapeDtypeStruct((B, L, W_FOX), f32)] + guest.out_shape,
        compiler_params=_cparams(("parallel", "parallel", "arbitrary")),
        name="fox_decode",
    )(page_table, qa, ka, va, kst, *guest.operands)
    return fox_o, dec


def _s5prep_kernel(lr_ref, li_ref, ldt_ref, bre_ref, bim_ref, cre_ref, cim_ref,
                   lamb_ref, bmat_ref, cmat_ref, lpow_ref, lams_ref, pw_ref, *, seg_len):
    lr = lr_ref[...]
    li = li_ref[...]
    dt = jnp.exp(ldt_ref[...])
    ar = lr * dt
    ai = li * dt
    mag = jnp.exp(ar)
    lbr = mag * jnp.cos(ai)
    lbi = mag * jnp.sin(ai)
    lamb_ref[:, :HALF] = jnp.broadcast_to(lbr, (SUBLANES, HALF))
    lamb_ref[:, HALF:] = jnp.broadcast_to(lbi, (SUBLANES, HALF))
    den = lr * lr + li * li
    zr = ((lbr - 1.0) * lr + lbi * li) / den
    zi = (lbi * lr - (lbr - 1.0) * li) / den
    bre = bre_ref[...]
    bim = bim_ref[...]
    bmat_ref[:, :HALF] = zr * bre - zi * bim
    bmat_ref[:, HALF:] = zr * bim + zi * bre
    cmat_ref[:HALF, :] = cre_ref[...]
    cmat_ref[HALF:, :] = -cim_ref[...]
    jj = (lax.broadcasted_iota(jnp.int32, (SUBLANES, HALF), 0) + 1).astype(f32)
    magp = jnp.exp(ar * jj)
    pw_ref[:SUBLANES, :HALF] = magp * jnp.cos(ai * jj)
    pw_ref[:SUBLANES, HALF:] = magp * jnp.sin(ai * jj)
    n = SUBLANES
    while n < seg_len:
        sr = pw_ref[n - 1:n, :HALF]
        si = pw_ref[n - 1:n, HALF:]
        pr = pw_ref[:n, :HALF]
        pi = pw_ref[:n, HALF:]
        pw_ref[n:2 * n, :HALF] = pr * sr - pi * si
        pw_ref[n:2 * n, HALF:] = pr * si + pi * sr
        n *= 2
    lpow_ref[...] = pw_ref[...].astype(bf16)
    lams_ref[...] = pw_ref[seg_len - SUBLANES:, :]


def _s5prep(lam_re, lam_im, log_dt, b_re, b_im, c_re, c_im, seg_len):
    depth = lam_re.shape[0]
    assert seg_len >= SUBLANES and seg_len & (seg_len - 1) == 0
    nb, gb = N_SSM_BLOCKS, GROUPS_PER_BLOCK
    eye = jnp.eye(gb, dtype=f32)
    half = lambda a: a.reshape(depth, nb, 1, HALF)
    ldt = jnp.broadcast_to(log_dt[:, :, None], (depth, N_SSM_GROUPS, SSM_STATE))

    def embed_b(b):
        t = b.reshape(depth, nb, gb, SSM_STATE, SSM_GROUP).transpose(0, 1, 2, 4, 3)
        return (t[:, :, :, :, None, :] * eye[None, None, :, None, :, None]).reshape(depth, nb, LANES, HALF)

    def embed_c(c):
        t = c.reshape(depth, nb, gb, SSM_GROUP, SSM_STATE).transpose(0, 1, 2, 4, 3)
        return (t[:, :, :, :, None, :] * eye[None, None, :, None, :, None]).reshape(depth, nb, HALF, LANES)

    blk = lambda shape: pl.BlockSpec((None, None) + shape, lambda l, cb: (l, cb) + (0,) * len(shape))
    arr = lambda shape, dt: jax.ShapeDtypeStruct((depth, nb) + shape, dt)
    return pl.pallas_call(
        functools.partial(_s5prep_kernel, seg_len=seg_len),
        grid=(depth, nb),
        in_specs=[blk((1, HALF))] * 3 + [blk((LANES, HALF))] * 2 + [blk((HALF, LANES))] * 2,
        out_specs=(blk((SUBLANES, STATE_W)), blk((LANES, STATE_W)), blk((STATE_W, LANES)),
                   blk((seg_len, STATE_W)), blk((SUBLANES, STATE_W))),
        out_shape=(arr((SUBLANES, STATE_W), f32),
                   arr((LANES, STATE_W), f32),
                   arr((STATE_W, LANES), f32),
                   arr((seg_len, STATE_W), bf16),
                   arr((SUBLANES, STATE_W), f32)),
        scratch_shapes=[pltpu.VMEM((seg_len, STATE_W), f32)],
        compiler_params=_cparams(("parallel", "parallel")),
        name="s5_tables",
    )(half(lam_re), half(lam_im), half(ldt), embed_b(b_re), embed_b(b_im), embed_c(c_re), embed_c(c_im))


def _s5scan_kernel(u_ref, bmat_ref, cmat_ref, lamb_ref, dsk_ref, y_ref, hend_ref,
                   up_ref, hb_ref, hq_ref, hcar_ref, *, tj):
    j = pl.program_id(1)
    nb = N_SSM_BLOCKS

    @pl.when(j == 0)
    def _():
        hcar_ref[...] = jnp.zeros_like(hcar_ref)

    for cb in range(nb):
        for s in range(N_SEG):
            up_ref[cb, pl.ds(s, tj, stride=N_SEG), :] = u_ref[s, :, LANES * cb:LANES * (cb + 1)]
        hb_ref[cb] = _dot(up_ref[cb].astype(bf16), bmat_ref[cb])

    def rec(t2, h):
        r0 = pl.multiple_of(t2 * 2 * N_SEG, 2 * N_SEG)
        new = []
        for cb in range(nb):
            hr, hi = h[2 * cb], h[2 * cb + 1]
            lr = lamb_ref[cb, :, :HALF]
            li = lamb_ref[cb, :, HALF:]
            rs, is_ = [], []
            for d in range(2):
                bu = hb_ref[cb, pl.ds(r0 + d * N_SEG, N_SEG), :]
                hr, hi = lr * hr - li * hi + bu[:, :HALF], lr * hi + li * hr + bu[:, HALF:]
                rs.append(hr)
                is_.append(hi)
            hq_ref[cb, pl.ds(r0, 2 * N_SEG), :HALF] = jnp.concatenate(rs, axis=0).astype(bf16)
            hq_ref[cb, pl.ds(r0, 2 * N_SEG), HALF:] = jnp.concatenate(is_, axis=0).astype(bf16)
            new += [hr, hi]
        return tuple(new)

    h0 = tuple(hcar_ref[cb, :, HALF * part:HALF * (part + 1)] for cb in range(nb) for part in range(2))
    h = lax.fori_loop(0, tj // 2, rec, h0, unroll=True)
    for cb in range(nb):
        hcar_ref[cb, :, :HALF] = h[2 * cb]
        hcar_ref[cb, :, HALF:] = h[2 * cb + 1]
        up_ref[cb] = _dot(hq_ref[cb], cmat_ref[cb]) + dsk_ref[cb] * up_ref[cb]
        for s in range(N_SEG):
            y_ref[s, :, LANES * cb:LANES * (cb + 1)] = up_ref[cb, pl.ds(s, tj, stride=N_SEG), :]

    @pl.when(j == pl.num_programs(1) - 1)
    def _():
        hend_ref[...] = hcar_ref[...]


def _s5scan(l, u, bmat_all, cmat_all, lamb_all, dsk_all, tj):
    B, L, _ = u.shape
    seg_len = L // N_SEG
    u4 = u.reshape(B, N_SEG, seg_len, W_SSM)
    nb = N_SSM_BLOCKS
    tile = pl.BlockSpec((None, N_SEG, tj, W_SSM), lambda b, j: (b, 0, j, 0))
    y4, hend = pl.pallas_call(
        functools.partial(_s5scan_kernel, tj=tj),
        grid=(B, seg_len // tj),
        in_specs=[tile, _layer(l, bmat_all.shape[1:]), _layer(l, cmat_all.shape[1:]),
                  _layer(l, lamb_all.shape[1:]), _layer(l, dsk_all.shape[1:])],
        out_specs=(tile, pl.BlockSpec((None, nb, N_SEG, STATE_W), lambda b, j: (b, 0, 0, 0))),
        out_shape=(jax.ShapeDtypeStruct((B, N_SEG, seg_len, W_SSM), f32),
                   jax.ShapeDtypeStruct((B, nb, N_SEG, STATE_W), f32)),
        scratch_shapes=[pltpu.VMEM((nb, N_SEG * tj, LANES), f32), pltpu.VMEM((nb, N_SEG * tj, STATE_W), f32),
                        pltpu.VMEM((nb, N_SEG * tj, STATE_W), bf16), pltpu.VMEM((nb, N_SEG, STATE_W), f32)],
        compiler_params=_cparams(("parallel", "arbitrary")),
        name="prompt_s5_scan",
    )(u4, bmat_all, cmat_all, lamb_all, dsk_all)
    return y4.reshape(B, L, W_SSM), hend


def _s5fix_kernel(hend_ref, lams_ref, cmat_ref, hlast_ref, cp_ref, hs_ref):
    he = hend_ref[...]
    lsr = lams_ref[SUBLANES - 1:, :HALF]
    lsi = lams_ref[SUBLANES - 1:, HALF:]
    hs_ref[...] = jnp.zeros_like(hs_ref)
    hr = jnp.zeros((1, HALF), f32)
    hi = jnp.zeros((1, HALF), f32)
    for s in range(N_SEG):
        hs_ref[s:s + 1, :HALF] = hr
        hs_ref[s:s + 1, HALF:] = hi
        nr = lsr * hr - lsi * hi + he[s:s + 1, :HALF]
        ni = lsr * hi + lsi * hr + he[s:s + 1, HALF:]
        hr, hi = nr, ni
    hlast_ref[:, :HALF] = hr
    hlast_ref[:, HALF:] = hi
    hst_r = hs_ref[:, :HALF].T
    hst_i = hs_ref[:, HALF:].T
    cre = cmat_ref[:HALF, :]
    mci = cmat_ref[HALF:, :]
    for s in range(N_SEG):
        col_r = hst_r[:, s:s + 1]
        col_i = hst_i[:, s:s + 1]
        cp_ref[s, :HALF, :] = (cre * col_r + mci * col_i).astype(bf16)
        cp_ref[s, HALF:, :] = (mci * col_r - cre * col_i).astype(bf16)


def _s5fix(l, hend, lams_all, cmat_all):
    B, nb = hend.shape[:2]
    return pl.pallas_call(
        _s5fix_kernel,
        grid=(B, nb),
        in_specs=[pl.BlockSpec((None, None, N_SEG, STATE_W), lambda b, cb: (b, cb, 0, 0)),
                  pl.BlockSpec((None, None, SUBLANES, STATE_W), lambda b, cb: (l, cb, 0, 0)),
                  pl.BlockSpec((None, None, STATE_W, LANES), lambda b, cb: (l, cb, 0, 0))],
        out_specs=(pl.BlockSpec((None, None, 1, STATE_W), lambda b, cb: (b, cb, 0, 0)),
                   pl.BlockSpec((None, N_SEG, None, STATE_W, LANES), lambda b, cb: (b, 0, cb, 0, 0))),
        out_shape=(jax.ShapeDtypeStruct((B, nb, 1, STATE_W), f32),
                   jax.ShapeDtypeStruct((B, N_SEG, nb, STATE_W, LANES), bf16)),
        scratch_shapes=[pltpu.VMEM((LANES, STATE_W), f32)],
        compiler_params=_cparams(("parallel", "parallel")),
        name="prompt_s5_fix",
    )(hend, lams_all, cmat_all)


def _pair_attend(q, k, v):
    lane = lax.broadcasted_iota(jnp.int32, q.shape, 1)
    zero = jnp.zeros_like(q)
    outs = []
    for e in range(2):
        qe = jnp.where((lane < HEAD_DIM) == (e == 0), q, zero)
        s = _dot_nt(qe, k)
        p = jnp.exp(s - jnp.max(s, axis=1, keepdims=True))
        outs.append(_dot(p.astype(bf16), v) / jnp.sum(p, axis=1, keepdims=True))
    return jnp.where(lane < HEAD_DIM, outs[0], outs[1])


N_OUTPROJ_INPUTS = 13


def _outproj_stages(x_ref, fox_ref, y_ref, lp_ref, cp_ref, qm_ref, mk_ref, mv_ref, sg_ref,
                    wglu_ref, bglu_ref, wout_ref, gfin_ref, o_ref, *, final, tiles_per_seg):
    tm = x_ref.shape[0]
    j0 = pl.multiple_of((pl.program_id(1) % tiles_per_seg) * tm, tm)
    corr = [_dot(lp_ref[cb, pl.ds(j0, tm), :], cp_ref[cb]) for cb in range(N_SSM_BLOCKS)]
    ssm_y = y_ref[...] + jnp.concatenate(corr, axis=1)
    yield
    s = _gelu(ssm_y)
    s = s * jax.nn.sigmoid(_dot(s.astype(bf16), wglu_ref[...]) + bglu_ref[...])
    yield
    qm = qm_ref[...]
    mk = mk_ref[...].astype(bf16)
    mv = mv_ref[...].astype(bf16)
    mem = []
    for jp in range(W_MEM // LANES):
        mem.append(_pair_attend(qm[:, LANES * jp:LANES * (jp + 1)], mk[:, LANES * jp:LANES * (jp + 1)],
                                mv[:, LANES * jp:LANES * (jp + 1)]))
        yield
    sg = sg_ref[...].astype(f32)
    mix = jnp.concatenate([fox_ref[...] * sg[:, :W_FOX],
                           s * sg[:, W_FOX:W_FOX + W_SSM],
                           jnp.concatenate(mem, axis=1) * sg[:, W_FOX + W_SSM:]], axis=1)
    xo = x_ref[...] + _dot(mix.astype(bf16), wout_ref[...])
    yield
    if final:
        xo = _rms(xo, gfin_ref[...])
    o_ref[...] = xo


def _s_inproj_kernel(x_ref, g_ref, w_ref, z_ref):
    xn = _rms(x_ref[...], g_ref[...])
    z_ref[...] = _dot3(xn, w_ref[...])


def _s_inproj(l, xs, g_all, wcat_all, tn):
    R, D = xs.shape
    return pl.pallas_call(
        _s_inproj_kernel,
        grid=(W_CAT // tn,),
        in_specs=[pl.BlockSpec((R, D), lambda n: (0, 0)), _layer(l, (1, D)),
                  pl.BlockSpec((None, D, tn), lambda n: (l, 0, n))],
        out_specs=pl.BlockSpec((R, tn), lambda n: (0, n)),
        out_shape=jax.ShapeDtypeStruct((R, W_CAT), f32),
        compiler_params=_cparams(("parallel",)),
        name="sample_inproj",
    )(xs, g_all, wcat_all)


def _full_max(x):
    return jnp.max(jnp.max(x, axis=0, keepdims=True), axis=1, keepdims=True)


def _full_min(x):
    return jnp.min(jnp.min(x, axis=0, keepdims=True), axis=1, keepdims=True)


def _full_sum(x):
    return jnp.sum(jnp.sum(x, axis=0, keepdims=True), axis=1, keepdims=True)


def _lane_sums(acc):
    ones = jnp.ones((SUBLANES, LANES), bf16)
    hi, lo = _split2(acc)
    return (_dot_nt(ones, hi) + _dot_nt(ones, lo))[0:1, :]


def _col_bcast(row):
    return jnp.broadcast_to(row, (LANES, row.shape[1])).T


class _DecodeGuest:
    N_FIXED_IN, N_OUT, N_SCRATCH = 3, 3, 5

    def __init__(self, l, n_seq, first_seq, n_steps, step_of, page_table, z, bf_all, ckt, cvt, lft, mkt, mvt):
        R, n_pages = page_table.shape
        n_pool, page = ckt.shape[1], ckt.shape[4]
        n_mem = mkt.shape[4]
        assert page == LANES and n_mem % LANES == 0 and first_seq % n_seq == 0
        assert first_seq + n_steps * n_seq <= R
        self.n_seq, self.n_pages, self.first_seq, self.step_of = n_seq, n_pages, first_seq, step_of
        self.per_seq = 2 * n_pages + 2
        z_blk = lambda *g: (first_seq // n_seq + step_of(*g[:-1]), 0, 0)
        self.in_specs = [pl.BlockSpec((n_seq, 1, W_CAT), z_blk), _layer(l, (1, LANES)),
                         _layer(l, (N_FOX_HEADS, n_pool, page))]
        self.operands = [z.reshape(R, 1, W_CAT), bf_all, lft]
        for s in range(n_seq):
            seq = lambda g, s=s: first_seq + step_of(*g) * n_seq + s
            paged = [pl.BlockSpec((None, None, N_FOX_HEADS, HEAD_DIM, page),
                                  lambda *g, seq=seq, p=p: (l, g[-1][seq(g[:-1]), p], 0, 0, 0))
                     for p in range(n_pages)]
            mem_spec = pl.BlockSpec((None, None, N_MEM_HEADS, HEAD_DIM, n_mem),
                                    lambda *g, seq=seq: (l, seq(g[:-1]), 0, 0, 0))
            self.in_specs += paged + paged + [mem_spec, mem_spec]
            self.operands += [ckt] * n_pages + [cvt] * n_pages + [mkt, mvt]
        out = lambda w: pl.BlockSpec((n_seq, 1, w), lambda *g: (step_of(*g[:-1]), 0, 0))
        self.out_specs = [out(W_FOX), out(W_MEM), out(LANES)]
        n_out = n_steps * n_seq
        self.out_shape = [jax.ShapeDtypeStruct((n_out, 1, w), f32) for w in (W_FOX, W_MEM, LANES)]
        n_rows = N_FOX_HEADS * n_pages
        self.scratch_shapes = [pltpu.VMEM((n_seq, rows, LANES), f32)
                               for rows in (n_rows, n_rows, n_rows, W_FOX, W_MEM)]

    def chains(self, pt_ref, step, ins, outs, scratch):
        z_ref, bf_ref, lfc_ref = ins[:self.N_FIXED_IN]
        fox_ref, mem_ref, lf_ref = outs
        n_pages = self.n_pages
        result = []
        for i in range(self.n_seq):
            mine = ins[self.N_FIXED_IN + i * self.per_seq:self.N_FIXED_IN + (i + 1) * self.per_seq]
            result.append(_decode_one(
                pt_ref, self.first_seq + step * self.n_seq + i, z_ref.at[i], bf_ref, lfc_ref,
                mine[:n_pages], mine[n_pages:2 * n_pages], mine[2 * n_pages], mine[2 * n_pages + 1],
                fox_ref.at[i], mem_ref.at[i], lf_ref.at[i], *[s.at[i] for s in scratch]))
        return result

    @property
    def n_in(self):
        return self.N_FIXED_IN + self.n_seq * self.per_seq


def _run_chains(chains):
    while chains:
        chains = [c for c in chains if next(c, _DONE) is not _DONE]


def _outproj_decode_kernel(pt_ref, *refs, guest, final, tiles_per_seg):
    n_in = N_OUTPROJ_INPUTS + guest.n_in
    op_in, g_in = refs[:N_OUTPROJ_INPUTS], refs[N_OUTPROJ_INPUTS:n_in]
    o_ref = refs[n_in]
    g_out = refs[n_in + 1:n_in + 1 + guest.N_OUT]
    g_scr = refs[n_in + 1 + guest.N_OUT:]
    step = guest.step_of(pl.program_id(0), pl.program_id(1))
    _run_chains([_outproj_stages(*op_in, o_ref, final=final, tiles_per_seg=tiles_per_seg)]
                + guest.chains(pt_ref, step, g_in, g_out, g_scr))


_DONE = object()


def _decode_one(pt_ref, r, z_ref, bf_ref, lfc_ref, ck, cv, mk_ref, mv_ref, fox_ref, mem_ref, lf_ref,
                s_scr, lfs_scr, p_scr, acc_scr, accm_scr):
    n_pages = len(ck)
    z = z_ref[...]
    q = z[:, OFF_Q:OFF_Q + W_FOX] * QK_SCALE
    k_new = z[:, OFF_K:OFF_K + W_FOX]
    v_new = z[:, OFF_V:OFF_V + W_FOX]
    lf_new = _log_sigmoid(z[:, OFF_F:OFF_F + LANES] + bf_ref[...])
    lf_ref[...] = lf_new

    qt = _col_bcast(q)
    for h in range(N_FOX_HEADS):
        qh = qt[HEAD_DIM * h:HEAD_DIM * (h + 1), :]
        for p in range(n_pages):
            row = h * n_pages + p
            s_scr[row:row + 1, :] = jnp.sum(ck[p][h] * qh, axis=0, keepdims=True)
            lfs_scr[row:row + 1, :] = lfc_ref[h, pl.ds(pt_ref[r, p], 1), :]
        yield

    n_rows = N_FOX_HEADS * n_pages
    ur = lax.broadcasted_iota(jnp.int32, (LANES, LANES), 0)
    uc = lax.broadcasted_iota(jnp.int32, (LANES, LANES), 1)
    upper = jnp.where(ur <= uc, 1.0, 0.0).astype(bf16)
    cs = sum(_dot(pc, upper) for pc in _split3(lfs_scr[...]))
    tot = jnp.broadcast_to(cs[:, LANES - 1:], (n_rows, LANES))
    mr = lax.broadcasted_iota(jnp.int32, (n_rows, n_rows), 0)
    mc = lax.broadcasted_iota(jnp.int32, (n_rows, n_rows), 1)
    before = jnp.where((mc < mr) & (mc // n_pages == mr // n_pages), 1.0, 0.0).astype(bf16)
    c = cs + sum(_dot(before, pc) for pc in _split3(tot))
    s_all = s_scr[...] - c
    yield

    qk_new = q * k_new
    lane_head = lax.broadcasted_iota(jnp.int32, (1, W_FOX), 1) // HEAD_DIM
    pn_exp = jnp.zeros((1, W_FOX), f32)
    l_exp = jnp.zeros((1, W_FOX), f32)
    for h in range(N_FOX_HEADS):
        rows = slice(h * n_pages, (h + 1) * n_pages)
        s_h = s_all[rows, :]
        c_new = c[(h + 1) * n_pages - 1:(h + 1) * n_pages, LANES - 1:] + lf_new[:, h:h + 1]
        s_new = jnp.sum(qk_new[:, HEAD_DIM * h:HEAD_DIM * (h + 1)], axis=1, keepdims=True) - c_new
        m = jnp.maximum(_full_max(s_h), s_new)
        p_h = jnp.exp(s_h - m)
        pn = jnp.exp(s_new - m)
        p_scr[rows, :] = p_h
        pn_exp = jnp.where(lane_head == h, pn, pn_exp)
        l_exp = jnp.where(lane_head == h, _full_sum(p_h) + pn, l_exp)
        acc = jnp.zeros((HEAD_DIM, LANES), f32)
        for p in range(n_pages):
            acc = acc + cv[p][h] * p_scr[h * n_pages + p:h * n_pages + p + 1, :]
        acc_scr[HEAD_DIM * h:HEAD_DIM * (h + 1), :] = acc
        yield
    fox_ref[...] = (_lane_sums(acc_scr[...]) + pn_exp * v_new) / l_exp
    yield

    qm = z[:, OFF_QM:OFF_QM + W_MEM] * QK_SCALE
    qmt = _col_bcast(qm)
    n_halves = mk_ref.shape[2] // LANES
    mem_head = lax.broadcasted_iota(jnp.int32, (1, W_MEM), 1) // HEAD_DIM
    lm_exp = jnp.zeros((1, W_MEM), f32)
    for h in range(N_MEM_HEADS):
        qh = qmt[HEAD_DIM * h:HEAD_DIM * (h + 1), :]
        ss = [jnp.sum(mk_ref[h, :, LANES * t:LANES * (t + 1)] * qh, axis=0, keepdims=True) for t in range(n_halves)]
        m = functools.reduce(jnp.maximum, [jnp.max(s, axis=1, keepdims=True) for s in ss])
        ps = [jnp.exp(s - m) for s in ss]
        lm = sum(jnp.sum(p, axis=1, keepdims=True) for p in ps)
        lm_exp = jnp.where(mem_head == h, lm, lm_exp)
        accm_scr[HEAD_DIM * h:HEAD_DIM * (h + 1), :] = sum(
            mv_ref[h, :, LANES * t:LANES * (t + 1)] * ps[t] for t in range(n_halves))
    mem_ref[...] = _lane_sums(accm_scr[...]) / lm_exp


def _outproj_decode(l, x, fox_o, y, lpow_all, cp, qm, mk_all, mv_all, sg, wglu_all, bglu_all, wout_all, gfin,
                    page_table, z, bf_all, ckt, cvt, lft, mkt, mvt, first_seq, tm, final):
    B, L, D = x.shape
    seg_len = L // N_SEG
    tiles_per_seg = seg_len // tm
    n_i = L // tm
    n_steps = B * n_i
    n_left = page_table.shape[0] - first_seq
    assert n_left % n_steps == 0
    guest = _DecodeGuest(l, n_left // n_steps, first_seq, n_steps, lambda b, i: b * n_i + i,
                         page_table, z, bf_all, ckt, cvt, lft, mkt, mvt)
    nb = N_SSM_BLOCKS
    row = lambda w: pl.BlockSpec((None, tm, w), lambda b, i, pt: (b, i, 0))
    mem_kv = pl.BlockSpec((None, None) + mk_all.shape[2:], lambda b, i, pt: (l, b, 0, 0))
    in_specs = [row(D), row(W_FOX), row(W_SSM), _layer(l, lpow_all.shape[1:]),
                pl.BlockSpec((None, None, nb, STATE_W, LANES), lambda b, i, pt: (b, i // tiles_per_seg, 0, 0, 0)),
                row(W_MEM), mem_kv, mem_kv, row(D),
                _layer(l, wglu_all.shape[1:]), _layer(l, (1, W_SSM)), _layer(l, wout_all.shape[1:]),
                pl.BlockSpec((1, D), lambda b, i, pt: (0, 0))]
    assert len(in_specs) == N_OUTPROJ_INPUTS
    xo, *dec = pl.pallas_call(
        functools.partial(_outproj_decode_kernel, guest=guest, final=final, tiles_per_seg=tiles_per_seg),
        grid_spec=pltpu.PrefetchScalarGridSpec(
            num_scalar_prefetch=1,
            grid=(B, n_i),
            in_specs=in_specs + guest.in_specs,
            out_specs=[row(D)] + guest.out_specs,
            scratch_shapes=guest.scratch_shapes),
        out_shape=[jax.ShapeDtypeStruct((B, L, D), f32)] + guest.out_shape,
        compiler_params=_cparams(("parallel", "parallel")),
        name="outproj_decode",
    )(page_table, x, fox_o, y, lpow_all, cp, qm, mk_all, mv_all, sg, wglu_all, bglu_all, wout_all, gfin,
      *guest.operands)
    return xo, dec


def _s_out_kernel(x_ref, z_ref, fox_ref, mem_ref, h0_ref, bmat_ref, cmat_ref, lamb_ref, dsk_ref,
                  wglu_ref, bglu_ref, wout_ref, gfin_ref, o_ref, h_ref, *, final):
    z = z_ref[...]
    u = z[:, OFF_U:OFF_U + W_SSM]
    ys = []
    for cb in range(N_SSM_BLOCKS):
        ucb = u[:, LANES * cb:LANES * (cb + 1)]
        bu = _dot3(ucb, bmat_ref[cb])
        lr = lamb_ref[cb, 0:1, :HALF]
        li = lamb_ref[cb, 0:1, HALF:]
        h0r = h0_ref[:, STATE_W * cb:STATE_W * cb + HALF]
        h0i = h0_ref[:, STATE_W * cb + HALF:STATE_W * (cb + 1)]
        hr = lr * h0r - li * h0i + bu[:, :HALF]
        hi = lr * h0i + li * h0r + bu[:, HALF:]
        h_ref[:, STATE_W * cb:STATE_W * cb + HALF] = hr
        h_ref[:, STATE_W * cb + HALF:STATE_W * (cb + 1)] = hi
        ys.append(_dot3(jnp.concatenate([hr, hi], axis=1), cmat_ref[cb]) + dsk_ref[cb] * ucb)
    s = _gelu(jnp.concatenate(ys, axis=1))
    s = s * jax.nn.sigmoid(_dot3(s, wglu_ref[...]) + bglu_ref[...])
    sg = _silu(z[:, OFF_G:W_MAIN])
    mix = jnp.concatenate([fox_ref[...] * sg[:, :W_FOX],
                           s * sg[:, W_FOX:W_FOX + W_SSM],
                           mem_ref[...] * sg[:, W_FOX + W_SSM:]], axis=1)
    xo = x_ref[...] + _dot3(mix, wout_ref[...])
    if final:
        xo = _rms(xo, gfin_ref[...])
    o_ref[...] = xo


def _s_out(l, xs, z, fox, mem, h0_all, bmat_all, cmat_all, lamb_all, dsk_all, wglu_all, bglu_all, wout_all,
           gfin, final):
    R, D = xs.shape
    whole = lambda a: pl.BlockSpec(a.shape, lambda i: (0,) * a.ndim)
    hspec = pl.BlockSpec((R, h0_all.shape[2]), lambda i: (0, 0))
    return pl.pallas_call(
        functools.partial(_s_out_kernel, final=final),
        grid=(1,),
        in_specs=[whole(xs), whole(z), whole(fox), whole(mem), _layer(l, h0_all.shape[1:]),
                  _layer(l, bmat_all.shape[1:]), _layer(l, cmat_all.shape[1:]), _layer(l, lamb_all.shape[1:]),
                  _layer(l, dsk_all.shape[1:]), _layer(l, wglu_all.shape[1:]), _layer(l, (1, W_SSM)),
                  _layer(l, wout_all.shape[1:]), whole(gfin)],
        out_specs=(whole(xs), hspec),
        out_shape=(jax.ShapeDtypeStruct((R, D), f32), jax.ShapeDtypeStruct(h0_all.shape[1:], f32)),
        compiler_params=_cparams(("arbitrary",)),
        name="sample_out",
    )(xs, z, fox, mem, h0_all, bmat_all, cmat_all, lamb_all, dsk_all, wglu_all, bglu_all, wout_all, gfin)


def _state_to_cols(h):
    lead = h.shape[:-3]
    n = len(lead)
    t = h.reshape(lead + (N_SSM_BLOCKS, GROUPS_PER_BLOCK, SSM_STATE, 2))
    t = t.transpose(tuple(range(n)) + (n, n + 3, n + 1, n + 2))
    return t.reshape(lead + (N_SSM_BLOCKS * STATE_W,))


def _cols_to_state(c):
    lead = c.shape[:-1]
    n = len(lead)
    t = c.reshape(lead + (N_SSM_BLOCKS, 2, GROUPS_PER_BLOCK, SSM_STATE))
    t = t.transpose(tuple(range(n)) + (n, n + 2, n + 3, n + 1))
    return t.reshape(lead + (N_SSM_GROUPS, SSM_STATE, 2))


def _tile(n, pref):
    t = min(n, pref)
    assert n % t == 0
    return t


def kernel(x_prompt, x_sample, mem_prompt, cache_k, cache_v, cache_logf, state_ssm, cache_mem_k, cache_mem_v,
           page_table, g_norm, w_in, b_f, lam_re, lam_im, log_dt, b_re, b_im, c_re, c_im, d_skip, w_glu, b_glu,
           g_mem, w_mem_kv, w_out, g_final):
    depth = w_in.shape[0]
    B, L, D = x_prompt.shape
    R = x_sample.shape[0]
    assert x_sample.shape[1] == 1 and L % N_SEG == 0
    seg_len = L // N_SEG
    tm = _tile(seg_len, 512)
    tq = _tile(L, 1024)
    tk = _tile(tq, 512)
    tj = _tile(seg_len, 128)

    seg = dict(zip(("q", "k", "v", "f", "ga", "u", "gs", "qm", "gm"),
                   (w_in[:, :, a:b] for a, b in zip(IN_CUTS[:-1], IN_CUTS[1:]))))
    wcat_all = jnp.concatenate(
        [seg[n] for n in ("q", "k", "v", "u", "qm", "ga", "gs", "gm", "f")]
        + [jnp.zeros((depth, D, W_CAT - W_MAIN - N_FOX_HEADS), f32)], axis=2)
    wm_all = wcat_all[:, :, :W_MAIN].astype(bf16)
    wf = wcat_all[:, :, OFF_F:OFF_F + LANES]
    wfh = wf.astype(bf16)
    wf_all = jnp.concatenate([wfh, (wf - wfh.astype(f32)).astype(bf16)], axis=2)
    bf_all = jnp.pad(b_f, ((0, 0), (0, LANES - N_FOX_HEADS))).reshape(depth, 1, LANES)
    gn_all = g_norm.reshape(depth, 1, D)
    gmem_all = g_mem.reshape(depth, 1, D)
    bglu_all = b_glu.reshape(depth, 1, W_SSM)
    gfin = g_final.reshape(1, D)
    dsk_all = d_skip.reshape(depth, N_SSM_BLOCKS, 1, LANES)
    wglu_bf = w_glu.astype(bf16)
    wout_bf = w_out.astype(bf16)
    ckt = jnp.transpose(cache_k, (0, 1, 3, 4, 2))
    cvt = jnp.transpose(cache_v, (0, 1, 3, 4, 2))
    lft = jnp.transpose(cache_logf, (0, 3, 1, 2))
    mkt = jnp.transpose(cache_mem_k, (0, 1, 3, 4, 2))
    mvt = jnp.transpose(cache_mem_v, (0, 1, 3, 4, 2))
    h0_all = _state_to_cols(state_ssm)

    lamb_all, bmat_all, cmat_all, lpow_all, lams_all = _s5prep(lam_re, lam_im, log_dt, b_re, b_im, c_re, c_im, seg_len)
    bmat_bf = bmat_all.astype(bf16)
    cmat_bf = cmat_all.astype(bf16)
    mk_all, mv_all = _memkv(mem_prompt, gmem_all, w_mem_kv.astype(bf16))

    xp = x_prompt
    xs = x_sample.reshape(R, D)
    outs = {n: [] for n in ("lfp", "hp", "z", "lfs", "hs")}
    kv_prev = ()
    for l in range(depth):
        final = l == depth - 1
        qa, kt_all, ka, vt_all, va, kst, lf, sg, u, qm = _inproj(l, xp, gn_all, wm_all, wf_all, bf_all, kv_prev, tk)
        kv_prev = (kt_all, vt_all)
        z = _s_inproj(l, xs, gn_all, wcat_all, 256)
        caches = (page_table, z, bf_all, ckt, cvt, lft, mkt, mvt)
        fox_o, dec_a = _fox(l, qa, ka, va, kst, *caches, SEQS_PER_FOX_STEP, tq, tk)
        y, hend = _s5scan(l, u, bmat_bf, cmat_bf, lamb_all, dsk_all, tj)
        hlast, cp = _s5fix(l, hend, lams_all, cmat_all)
        outs["lfp"].append(lf)
        outs["hp"].append(hlast.reshape(B, N_SSM_BLOCKS * STATE_W))
        xp, dec_b = _outproj_decode(
            l, xp, fox_o, y, lpow_all, cp, qm, mk_all, mv_all, sg, wglu_bf, bglu_all, wout_bf, gfin,
            *caches, dec_a[0].shape[0], tm, final)
        fox_s, mem_s, lf_s = (jnp.concatenate([a, b], axis=0).reshape(R, -1) for a, b in zip(dec_a, dec_b))
        xs, h_new = _s_out(l, xs, z, fox_s, mem_s, h0_all, bmat_all, cmat_all, lamb_all, dsk_all,
                           w_glu, bglu_all, w_out, gfin, final)
        outs["z"].append(z)
        outs["lfs"].append(lf_s)
        outs["hs"].append(h_new)

    st = lambda n: jnp.stack(outs[n])
    z_all = st("z")
    n_mem = mk_all.shape[2]
    return (xp, xs.reshape(R, 1, D),
            kt_all.reshape(depth, B, N_FOX_HEADS, HEAD_DIM, L).transpose(0, 1, 4, 2, 3),
            vt_all.reshape(depth, B, N_FOX_HEADS, HEAD_DIM, L).transpose(0, 1, 4, 2, 3),
            st("lfp"),
            _cols_to_state(st("hp")),
            mk_all.reshape(depth, B, n_mem, N_MEM_HEADS, HEAD_DIM),
            mv_all.reshape(depth, B, n_mem, N_MEM_HEADS, HEAD_DIM),
            z_all[:, :, OFF_K:OFF_K + W_FOX].reshape(depth, R, 1, N_FOX_HEADS, HEAD_DIM),
            z_all[:, :, OFF_V:OFF_V + W_FOX].reshape(depth, R, 1, N_FOX_HEADS, HEAD_DIM),
            st("lfs")[:, :, :N_FOX_HEADS].reshape(depth, R, 1, N_FOX_HEADS),
            _cols_to_state(st("hs")))
```

```python
import functools

import jax
import jax.numpy as jnp
from jax import lax
from jax.experimental import pallas as pl
from jax.experimental.pallas import tpu as pltpu

f32 = jnp.float32
bf16 = jnp.bfloat16

HEAD_DIM = 64
N_FOX_HEADS = 6
W_FOX = N_FOX_HEADS * HEAD_DIM
SSM_GROUP = 16
N_SSM_GROUPS = 24
W_SSM = N_SSM_GROUPS * SSM_GROUP
SSM_STATE = 64
N_MEM_HEADS = 4
W_MEM = N_MEM_HEADS * HEAD_DIM
W_MIX = W_FOX + W_SSM + W_MEM
RMS_EPS = 1e-6
QK_SCALE = HEAD_DIM ** -0.5

LANES = 128
SUBLANES = 8
GROUPS_PER_BLOCK = LANES // SSM_GROUP
N_SSM_BLOCKS = N_SSM_GROUPS // GROUPS_PER_BLOCK
HALF = GROUPS_PER_BLOCK * SSM_STATE
STATE_W = 2 * HALF
N_SEG = SUBLANES
PIECE_PITCH = SUBLANES

OFF_Q, OFF_K, OFF_V, OFF_U, OFF_QM, OFF_G = 0, 384, 768, 1152, 1536, 1792
W_MAIN = OFF_G + W_MIX
OFF_F = W_MAIN
W_CAT = 3072
IN_CUTS = (0, 384, 768, 1152, 1158, 1542, 1926, 2310, 2566, 2822)

VMEM_LIMIT = 56 * 1024 * 1024

SKIP_GAP = 110.0
NORM_SLACK = 1.02
KST_KN, KST_QN, KST_CEND, KST_ROWS = 0, SUBLANES, 2 * SUBLANES, 3 * SUBLANES


def _cparams(sem):
    return pltpu.CompilerParams(dimension_semantics=sem, vmem_limit_bytes=VMEM_LIMIT)


def _dot(a, b):
    return jnp.dot(a, b, preferred_element_type=f32)


def _dot_nt(a, b):
    return lax.dot_general(a, b, (((1,), (1,)), ((), ())), preferred_element_type=f32)


def _split2(a):
    hi = a.astype(bf16)
    lo = (a - hi.astype(f32)).astype(bf16)
    return hi, lo


def _split3(a):
    hi = a.astype(bf16)
    r = a - hi.astype(f32)
    mid = r.astype(bf16)
    lo = (r - mid.astype(f32)).astype(bf16)
    return hi, mid, lo


def _dot3(a, b):
    ah, al = _split2(a)
    bh, bl = _split2(b)
    return _dot(ah, bh) + _dot(al, bh) + _dot(ah, bl)


def _rms(x, g):
    ms = jnp.mean(x * x, axis=-1, keepdims=True)
    return x * lax.rsqrt(ms + RMS_EPS) * g


def _log_sigmoid(x):
    return jnp.minimum(x, 0.0) - jnp.log1p(jnp.exp(-jnp.abs(x)))


def _silu(x):
    return x * jax.nn.sigmoid(x)


def _gelu(x):
    return 0.5 * x * (1.0 + lax.erf(x * (2.0 ** -0.5)))


def _tri(n):
    r = lax.broadcasted_iota(jnp.int32, (n, n), 0)
    c = lax.broadcasted_iota(jnp.int32, (n, n), 1)
    return jnp.where(c <= r, 1.0, 0.0).astype(bf16)


def _pack_pieces(x):
    packed = None
    for i, pc in enumerate(_split3(x)):
        pc = pc.astype(f32)
        if i:
            pc = pltpu.roll(pc, PIECE_PITCH * i, 1)
        packed = pc if packed is None else packed + pc
    return packed.astype(bf16)


def _unpack_pieces(y):
    return y + pltpu.roll(y, LANES - PIECE_PITCH, 1) + pltpu.roll(y, LANES - 2 * PIECE_PITCH, 1)


def _layer(l, shape):
    return pl.BlockSpec((None,) + tuple(shape), lambda *_: (l,) + (0,) * len(shape),
                        pipeline_mode=pl.Buffered(1))


def _stage_w_kernel(w_ref, cat_ref, wm_ref, wf_ref):
    w = w_ref[...]
    seg = dict(zip(("q", "k", "v", "f", "ga", "u", "gs", "qm", "gm"),
                   (w[:, a:b] for a, b in zip(IN_CUTS[:-1], IN_CUTS[1:]))))
    main = jnp.concatenate([seg[n] for n in ("q", "k", "v", "u", "qm", "ga", "gs", "gm")], axis=1)
    wf = jnp.concatenate([seg["f"], jnp.zeros((w.shape[0], LANES - N_FOX_HEADS), f32)], axis=1)
    cat_ref[:, :W_MAIN] = main
    cat_ref[:, OFF_F:OFF_F + LANES] = wf
    cat_ref[:, OFF_F + LANES:] = jnp.zeros((w.shape[0], W_CAT - OFF_F - LANES), f32)
    wm_ref[...] = main.astype(bf16)
    wfh = wf.astype(bf16)
    wf_ref[:, :LANES] = wfh
    wf_ref[:, LANES:] = (wf - wfh.astype(f32)).astype(bf16)


def _stage_w(w_in, tr):
    depth, D, d_in = w_in.shape
    assert d_in == IN_CUTS[-1]
    blk = lambda w: pl.BlockSpec((None, tr, w), lambda l, i: (l, i, 0))
    return pl.pallas_call(
        _stage_w_kernel,
        grid=(depth, D // tr),
        in_specs=[blk(d_in)],
        out_specs=(blk(W_CAT), blk(W_MAIN), blk(2 * LANES)),
        out_shape=(jax.ShapeDtypeStruct((depth, D, W_CAT), f32), jax.ShapeDtypeStruct((depth, D, W_MAIN), bf16),
                   jax.ShapeDtypeStruct((depth, D, 2 * LANES), bf16)),
        compiler_params=_cparams(("parallel", "parallel")),
        name="stage_weights",
    )(w_in)


def _head_block(x, h):
    blk = x[:, LANES * (h // 2):LANES * (h // 2 + 1)]
    return blk if h % 2 == 0 else pltpu.roll(blk, HEAD_DIM, 1)


def _inproj_kernel(x_ref, g_ref, wm_ref, wf_ref, bf_ref, *rest):
    qa_ref, kt_ref, ka_ref, vt_ref, va_ref, kst_ref, lf_ref, sg_ref, u_ref, qm_ref, carry_ref = rest[-11:]
    i = pl.program_id(1)

    @pl.when(i == 0)
    def _():
        carry_ref[...] = jnp.zeros_like(carry_ref)

    xn = _rms(x_ref[...], g_ref[...])
    xh = xn.astype(bf16)
    tm = xn.shape[0]
    lane = lax.broadcasted_iota(jnp.int32, (tm, LANES), 1)
    low = lane < HEAD_DIM

    ff = _dot(xh, wf_ref[...])
    f = ff[:, :LANES] + ff[:, LANES:] + bf_ref[...]
    lf = jnp.where(lane < N_FOX_HEADS, _log_sigmoid(f), 0.0)
    lf_ref[...] = lf.T[:N_FOX_HEADS, :]
    cs = _unpack_pieces(_dot(_tri(tm), _pack_pieces(lf)))
    cs = jnp.where(lane < N_FOX_HEADS, cs, 0.0) + carry_ref[0:1, :]
    carry_ref[...] = jnp.broadcast_to(cs[tm - 1:, :], carry_ref.shape)
    prow = lax.broadcasted_iota(jnp.int32, (LANES, N_FOX_HEADS * LANES), 0)
    pcol = lax.broadcasted_iota(jnp.int32, (LANES, N_FOX_HEADS * LANES), 1)
    place = (pcol == (prow % PIECE_PITCH) * LANES + HEAD_DIM + prow // PIECE_PITCH) & (prow % PIECE_PITCH < N_FOX_HEADS)
    caug = _dot(_pack_pieces(-cs), jnp.where(place, 1.0, 0.0).astype(bf16))
    ones3 = jnp.where((lane >= HEAD_DIM) & (lane < HEAD_DIM + 3), 1.0, 0.0)

    qk = _dot(xh, wm_ref[:, OFF_Q:OFF_V])
    q = qk[:, :W_FOX] * QK_SCALE
    k = qk[:, W_FOX:]
    kt_ref[...] = k.T
    kb = k.astype(bf16).astype(f32)
    qb = q.astype(bf16).astype(f32)
    squares = jnp.concatenate([(kb * kb).astype(bf16), (qb * qb).astype(bf16)], axis=1)
    srow = lax.broadcasted_iota(jnp.int32, (2 * W_FOX, LANES), 0)
    scol = lax.broadcasted_iota(jnp.int32, (2 * W_FOX, LANES), 1)
    target = (srow // W_FOX) * SUBLANES + (srow % W_FOX) // HEAD_DIM
    sums = _dot(squares, jnp.where(scol == target, 1.0, 0.0).astype(bf16))
    norms = jnp.sqrt(jnp.max(sums, axis=0, keepdims=True)) * NORM_SLACK
    kst_ref[:KST_CEND, :] = _col_bcast(norms)[:KST_CEND, :]
    kst_ref[KST_CEND:, :] = _col_bcast(cs[tm - 1:, :])[:SUBLANES, :]
    vu = _dot(xh, wm_ref[:, OFF_V:OFF_QM])
    v = vu[:, :W_FOX]
    vt_ref[...] = v.T
    u_ref[...] = vu[:, W_FOX:]
    for h in range(N_FOX_HEADS):
        qa_ref[h] = jnp.where(low, _head_block(q, h), ones3).astype(bf16)
        ka_ref[h] = jnp.where(low, _head_block(k, h), caug[:, LANES * h:LANES * (h + 1)]).astype(bf16)
        va_ref[h] = jnp.where(low, _head_block(v, h), 1.0).astype(bf16)
    rest = _dot(xh, wm_ref[:, OFF_QM:W_MAIN])
    qm_ref[...] = (rest[:, :W_MEM] * QK_SCALE).astype(bf16)
    sg_ref[...] = _silu(rest[:, W_MEM:]).astype(bf16)


def _inproj(l, x, g_all, wm_all, wf_all, bf_all, kv_prev, tm):
    B, L, D = x.shape
    depth = wm_all.shape[0]
    row = lambda w: pl.BlockSpec((None, tm, w), lambda b, i: (b, i, 0))
    heads = pl.BlockSpec((None, N_FOX_HEADS, tm, LANES), lambda b, i: (b, 0, i, 0))
    slab = pl.BlockSpec((None, None, W_FOX, tm), lambda b, i: (l, b, 0, i))
    aug = jax.ShapeDtypeStruct((B, N_FOX_HEADS, L, LANES), bf16)
    kvt = jax.ShapeDtypeStruct((depth, B, W_FOX, L), f32)
    out_shape = (
        aug,
        kvt,
        aug,
        kvt,
        aug,
        jax.ShapeDtypeStruct((B, L // tm, KST_ROWS, LANES), f32),
        jax.ShapeDtypeStruct((depth, N_FOX_HEADS, B * L), f32),
        jax.ShapeDtypeStruct((B, L, W_MIX), bf16),
        jax.ShapeDtypeStruct((B, L, W_SSM), f32),
        jax.ShapeDtypeStruct((B, L, W_MEM), bf16),
    )
    kst = pl.BlockSpec((None, None, KST_ROWS, LANES), lambda b, i: (b, i, 0, 0))
    lft = pl.BlockSpec((None, N_FOX_HEADS, tm), lambda b, i: (l, 0, b * (L // tm) + i))
    out_specs = (heads, slab, heads, slab, heads, kst, lft, row(W_MIX), row(W_SSM), row(W_MEM))
    in_specs = [row(D), _layer(l, (1, D)), _layer(l, wm_all.shape[1:]), _layer(l, wf_all.shape[1:]),
                _layer(l, (1, LANES))]
    n_in = len(in_specs)
    return pl.pallas_call(
        _inproj_kernel,
        grid=(B, L // tm),
        in_specs=in_specs + [pl.BlockSpec(memory_space=pl.ANY)] * len(kv_prev),
        out_specs=out_specs,
        out_shape=out_shape,
        input_output_aliases={n_in + n: out for n, out in zip(range(len(kv_prev)), (1, 3, 6))},
        scratch_shapes=[pltpu.VMEM((SUBLANES, LANES), f32)],
        compiler_params=_cparams(("parallel", "arbitrary")),
        name="prompt_inproj",
    )(x, g_all, wm_all, wf_all, bf_all, *kv_prev)


def _memkv_kernel(m_ref, g_ref, w_ref, k_ref, v_ref):
    mn = _rms(m_ref[...], g_ref[...]).astype(bf16)
    kv = _dot(mn, w_ref[...])
    k_ref[...] = kv[:, :W_MEM]
    v_ref[...] = kv[:, W_MEM:]


def _memkv(mem, g_all, w_all):
    B, N, D = mem.shape
    depth = w_all.shape[0]
    out = pl.BlockSpec((None, None, N, W_MEM), lambda l, b: (l, b, 0, 0))
    return pl.pallas_call(
        _memkv_kernel,
        grid=(depth, B),
        in_specs=[pl.BlockSpec((None, N, D), lambda l, b: (b, 0, 0)),
                  pl.BlockSpec((None, 1, D), lambda l, b: (l, 0, 0)),
                  pl.BlockSpec((None,) + w_all.shape[1:], lambda l, b: (l, 0, 0))],
        out_specs=(out, out),
        out_shape=(jax.ShapeDtypeStruct((depth, B, N, W_MEM), f32),) * 2,
        compiler_params=_cparams(("parallel", "parallel")),
        name="prompt_memkv",
    )(mem, g_all, w_all)


def _first_needed_block(kst_ref, h, i, n_diag):
    n_full = i * n_diag
    mine = pl.ds(n_full, n_diag)
    qn = jnp.max(kst_ref[mine, KST_QN + h, :], axis=0, keepdims=True)
    kn_own = jnp.max(kst_ref[mine, KST_KN + h, :], axis=0, keepdims=True)
    lb = -qn * kn_own - kst_ref[pl.ds(jnp.maximum(n_full - 1, 0), 1), KST_CEND + h, :]
    ub = qn * kst_ref[:, KST_KN + h, :] - kst_ref[:, KST_CEND + h, :]
    kj = lax.broadcasted_iota(jnp.int32, ub.shape, 0)
    needed = (ub - lb >= -SKIP_GAP) & (kj < n_full)
    return _full_min(jnp.where(needed, kj, n_full).astype(f32))


def _fox_kernel(q_ref, k_ref, v_ref, kst_ref, o_ref, acc_ref, m_ref, *, tq, tk):
    j = pl.program_id(1)
    i = pl.program_id(2)
    acc_ref[...] = jnp.zeros_like(acc_ref)
    m_ref[...] = jnp.full_like(m_ref, -jnp.inf)

    def step(kj, r0):
        k0 = pl.multiple_of(kj * tk, tk)
        rows = slice(0, tq) if r0 is None else slice(r0, tq)
        n = rows.stop - rows.start
        for e in range(2):
            s = _dot_nt(q_ref[e, rows, :], k_ref[e, pl.ds(k0, tk), :])
            if r0 is not None:
                row = r0 + lax.broadcasted_iota(jnp.int32, (n, tk), 0)
                col = r0 + lax.broadcasted_iota(jnp.int32, (n, tk), 1)
                s = jnp.where(col <= row, s, -jnp.inf)
            m_old = m_ref[e, rows, :]
            m_new = jnp.maximum(m_old, jnp.max(s, axis=1, keepdims=True))
            p = jnp.exp((s - jnp.tile(m_new, (1, tk // LANES))).astype(bf16))
            acc_ref[e, rows, :] = (jnp.exp(m_old - m_new) * acc_ref[e, rows, :]
                                   + _dot(p, v_ref[e, pl.ds(k0, tk), :]))
            m_ref[e, rows, :] = m_new

    n_diag = tq // tk
    first = jnp.minimum(*[_first_needed_block(kst_ref, 2 * j + e, i, n_diag) for e in range(2)])
    t_first = jnp.min(first).astype(jnp.int32) // n_diag

    def body(t, carry):
        for d in range(n_diag):
            step(t * n_diag + d, None)
        return carry

    lax.fori_loop(t_first, i, body, 0)
    for d in range(n_diag):
        step(i * n_diag + d, d * tk)
    lane = lax.broadcasted_iota(jnp.int32, (tq, LANES), 1)
    a0 = acc_ref[0]
    a1 = acc_ref[1]
    o_ref[...] = jnp.where(lane < HEAD_DIM, a0 / pltpu.roll(a0, HEAD_DIM, 1), pltpu.roll(a1, HEAD_DIM, 1) / a1)


def _fox(qa, ka, va, kst, tq, tk):
    B, _, L, _ = qa.shape
    n_pairs = W_FOX // LANES
    assert kst.shape[1] == L // tk
    return pl.pallas_call(
        functools.partial(_fox_kernel, tq=tq, tk=tk),
        grid=(B, n_pairs, L // tq),
        in_specs=[pl.BlockSpec((None, 2, tq, LANES), lambda b, j, i: (b, j, i, 0)),
                  pl.BlockSpec((None, 2, L, LANES), lambda b, j, i: (b, j, 0, 0)),
                  pl.BlockSpec((None, 2, L, LANES), lambda b, j, i: (b, j, 0, 0)),
                  pl.BlockSpec((None,) + kst.shape[1:], lambda b, j, i: (b, 0, 0, 0))],
        out_specs=pl.BlockSpec((None, tq, LANES), lambda b, j, i: (b, i, j)),
        out_shape=jax.ShapeDtypeStruct((B, L, W_FOX), f32),
        scratch_shapes=[pltpu.VMEM((2, tq, LANES), f32), pltpu.VMEM((2, tq, LANES), f32)],
        compiler_params=_cparams(("parallel", "parallel", "arbitrary")),
        name="prompt_fox",
    )(qa, ka, va, kst)


def _s5prep_kernel(lr_ref, li_ref, ldt_ref, bre_ref, bim_ref, cre_ref, cim_ref,
                   lamb_ref, bmat_ref, cmat_ref, lpow_ref, lams_ref, pw_ref, *, seg_len):
    lr = lr_ref[...]
    li = li_ref[...]
    dt = jnp.exp(ldt_ref[...])
    ar = lr * dt
    ai = li * dt
    mag = jnp.exp(ar)
    lbr = mag * jnp.cos(ai)
    lbi = mag * jnp.sin(ai)
    lamb_ref[:, :HALF] = jnp.broadcast_to(lbr, (SUBLANES, HALF))
    lamb_ref[:, HALF:] = jnp.broadcast_to(lbi, (SUBLANES, HALF))
    den = lr * lr + li * li
    zr = ((lbr - 1.0) * lr + lbi * li) / den
    zi = (lbi * lr - (lbr - 1.0) * li) / den
    bre = bre_ref[...]
    bim = bim_ref[...]
    bmat_ref[:, :HALF] = zr * bre - zi * bim
    bmat_ref[:, HALF:] = zr * bim + zi * bre
    cmat_ref[:HALF, :] = cre_ref[...]
    cmat_ref[HALF:, :] = -cim_ref[...]
    jj = (lax.broadcasted_iota(jnp.int32, (SUBLANES, HALF), 0) + 1).astype(f32)
    magp = jnp.exp(ar * jj)
    pw_ref[:SUBLANES, :HALF] = magp * jnp.cos(ai * jj)
    pw_ref[:SUBLANES, HALF:] = magp * jnp.sin(ai * jj)
    n = SUBLANES
    while n < seg_len:
        sr = pw_ref[n - 1:n, :HALF]
        si = pw_ref[n - 1:n, HALF:]
        pr = pw_ref[:n, :HALF]
        pi = pw_ref[:n, HALF:]
        pw_ref[n:2 * n, :HALF] = pr * sr - pi * si
        pw_ref[n:2 * n, HALF:] = pr * si + pi * sr
        n *= 2
    lpow_ref[...] = pw_ref[...].astype(bf16)
    lams_ref[...] = pw_ref[seg_len - SUBLANES:, :]


def _s5prep(lam_re, lam_im, log_dt, b_re, b_im, c_re, c_im, seg_len):
    depth = lam_re.shape[0]
    assert seg_len >= SUBLANES and seg_len & (seg_len - 1) == 0
    nb, gb = N_SSM_BLOCKS, GROUPS_PER_BLOCK
    eye = jnp.eye(gb, dtype=f32)
    half = lambda a: a.reshape(depth, nb, 1, HALF)
    ldt = jnp.broadcast_to(log_dt[:, :, None], (depth, N_SSM_GROUPS, SSM_STATE))

    def embed_b(b):
        t = b.reshape(depth, nb, gb, SSM_STATE, SSM_GROUP).transpose(0, 1, 2, 4, 3)
        return (t[:, :, :, :, None, :] * eye[None, None, :, None, :, None]).reshape(depth, nb, LANES, HALF)

    def embed_c(c):
        t = c.reshape(depth, nb, gb, SSM_GROUP, SSM_STATE).transpose(0, 1, 2, 4, 3)
        return (t[:, :, :, :, None, :] * eye[None, None, :, None, :, None]).reshape(depth, nb, HALF, LANES)

    blk = lambda shape: pl.BlockSpec((None, None) + shape, lambda l, cb: (l, cb) + (0,) * len(shape))
    arr = lambda shape, dt: jax.ShapeDtypeStruct((depth, nb) + shape, dt)
    return pl.pallas_call(
        functools.partial(_s5prep_kernel, seg_len=seg_len),
        grid=(depth, nb),
        in_specs=[blk((1, HALF))] * 3 + [blk((LANES, HALF))] * 2 + [blk((HALF, LANES))] * 2,
        out_specs=(blk((SUBLANES, STATE_W)), blk((LANES, STATE_W)), blk((STATE_W, LANES)),
                   blk((seg_len, STATE_W)), blk((SUBLANES, STATE_W))),
        out_shape=(arr((SUBLANES, STATE_W), f32),
                   arr((LANES, STATE_W), f32),
                   arr((STATE_W, LANES), f32),
                   arr((seg_len, STATE_W), bf16),
                   arr((SUBLANES, STATE_W), f32)),
        scratch_shapes=[pltpu.VMEM((seg_len, STATE_W), f32)],
        compiler_params=_cparams(("parallel", "parallel")),
        name="s5_tables",
    )(half(lam_re), half(lam_im), half(ldt), embed_b(b_re), embed_b(b_im), embed_c(c_re), embed_c(c_im))


def _s5scan_kernel(u_ref, bmat_ref, cmat_ref, lamb_ref, dsk_ref, y_ref, hend_ref,
                   up_ref, hb_ref, hq_ref, hcar_ref, *, tj):
    j = pl.program_id(1)
    nb = N_SSM_BLOCKS

    @pl.when(j == 0)
    def _():
        hcar_ref[...] = jnp.zeros_like(hcar_ref)

    for cb in range(nb):
        for s in range(N_SEG):
            up_ref[cb, pl.ds(s, tj, stride=N_SEG), :] = u_ref[s, :, LANES * cb:LANES * (cb + 1)]
        hb_ref[cb] = _dot(up_ref[cb].astype(bf16), bmat_ref[cb])

    def rec(t2, h):
        r0 = pl.multiple_of(t2 * 2 * N_SEG, 2 * N_SEG)
        new = []
        for cb in range(nb):
            hr, hi = h[2 * cb], h[2 * cb + 1]
            lr = lamb_ref[cb, :, :HALF]
            li = lamb_ref[cb, :, HALF:]
            rs, is_ = [], []
            for d in range(2):
                bu = hb_ref[cb, pl.ds(r0 + d * N_SEG, N_SEG), :]
                hr, hi = lr * hr - li * hi + bu[:, :HALF], lr * hi + li * hr + bu[:, HALF:]
                rs.append(hr)
                is_.append(hi)
            hq_ref[cb, pl.ds(r0, 2 * N_SEG), :HALF] = jnp.concatenate(rs, axis=0).astype(bf16)
            hq_ref[cb, pl.ds(r0, 2 * N_SEG), HALF:] = jnp.concatenate(is_, axis=0).astype(bf16)
            new += [hr, hi]
        return tuple(new)

    h0 = tuple(hcar_ref[cb, :, HALF * part:HALF * (part + 1)] for cb in range(nb) for part in range(2))
    h = lax.fori_loop(0, tj // 2, rec, h0, unroll=True)
    for cb in range(nb):
        hcar_ref[cb, :, :HALF] = h[2 * cb]
        hcar_ref[cb, :, HALF:] = h[2 * cb + 1]
        up_ref[cb] = _dot(hq_ref[cb], cmat_ref[cb]) + dsk_ref[cb] * up_ref[cb]
        for s in range(N_SEG):
            y_ref[s, :, LANES * cb:LANES * (cb + 1)] = up_ref[cb, pl.ds(s, tj, stride=N_SEG), :]

    @pl.when(j == pl.num_programs(1) - 1)
    def _():
        hend_ref[...] = hcar_ref[...]


def _s5scan(l, u, bmat_all, cmat_all, lamb_all, dsk_all, tj):
    B, L, _ = u.shape
    seg_len = L // N_SEG
    u4 = u.reshape(B, N_SEG, seg_len, W_SSM)
    nb = N_SSM_BLOCKS
    tile = pl.BlockSpec((None, N_SEG, tj, W_SSM), lambda b, j: (b, 0, j, 0))
    y4, hend = pl.pallas_call(
        functools.partial(_s5scan_kernel, tj=tj),
        grid=(B, seg_len // tj),
        in_specs=[tile, _layer(l, bmat_all.shape[1:]), _layer(l, cmat_all.shape[1:]),
                  _layer(l, lamb_all.shape[1:]), _layer(l, dsk_all.shape[1:])],
        out_specs=(tile, pl.BlockSpec((None, nb, N_SEG, STATE_W), lambda b, j: (b, 0, 0, 0))),
        out_shape=(jax.ShapeDtypeStruct((B, N_SEG, seg_len, W_SSM), f32),
                   jax.ShapeDtypeStruct((B, nb, N_SEG, STATE_W), f32)),
        scratch_shapes=[pltpu.VMEM((nb, N_SEG * tj, LANES), f32), pltpu.VMEM((nb, N_SEG * tj, STATE_W), f32),
                        pltpu.VMEM((nb, N_SEG * tj, STATE_W), bf16), pltpu.VMEM((nb, N_SEG, STATE_W), f32)],
        compiler_params=_cparams(("parallel", "arbitrary")),
        name="prompt_s5_scan",
    )(u4, bmat_all, cmat_all, lamb_all, dsk_all)
    return y4.reshape(B, L, W_SSM), hend


def _s5fix_kernel(hend_ref, lams_ref, cmat_ref, hlast_ref, cp_ref, hs_ref):
    he = hend_ref[...]
    lsr = lams_ref[SUBLANES - 1:, :HALF]
    lsi = lams_ref[SUBLANES - 1:, HALF:]
    hs_ref[...] = jnp.zeros_like(hs_ref)
    hr = jnp.zeros((1, HALF), f32)
    hi = jnp.zeros((1, HALF), f32)
    for s in range(N_SEG):
        hs_ref[s:s + 1, :HALF] = hr
        hs_ref[s:s + 1, HALF:] = hi
        nr = lsr * hr - lsi * hi + he[s:s + 1, :HALF]
        ni = lsr * hi + lsi * hr + he[s:s + 1, HALF:]
        hr, hi = nr, ni
    hlast_ref[:, :HALF] = hr
    hlast_ref[:, HALF:] = hi
    hst_r = hs_ref[:, :HALF].T
    hst_i = hs_ref[:, HALF:].T
    cre = cmat_ref[:HALF, :]
    mci = cmat_ref[HALF:, :]
    for s in range(N_SEG):
        col_r = hst_r[:, s:s + 1]
        col_i = hst_i[:, s:s + 1]
        cp_ref[s, :HALF, :] = (cre * col_r + mci * col_i).astype(bf16)
        cp_ref[s, HALF:, :] = (mci * col_r - cre * col_i).astype(bf16)


def _s5fix(l, hend, lams_all, cmat_all):
    B, nb = hend.shape[:2]
    return pl.pallas_call(
        _s5fix_kernel,
        grid=(B, nb),
        in_specs=[pl.BlockSpec((None, None, N_SEG, STATE_W), lambda b, cb: (b, cb, 0, 0)),
                  pl.BlockSpec((None, None, SUBLANES, STATE_W), lambda b, cb: (l, cb, 0, 0)),
                  pl.BlockSpec((None, None, STATE_W, LANES), lambda b, cb: (l, cb, 0, 0))],
        out_specs=(pl.BlockSpec((None, None, 1, STATE_W), lambda b, cb: (b, cb, 0, 0)),
                   pl.BlockSpec((None, N_SEG, None, STATE_W, LANES), lambda b, cb: (b, 0, cb, 0, 0))),
        out_shape=(jax.ShapeDtypeStruct((B, nb, 1, STATE_W), f32),
                   jax.ShapeDtypeStruct((B, N_SEG, nb, STATE_W, LANES), bf16)),
        scratch_shapes=[pltpu.VMEM((LANES, STATE_W), f32)],
        compiler_params=_cparams(("parallel", "parallel")),
        name="prompt_s5_fix",
    )(hend, lams_all, cmat_all)


def _pair_attend(q, k, v):
    lane = lax.broadcasted_iota(jnp.int32, q.shape, 1)
    zero = jnp.zeros_like(q)
    outs = []
    for e in range(2):
        qe = jnp.where((lane < HEAD_DIM) == (e == 0), q, zero)
        s = _dot_nt(qe, k)
        p = jnp.exp(s - jnp.max(s, axis=1, keepdims=True))
        outs.append(_dot(p.astype(bf16), v) / jnp.sum(p, axis=1, keepdims=True))
    return jnp.where(lane < HEAD_DIM, outs[0], outs[1])


N_OUTPROJ_INPUTS = 13


def _outproj_stages(x_ref, fox_ref, y_ref, lp_ref, cp_ref, qm_ref, mk_ref, mv_ref, sg_ref,
                    wglu_ref, bglu_ref, wout_ref, gfin_ref, o_ref, *, final, tiles_per_seg):
    tm = x_ref.shape[0]
    j0 = pl.multiple_of((pl.program_id(1) % tiles_per_seg) * tm, tm)
    corr = [_dot(lp_ref[cb, pl.ds(j0, tm), :], cp_ref[cb]) for cb in range(N_SSM_BLOCKS)]
    ssm_y = y_ref[...] + jnp.concatenate(corr, axis=1)
    yield
    s = _gelu(ssm_y)
    s = s * jax.nn.sigmoid(_dot(s.astype(bf16), wglu_ref[...]) + bglu_ref[...])
    yield
    qm = qm_ref[...]
    mk = mk_ref[...].astype(bf16)
    mv = mv_ref[...].astype(bf16)
    mem = []
    for jp in range(W_MEM // LANES):
        mem.append(_pair_attend(qm[:, LANES * jp:LANES * (jp + 1)], mk[:, LANES * jp:LANES * (jp + 1)],
                                mv[:, LANES * jp:LANES * (jp + 1)]))
        yield
    sg = sg_ref[...].astype(f32)
    mix = jnp.concatenate([fox_ref[...] * sg[:, :W_FOX],
                           s * sg[:, W_FOX:W_FOX + W_SSM],
                           jnp.concatenate(mem, axis=1) * sg[:, W_FOX + W_SSM:]], axis=1)
    xo = x_ref[...] + _dot(mix.astype(bf16), wout_ref[...])
    yield
    if final:
        xo = _rms(xo, gfin_ref[...])
    o_ref[...] = xo


def _s_inproj_kernel(x_ref, g_ref, w_ref, z_ref):
    xn = _rms(x_ref[...], g_ref[...])
    z_ref[...] = _dot3(xn, w_ref[...])


def _s_inproj(l, xs, g_all, wcat_all, tn):
    R, D = xs.shape
    return pl.pallas_call(
        _s_inproj_kernel,
        grid=(W_CAT // tn,),
        in_specs=[pl.BlockSpec((R, D), lambda n: (0, 0)), _layer(l, (1, D)),
                  pl.BlockSpec((None, D, tn), lambda n: (l, 0, n))],
        out_specs=pl.BlockSpec((R, tn), lambda n: (0, n)),
        out_shape=jax.ShapeDtypeStruct((R, W_CAT), f32),
        compiler_params=_cparams(("parallel",)),
        name="sample_inproj",
    )(xs, g_all, wcat_all)


def _full_max(x):
    return jnp.max(jnp.max(x, axis=0, keepdims=True), axis=1, keepdims=True)


def _full_min(x):
    return jnp.min(jnp.min(x, axis=0, keepdims=True), axis=1, keepdims=True)


def _full_sum(x):
    return jnp.sum(jnp.sum(x, axis=0, keepdims=True), axis=1, keepdims=True)


def _lane_sums(acc):
    ones = jnp.ones((SUBLANES, LANES), bf16)
    hi, lo = _split2(acc)
    return (_dot_nt(ones, hi) + _dot_nt(ones, lo))[0:1, :]


def _col_bcast(row):
    return jnp.broadcast_to(row, (LANES, row.shape[1])).T


class _DecodeGuest:
    N_FIXED_IN, N_OUT, N_SCRATCH = 3, 3, 5

    def __init__(self, l, n_seq, first_seq, n_steps, step_of, page_table, z, bf_all, ckt, cvt, lft, mkt, mvt):
        R, n_pages = page_table.shape
        n_pool, page = ckt.shape[1], ckt.shape[4]
        n_mem = mkt.shape[4]
        assert page == LANES and n_mem % LANES == 0 and first_seq % n_seq == 0
        assert first_seq + n_steps * n_seq <= R
        self.n_seq, self.n_pages, self.first_seq, self.step_of = n_seq, n_pages, first_seq, step_of
        self.per_seq = 2 * n_pages + 2
        z_blk = lambda *g: (first_seq // n_seq + step_of(*g[:-1]), 0, 0)
        self.in_specs = [pl.BlockSpec((n_seq, 1, W_CAT), z_blk), _layer(l, (1, LANES)),
                         _layer(l, (N_FOX_HEADS, n_pool, page))]
        self.operands = [z.reshape(R, 1, W_CAT), bf_all, lft]
        for s in range(n_seq):
            seq = lambda g, s=s: first_seq + step_of(*g) * n_seq + s
            paged = [pl.BlockSpec((None, None, N_FOX_HEADS, HEAD_DIM, page),
                                  lambda *g, seq=seq, p=p: (l, g[-1][seq(g[:-1]), p], 0, 0, 0))
                     for p in range(n_pages)]
            mem_spec = pl.BlockSpec((None, None, N_MEM_HEADS, HEAD_DIM, n_mem),
                                    lambda *g, seq=seq: (l, seq(g[:-1]), 0, 0, 0))
            self.in_specs += paged + paged + [mem_spec, mem_spec]
            self.operands += [ckt] * n_pages + [cvt] * n_pages + [mkt, mvt]
        out = lambda w: pl.BlockSpec((n_seq, 1, w), lambda *g: (step_of(*g[:-1]), 0, 0))
        self.out_specs = [out(W_FOX), out(W_MEM), out(LANES)]
        n_out = n_steps * n_seq
        self.out_shape = [jax.ShapeDtypeStruct((n_out, 1, w), f32) for w in (W_FOX, W_MEM, LANES)]
        n_rows = N_FOX_HEADS * n_pages
        self.scratch_shapes = [pltpu.VMEM((n_seq, rows, LANES), f32)
                               for rows in (n_rows, n_rows, n_rows, W_FOX, W_MEM)]

    def chains(self, pt_ref, step, ins, outs, scratch):
        z_ref, bf_ref, lfc_ref = ins[:self.N_FIXED_IN]
        fox_ref, mem_ref, lf_ref = outs
        n_pages = self.n_pages
        result = []
        for i in range(self.n_seq):
            mine = ins[self.N_FIXED_IN + i * self.per_seq:self.N_FIXED_IN + (i + 1) * self.per_seq]
            result.append(_decode_one(
                pt_ref, self.first_seq + step * self.n_seq + i, z_ref.at[i], bf_ref, lfc_ref,
                mine[:n_pages], mine[n_pages:2 * n_pages], mine[2 * n_pages], mine[2 * n_pages + 1],
                fox_ref.at[i], mem_ref.at[i], lf_ref.at[i], *[s.at[i] for s in scratch]))
        return result

    @property
    def n_in(self):
        return self.N_FIXED_IN + self.n_seq * self.per_seq


def _run_chains(chains):
    while chains:
        chains = [c for c in chains if next(c, _DONE) is not _DONE]


def _outproj_decode_kernel(pt_ref, *refs, guest, final, tiles_per_seg):
    n_in = N_OUTPROJ_INPUTS + guest.n_in
    op_in, g_in = refs[:N_OUTPROJ_INPUTS], refs[N_OUTPROJ_INPUTS:n_in]
    o_ref = refs[n_in]
    g_out = refs[n_in + 1:n_in + 1 + guest.N_OUT]
    g_scr = refs[n_in + 1 + guest.N_OUT:]
    step = guest.step_of(pl.program_id(0), pl.program_id(1))
    _run_chains([_outproj_stages(*op_in, o_ref, final=final, tiles_per_seg=tiles_per_seg)]
                + guest.chains(pt_ref, step, g_in, g_out, g_scr))


_DONE = object()


def _decode_one(pt_ref, r, z_ref, bf_ref, lfc_ref, ck, cv, mk_ref, mv_ref, fox_ref, mem_ref, lf_ref,
                s_scr, lfs_scr, p_scr, acc_scr, accm_scr):
    n_pages = len(ck)
    z = z_ref[...]
    q = z[:, OFF_Q:OFF_Q + W_FOX] * QK_SCALE
    k_new = z[:, OFF_K:OFF_K + W_FOX]
    v_new = z[:, OFF_V:OFF_V + W_FOX]
    lf_new = _log_sigmoid(z[:, OFF_F:OFF_F + LANES] + bf_ref[...])
    lf_ref[...] = lf_new

    qt = _col_bcast(q)
    for h in range(N_FOX_HEADS):
        qh = qt[HEAD_DIM * h:HEAD_DIM * (h + 1), :]
        for p in range(n_pages):
            row = h * n_pages + p
            s_scr[row:row + 1, :] = jnp.sum(ck[p][h] * qh, axis=0, keepdims=True)
            lfs_scr[row:row + 1, :] = lfc_ref[h, pl.ds(pt_ref[r, p], 1), :]
        yield

    n_rows = N_FOX_HEADS * n_pages
    ur = lax.broadcasted_iota(jnp.int32, (LANES, LANES), 0)
    uc = lax.broadcasted_iota(jnp.int32, (LANES, LANES), 1)
    upper = jnp.where(ur <= uc, 1.0, 0.0).astype(bf16)
    cs = sum(_dot(pc, upper) for pc in _split3(lfs_scr[...]))
    tot = jnp.broadcast_to(cs[:, LANES - 1:], (n_rows, LANES))
    mr = lax.broadcasted_iota(jnp.int32, (n_rows, n_rows), 0)
    mc = lax.broadcasted_iota(jnp.int32, (n_rows, n_rows), 1)
    before = jnp.where((mc < mr) & (mc // n_pages == mr // n_pages), 1.0, 0.0).astype(bf16)
    c = cs + sum(_dot(before, pc) for pc in _split3(tot))
    s_all = s_scr[...] - c
    yield

    qk_new = q * k_new
    lane_head = lax.broadcasted_iota(jnp.int32, (1, W_FOX), 1) // HEAD_DIM
    pn_exp = jnp.zeros((1, W_FOX), f32)
    l_exp = jnp.zeros((1, W_FOX), f32)
    for h in range(N_FOX_HEADS):
        rows = slice(h * n_pages, (h + 1) * n_pages)
        s_h = s_all[rows, :]
        c_new = c[(h + 1) * n_pages - 1:(h + 1) * n_pages, LANES - 1:] + lf_new[:, h:h + 1]
        s_new = jnp.sum(qk_new[:, HEAD_DIM * h:HEAD_DIM * (h + 1)], axis=1, keepdims=True) - c_new
        m = jnp.maximum(_full_max(s_h), s_new)
        p_h = jnp.exp(s_h - m)
        pn = jnp.exp(s_new - m)
        p_scr[rows, :] = p_h
        pn_exp = jnp.where(lane_head == h, pn, pn_exp)
        l_exp = jnp.where(lane_head == h, _full_sum(p_h) + pn, l_exp)
        acc = jnp.zeros((HEAD_DIM, LANES), f32)
        for p in range(n_pages):
            acc = acc + cv[p][h] * p_scr[h * n_pages + p:h * n_pages + p + 1, :]
        acc_scr[HEAD_DIM * h:HEAD_DIM * (h + 1), :] = acc
        yield
    fox_ref[...] = (_lane_sums(acc_scr[...]) + pn_exp * v_new) / l_exp
    yield

    qm = z[:, OFF_QM:OFF_QM + W_MEM] * QK_SCALE
    qmt = _col_bcast(qm)
    n_halves = mk_ref.shape[2] // LANES
    mem_head = lax.broadcasted_iota(jnp.int32, (1, W_MEM), 1) // HEAD_DIM
    lm_exp = jnp.zeros((1, W_MEM), f32)
    for h in range(N_MEM_HEADS):
        qh = qmt[HEAD_DIM * h:HEAD_DIM * (h + 1), :]
        ss = [jnp.sum(mk_ref[h, :, LANES * t:LANES * (t + 1)] * qh, axis=0, keepdims=True) for t in range(n_halves)]
        m = functools.reduce(jnp.maximum, [jnp.max(s, axis=1, keepdims=True) for s in ss])
        ps = [jnp.exp(s - m) for s in ss]
        lm = sum(jnp.sum(p, axis=1, keepdims=True) for p in ps)
        lm_exp = jnp.where(mem_head == h, lm, lm_exp)
        accm_scr[HEAD_DIM * h:HEAD_DIM * (h + 1), :] = sum(
            mv_ref[h, :, LANES * t:LANES * (t + 1)] * ps[t] for t in range(n_halves))
    mem_ref[...] = _lane_sums(accm_scr[...]) / lm_exp


def _outproj_decode(l, x, fox_o, y, lpow_all, cp, qm, mk_all, mv_all, sg, wglu_all, bglu_all, wout_all, gfin,
                    page_table, z, bf_all, ckt, cvt, lft, mkt, mvt, first_seq, tm, final):
    B, L, D = x.shape
    seg_len = L // N_SEG
    tiles_per_seg = seg_len // tm
    n_i = L // tm
    n_steps = B * n_i
    n_left = page_table.shape[0] - first_seq
    assert n_left % n_steps == 0
    guest = _DecodeGuest(l, n_left // n_steps, first_seq, n_steps, lambda b, i: b * n_i + i,
                         page_table, z, bf_all, ckt, cvt, lft, mkt, mvt)
    nb = N_SSM_BLOCKS
    row = lambda w: pl.BlockSpec((None, tm, w), lambda b, i, pt: (b, i, 0))
    mem_kv = pl.BlockSpec((None, None) + mk_all.shape[2:], lambda b, i, pt: (l, b, 0, 0))
    in_specs = [row(D), row(W_FOX), row(W_SSM), _layer(l, lpow_all.shape[1:]),
                pl.BlockSpec((None, None, nb, STATE_W, LANES), lambda b, i, pt: (b, i // tiles_per_seg, 0, 0, 0)),
                row(W_MEM), mem_kv, mem_kv, row(D),
                _layer(l, wglu_all.shape[1:]), _layer(l, (1, W_SSM)), _layer(l, wout_all.shape[1:]),
                pl.BlockSpec((1, D), lambda b, i, pt: (0, 0))]
    assert len(in_specs) == N_OUTPROJ_INPUTS
    xo, *dec = pl.pallas_call(
        functools.partial(_outproj_decode_kernel, guest=guest, final=final, tiles_per_seg=tiles_per_seg),
        grid_spec=pltpu.PrefetchScalarGridSpec(
            num_scalar_prefetch=1,
            grid=(B, n_i),
            in_specs=in_specs + guest.in_specs,
            out_specs=[row(D)] + guest.out_specs,
            scratch_shapes=guest.scratch_shapes),
        out_shape=[jax.ShapeDtypeStruct((B, L, D), f32)] + guest.out_shape,
        compiler_params=_cparams(("parallel", "parallel")),
        name="outproj_decode",
    )(page_table, x, fox_o, y, lpow_all, cp, qm, mk_all, mv_all, sg, wglu_all, bglu_all, wout_all, gfin,
      *guest.operands)
    return xo, dec


def _s_out_kernel(x_ref, z_ref, fox_ref, mem_ref, h0_ref, bmat_ref, cmat_ref, lamb_ref, dsk_ref,
                  wglu_ref, bglu_ref, wout_ref, gfin_ref, o_ref, h_ref, *, final):
    z = z_ref[...]
    u = z[:, OFF_U:OFF_U + W_SSM]
    ys = []
    for cb in range(N_SSM_BLOCKS):
        ucb = u[:, LANES * cb:LANES * (cb + 1)]
        bu = _dot3(ucb, bmat_ref[cb])
        lr = lamb_ref[cb, 0:1, :HALF]
        li = lamb_ref[cb, 0:1, HALF:]
        h0r = h0_ref[:, STATE_W * cb:STATE_W * cb + HALF]
        h0i = h0_ref[:, STATE_W * cb + HALF:STATE_W * (cb + 1)]
        hr = lr * h0r - li * h0i + bu[:, :HALF]
        hi = lr * h0i + li * h0r + bu[:, HALF:]
        h_ref[:, STATE_W * cb:STATE_W * cb + HALF] = hr
        h_ref[:, STATE_W * cb + HALF:STATE_W * (cb + 1)] = hi
        ys.append(_dot3(jnp.concatenate([hr, hi], axis=1), cmat_ref[cb]) + dsk_ref[cb] * ucb)
    s = _gelu(jnp.concatenate(ys, axis=1))
    s = s * jax.nn.sigmoid(_dot3(s, wglu_ref[...]) + bglu_ref[...])
    sg = _silu(z[:, OFF_G:W_MAIN])
    mix = jnp.concatenate([fox_ref[...] * sg[:, :W_FOX],
                           s * sg[:, W_FOX:W_FOX + W_SSM],
                           mem_ref[...] * sg[:, W_FOX + W_SSM:]], axis=1)
    xo = x_ref[...] + _dot3(mix, wout_ref[...])
    if final:
        xo = _rms(xo, gfin_ref[...])
    o_ref[...] = xo


def _s_out(l, xs, z, fox, mem, h0_all, bmat_all, cmat_all, lamb_all, dsk_all, wglu_all, bglu_all, wout_all,
           gfin, final):
    R, D = xs.shape
    whole = lambda a: pl.BlockSpec(a.shape, lambda i: (0,) * a.ndim)
    hspec = pl.BlockSpec((R, h0_all.shape[2]), lambda i: (0, 0))
    return pl.pallas_call(
        functools.partial(_s_out_kernel, final=final),
        grid=(1,),
        in_specs=[whole(xs), whole(z), whole(fox), whole(mem), _layer(l, h0_all.shape[1:]),
                  _layer(l, bmat_all.shape[1:]), _layer(l, cmat_all.shape[1:]), _layer(l, lamb_all.shape[1:]),
                  _layer(l, dsk_all.shape[1:]), _layer(l, wglu_all.shape[1:]), _layer(l, (1, W_SSM)),
                  _layer(l, wout_all.shape[1:]), whole(gfin)],
        out_specs=(whole(xs), hspec),
        out_shape=(jax.ShapeDtypeStruct((R, D), f32), jax.ShapeDtypeStruct(h0_all.shape[1:], f32)),
        compiler_params=_cparams(("arbitrary",)),
        name="sample_out",
    )(xs, z, fox, mem, h0_all, bmat_all, cmat_all, lamb_all, dsk_all, wglu_all, bglu_all, wout_all, gfin)


def _state_to_cols(h):
    lead = h.shape[:-3]
    n = len(lead)
    t = h.reshape(lead + (N_SSM_BLOCKS, GROUPS_PER_BLOCK, SSM_STATE, 2))
    t = t.transpose(tuple(range(n)) + (n, n + 3, n + 1, n + 2))
    return t.reshape(lead + (N_SSM_BLOCKS * STATE_W,))


def _cols_to_state(c):
    lead = c.shape[:-1]
    n = len(lead)
    t = c.reshape(lead + (N_SSM_BLOCKS, 2, GROUPS_PER_BLOCK, SSM_STATE))
    t = t.transpose(tuple(range(n)) + (n, n + 2, n + 3, n + 1))
    return t.reshape(lead + (N_SSM_GROUPS, SSM_STATE, 2))


def _tile(n, pref):
    t = min(n, pref)
    assert n % t == 0
    return t


def kernel(x_prompt, x_sample, mem_prompt, cache_k, cache_v, cache_logf, state_ssm, cache_mem_k, cache_mem_v,
           page_table, g_norm, w_in, b_f, lam_re, lam_im, log_dt, b_re, b_im, c_re, c_im, d_skip, w_glu, b_glu,
           g_mem, w_mem_kv, w_out, g_final):
    depth = w_in.shape[0]
    B, L, D = x_prompt.shape
    R = x_sample.shape[0]
    assert x_sample.shape[1] == 1 and L % N_SEG == 0
    seg_len = L // N_SEG
    tm = _tile(seg_len, 256)
    tq = _tile(L, 1024)
    tk = _tile(tq, 512)
    tj = _tile(seg_len, 128)

    wcat_all, wm_all, wf_all = _stage_w(w_in, _tile(D, 256))
    bf_all = jnp.pad(b_f, ((0, 0), (0, LANES - N_FOX_HEADS))).reshape(depth, 1, LANES)
    gn_all = g_norm.reshape(depth, 1, D)
    gmem_all = g_mem.reshape(depth, 1, D)
    bglu_all = b_glu.reshape(depth, 1, W_SSM)
    gfin = g_final.reshape(1, D)
    dsk_all = d_skip.reshape(depth, N_SSM_BLOCKS, 1, LANES)
    wglu_bf = w_glu.astype(bf16)
    wout_bf = w_out.astype(bf16)
    ckt = jnp.transpose(cache_k, (0, 1, 3, 4, 2))
    cvt = jnp.transpose(cache_v, (0, 1, 3, 4, 2))
    lft = jnp.transpose(cache_logf, (0, 3, 1, 2))
    mkt = jnp.transpose(cache_mem_k, (0, 1, 3, 4, 2))
    mvt = jnp.transpose(cache_mem_v, (0, 1, 3, 4, 2))
    h0_all = _state_to_cols(state_ssm)

    lamb_all, bmat_all, cmat_all, lpow_all, lams_all = _s5prep(lam_re, lam_im, log_dt, b_re, b_im, c_re, c_im, seg_len)
    bmat_bf = bmat_all.astype(bf16)
    cmat_bf = cmat_all.astype(bf16)
    mk_all, mv_all = _memkv(mem_prompt, gmem_all, w_mem_kv.astype(bf16))

    xp = x_prompt
    xs = x_sample.reshape(R, D)
    outs = {n: [] for n in ("hp", "z", "lfs", "hs")}
    kv_prev = ()
    for l in range(depth):
        final = l == depth - 1
        qa, kt_all, ka, vt_all, va, kst, lfp_all, sg, u, qm = _inproj(l, xp, gn_all, wm_all, wf_all, bf_all,
                                                                      kv_prev, tk)
        kv_prev = (kt_all, vt_all, lfp_all)
        fox_o = _fox(qa, ka, va, kst, tq, tk)
        y, hend = _s5scan(l, u, bmat_bf, cmat_bf, lamb_all, dsk_all, tj)
        hlast, cp = _s5fix(l, hend, lams_all, cmat_all)
        outs["hp"].append(hlast.reshape(B, N_SSM_BLOCKS * STATE_W))

        z = _s_inproj(l, xs, gn_all, wcat_all, 256)
        xp, dec = _outproj_decode(
            l, xp, fox_o, y, lpow_all, cp, qm, mk_all, mv_all, sg, wglu_bf, bglu_all, wout_bf, gfin,
            page_table, z, bf_all, ckt, cvt, lft, mkt, mvt, 0, tm, final)
        fox_s, mem_s, lf_s = (d.reshape(R, -1) for d in dec)
        xs, h_new = _s_out(l, xs, z, fox_s, mem_s, h0_all, bmat_all, cmat_all, lamb_all, dsk_all,
                           w_glu, bglu_all, w_out, gfin, final)
        outs["z"].append(z)
        outs["lfs"].append(lf_s)
        outs["hs"].append(h_new)

    st = lambda n: jnp.stack(outs[n])
    z_all = st("z")
    n_mem = mk_all.shape[2]
    return (xp, xs.reshape(R, 1, D),
            kt_all.reshape(depth, B, N_FOX_HEADS, HEAD_DIM, L).transpose(0, 1, 4, 2, 3),
            vt_all.reshape(depth, B, N_FOX_HEADS, HEAD_DIM, L).transpose(0, 1, 4, 2, 3),
            lfp_all.reshape(depth, N_FOX_HEADS, B, L).transpose(0, 2, 3, 1),
            _cols_to_state(st("hp")),
            mk_all.reshape(depth, B, n_mem, N_MEM_HEADS, HEAD_DIM),
            mv_all.reshape(depth, B, n_mem, N_MEM_HEADS, HEAD_DIM),
            z_all[:, :, OFF_K:OFF_K + W_FOX].reshape(depth, R, 1, N_FOX_HEADS, HEAD_DIM),
            z_all[:, :, OFF_V:OFF_V + W_FOX].reshape(depth, R, 1, N_FOX_HEADS, HEAD_DIM),
            st("lfs")[:, :, :N_FOX_HEADS].reshape(depth, R, 1, N_FOX_HEADS),
            _cols_to_state(st("hs")))
```

```python
import functools

import jax
import jax.numpy as jnp
from jax import lax
from jax.experimental import pallas as pl
from jax.experimental.pallas import tpu as pltpu

f32 = jnp.float32
bf16 = jnp.bfloat16

HEAD_DIM = 64
N_FOX_HEADS = 6
W_FOX = N_FOX_HEADS * HEAD_DIM
SSM_GROUP = 16
N_SSM_GROUPS = 24
W_SSM = N_SSM_GROUPS * SSM_GROUP
SSM_STATE = 64
N_MEM_HEADS = 4
W_MEM = N_MEM_HEADS * HEAD_DIM
W_MIX = W_FOX + W_SSM + W_MEM
RMS_EPS = 1e-6
QK_SCALE = HEAD_DIM ** -0.5

LANES = 128
SUBLANES = 8
GROUPS_PER_BLOCK = LANES // SSM_GROUP
N_SSM_BLOCKS = N_SSM_GROUPS // GROUPS_PER_BLOCK
HALF = GROUPS_PER_BLOCK * SSM_STATE
STATE_W = 2 * HALF
N_SEG = SUBLANES
PIECE_PITCH = SUBLANES

OFF_Q, OFF_K, OFF_V, OFF_U, OFF_QM, OFF_G = 0, 384, 768, 1152, 1536, 1792
W_MAIN = OFF_G + W_MIX
OFF_F = W_MAIN
W_CAT = 3072
IN_CUTS = (0, 384, 768, 1152, 1158, 1542, 1926, 2310, 2566, 2822)

VMEM_LIMIT = 56 * 1024 * 1024

TQ_FOX, TK_FOX = 1024, 512
TM_OUT = 256
TJ_SCAN = 128
TR_STAGE = 256
TN_SAMPLE = 256

SKIP_GAP = 110.0
NORM_SLACK = 1.02
KST_KN, KST_QN, KST_CEND, KST_ROWS = 0, SUBLANES, 2 * SUBLANES, 3 * SUBLANES


def _cparams(sem):
    return pltpu.CompilerParams(dimension_semantics=sem, vmem_limit_bytes=VMEM_LIMIT)


def _dot(a, b):
    return jnp.dot(a, b, preferred_element_type=f32)


def _dot_nt(a, b):
    return lax.dot_general(a, b, (((1,), (1,)), ((), ())), preferred_element_type=f32)


def _split2(a):
    hi = a.astype(bf16)
    lo = (a - hi.astype(f32)).astype(bf16)
    return hi, lo


def _split3(a):
    hi = a.astype(bf16)
    r = a - hi.astype(f32)
    mid = r.astype(bf16)
    lo = (r - mid.astype(f32)).astype(bf16)
    return hi, mid, lo


def _dot3(a, b):
    ah, al = _split2(a)
    bh, bl = _split2(b)
    return _dot(ah, bh) + _dot(al, bh) + _dot(ah, bl)


def _rms(x, g):
    ms = jnp.mean(x * x, axis=-1, keepdims=True)
    return x * lax.rsqrt(ms + RMS_EPS) * g


def _log_sigmoid(x):
    return jnp.minimum(x, 0.0) - jnp.log1p(jnp.exp(-jnp.abs(x)))


def _silu(x):
    return x * jax.nn.sigmoid(x)


def _gelu(x):
    return 0.5 * x * (1.0 + lax.erf(x * (2.0 ** -0.5)))


def _tri(n):
    r = lax.broadcasted_iota(jnp.int32, (n, n), 0)
    c = lax.broadcasted_iota(jnp.int32, (n, n), 1)
    return jnp.where(c <= r, 1.0, 0.0).astype(bf16)


def _pack_pieces(x):
    packed = None
    for i, pc in enumerate(_split3(x)):
        pc = pc.astype(f32)
        if i:
            pc = pltpu.roll(pc, PIECE_PITCH * i, 1)
        packed = pc if packed is None else packed + pc
    return packed.astype(bf16)


def _unpack_pieces(y):
    return y + pltpu.roll(y, LANES - PIECE_PITCH, 1) + pltpu.roll(y, LANES - 2 * PIECE_PITCH, 1)


def _layer(l, shape):
    return pl.BlockSpec((None,) + tuple(shape), lambda *_: (l,) + (0,) * len(shape),
                        pipeline_mode=pl.Buffered(1))


def _stage_w_kernel(w_ref, cat_ref, wm_ref, wf_ref):
    w = w_ref[...]
    seg = dict(zip(("q", "k", "v", "f", "ga", "u", "gs", "qm", "gm"),
                   (w[:, a:b] for a, b in zip(IN_CUTS[:-1], IN_CUTS[1:]))))
    main = jnp.concatenate([seg[n] for n in ("q", "k", "v", "u", "qm", "ga", "gs", "gm")], axis=1)
    wf = jnp.concatenate([seg["f"], jnp.zeros((w.shape[0], LANES - N_FOX_HEADS), f32)], axis=1)
    cat_ref[:, :W_MAIN] = main
    cat_ref[:, OFF_F:OFF_F + LANES] = wf
    cat_ref[:, OFF_F + LANES:] = jnp.zeros((w.shape[0], W_CAT - OFF_F - LANES), f32)
    wm_ref[...] = main.astype(bf16)
    wfh = wf.astype(bf16)
    wf_ref[:, :LANES] = wfh
    wf_ref[:, LANES:] = (wf - wfh.astype(f32)).astype(bf16)


def _stage_w(w_in, tr):
    depth, D, d_in = w_in.shape
    assert d_in == IN_CUTS[-1]
    blk = lambda w: pl.BlockSpec((None, tr, w), lambda l, i: (l, i, 0))
    return pl.pallas_call(
        _stage_w_kernel,
        grid=(depth, D // tr),
        in_specs=[blk(d_in)],
        out_specs=(blk(W_CAT), blk(W_MAIN), blk(2 * LANES)),
        out_shape=(jax.ShapeDtypeStruct((depth, D, W_CAT), f32), jax.ShapeDtypeStruct((depth, D, W_MAIN), bf16),
                   jax.ShapeDtypeStruct((depth, D, 2 * LANES), bf16)),
        compiler_params=_cparams(("parallel", "parallel")),
        name="stage_weights",
    )(w_in)


def _head_block(x, h):
    blk = x[:, LANES * (h // 2):LANES * (h // 2 + 1)]
    return blk if h % 2 == 0 else pltpu.roll(blk, HEAD_DIM, 1)


def _inproj_kernel(x_ref, g_ref, wm_ref, wf_ref, bf_ref, *rest):
    qa_ref, kt_ref, ka_ref, vt_ref, va_ref, kst_ref, lf_ref, sg_ref, u_ref, qm_ref, carry_ref = rest[-11:]
    i = pl.program_id(1)

    @pl.when(i == 0)
    def _():
        carry_ref[...] = jnp.zeros_like(carry_ref)

    xn = _rms(x_ref[...], g_ref[...])
    xh = xn.astype(bf16)
    tm = xn.shape[0]
    lane = lax.broadcasted_iota(jnp.int32, (tm, LANES), 1)
    low = lane < HEAD_DIM

    ff = _dot(xh, wf_ref[...])
    f = ff[:, :LANES] + ff[:, LANES:] + bf_ref[...]
    lf = jnp.where(lane < N_FOX_HEADS, _log_sigmoid(f), 0.0)
    lf_ref[...] = lf.T[:N_FOX_HEADS, :]
    cs = _unpack_pieces(_dot(_tri(tm), _pack_pieces(lf)))
    cs = jnp.where(lane < N_FOX_HEADS, cs, 0.0) + carry_ref[0:1, :]
    carry_ref[...] = jnp.broadcast_to(cs[tm - 1:, :], carry_ref.shape)
    prow = lax.broadcasted_iota(jnp.int32, (LANES, N_FOX_HEADS * LANES), 0)
    pcol = lax.broadcasted_iota(jnp.int32, (LANES, N_FOX_HEADS * LANES), 1)
    place = (pcol == (prow % PIECE_PITCH) * LANES + HEAD_DIM + prow // PIECE_PITCH) & (prow % PIECE_PITCH < N_FOX_HEADS)
    caug = _dot(_pack_pieces(-cs), jnp.where(place, 1.0, 0.0).astype(bf16))
    ones3 = jnp.where((lane >= HEAD_DIM) & (lane < HEAD_DIM + 3), 1.0, 0.0)

    qk = _dot(xh, wm_ref[:, OFF_Q:OFF_V])
    q = qk[:, :W_FOX] * QK_SCALE
    k = qk[:, W_FOX:]
    kt_ref[...] = k.T
    kb = k.astype(bf16).astype(f32)
    qb = q.astype(bf16).astype(f32)
    squares = jnp.concatenate([(kb * kb).astype(bf16), (qb * qb).astype(bf16)], axis=1)
    srow = lax.broadcasted_iota(jnp.int32, (2 * W_FOX, LANES), 0)
    scol = lax.broadcasted_iota(jnp.int32, (2 * W_FOX, LANES), 1)
    target = (srow // W_FOX) * SUBLANES + (srow % W_FOX) // HEAD_DIM
    sums = _dot(squares, jnp.where(scol == target, 1.0, 0.0).astype(bf16))
    norms = jnp.sqrt(jnp.max(sums, axis=0, keepdims=True)) * NORM_SLACK
    kst_ref[:KST_CEND, :] = _col_bcast(norms)[:KST_CEND, :]
    kst_ref[KST_CEND:, :] = _col_bcast(cs[tm - 1:, :])[:SUBLANES, :]
    vu = _dot(xh, wm_ref[:, OFF_V:OFF_QM])
    v = vu[:, :W_FOX]
    vt_ref[...] = v.T
    u_ref[...] = vu[:, W_FOX:]
    for h in range(N_FOX_HEADS):
        qa_ref[h] = jnp.where(low, _head_block(q, h), ones3).astype(bf16)
        ka_ref[h] = jnp.where(low, _head_block(k, h), caug[:, LANES * h:LANES * (h + 1)]).astype(bf16)
        va_ref[h] = jnp.where(low, _head_block(v, h), 1.0).astype(bf16)
    rest = _dot(xh, wm_ref[:, OFF_QM:W_MAIN])
    qm_ref[...] = (rest[:, :W_MEM] * QK_SCALE).astype(bf16)
    sg_ref[...] = _silu(rest[:, W_MEM:]).astype(bf16)


def _inproj(l, x, g_all, wm_all, wf_all, bf_all, kv_prev, tm):
    B, L, D = x.shape
    depth = wm_all.shape[0]
    row = lambda w: pl.BlockSpec((None, tm, w), lambda b, i: (b, i, 0))
    heads = pl.BlockSpec((None, N_FOX_HEADS, tm, LANES), lambda b, i: (b, 0, i, 0))
    slab = pl.BlockSpec((None, None, W_FOX, tm), lambda b, i: (l, b, 0, i))
    aug = jax.ShapeDtypeStruct((B, N_FOX_HEADS, L, LANES), bf16)
    kvt = jax.ShapeDtypeStruct((depth, B, W_FOX, L), f32)
    out_shape = (
        aug,
        kvt,
        aug,
        kvt,
        aug,
        jax.ShapeDtypeStruct((B, L // tm, KST_ROWS, LANES), f32),
        jax.ShapeDtypeStruct((depth, N_FOX_HEADS, B * L), f32),
        jax.ShapeDtypeStruct((B, L, W_MIX), bf16),
        jax.ShapeDtypeStruct((B, L, W_SSM), f32),
        jax.ShapeDtypeStruct((B, L, W_MEM), bf16),
    )
    kst = pl.BlockSpec((None, None, KST_ROWS, LANES), lambda b, i: (b, i, 0, 0))
    lft = pl.BlockSpec((None, N_FOX_HEADS, tm), lambda b, i: (l, 0, b * (L // tm) + i))
    out_specs = (heads, slab, heads, slab, heads, kst, lft, row(W_MIX), row(W_SSM), row(W_MEM))
    in_specs = [row(D), _layer(l, (1, D)), _layer(l, wm_all.shape[1:]), _layer(l, wf_all.shape[1:]),
                _layer(l, (1, LANES))]
    n_in = len(in_specs)
    return pl.pallas_call(
        _inproj_kernel,
        grid=(B, L // tm),
        in_specs=in_specs + [pl.BlockSpec(memory_space=pl.ANY)] * len(kv_prev),
        out_specs=out_specs,
        out_shape=out_shape,
        input_output_aliases={n_in + n: out for n, out in zip(range(len(kv_prev)), (1, 3, 6))},
        scratch_shapes=[pltpu.VMEM((SUBLANES, LANES), f32)],
        compiler_params=_cparams(("parallel", "arbitrary")),
        name="prompt_inproj",
    )(x, g_all, wm_all, wf_all, bf_all, *kv_prev)


def _memkv_kernel(m_ref, g_ref, w_ref, k_ref, v_ref):
    mn = _rms(m_ref[...], g_ref[...]).astype(bf16)
    kv = _dot(mn, w_ref[...])
    k_ref[...] = kv[:, :W_MEM]
    v_ref[...] = kv[:, W_MEM:]


def _memkv(mem, g_all, w_all):
    B, N, D = mem.shape
    depth = w_all.shape[0]
    out = pl.BlockSpec((None, None, N, W_MEM), lambda l, b: (l, b, 0, 0))
    return pl.pallas_call(
        _memkv_kernel,
        grid=(depth, B),
        in_specs=[pl.BlockSpec((None, N, D), lambda l, b: (b, 0, 0)),
                  pl.BlockSpec((None, 1, D), lambda l, b: (l, 0, 0)),
                  pl.BlockSpec((None,) + w_all.shape[1:], lambda l, b: (l, 0, 0))],
        out_specs=(out, out),
        out_shape=(jax.ShapeDtypeStruct((depth, B, N, W_MEM), f32),) * 2,
        compiler_params=_cparams(("parallel", "parallel")),
        name="prompt_memkv",
    )(mem, g_all, w_all)


def _first_needed_block(kst_ref, h, i, n_diag):
    n_full = i * n_diag
    mine = pl.ds(n_full, n_diag)
    qn = jnp.max(kst_ref[mine, KST_QN + h, :], axis=0, keepdims=True)
    kn_own = jnp.max(kst_ref[mine, KST_KN + h, :], axis=0, keepdims=True)
    lb = -qn * kn_own - kst_ref[pl.ds(jnp.maximum(n_full - 1, 0), 1), KST_CEND + h, :]
    ub = qn * kst_ref[:, KST_KN + h, :] - kst_ref[:, KST_CEND + h, :]
    kj = lax.broadcasted_iota(jnp.int32, ub.shape, 0)
    needed = (ub - lb >= -SKIP_GAP) & (kj < n_full)
    return _full_min(jnp.where(needed, kj, n_full).astype(f32))


def _fox_kernel(q_ref, k_ref, v_ref, kst_ref, o_ref, acc_ref, m_ref, *, tq, tk):
    j = pl.program_id(1)
    i = pl.program_id(2)
    acc_ref[...] = jnp.zeros_like(acc_ref)
    m_ref[...] = jnp.full_like(m_ref, -jnp.inf)

    def step(kj, r0):
        k0 = pl.multiple_of(kj * tk, tk)
        rows = slice(0, tq) if r0 is None else slice(r0, tq)
        n = rows.stop - rows.start
        for e in range(2):
            s = _dot_nt(q_ref[e, rows, :], k_ref[e, pl.ds(k0, tk), :])
            if r0 is not None:
                row = r0 + lax.broadcasted_iota(jnp.int32, (n, tk), 0)
                col = r0 + lax.broadcasted_iota(jnp.int32, (n, tk), 1)
                s = jnp.where(col <= row, s, -jnp.inf)
            m_old = m_ref[e, rows, :]
            m_new = jnp.maximum(m_old, jnp.max(s, axis=1, keepdims=True))
            p = jnp.exp((s - jnp.tile(m_new, (1, tk // LANES))).astype(bf16))
            acc_ref[e, rows, :] = (jnp.exp(m_old - m_new) * acc_ref[e, rows, :]
                                   + _dot(p, v_ref[e, pl.ds(k0, tk), :]))
            m_ref[e, rows, :] = m_new

    n_diag = tq // tk
    first = jnp.minimum(*[_first_needed_block(kst_ref, 2 * j + e, i, n_diag) for e in range(2)])
    t_first = jnp.min(first).astype(jnp.int32) // n_diag

    def body(t, carry):
        for d in range(n_diag):
            step(t * n_diag + d, None)
        return carry

    lax.fori_loop(t_first, i, body, 0)
    for d in range(n_diag):
        step(i * n_diag + d, d * tk)
    lane = lax.broadcasted_iota(jnp.int32, (tq, LANES), 1)
    a0 = acc_ref[0]
    a1 = acc_ref[1]
    o = jnp.where(lane < HEAD_DIM, a0 / pltpu.roll(a0, HEAD_DIM, 1), pltpu.roll(a1, HEAD_DIM, 1) / a1)
    o_ref[...] = o.astype(o_ref.dtype)


def _fox(qa, ka, va, kst, tq, tk):
    B, _, L, _ = qa.shape
    n_pairs = W_FOX // LANES
    assert kst.shape[1] == L // tk
    return pl.pallas_call(
        functools.partial(_fox_kernel, tq=tq, tk=tk),
        grid=(B, n_pairs, L // tq),
        in_specs=[pl.BlockSpec((None, 2, tq, LANES), lambda b, j, i: (b, j, i, 0)),
                  pl.BlockSpec((None, 2, L, LANES), lambda b, j, i: (b, j, 0, 0)),
                  pl.BlockSpec((None, 2, L, LANES), lambda b, j, i: (b, j, 0, 0)),
                  pl.BlockSpec((None,) + kst.shape[1:], lambda b, j, i: (b, 0, 0, 0))],
        out_specs=pl.BlockSpec((None, tq, LANES), lambda b, j, i: (b, i, j)),
        out_shape=jax.ShapeDtypeStruct((B, L, W_FOX), bf16),
        scratch_shapes=[pltpu.VMEM((2, tq, LANES), f32), pltpu.VMEM((2, tq, LANES), f32)],
        compiler_params=_cparams(("parallel", "parallel", "arbitrary")),
        name="prompt_fox",
    )(qa, ka, va, kst)


def _s5prep_kernel(lr_ref, li_ref, ldt_ref, bre_ref, bim_ref, cre_ref, cim_ref,
                   lamb_ref, bmat_ref, cmat_ref, lpow_ref, lams_ref, pw_ref, *, seg_len):
    lr = lr_ref[...]
    li = li_ref[...]
    dt = jnp.exp(ldt_ref[...])
    ar = lr * dt
    ai = li * dt
    mag = jnp.exp(ar)
    lbr = mag * jnp.cos(ai)
    lbi = mag * jnp.sin(ai)
    lamb_ref[:, :HALF] = jnp.broadcast_to(lbr, (SUBLANES, HALF))
    lamb_ref[:, HALF:] = jnp.broadcast_to(lbi, (SUBLANES, HALF))
    den = lr * lr + li * li
    zr = ((lbr - 1.0) * lr + lbi * li) / den
    zi = (lbi * lr - (lbr - 1.0) * li) / den
    bre = bre_ref[...]
    bim = bim_ref[...]
    bmat_ref[:, :HALF] = zr * bre - zi * bim
    bmat_ref[:, HALF:] = zr * bim + zi * bre
    cmat_ref[:HALF, :] = cre_ref[...]
    cmat_ref[HALF:, :] = -cim_ref[...]
    jj = (lax.broadcasted_iota(jnp.int32, (SUBLANES, HALF), 0) + 1).astype(f32)
    magp = jnp.exp(ar * jj)
    pw_ref[:SUBLANES, :HALF] = magp * jnp.cos(ai * jj)
    pw_ref[:SUBLANES, HALF:] = magp * jnp.sin(ai * jj)
    n = SUBLANES
    while n < seg_len:
        sr = pw_ref[n - 1:n, :HALF]
        si = pw_ref[n - 1:n, HALF:]
        pr = pw_ref[:n, :HALF]
        pi = pw_ref[:n, HALF:]
        pw_ref[n:2 * n, :HALF] = pr * sr - pi * si
        pw_ref[n:2 * n, HALF:] = pr * si + pi * sr
        n *= 2
    lpow_ref[...] = pw_ref[...].astype(bf16)
    lams_ref[...] = pw_ref[seg_len - SUBLANES:, :]


def _s5prep(lam_re, lam_im, log_dt, b_re, b_im, c_re, c_im, seg_len):
    depth = lam_re.shape[0]
    assert seg_len >= SUBLANES and seg_len & (seg_len - 1) == 0
    nb, gb = N_SSM_BLOCKS, GROUPS_PER_BLOCK
    eye = jnp.eye(gb, dtype=f32)
    half = lambda a: a.reshape(depth, nb, 1, HALF)
    ldt = jnp.broadcast_to(log_dt[:, :, None], (depth, N_SSM_GROUPS, SSM_STATE))

    def embed_b(b):
        t = b.reshape(depth, nb, gb, SSM_STATE, SSM_GROUP).transpose(0, 1, 2, 4, 3)
        return (t[:, :, :, :, None, :] * eye[None, None, :, None, :, None]).reshape(depth, nb, LANES, HALF)

    def embed_c(c):
        t = c.reshape(depth, nb, gb, SSM_GROUP, SSM_STATE).transpose(0, 1, 2, 4, 3)
        return (t[:, :, :, :, None, :] * eye[None, None, :, None, :, None]).reshape(depth, nb, HALF, LANES)

    blk = lambda shape: pl.BlockSpec((None, None) + shape, lambda l, cb: (l, cb) + (0,) * len(shape))
    arr = lambda shape, dt: jax.ShapeDtypeStruct((depth, nb) + shape, dt)
    return pl.pallas_call(
        functools.partial(_s5prep_kernel, seg_len=seg_len),
        grid=(depth, nb),
        in_specs=[blk((1, HALF))] * 3 + [blk((LANES, HALF))] * 2 + [blk((HALF, LANES))] * 2,
        out_specs=(blk((SUBLANES, STATE_W)), blk((LANES, STATE_W)), blk((STATE_W, LANES)),
                   blk((seg_len, STATE_W)), blk((SUBLANES, STATE_W))),
        out_shape=(arr((SUBLANES, STATE_W), f32),
                   arr((LANES, STATE_W), f32),
                   arr((STATE_W, LANES), f32),
                   arr((seg_len, STATE_W), bf16),
                   arr((SUBLANES, STATE_W), f32)),
        scratch_shapes=[pltpu.VMEM((seg_len, STATE_W), f32)],
        compiler_params=_cparams(("parallel", "parallel")),
        name="s5_tables",
    )(half(lam_re), half(lam_im), half(ldt), embed_b(b_re), embed_b(b_im), embed_c(c_re), embed_c(c_im))


def _s5scan_kernel(u_ref, bmat_ref, cmat_ref, lamb_ref, dsk_ref, y_ref, hend_ref,
                   up_ref, hb_ref, hq_ref, hcar_ref, *, tj):
    j = pl.program_id(1)
    nb = N_SSM_BLOCKS

    @pl.when(j == 0)
    def _():
        hcar_ref[...] = jnp.zeros_like(hcar_ref)

    for cb in range(nb):
        for s in range(N_SEG):
            up_ref[cb, pl.ds(s, tj, stride=N_SEG), :] = u_ref[s, :, LANES * cb:LANES * (cb + 1)]
        hb_ref[cb] = _dot(up_ref[cb].astype(bf16), bmat_ref[cb])

    def rec(t2, h):
        r0 = pl.multiple_of(t2 * 2 * N_SEG, 2 * N_SEG)
        new = []
        for cb in range(nb):
            hr, hi = h[2 * cb], h[2 * cb + 1]
            lr = lamb_ref[cb, :, :HALF]
            li = lamb_ref[cb, :, HALF:]
            rs, is_ = [], []
            for d in range(2):
                bu = hb_ref[cb, pl.ds(r0 + d * N_SEG, N_SEG), :]
                hr, hi = lr * hr - li * hi + bu[:, :HALF], lr * hi + li * hr + bu[:, HALF:]
                rs.append(hr)
                is_.append(hi)
            hq_ref[cb, pl.ds(r0, 2 * N_SEG), :HALF] = jnp.concatenate(rs, axis=0).astype(bf16)
            hq_ref[cb, pl.ds(r0, 2 * N_SEG), HALF:] = jnp.concatenate(is_, axis=0).astype(bf16)
            new += [hr, hi]
        return tuple(new)

    h0 = tuple(hcar_ref[cb, :, HALF * part:HALF * (part + 1)] for cb in range(nb) for part in range(2))
    h = lax.fori_loop(0, tj // 2, rec, h0, unroll=True)
    for cb in range(nb):
        hcar_ref[cb, :, :HALF] = h[2 * cb]
        hcar_ref[cb, :, HALF:] = h[2 * cb + 1]
        up_ref[cb] = _dot(hq_ref[cb], cmat_ref[cb]) + dsk_ref[cb] * up_ref[cb]
        for s in range(N_SEG):
            y_ref[s, :, LANES * cb:LANES * (cb + 1)] = up_ref[cb, pl.ds(s, tj, stride=N_SEG), :].astype(y_ref.dtype)

    @pl.when(j == pl.num_programs(1) - 1)
    def _():
        hend_ref[...] = hcar_ref[...]


def _s5scan(l, u, bmat_all, cmat_all, lamb_all, dsk_all, tj):
    B, L, _ = u.shape
    seg_len = L // N_SEG
    u4 = u.reshape(B, N_SEG, seg_len, W_SSM)
    nb = N_SSM_BLOCKS
    tile = pl.BlockSpec((None, N_SEG, tj, W_SSM), lambda b, j: (b, 0, j, 0))
    y4, hend = pl.pallas_call(
        functools.partial(_s5scan_kernel, tj=tj),
        grid=(B, seg_len // tj),
        in_specs=[tile, _layer(l, bmat_all.shape[1:]), _layer(l, cmat_all.shape[1:]),
                  _layer(l, lamb_all.shape[1:]), _layer(l, dsk_all.shape[1:])],
        out_specs=(tile, pl.BlockSpec((None, nb, N_SEG, STATE_W), lambda b, j: (b, 0, 0, 0))),
        out_shape=(jax.ShapeDtypeStruct((B, N_SEG, seg_len, W_SSM), bf16),
                   jax.ShapeDtypeStruct((B, nb, N_SEG, STATE_W), f32)),
        scratch_shapes=[pltpu.VMEM((nb, N_SEG * tj, LANES), f32), pltpu.VMEM((nb, N_SEG * tj, STATE_W), f32),
                        pltpu.VMEM((nb, N_SEG * tj, STATE_W), bf16), pltpu.VMEM((nb, N_SEG, STATE_W), f32)],
        compiler_params=_cparams(("parallel", "arbitrary")),
        name="prompt_s5_scan",
    )(u4, bmat_all, cmat_all, lamb_all, dsk_all)
    return y4.reshape(B, L, W_SSM), hend


def _s5fix_kernel(hend_ref, lams_ref, cmat_ref, hlast_ref, cp_ref, hs_ref):
    he = hend_ref[...]
    lsr = lams_ref[SUBLANES - 1:, :HALF]
    lsi = lams_ref[SUBLANES - 1:, HALF:]
    hs_ref[...] = jnp.zeros_like(hs_ref)
    hr = jnp.zeros((1, HALF), f32)
    hi = jnp.zeros((1, HALF), f32)
    for s in range(N_SEG):
        hs_ref[s:s + 1, :HALF] = hr
        hs_ref[s:s + 1, HALF:] = hi
        nr = lsr * hr - lsi * hi + he[s:s + 1, :HALF]
        ni = lsr * hi + lsi * hr + he[s:s + 1, HALF:]
        hr, hi = nr, ni
    hlast_ref[:, :HALF] = hr
    hlast_ref[:, HALF:] = hi
    hst_r = hs_ref[:, :HALF].T
    hst_i = hs_ref[:, HALF:].T
    cre = cmat_ref[:HALF, :]
    mci = cmat_ref[HALF:, :]
    for s in range(N_SEG):
        col_r = hst_r[:, s:s + 1]
        col_i = hst_i[:, s:s + 1]
        cp_ref[s, :HALF, :] = (cre * col_r + mci * col_i).astype(bf16)
        cp_ref[s, HALF:, :] = (mci * col_r - cre * col_i).astype(bf16)


def _s5fix(l, hend, lams_all, cmat_all):
    B, nb = hend.shape[:2]
    return pl.pallas_call(
        _s5fix_kernel,
        grid=(B, nb),
        in_specs=[pl.BlockSpec((None, None, N_SEG, STATE_W), lambda b, cb: (b, cb, 0, 0)),
                  pl.BlockSpec((None, None, SUBLANES, STATE_W), lambda b, cb: (l, cb, 0, 0)),
                  pl.BlockSpec((None, None, STATE_W, LANES), lambda b, cb: (l, cb, 0, 0))],
        out_specs=(pl.BlockSpec((None, None, 1, STATE_W), lambda b, cb: (b, cb, 0, 0)),
                   pl.BlockSpec((None, N_SEG, None, STATE_W, LANES), lambda b, cb: (b, 0, cb, 0, 0))),
        out_shape=(jax.ShapeDtypeStruct((B, nb, 1, STATE_W), f32),
                   jax.ShapeDtypeStruct((B, N_SEG, nb, STATE_W, LANES), bf16)),
        scratch_shapes=[pltpu.VMEM((LANES, STATE_W), f32)],
        compiler_params=_cparams(("parallel", "parallel")),
        name="prompt_s5_fix",
    )(hend, lams_all, cmat_all)


def _pair_attend(q, k, v):
    lane = lax.broadcasted_iota(jnp.int32, q.shape, 1)
    zero = jnp.zeros_like(q)
    outs = []
    for e in range(2):
        qe = jnp.where((lane < HEAD_DIM) == (e == 0), q, zero)
        s = _dot_nt(qe, k)
        p = jnp.exp(s - jnp.max(s, axis=1, keepdims=True))
        outs.append(_dot(p.astype(bf16), v) / jnp.sum(p, axis=1, keepdims=True))
    return jnp.where(lane < HEAD_DIM, outs[0], outs[1])


N_OUTPROJ_INPUTS = 13


def _outproj_stages(x_ref, fox_ref, y_ref, lp_ref, cp_ref, qm_ref, mk_ref, mv_ref, sg_ref,
                    wglu_ref, bglu_ref, wout_ref, gfin_ref, o_ref, *, final, tiles_per_seg):
    tm = x_ref.shape[0]
    j0 = pl.multiple_of((pl.program_id(1) % tiles_per_seg) * tm, tm)
    corr = [_dot(lp_ref[cb, pl.ds(j0, tm), :], cp_ref[cb]) for cb in range(N_SSM_BLOCKS)]
    ssm_y = y_ref[...] + jnp.concatenate(corr, axis=1)
    yield
    s = _gelu(ssm_y)
    s = s * jax.nn.sigmoid(_dot(s.astype(bf16), wglu_ref[...]) + bglu_ref[...])
    yield
    qm = qm_ref[...]
    mk = mk_ref[...].astype(bf16)
    mv = mv_ref[...].astype(bf16)
    mem = []
    for jp in range(W_MEM // LANES):
        mem.append(_pair_attend(qm[:, LANES * jp:LANES * (jp + 1)], mk[:, LANES * jp:LANES * (jp + 1)],
                                mv[:, LANES * jp:LANES * (jp + 1)]))
        yield
    sg = sg_ref[...].astype(f32)
    mix = jnp.concatenate([fox_ref[...] * sg[:, :W_FOX],
                           s * sg[:, W_FOX:W_FOX + W_SSM],
                           jnp.concatenate(mem, axis=1) * sg[:, W_FOX + W_SSM:]], axis=1)
    xo = x_ref[...] + _dot(mix.astype(bf16), wout_ref[...])
    yield
    if final:
        xo = _rms(xo, gfin_ref[...])
    o_ref[...] = xo


def _s_inproj_kernel(x_ref, g_ref, w_ref, z_ref):
    xn = _rms(x_ref[...], g_ref[...])
    z_ref[...] = _dot3(xn, w_ref[...])


def _s_inproj(l, xs, g_all, wcat_all, tn):
    R, D = xs.shape
    return pl.pallas_call(
        _s_inproj_kernel,
        grid=(W_CAT // tn,),
        in_specs=[pl.BlockSpec((R, D), lambda n: (0, 0)), _layer(l, (1, D)),
                  pl.BlockSpec((None, D, tn), lambda n: (l, 0, n))],
        out_specs=pl.BlockSpec((R, tn), lambda n: (0, n)),
        out_shape=jax.ShapeDtypeStruct((R, W_CAT), f32),
        compiler_params=_cparams(("parallel",)),
        name="sample_inproj",
    )(xs, g_all, wcat_all)


def _full_max(x):
    return jnp.max(jnp.max(x, axis=0, keepdims=True), axis=1, keepdims=True)


def _full_min(x):
    return jnp.min(jnp.min(x, axis=0, keepdims=True), axis=1, keepdims=True)


def _full_sum(x):
    return jnp.sum(jnp.sum(x, axis=0, keepdims=True), axis=1, keepdims=True)


def _lane_sums(acc):
    ones = jnp.ones((SUBLANES, LANES), bf16)
    hi, lo = _split2(acc)
    return (_dot_nt(ones, hi) + _dot_nt(ones, lo))[0:1, :]


def _col_bcast(row):
    return jnp.broadcast_to(row, (LANES, row.shape[1])).T


class _DecodeGuest:
    N_FIXED_IN, N_OUT, N_SCRATCH = 3, 3, 5

    def __init__(self, l, n_seq, first_seq, n_steps, step_of, page_table, z, bf_all, ckt, cvt, lft, mkt, mvt):
        R, n_pages = page_table.shape
        n_pool, page = ckt.shape[1], ckt.shape[4]
        n_mem = mkt.shape[4]
        assert page == LANES and n_mem % LANES == 0 and first_seq % n_seq == 0
        assert first_seq + n_steps * n_seq <= R
        self.n_seq, self.n_pages, self.first_seq, self.step_of = n_seq, n_pages, first_seq, step_of
        self.per_seq = 2 * n_pages + 2
        z_blk = lambda *g: (first_seq // n_seq + step_of(*g[:-1]), 0, 0)
        self.in_specs = [pl.BlockSpec((n_seq, 1, W_CAT), z_blk), _layer(l, (1, LANES)),
                         _layer(l, (N_FOX_HEADS, n_pool, page))]
        self.operands = [z.reshape(R, 1, W_CAT), bf_all, lft]
        for s in range(n_seq):
            seq = lambda g, s=s: first_seq + step_of(*g) * n_seq + s
            paged = [pl.BlockSpec((None, None, N_FOX_HEADS, HEAD_DIM, page),
                                  lambda *g, seq=seq, p=p: (l, g[-1][seq(g[:-1]), p], 0, 0, 0))
                     for p in range(n_pages)]
            mem_spec = pl.BlockSpec((None, None, N_MEM_HEADS, HEAD_DIM, n_mem),
                                    lambda *g, seq=seq: (l, seq(g[:-1]), 0, 0, 0))
            self.in_specs += paged + paged + [mem_spec, mem_spec]
            self.operands += [ckt] * n_pages + [cvt] * n_pages + [mkt, mvt]
        out = lambda w: pl.BlockSpec((n_seq, 1, w), lambda *g: (step_of(*g[:-1]), 0, 0))
        self.out_specs = [out(W_FOX), out(W_MEM), out(LANES)]
        n_out = n_steps * n_seq
        self.out_shape = [jax.ShapeDtypeStruct((n_out, 1, w), f32) for w in (W_FOX, W_MEM, LANES)]
        n_rows = N_FOX_HEADS * n_pages
        self.scratch_shapes = [pltpu.VMEM((n_seq, rows, LANES), f32)
                               for rows in (n_rows, n_rows, n_rows, W_FOX, W_MEM)]

    def chains(self, pt_ref, step, ins, outs, scratch):
        z_ref, bf_ref, lfc_ref = ins[:self.N_FIXED_IN]
        fox_ref, mem_ref, lf_ref = outs
        n_pages = self.n_pages
        result = []
        for i in range(self.n_seq):
            mine = ins[self.N_FIXED_IN + i * self.per_seq:self.N_FIXED_IN + (i + 1) * self.per_seq]
            result.append(_decode_one(
                pt_ref, self.first_seq + step * self.n_seq + i, z_ref.at[i], bf_ref, lfc_ref,
                mine[:n_pages], mine[n_pages:2 * n_pages], mine[2 * n_pages], mine[2 * n_pages + 1],
                fox_ref.at[i], mem_ref.at[i], lf_ref.at[i], *[s.at[i] for s in scratch]))
        return result

    @property
    def n_in(self):
        return self.N_FIXED_IN + self.n_seq * self.per_seq


def _run_chains(chains):
    while chains:
        chains = [c for c in chains if next(c, _DONE) is not _DONE]


def _outproj_decode_kernel(pt_ref, *refs, guest, final, tiles_per_seg):
    n_in = N_OUTPROJ_INPUTS + guest.n_in
    op_in, g_in = refs[:N_OUTPROJ_INPUTS], refs[N_OUTPROJ_INPUTS:n_in]
    o_ref = refs[n_in]
    g_out = refs[n_in + 1:n_in + 1 + guest.N_OUT]
    g_scr = refs[n_in + 1 + guest.N_OUT:]
    step = guest.step_of(pl.program_id(0), pl.program_id(1))
    _run_chains([_outproj_stages(*op_in, o_ref, final=final, tiles_per_seg=tiles_per_seg)]
                + guest.chains(pt_ref, step, g_in, g_out, g_scr))


_DONE = object()


def _decode_one(pt_ref, r, z_ref, bf_ref, lfc_ref, ck, cv, mk_ref, mv_ref, fox_ref, mem_ref, lf_ref,
                s_scr, lfs_scr, p_scr, acc_scr, accm_scr):
    n_pages = len(ck)
    z = z_ref[...]
    q = z[:, OFF_Q:OFF_Q + W_FOX] * QK_SCALE
    k_new = z[:, OFF_K:OFF_K + W_FOX]
    v_new = z[:, OFF_V:OFF_V + W_FOX]
    lf_new = _log_sigmoid(z[:, OFF_F:OFF_F + LANES] + bf_ref[...])
    lf_ref[...] = lf_new

    qt = _col_bcast(q)
    for h in range(N_FOX_HEADS):
        qh = qt[HEAD_DIM * h:HEAD_DIM * (h + 1), :]
        for p in range(n_pages):
            row = h * n_pages + p
            s_scr[row:row + 1, :] = jnp.sum(ck[p][h] * qh, axis=0, keepdims=True)
            lfs_scr[row:row + 1, :] = lfc_ref[h, pl.ds(pt_ref[r, p], 1), :]
        yield

    n_rows = N_FOX_HEADS * n_pages
    ur = lax.broadcasted_iota(jnp.int32, (LANES, LANES), 0)
    uc = lax.broadcasted_iota(jnp.int32, (LANES, LANES), 1)
    upper = jnp.where(ur <= uc, 1.0, 0.0).astype(bf16)
    cs = sum(_dot(pc, upper) for pc in _split3(lfs_scr[...]))
    tot = jnp.broadcast_to(cs[:, LANES - 1:], (n_rows, LANES))
    mr = lax.broadcasted_iota(jnp.int32, (n_rows, n_rows), 0)
    mc = lax.broadcasted_iota(jnp.int32, (n_rows, n_rows), 1)
    before = jnp.where((mc < mr) & (mc // n_pages == mr // n_pages), 1.0, 0.0).astype(bf16)
    c = cs + sum(_dot(before, pc) for pc in _split3(tot))
    s_all = s_scr[...] - c
    yield

    qk_new = q * k_new
    lane_head = lax.broadcasted_iota(jnp.int32, (1, W_FOX), 1) // HEAD_DIM
    pn_exp = jnp.zeros((1, W_FOX), f32)
    l_exp = jnp.zeros((1, W_FOX), f32)
    for h in range(N_FOX_HEADS):
        rows = slice(h * n_pages, (h + 1) * n_pages)
        s_h = s_all[rows, :]
        c_new = c[(h + 1) * n_pages - 1:(h + 1) * n_pages, LANES - 1:] + lf_new[:, h:h + 1]
        s_new = jnp.sum(qk_new[:, HEAD_DIM * h:HEAD_DIM * (h + 1)], axis=1, keepdims=True) - c_new
        m = jnp.maximum(_full_max(s_h), s_new)
        p_h = jnp.exp(s_h - m)
        pn = jnp.exp(s_new - m)
        p_scr[rows, :] = p_h
        pn_exp = jnp.where(lane_head == h, pn, pn_exp)
        l_exp = jnp.where(lane_head == h, _full_sum(p_h) + pn, l_exp)
        acc = jnp.zeros((HEAD_DIM, LANES), f32)
        for p in range(n_pages):
            acc = acc + cv[p][h] * p_scr[h * n_pages + p:h * n_pages + p + 1, :]
        acc_scr[HEAD_DIM * h:HEAD_DIM * (h + 1), :] = acc
        yield
    fox_ref[...] = (_lane_sums(acc_scr[...]) + pn_exp * v_new) / l_exp
    yield

    qm = z[:, OFF_QM:OFF_QM + W_MEM] * QK_SCALE
    qmt = _col_bcast(qm)
    n_halves = mk_ref.shape[2] // LANES
    mem_head = lax.broadcasted_iota(jnp.int32, (1, W_MEM), 1) // HEAD_DIM
    lm_exp = jnp.zeros((1, W_MEM), f32)
    for h in range(N_MEM_HEADS):
        qh = qmt[HEAD_DIM * h:HEAD_DIM * (h + 1), :]
        ss = [jnp.sum(mk_ref[h, :, LANES * t:LANES * (t + 1)] * qh, axis=0, keepdims=True) for t in range(n_halves)]
        m = functools.reduce(jnp.maximum, [jnp.max(s, axis=1, keepdims=True) for s in ss])
        ps = [jnp.exp(s - m) for s in ss]
        lm = sum(jnp.sum(p, axis=1, keepdims=True) for p in ps)
        lm_exp = jnp.where(mem_head == h, lm, lm_exp)
        accm_scr[HEAD_DIM * h:HEAD_DIM * (h + 1), :] = sum(
            mv_ref[h, :, LANES * t:LANES * (t + 1)] * ps[t] for t in range(n_halves))
    mem_ref[...] = _lane_sums(accm_scr[...]) / lm_exp


def _outproj_decode(l, x, fox_o, y, lpow_all, cp, qm, mk_all, mv_all, sg, wglu_all, bglu_all, wout_all, gfin,
                    page_table, z, bf_all, ckt, cvt, lft, mkt, mvt, first_seq, tm, final):
    B, L, D = x.shape
    seg_len = L // N_SEG
    tiles_per_seg = seg_len // tm
    n_i = L // tm
    n_steps = B * n_i
    n_left = page_table.shape[0] - first_seq
    assert n_left % n_steps == 0
    guest = _DecodeGuest(l, n_left // n_steps, first_seq, n_steps, lambda b, i: b * n_i + i,
                         page_table, z, bf_all, ckt, cvt, lft, mkt, mvt)
    nb = N_SSM_BLOCKS
    row = lambda w: pl.BlockSpec((None, tm, w), lambda b, i, pt: (b, i, 0))
    mem_kv = pl.BlockSpec((None, None) + mk_all.shape[2:], lambda b, i, pt: (l, b, 0, 0))
    in_specs = [row(D), row(W_FOX), row(W_SSM), _layer(l, lpow_all.shape[1:]),
                pl.BlockSpec((None, None, nb, STATE_W, LANES), lambda b, i, pt: (b, i // tiles_per_seg, 0, 0, 0)),
                row(W_MEM), mem_kv, mem_kv, row(D),
                _layer(l, wglu_all.shape[1:]), _layer(l, (1, W_SSM)), _layer(l, wout_all.shape[1:]),
                pl.BlockSpec((1, D), lambda b, i, pt: (0, 0))]
    assert len(in_specs) == N_OUTPROJ_INPUTS
    xo, *dec = pl.pallas_call(
        functools.partial(_outproj_decode_kernel, guest=guest, final=final, tiles_per_seg=tiles_per_seg),
        grid_spec=pltpu.PrefetchScalarGridSpec(
            num_scalar_prefetch=1,
            grid=(B, n_i),
            in_specs=in_specs + guest.in_specs,
            out_specs=[row(D)] + guest.out_specs,
            scratch_shapes=guest.scratch_shapes),
        out_shape=[jax.ShapeDtypeStruct((B, L, D), f32)] + guest.out_shape,
        compiler_params=_cparams(("parallel", "parallel")),
        name="outproj_decode",
    )(page_table, x, fox_o, y, lpow_all, cp, qm, mk_all, mv_all, sg, wglu_all, bglu_all, wout_all, gfin,
      *guest.operands)
    return xo, dec


def _s_out_kernel(x_ref, z_ref, fox_ref, mem_ref, h0_ref, bmat_ref, cmat_ref, lamb_ref, dsk_ref,
                  wglu_ref, bglu_ref, wout_ref, gfin_ref, o_ref, h_ref, *, final):
    z = z_ref[...]
    u = z[:, OFF_U:OFF_U + W_SSM]
    ys = []
    for cb in range(N_SSM_BLOCKS):
        ucb = u[:, LANES * cb:LANES * (cb + 1)]
        bu = _dot3(ucb, bmat_ref[cb])
        lr = lamb_ref[cb, 0:1, :HALF]
        li = lamb_ref[cb, 0:1, HALF:]
        h0r = h0_ref[:, STATE_W * cb:STATE_W * cb + HALF]
        h0i = h0_ref[:, STATE_W * cb + HALF:STATE_W * (cb + 1)]
        hr = lr * h0r - li * h0i + bu[:, :HALF]
        hi = lr * h0i + li * h0r + bu[:, HALF:]
        h_ref[:, STATE_W * cb:STATE_W * cb + HALF] = hr
        h_ref[:, STATE_W * cb + HALF:STATE_W * (cb + 1)] = hi
        ys.append(_dot3(jnp.concatenate([hr, hi], axis=1), cmat_ref[cb]) + dsk_ref[cb] * ucb)
    s = _gelu(jnp.concatenate(ys, axis=1))
    s = s * jax.nn.sigmoid(_dot3(s, wglu_ref[...]) + bglu_ref[...])
    sg = _silu(z[:, OFF_G:W_MAIN])
    mix = jnp.concatenate([fox_ref[...] * sg[:, :W_FOX],
                           s * sg[:, W_FOX:W_FOX + W_SSM],
                           mem_ref[...] * sg[:, W_FOX + W_SSM:]], axis=1)
    xo = x_ref[...] + _dot3(mix, wout_ref[...])
    if final:
        xo = _rms(xo, gfin_ref[...])
    o_ref[...] = xo


def _s_out(l, xs, z, fox, mem, h0_all, bmat_all, cmat_all, lamb_all, dsk_all, wglu_all, bglu_all, wout_all,
           gfin, final):
    R, D = xs.shape
    whole = lambda a: pl.BlockSpec(a.shape, lambda i: (0,) * a.ndim)
    hspec = pl.BlockSpec((R, h0_all.shape[2]), lambda i: (0, 0))
    return pl.pallas_call(
        functools.partial(_s_out_kernel, final=final),
        grid=(1,),
        in_specs=[whole(xs), whole(z), whole(fox), whole(mem), _layer(l, h0_all.shape[1:]),
                  _layer(l, bmat_all.shape[1:]), _layer(l, cmat_all.shape[1:]), _layer(l, lamb_all.shape[1:]),
                  _layer(l, dsk_all.shape[1:]), _layer(l, wglu_all.shape[1:]), _layer(l, (1, W_SSM)),
                  _layer(l, wout_all.shape[1:]), whole(gfin)],
        out_specs=(whole(xs), hspec),
        out_shape=(jax.ShapeDtypeStruct((R, D), f32), jax.ShapeDtypeStruct(h0_all.shape[1:], f32)),
        compiler_params=_cparams(("arbitrary",)),
        name="sample_out",
    )(xs, z, fox, mem, h0_all, bmat_all, cmat_all, lamb_all, dsk_all, wglu_all, bglu_all, wout_all, gfin)


def _state_to_cols(h):
    lead = h.shape[:-3]
    n = len(lead)
    t = h.reshape(lead + (N_SSM_BLOCKS, GROUPS_PER_BLOCK, SSM_STATE, 2))
    t = t.transpose(tuple(range(n)) + (n, n + 3, n + 1, n + 2))
    return t.reshape(lead + (N_SSM_BLOCKS * STATE_W,))


def _cols_to_state(c):
    lead = c.shape[:-1]
    n = len(lead)
    t = c.reshape(lead + (N_SSM_BLOCKS, 2, GROUPS_PER_BLOCK, SSM_STATE))
    t = t.transpose(tuple(range(n)) + (n, n + 2, n + 3, n + 1))
    return t.reshape(lead + (N_SSM_GROUPS, SSM_STATE, 2))


def _tile(n, pref):
    t = min(n, pref)
    assert n % t == 0
    return t


def kernel(x_prompt, x_sample, mem_prompt, cache_k, cache_v, cache_logf, state_ssm, cache_mem_k, cache_mem_v,
           page_table, g_norm, w_in, b_f, lam_re, lam_im, log_dt, b_re, b_im, c_re, c_im, d_skip, w_glu, b_glu,
           g_mem, w_mem_kv, w_out, g_final):
    depth = w_in.shape[0]
    B, L, D = x_prompt.shape
    R = x_sample.shape[0]
    assert x_sample.shape[1] == 1 and L % N_SEG == 0
    seg_len = L // N_SEG
    tm = _tile(seg_len, TM_OUT)
    tq = _tile(L, TQ_FOX)
    tk = _tile(tq, TK_FOX)
    tj = _tile(seg_len, TJ_SCAN)

    wcat_all, wm_all, wf_all = _stage_w(w_in, _tile(D, TR_STAGE))
    bf_all = jnp.pad(b_f, ((0, 0), (0, LANES - N_FOX_HEADS))).reshape(depth, 1, LANES)
    gn_all = g_norm.reshape(depth, 1, D)
    gmem_all = g_mem.reshape(depth, 1, D)
    bglu_all = b_glu.reshape(depth, 1, W_SSM)
    gfin = g_final.reshape(1, D)
    dsk_all = d_skip.reshape(depth, N_SSM_BLOCKS, 1, LANES)
    wglu_bf = w_glu.astype(bf16)
    wout_bf = w_out.astype(bf16)
    ckt = jnp.transpose(cache_k, (0, 1, 3, 4, 2))
    cvt = jnp.transpose(cache_v, (0, 1, 3, 4, 2))
    lft = jnp.transpose(cache_logf, (0, 3, 1, 2))
    mkt = jnp.transpose(cache_mem_k, (0, 1, 3, 4, 2))
    mvt = jnp.transpose(cache_mem_v, (0, 1, 3, 4, 2))
    h0_all = _state_to_cols(state_ssm)

    lamb_all, bmat_all, cmat_all, lpow_all, lams_all = _s5prep(lam_re, lam_im, log_dt, b_re, b_im, c_re, c_im, seg_len)
    bmat_bf = bmat_all.astype(bf16)
    cmat_bf = cmat_all.astype(bf16)
    mk_all, mv_all = _memkv(mem_prompt, gmem_all, w_mem_kv.astype(bf16))

    xp = x_prompt
    xs = x_sample.reshape(R, D)
    outs = {n: [] for n in ("hp", "z", "lfs", "hs")}
    kv_prev = ()
    for l in range(depth):
        final = l == depth - 1
        qa, kt_all, ka, vt_all, va, kst, lfp_all, sg, u, qm = _inproj(l, xp, gn_all, wm_all, wf_all, bf_all,
                                                                      kv_prev, tk)
        kv_prev = (kt_all, vt_all, lfp_all)
        fox_o = _fox(qa, ka, va, kst, tq, tk)
        y, hend = _s5scan(l, u, bmat_bf, cmat_bf, lamb_all, dsk_all, tj)
        hlast, cp = _s5fix(l, hend, lams_all, cmat_all)
        outs["hp"].append(hlast.reshape(B, N_SSM_BLOCKS * STATE_W))

        z = _s_inproj(l, xs, gn_all, wcat_all, TN_SAMPLE)
        xp, dec = _outproj_decode(
            l, xp, fox_o, y, lpow_all, cp, qm, mk_all, mv_all, sg, wglu_bf, bglu_all, wout_bf, gfin,
            page_table, z, bf_all, ckt, cvt, lft, mkt, mvt, 0, tm, final)
        fox_s, mem_s, lf_s = (d.reshape(R, -1) for d in dec)
        xs, h_new = _s_out(l, xs, z, fox_s, mem_s, h0_all, bmat_all, cmat_all, lamb_all, dsk_all,
                           w_glu, bglu_all, w_out, gfin, final)
        outs["z"].append(z)
        outs["lfs"].append(lf_s)
        outs["hs"].append(h_new)

    st = lambda n: jnp.stack(outs[n])
    z_all = st("z")
    n_mem = mk_all.shape[2]
    return (xp, xs.reshape(R, 1, D),
            kt_all.reshape(depth, B, N_FOX_HEADS, HEAD_DIM, L).transpose(0, 1, 4, 2, 3),
            vt_all.reshape(depth, B, N_FOX_HEADS, HEAD_DIM, L).transpose(0, 1, 4, 2, 3),
            lfp_all.reshape(depth, N_FOX_HEADS, B, L).transpose(0, 2, 3, 1),
            _cols_to_state(st("hp")),
            mk_all.reshape(depth, B, n_mem, N_MEM_HEADS, HEAD_DIM),
            mv_all.reshape(depth, B, n_mem, N_MEM_HEADS, HEAD_DIM),
            z_all[:, :, OFF_K:OFF_K + W_FOX].reshape(depth, R, 1, N_FOX_HEADS, HEAD_DIM),
            z_all[:, :, OFF_V:OFF_V + W_FOX].reshape(depth, R, 1, N_FOX_HEADS, HEAD_DIM),
            st("lfs")[:, :, :N_FOX_HEADS].reshape(depth, R, 1, N_FOX_HEADS),
            _cols_to_state(st("hs")))
```

```python
import functools

import jax
import jax.numpy as jnp
from jax import lax
from jax.experimental import pallas as pl
from jax.experimental.pallas import tpu as pltpu

f32 = jnp.float32
bf16 = jnp.bfloat16

HEAD_DIM = 64
N_FOX_HEADS = 6
W_FOX = N_FOX_HEADS * HEAD_DIM
SSM_GROUP = 16
N_SSM_GROUPS = 24
W_SSM = N_SSM_GROUPS * SSM_GROUP
SSM_STATE = 64
N_MEM_HEADS = 4
W_MEM = N_MEM_HEADS * HEAD_DIM
W_MIX = W_FOX + W_SSM + W_MEM
RMS_EPS = 1e-6
QK_SCALE = HEAD_DIM ** -0.5

LANES = 128
SUBLANES = 8
GROUPS_PER_BLOCK = LANES // SSM_GROUP
N_SSM_BLOCKS = N_SSM_GROUPS // GROUPS_PER_BLOCK
HALF = GROUPS_PER_BLOCK * SSM_STATE
STATE_W = 2 * HALF
N_SEG = SUBLANES
PIECE_PITCH = SUBLANES

OFF_Q, OFF_K, OFF_V, OFF_U, OFF_QM, OFF_G = 0, 384, 768, 1152, 1536, 1792
W_MAIN = OFF_G + W_MIX
OFF_F = W_MAIN
W_CAT = 3072
IN_CUTS = (0, 384, 768, 1152, 1158, 1542, 1926, 2310, 2566, 2822)

VMEM_LIMIT = 56 * 1024 * 1024

TQ_FOX, TK_FOX = 1024, 512
TM_OUT = 256
TJ_SCAN = 128
TN_SAMPLE = 256

SKIP_GAP = 110.0
NORM_SLACK = 1.02
KST_KN, KST_QN, KST_CEND, KST_ROWS = 0, SUBLANES, 2 * SUBLANES, 3 * SUBLANES


def _cparams(sem):
    return pltpu.CompilerParams(dimension_semantics=sem, vmem_limit_bytes=VMEM_LIMIT)


def _dot(a, b):
    return jnp.dot(a, b, preferred_element_type=f32)


def _dot_nt(a, b):
    return lax.dot_general(a, b, (((1,), (1,)), ((), ())), preferred_element_type=f32)


def _split2(a):
    hi = a.astype(bf16)
    lo = (a - hi.astype(f32)).astype(bf16)
    return hi, lo


def _split3(a):
    hi = a.astype(bf16)
    r = a - hi.astype(f32)
    mid = r.astype(bf16)
    lo = (r - mid.astype(f32)).astype(bf16)
    return hi, mid, lo


def _dot3(a, b):
    ah, al = _split2(a)
    bh, bl = _split2(b)
    return _dot(ah, bh) + _dot(al, bh) + _dot(ah, bl)


def _rms(x, g):
    ms = jnp.mean(x * x, axis=-1, keepdims=True)
    return x * lax.rsqrt(ms + RMS_EPS) * g


def _log_sigmoid(x):
    return jnp.minimum(x, 0.0) - jnp.log1p(jnp.exp(-jnp.abs(x)))


def _silu(x):
    return x * jax.nn.sigmoid(x)


def _gelu(x):
    return 0.5 * x * (1.0 + lax.erf(x * (2.0 ** -0.5)))


def _tri(n):
    r = lax.broadcasted_iota(jnp.int32, (n, n), 0)
    c = lax.broadcasted_iota(jnp.int32, (n, n), 1)
    return jnp.where(c <= r, 1.0, 0.0).astype(bf16)


def _pack_pieces(x):
    packed = None
    for i, pc in enumerate(_split3(x)):
        pc = pc.astype(f32)
        if i:
            pc = pltpu.roll(pc, PIECE_PITCH * i, 1)
        packed = pc if packed is None else packed + pc
    return packed.astype(bf16)


def _unpack_pieces(y):
    return y + pltpu.roll(y, LANES - PIECE_PITCH, 1) + pltpu.roll(y, LANES - 2 * PIECE_PITCH, 1)


def _layer(l, shape):
    return pl.BlockSpec((None,) + tuple(shape), lambda *_: (l,) + (0,) * len(shape),
                        pipeline_mode=pl.Buffered(1))


def _stage_plan():
    cut = dict(zip(("q", "k", "v", "f", "ga", "u", "gs", "qm", "gm"), zip(IN_CUTS[:-1], IN_CUTS[1:])))
    src, valid = [], []
    for name in ("q", "k", "v", "u", "qm", "ga", "gs", "gm"):
        a, b = cut[name]
        assert (b - a) % LANES == 0
        src += list(range(a, b, LANES))
        valid += [LANES] * ((b - a) // LANES)
    assert len(src) * LANES == W_MAIN == OFF_F
    src.append(cut["f"][0])
    valid.append(N_FOX_HEADS)
    while len(src) * LANES < W_CAT:
        src.append(0)
        valid.append(0)
    return src, valid


def _stage_w_kernel(src_ref, valid_ref, w_ref, cat_ref, wm_ref, wf_ref):
    t = pl.program_id(0)
    for l in range(w_ref.shape[1]):
        w = w_ref[:, l, :]
        row = lax.broadcasted_iota(jnp.int32, w.shape, 0)
        wt = jnp.where(row < valid_ref[t], w, 0.0).T
        hi = wt.astype(bf16)
        cat_ref[l] = wt
        wm_ref[l] = hi

        @pl.when(t == OFF_F // LANES)
        def _():
            wf_ref[l, :, :LANES] = hi
            wf_ref[l, :, LANES:] = (wt - hi.astype(f32)).astype(bf16)


def _stage_w(w_in):
    depth, D, d_in = w_in.shape
    assert d_in == IN_CUTS[-1]
    src, valid = _stage_plan()
    whole = pl.Element
    out = pl.BlockSpec((depth, D, LANES), lambda t, src, valid: (0, 0, t))
    return pl.pallas_call(
        _stage_w_kernel,
        grid_spec=pltpu.PrefetchScalarGridSpec(
            num_scalar_prefetch=2,
            grid=(len(src),),
            in_specs=[pl.BlockSpec((pl.Element(LANES), whole(depth), whole(D)), lambda t, src, valid: (src[t], 0, 0))],
            out_specs=(out, out, pl.BlockSpec((depth, D, 2 * LANES), lambda t, src, valid: (0, 0, 0)))),
        out_shape=(jax.ShapeDtypeStruct((depth, D, W_CAT), f32), jax.ShapeDtypeStruct((depth, D, W_CAT), bf16),
                   jax.ShapeDtypeStruct((depth, D, 2 * LANES), bf16)),
        compiler_params=_cparams(("arbitrary",)),
        name="stage_weights",
    )(jnp.asarray(src, jnp.int32), jnp.asarray(valid, jnp.int32), jnp.transpose(w_in, (2, 0, 1)))


def _head_block(x, h):
    blk = x[:, LANES * (h // 2):LANES * (h // 2 + 1)]
    return blk if h % 2 == 0 else pltpu.roll(blk, HEAD_DIM, 1)


def _inproj_kernel(x_ref, g_ref, wm_ref, wf_ref, bf_ref, *rest):
    qa_ref, kt_ref, ka_ref, vt_ref, va_ref, kst_ref, lf_ref, sg_ref, u_ref, qm_ref, carry_ref = rest[-11:]
    i = pl.program_id(1)

    @pl.when(i == 0)
    def _():
        carry_ref[...] = jnp.zeros_like(carry_ref)

    xn = _rms(x_ref[...], g_ref[...])
    xh = xn.astype(bf16)
    tm = xn.shape[0]
    lane = lax.broadcasted_iota(jnp.int32, (tm, LANES), 1)
    low = lane < HEAD_DIM

    ff = _dot(xh, wf_ref[...])
    f = ff[:, :LANES] + ff[:, LANES:] + bf_ref[...]
    lf = jnp.where(lane < N_FOX_HEADS, _log_sigmoid(f), 0.0)
    lf_ref[...] = lf.T[:N_FOX_HEADS, :]
    cs = _unpack_pieces(_dot(_tri(tm), _pack_pieces(lf)))
    cs = jnp.where(lane < N_FOX_HEADS, cs, 0.0) + carry_ref[0:1, :]
    carry_ref[...] = jnp.broadcast_to(cs[tm - 1:, :], carry_ref.shape)
    prow = lax.broadcasted_iota(jnp.int32, (LANES, N_FOX_HEADS * LANES), 0)
    pcol = lax.broadcasted_iota(jnp.int32, (LANES, N_FOX_HEADS * LANES), 1)
    place = (pcol == (prow % PIECE_PITCH) * LANES + HEAD_DIM + prow // PIECE_PITCH) & (prow % PIECE_PITCH < N_FOX_HEADS)
    caug = _dot(_pack_pieces(-cs), jnp.where(place, 1.0, 0.0).astype(bf16))
    ones3 = jnp.where((lane >= HEAD_DIM) & (lane < HEAD_DIM + 3), 1.0, 0.0)

    qk = _dot(xh, wm_ref[:, OFF_Q:OFF_V])
    q = qk[:, :W_FOX] * QK_SCALE
    k = qk[:, W_FOX:]
    kt_ref[...] = k.T
    kb = k.astype(bf16).astype(f32)
    qb = q.astype(bf16).astype(f32)
    squares = jnp.concatenate([(kb * kb).astype(bf16), (qb * qb).astype(bf16)], axis=1)
    srow = lax.broadcasted_iota(jnp.int32, (2 * W_FOX, LANES), 0)
    scol = lax.broadcasted_iota(jnp.int32, (2 * W_FOX, LANES), 1)
    target = (srow // W_FOX) * SUBLANES + (srow % W_FOX) // HEAD_DIM
    sums = _dot(squares, jnp.where(scol == target, 1.0, 0.0).astype(bf16))
    norms = jnp.sqrt(jnp.max(sums, axis=0, keepdims=True)) * NORM_SLACK
    kst_ref[:KST_CEND, :] = _col_bcast(norms)[:KST_CEND, :]
    kst_ref[KST_CEND:, :] = _col_bcast(cs[tm - 1:, :])[:SUBLANES, :]
    vu = _dot(xh, wm_ref[:, OFF_V:OFF_QM])
    v = vu[:, :W_FOX]
    vt_ref[...] = v.T
    u_ref[...] = vu[:, W_FOX:]
    for h in range(N_FOX_HEADS):
        qa_ref[h] = jnp.where(low, _head_block(q, h), ones3).astype(bf16)
        ka_ref[h] = jnp.where(low, _head_block(k, h), caug[:, LANES * h:LANES * (h + 1)]).astype(bf16)
        va_ref[h] = jnp.where(low, _head_block(v, h), 1.0).astype(bf16)
    rest = _dot(xh, wm_ref[:, OFF_QM:W_MAIN])
    qm_ref[...] = (rest[:, :W_MEM] * QK_SCALE).astype(bf16)
    sg_ref[...] = _silu(rest[:, W_MEM:]).astype(bf16)


def _inproj(l, x, g_all, wm_all, wf_all, bf_all, kv_prev, tm):
    B, L, D = x.shape
    depth = wm_all.shape[0]
    row = lambda w: pl.BlockSpec((None, tm, w), lambda b, i: (b, i, 0))
    heads = pl.BlockSpec((None, N_FOX_HEADS, tm, LANES), lambda b, i: (b, 0, i, 0))
    slab = pl.BlockSpec((None, None, W_FOX, tm), lambda b, i: (l, b, 0, i))
    aug = jax.ShapeDtypeStruct((B, N_FOX_HEADS, L, LANES), bf16)
    kvt = jax.ShapeDtypeStruct((depth, B, W_FOX, L), f32)
    out_shape = (
        aug,
        kvt,
        aug,
        kvt,
        aug,
        jax.ShapeDtypeStruct((B, L // tm, KST_ROWS, LANES), f32),
        jax.ShapeDtypeStruct((depth, N_FOX_HEADS, B * L), f32),
        jax.ShapeDtypeStruct((B, L, W_MIX), bf16),
        jax.ShapeDtypeStruct((B, L, W_SSM), f32),
        jax.ShapeDtypeStruct((B, L, W_MEM), bf16),
    )
    kst = pl.BlockSpec((None, None, KST_ROWS, LANES), lambda b, i: (b, i, 0, 0))
    lft = pl.BlockSpec((None, N_FOX_HEADS, tm), lambda b, i: (l, 0, b * (L // tm) + i))
    out_specs = (heads, slab, heads, slab, heads, kst, lft, row(W_MIX), row(W_SSM), row(W_MEM))
    in_specs = [row(D), _layer(l, (1, D)), _layer(l, wm_all.shape[1:]), _layer(l, wf_all.shape[1:]),
                _layer(l, (1, LANES))]
    n_in = len(in_specs)
    return pl.pallas_call(
        _inproj_kernel,
        grid=(B, L // tm),
        in_specs=in_specs + [pl.BlockSpec(memory_space=pl.ANY)] * len(kv_prev),
        out_specs=out_specs,
        out_shape=out_shape,
        input_output_aliases={n_in + n: out for n, out in zip(range(len(kv_prev)), (1, 3, 6))},
        scratch_shapes=[pltpu.VMEM((SUBLANES, LANES), f32)],
        compiler_params=_cparams(("parallel", "arbitrary")),
        name="prompt_inproj",
    )(x, g_all, wm_all, wf_all, bf_all, *kv_prev)


def _memkv_kernel(m_ref, g_ref, w_ref, k_ref, v_ref):
    mn = _rms(m_ref[...], g_ref[...]).astype(bf16)
    kv = _dot(mn, w_ref[...])
    k_ref[...] = kv[:, :W_MEM]
    v_ref[...] = kv[:, W_MEM:]


def _memkv(mem, g_all, w_all):
    B, N, D = mem.shape
    depth = w_all.shape[0]
    out = pl.BlockSpec((None, None, N, W_MEM), lambda l, b: (l, b, 0, 0))
    return pl.pallas_call(
        _memkv_kernel,
        grid=(depth, B),
        in_specs=[pl.BlockSpec((None, N, D), lambda l, b: (b, 0, 0)),
                  pl.BlockSpec((None, 1, D), lambda l, b: (l, 0, 0)),
                  pl.BlockSpec((None,) + w_all.shape[1:], lambda l, b: (l, 0, 0))],
        out_specs=(out, out),
        out_shape=(jax.ShapeDtypeStruct((depth, B, N, W_MEM), f32),) * 2,
        compiler_params=_cparams(("parallel", "parallel")),
        name="prompt_memkv",
    )(mem, g_all, w_all)


def _first_needed_block(kst_ref, h, i, n_diag):
    n_full = i * n_diag
    mine = pl.ds(n_full, n_diag)
    qn = jnp.max(kst_ref[mine, KST_QN + h, :], axis=0, keepdims=True)
    kn_own = jnp.max(kst_ref[mine, KST_KN + h, :], axis=0, keepdims=True)
    lb = -qn * kn_own - kst_ref[pl.ds(jnp.maximum(n_full - 1, 0), 1), KST_CEND + h, :]
    ub = qn * kst_ref[:, KST_KN + h, :] - kst_ref[:, KST_CEND + h, :]
    kj = lax.broadcasted_iota(jnp.int32, ub.shape, 0)
    needed = (ub - lb >= -SKIP_GAP) & (kj < n_full)
    return _full_min(jnp.where(needed, kj, n_full).astype(f32))


def _fox_kernel(q_ref, k_ref, v_ref, kst_ref, o_ref, acc_ref, m_ref, *, tq, tk):
    j = pl.program_id(1)
    i = pl.program_id(2)
    acc_ref[...] = jnp.zeros_like(acc_ref)
    m_ref[...] = jnp.full_like(m_ref, -jnp.inf)

    def step(kj, r0):
        k0 = pl.multiple_of(kj * tk, tk)
        rows = slice(0, tq) if r0 is None else slice(r0, tq)
        n = rows.stop - rows.start
        for e in range(2):
            s = _dot_nt(q_ref[e, rows, :], k_ref[e, pl.ds(k0, tk), :])
            if r0 is not None:
                row = r0 + lax.broadcasted_iota(jnp.int32, (n, tk), 0)
                col = r0 + lax.broadcasted_iota(jnp.int32, (n, tk), 1)
                s = jnp.where(col <= row, s, -jnp.inf)
            m_old = m_ref[e, rows, :]
            m_new = jnp.maximum(m_old, jnp.max(s, axis=1, keepdims=True))
            p = jnp.exp((s - jnp.tile(m_new, (1, tk // LANES))).astype(bf16))
            acc_ref[e, rows, :] = (jnp.exp(m_old - m_new) * acc_ref[e, rows, :]
                                   + _dot(p, v_ref[e, pl.ds(k0, tk), :]))
            m_ref[e, rows, :] = m_new

    n_diag = tq // tk
    first = jnp.minimum(*[_first_needed_block(kst_ref, 2 * j + e, i, n_diag) for e in range(2)])
    t_first = jnp.min(first).astype(jnp.int32) // n_diag

    def body(t, carry):
        for d in range(n_diag):
            step(t * n_diag + d, None)
        return carry

    lax.fori_loop(t_first, i, body, 0)
    for d in range(n_diag):
        step(i * n_diag + d, d * tk)
    lane = lax.broadcasted_iota(jnp.int32, (tq, LANES), 1)
    a0 = acc_ref[0]
    a1 = acc_ref[1]
    o = jnp.where(lane < HEAD_DIM, a0 / pltpu.roll(a0, HEAD_DIM, 1), pltpu.roll(a1, HEAD_DIM, 1) / a1)
    o_ref[...] = o.astype(o_ref.dtype)


def _fox(qa, ka, va, kst, tq, tk):
    B, _, L, _ = qa.shape
    n_pairs = W_FOX // LANES
    assert kst.shape[1] == L // tk
    return pl.pallas_call(
        functools.partial(_fox_kernel, tq=tq, tk=tk),
        grid=(B, n_pairs, L // tq),
        in_specs=[pl.BlockSpec((None, 2, tq, LANES), lambda b, j, i: (b, j, i, 0)),
                  pl.BlockSpec((None, 2, L, LANES), lambda b, j, i: (b, j, 0, 0)),
                  pl.BlockSpec((None, 2, L, LANES), lambda b, j, i: (b, j, 0, 0)),
                  pl.BlockSpec((None,) + kst.shape[1:], lambda b, j, i: (b, 0, 0, 0))],
        out_specs=pl.BlockSpec((None, tq, LANES), lambda b, j, i: (b, i, j)),
        out_shape=jax.ShapeDtypeStruct((B, L, W_FOX), bf16),
        scratch_shapes=[pltpu.VMEM((2, tq, LANES), f32), pltpu.VMEM((2, tq, LANES), f32)],
        compiler_params=_cparams(("parallel", "parallel", "arbitrary")),
        name="prompt_fox",
    )(qa, ka, va, kst)


def _s5prep_kernel(lr_ref, li_ref, ldt_ref, bre_ref, bim_ref, cre_ref, cim_ref,
                   lamb_ref, bmat_ref, cmat_ref, lpow_ref, lams_ref, pw_ref, *, seg_len):
    lr = lr_ref[...]
    li = li_ref[...]
    dt = jnp.exp(ldt_ref[...])
    ar = lr * dt
    ai = li * dt
    mag = jnp.exp(ar)
    lbr = mag * jnp.cos(ai)
    lbi = mag * jnp.sin(ai)
    lamb_ref[:, :HALF] = jnp.broadcast_to(lbr, (SUBLANES, HALF))
    lamb_ref[:, HALF:] = jnp.broadcast_to(lbi, (SUBLANES, HALF))
    den = lr * lr + li * li
    zr = ((lbr - 1.0) * lr + lbi * li) / den
    zi = (lbi * lr - (lbr - 1.0) * li) / den
    bre = bre_ref[...]
    bim = bim_ref[...]
    bmat_ref[:, :HALF] = zr * bre - zi * bim
    bmat_ref[:, HALF:] = zr * bim + zi * bre
    cmat_ref[:HALF, :] = cre_ref[...]
    cmat_ref[HALF:, :] = -cim_ref[...]
    jj = (lax.broadcasted_iota(jnp.int32, (SUBLANES, HALF), 0) + 1).astype(f32)
    magp = jnp.exp(ar * jj)
    pw_ref[:SUBLANES, :HALF] = magp * jnp.cos(ai * jj)
    pw_ref[:SUBLANES, HALF:] = magp * jnp.sin(ai * jj)
    n = SUBLANES
    while n < seg_len:
        sr = pw_ref[n - 1:n, :HALF]
        si = pw_ref[n - 1:n, HALF:]
        pr = pw_ref[:n, :HALF]
        pi = pw_ref[:n, HALF:]
        pw_ref[n:2 * n, :HALF] = pr * sr - pi * si
        pw_ref[n:2 * n, HALF:] = pr * si + pi * sr
        n *= 2
    lpow_ref[...] = pw_ref[...].astype(bf16)
    lams_ref[...] = pw_ref[seg_len - SUBLANES:, :]


def _s5prep(lam_re, lam_im, log_dt, b_re, b_im, c_re, c_im, seg_len):
    depth = lam_re.shape[0]
    assert seg_len >= SUBLANES and seg_len & (seg_len - 1) == 0
    nb, gb = N_SSM_BLOCKS, GROUPS_PER_BLOCK
    eye = jnp.eye(gb, dtype=f32)
    half = lambda a: a.reshape(depth, nb, 1, HALF)
    ldt = jnp.broadcast_to(log_dt[:, :, None], (depth, N_SSM_GROUPS, SSM_STATE))

    def embed_b(b):
        t = b.reshape(depth, nb, gb, SSM_STATE, SSM_GROUP).transpose(0, 1, 2, 4, 3)
        return (t[:, :, :, :, None, :] * eye[None, None, :, None, :, None]).reshape(depth, nb, LANES, HALF)

    def embed_c(c):
        t = c.reshape(depth, nb, gb, SSM_GROUP, SSM_STATE).transpose(0, 1, 2, 4, 3)
        return (t[:, :, :, :, None, :] * eye[None, None, :, None, :, None]).reshape(depth, nb, HALF, LANES)

    blk = lambda shape: pl.BlockSpec((None, None) + shape, lambda l, cb: (l, cb) + (0,) * len(shape))
    arr = lambda shape, dt: jax.ShapeDtypeStruct((depth, nb) + shape, dt)
    return pl.pallas_call(
        functools.partial(_s5prep_kernel, seg_len=seg_len),
        grid=(depth, nb),
        in_specs=[blk((1, HALF))] * 3 + [blk((LANES, HALF))] * 2 + [blk((HALF, LANES))] * 2,
        out_specs=(blk((SUBLANES, STATE_W)), blk((LANES, STATE_W)), blk((STATE_W, LANES)),
                   blk((seg_len, STATE_W)), blk((SUBLANES, STATE_W))),
        out_shape=(arr((SUBLANES, STATE_W), f32),
                   arr((LANES, STATE_W), f32),
                   arr((STATE_W, LANES), f32),
                   arr((seg_len, STATE_W), bf16),
                   arr((SUBLANES, STATE_W), f32)),
        scratch_shapes=[pltpu.VMEM((seg_len, STATE_W), f32)],
        compiler_params=_cparams(("parallel", "parallel")),
        name="s5_tables",
    )(half(lam_re), half(lam_im), half(ldt), embed_b(b_re), embed_b(b_im), embed_c(c_re), embed_c(c_im))


def _s5scan_kernel(u_ref, bmat_ref, cmat_ref, lamb_ref, dsk_ref, y_ref, hend_ref,
                   up_ref, hb_ref, hq_ref, hcar_ref, *, tj):
    j = pl.program_id(1)
    nb = N_SSM_BLOCKS

    @pl.when(j == 0)
    def _():
        hcar_ref[...] = jnp.zeros_like(hcar_ref)

    for cb in range(nb):
        for s in range(N_SEG):
            up_ref[cb, pl.ds(s, tj, stride=N_SEG), :] = u_ref[s, :, LANES * cb:LANES * (cb + 1)]
        hb_ref[cb] = _dot(up_ref[cb].astype(bf16), bmat_ref[cb])

    def rec(t2, h):
        r0 = pl.multiple_of(t2 * 2 * N_SEG, 2 * N_SEG)
        new = []
        for cb in range(nb):
            hr, hi = h[2 * cb], h[2 * cb + 1]
            lr = lamb_ref[cb, :, :HALF]
            li = lamb_ref[cb, :, HALF:]
            rs, is_ = [], []
            for d in range(2):
                bu = hb_ref[cb, pl.ds(r0 + d * N_SEG, N_SEG), :]
                hr, hi = lr * hr - li * hi + bu[:, :HALF], lr * hi + li * hr + bu[:, HALF:]
                rs.append(hr)
                is_.append(hi)
            hq_ref[cb, pl.ds(r0, 2 * N_SEG), :HALF] = jnp.concatenate(rs, axis=0).astype(bf16)
            hq_ref[cb, pl.ds(r0, 2 * N_SEG), HALF:] = jnp.concatenate(is_, axis=0).astype(bf16)
            new += [hr, hi]
        return tuple(new)

    h0 = tuple(hcar_ref[cb, :, HALF * part:HALF * (part + 1)] for cb in range(nb) for part in range(2))
    h = lax.fori_loop(0, tj // 2, rec, h0, unroll=True)
    for cb in range(nb):
        hcar_ref[cb, :, :HALF] = h[2 * cb]
        hcar_ref[cb, :, HALF:] = h[2 * cb + 1]
        up_ref[cb] = _dot(hq_ref[cb], cmat_ref[cb]) + dsk_ref[cb] * up_ref[cb]
        for s in range(N_SEG):
            y_ref[s, :, LANES * cb:LANES * (cb + 1)] = up_ref[cb, pl.ds(s, tj, stride=N_SEG), :].astype(y_ref.dtype)

    @pl.when(j == pl.num_programs(1) - 1)
    def _():
        hend_ref[...] = hcar_ref[...]


def _s5scan(l, u, bmat_all, cmat_all, lamb_all, dsk_all, tj):
    B, L, _ = u.shape
    seg_len = L // N_SEG
    u4 = u.reshape(B, N_SEG, seg_len, W_SSM)
    nb = N_SSM_BLOCKS
    tile = pl.BlockSpec((None, N_SEG, tj, W_SSM), lambda b, j: (b, 0, j, 0))
    y4, hend = pl.pallas_call(
        functools.partial(_s5scan_kernel, tj=tj),
        grid=(B, seg_len // tj),
        in_specs=[tile, _layer(l, bmat_all.shape[1:]), _layer(l, cmat_all.shape[1:]),
                  _layer(l, lamb_all.shape[1:]), _layer(l, dsk_all.shape[1:])],
        out_specs=(tile, pl.BlockSpec((None, nb, N_SEG, STATE_W), lambda b, j: (b, 0, 0, 0))),
        out_shape=(jax.ShapeDtypeStruct((B, N_SEG, seg_len, W_SSM), bf16),
                   jax.ShapeDtypeStruct((B, nb, N_SEG, STATE_W), f32)),
        scratch_shapes=[pltpu.VMEM((nb, N_SEG * tj, LANES), f32), pltpu.VMEM((nb, N_SEG * tj, STATE_W), f32),
                        pltpu.VMEM((nb, N_SEG * tj, STATE_W), bf16), pltpu.VMEM((nb, N_SEG, STATE_W), f32)],
        compiler_params=_cparams(("parallel", "arbitrary")),
        name="prompt_s5_scan",
    )(u4, bmat_all, cmat_all, lamb_all, dsk_all)
    return y4.reshape(B, L, W_SSM), hend


def _s5fix_kernel(hend_ref, lams_ref, cmat_ref, hlast_ref, cp_ref, hs_ref):
    he = hend_ref[...]
    lsr = lams_ref[SUBLANES - 1:, :HALF]
    lsi = lams_ref[SUBLANES - 1:, HALF:]
    hs_ref[...] = jnp.zeros_like(hs_ref)
    hr = jnp.zeros((1, HALF), f32)
    hi = jnp.zeros((1, HALF), f32)
    for s in range(N_SEG):
        hs_ref[s:s + 1, :HALF] = hr
        hs_ref[s:s + 1, HALF:] = hi
        nr = lsr * hr - lsi * hi + he[s:s + 1, :HALF]
        ni = lsr * hi + lsi * hr + he[s:s + 1, HALF:]
        hr, hi = nr, ni
    hlast_ref[:, :HALF] = hr
    hlast_ref[:, HALF:] = hi
    hst_r = hs_ref[:, :HALF].T
    hst_i = hs_ref[:, HALF:].T
    cre = cmat_ref[:HALF, :]
    mci = cmat_ref[HALF:, :]
    for s in range(N_SEG):
        col_r = hst_r[:, s:s + 1]
        col_i = hst_i[:, s:s + 1]
        cp_ref[s, :HALF, :] = (cre * col_r + mci * col_i).astype(bf16)
        cp_ref[s, HALF:, :] = (mci * col_r - cre * col_i).astype(bf16)


def _s5fix(l, hend, lams_all, cmat_all):
    B, nb = hend.shape[:2]
    return pl.pallas_call(
        _s5fix_kernel,
        grid=(B, nb),
        in_specs=[pl.BlockSpec((None, None, N_SEG, STATE_W), lambda b, cb: (b, cb, 0, 0)),
                  pl.BlockSpec((None, None, SUBLANES, STATE_W), lambda b, cb: (l, cb, 0, 0)),
                  pl.BlockSpec((None, None, STATE_W, LANES), lambda b, cb: (l, cb, 0, 0))],
        out_specs=(pl.BlockSpec((None, None, 1, STATE_W), lambda b, cb: (b, cb, 0, 0)),
                   pl.BlockSpec((None, N_SEG, None, STATE_W, LANES), lambda b, cb: (b, 0, cb, 0, 0))),
        out_shape=(jax.ShapeDtypeStruct((B, nb, 1, STATE_W), f32),
                   jax.ShapeDtypeStruct((B, N_SEG, nb, STATE_W, LANES), bf16)),
        scratch_shapes=[pltpu.VMEM((LANES, STATE_W), f32)],
        compiler_params=_cparams(("parallel", "parallel")),
        name="prompt_s5_fix",
    )(hend, lams_all, cmat_all)


def _pair_attend(q, k, v):
    lane = lax.broadcasted_iota(jnp.int32, q.shape, 1)
    zero = jnp.zeros_like(q)
    outs = []
    for e in range(2):
        qe = jnp.where((lane < HEAD_DIM) == (e == 0), q, zero)
        s = _dot_nt(qe, k)
        p = jnp.exp(s - jnp.max(s, axis=1, keepdims=True))
        outs.append(_dot(p.astype(bf16), v) / jnp.sum(p, axis=1, keepdims=True))
    return jnp.where(lane < HEAD_DIM, outs[0], outs[1])


N_OUTPROJ_INPUTS = 13


def _outproj_stages(x_ref, fox_ref, y_ref, lp_ref, cp_ref, qm_ref, mk_ref, mv_ref, sg_ref,
                    wglu_ref, bglu_ref, wout_ref, gfin_ref, o_ref, *, final, tiles_per_seg):
    tm = x_ref.shape[0]
    j0 = pl.multiple_of((pl.program_id(1) % tiles_per_seg) * tm, tm)
    corr = [_dot(lp_ref[cb, pl.ds(j0, tm), :], cp_ref[cb]) for cb in range(N_SSM_BLOCKS)]
    ssm_y = y_ref[...] + jnp.concatenate(corr, axis=1)
    yield
    s = _gelu(ssm_y)
    s = s * jax.nn.sigmoid(_dot(s.astype(bf16), wglu_ref[...]) + bglu_ref[...])
    yield
    qm = qm_ref[...]
    mk = mk_ref[...].astype(bf16)
    mv = mv_ref[...].astype(bf16)
    mem = []
    for jp in range(W_MEM // LANES):
        mem.append(_pair_attend(qm[:, LANES * jp:LANES * (jp + 1)], mk[:, LANES * jp:LANES * (jp + 1)],
                                mv[:, LANES * jp:LANES * (jp + 1)]))
        yield
    sg = sg_ref[...].astype(f32)
    mix = jnp.concatenate([fox_ref[...] * sg[:, :W_FOX],
                           s * sg[:, W_FOX:W_FOX + W_SSM],
                           jnp.concatenate(mem, axis=1) * sg[:, W_FOX + W_SSM:]], axis=1)
    xo = x_ref[...] + _dot(mix.astype(bf16), wout_ref[...])
    yield
    if final:
        xo = _rms(xo, gfin_ref[...])
    o_ref[...] = xo


def _s_inproj_kernel(x_ref, g_ref, w_ref, z_ref):
    xn = _rms(x_ref[...], g_ref[...])
    z_ref[...] = _dot3(xn, w_ref[...])


def _s_inproj(l, xs, g_all, wcat_all, tn):
    R, D = xs.shape
    return pl.pallas_call(
        _s_inproj_kernel,
        grid=(W_CAT // tn,),
        in_specs=[pl.BlockSpec((R, D), lambda n: (0, 0)), _layer(l, (1, D)),
                  pl.BlockSpec((None, D, tn), lambda n: (l, 0, n))],
        out_specs=pl.BlockSpec((R, tn), lambda n: (0, n)),
        out_shape=jax.ShapeDtypeStruct((R, W_CAT), f32),
        compiler_params=_cparams(("parallel",)),
        name="sample_inproj",
    )(xs, g_all, wcat_all)


def _full_max(x):
    return jnp.max(jnp.max(x, axis=0, keepdims=True), axis=1, keepdims=True)


def _full_min(x):
    return jnp.min(jnp.min(x, axis=0, keepdims=True), axis=1, keepdims=True)


def _full_sum(x):
    return jnp.sum(jnp.sum(x, axis=0, keepdims=True), axis=1, keepdims=True)


def _lane_sums(acc):
    ones = jnp.ones((SUBLANES, LANES), bf16)
    hi, lo = _split2(acc)
    return (_dot_nt(ones, hi) + _dot_nt(ones, lo))[0:1, :]


def _col_bcast(row):
    return jnp.broadcast_to(row, (LANES, row.shape[1])).T


class _DecodeGuest:
    N_FIXED_IN, N_OUT, N_SCRATCH = 3, 3, 5

    def __init__(self, l, n_seq, first_seq, n_steps, step_of, page_table, z, bf_all, ckt, cvt, lft, mkt, mvt):
        R, n_pages = page_table.shape
        n_pool, page = ckt.shape[1], ckt.shape[4]
        n_mem = mkt.shape[4]
        assert page == LANES and n_mem % LANES == 0 and first_seq % n_seq == 0
        assert first_seq + n_steps * n_seq <= R
        self.n_seq, self.n_pages, self.first_seq, self.step_of = n_seq, n_pages, first_seq, step_of
        self.per_seq = 2 * n_pages + 2
        z_blk = lambda *g: (first_seq // n_seq + step_of(*g[:-1]), 0, 0)
        self.in_specs = [pl.BlockSpec((n_seq, 1, W_CAT), z_blk), _layer(l, (1, LANES)),
                         _layer(l, (N_FOX_HEADS, n_pool, page))]
        self.operands = [z.reshape(R, 1, W_CAT), bf_all, lft]
        for s in range(n_seq):
            seq = lambda g, s=s: first_seq + step_of(*g) * n_seq + s
            paged = [pl.BlockSpec((None, None, N_FOX_HEADS, HEAD_DIM, page),
                                  lambda *g, seq=seq, p=p: (l, g[-1][seq(g[:-1]), p], 0, 0, 0))
                     for p in range(n_pages)]
            mem_spec = pl.BlockSpec((None, None, N_MEM_HEADS, HEAD_DIM, n_mem),
                                    lambda *g, seq=seq: (l, seq(g[:-1]), 0, 0, 0))
            self.in_specs += paged + paged + [mem_spec, mem_spec]
            self.operands += [ckt] * n_pages + [cvt] * n_pages + [mkt, mvt]
        out = lambda w: pl.BlockSpec((n_seq, 1, w), lambda *g: (step_of(*g[:-1]), 0, 0))
        self.out_specs = [out(W_FOX), out(W_MEM), out(LANES)]
        n_out = n_steps * n_seq
        self.out_shape = [jax.ShapeDtypeStruct((n_out, 1, w), f32) for w in (W_FOX, W_MEM, LANES)]
        n_rows = N_FOX_HEADS * n_pages
        self.scratch_shapes = [pltpu.VMEM((n_seq, rows, LANES), f32)
                               for rows in (n_rows, n_rows, n_rows, W_FOX, W_MEM)]

    def chains(self, pt_ref, step, ins, outs, scratch):
        z_ref, bf_ref, lfc_ref = ins[:self.N_FIXED_IN]
        fox_ref, mem_ref, lf_ref = outs
        n_pages = self.n_pages
        result = []
        for i in range(self.n_seq):
            mine = ins[self.N_FIXED_IN + i * self.per_seq:self.N_FIXED_IN + (i + 1) * self.per_seq]
            result.append(_decode_one(
                pt_ref, self.first_seq + step * self.n_seq + i, z_ref.at[i], bf_ref, lfc_ref,
                mine[:n_pages], mine[n_pages:2 * n_pages], mine[2 * n_pages], mine[2 * n_pages + 1],
                fox_ref.at[i], mem_ref.at[i], lf_ref.at[i], *[s.at[i] for s in scratch]))
        return result

    @property
    def n_in(self):
        return self.N_FIXED_IN + self.n_seq * self.per_seq


def _run_chains(chains):
    while chains:
        chains = [c for c in chains if next(c, _DONE) is not _DONE]


def _outproj_decode_kernel(pt_ref, *refs, guest, final, tiles_per_seg):
    n_in = N_OUTPROJ_INPUTS + guest.n_in
    op_in, g_in = refs[:N_OUTPROJ_INPUTS], refs[N_OUTPROJ_INPUTS:n_in]
    o_ref = refs[n_in]
    g_out = refs[n_in + 1:n_in + 1 + guest.N_OUT]
    g_scr = refs[n_in + 1 + guest.N_OUT:]
    step = guest.step_of(pl.program_id(0), pl.program_id(1))
    _run_chains([_outproj_stages(*op_in, o_ref, final=final, tiles_per_seg=tiles_per_seg)]
                + guest.chains(pt_ref, step, g_in, g_out, g_scr))


_DONE = object()


def _decode_one(pt_ref, r, z_ref, bf_ref, lfc_ref, ck, cv, mk_ref, mv_ref, fox_ref, mem_ref, lf_ref,
                s_scr, lfs_scr, p_scr, acc_scr, accm_scr):
    n_pages = len(ck)
    z = z_ref[...]
    q = z[:, OFF_Q:OFF_Q + W_FOX] * QK_SCALE
    k_new = z[:, OFF_K:OFF_K + W_FOX]
    v_new = z[:, OFF_V:OFF_V + W_FOX]
    lf_new = _log_sigmoid(z[:, OFF_F:OFF_F + LANES] + bf_ref[...])
    lf_ref[...] = lf_new

    qt = _col_bcast(q)
    for h in range(N_FOX_HEADS):
        qh = qt[HEAD_DIM * h:HEAD_DIM * (h + 1), :]
        for p in range(n_pages):
            row = h * n_pages + p
            s_scr[row:row + 1, :] = jnp.sum(ck[p][h] * qh, axis=0, keepdims=True)
            lfs_scr[row:row + 1, :] = lfc_ref[h, pl.ds(pt_ref[r, p], 1), :]
        yield

    n_rows = N_FOX_HEADS * n_pages
    ur = lax.broadcasted_iota(jnp.int32, (LANES, LANES), 0)
    uc = lax.broadcasted_iota(jnp.int32, (LANES, LANES), 1)
    upper = jnp.where(ur <= uc, 1.0, 0.0).astype(bf16)
    cs = sum(_dot(pc, upper) for pc in _split3(lfs_scr[...]))
    tot = jnp.broadcast_to(cs[:, LANES - 1:], (n_rows, LANES))
    mr = lax.broadcasted_iota(jnp.int32, (n_rows, n_rows), 0)
    mc = lax.broadcasted_iota(jnp.int32, (n_rows, n_rows), 1)
    before = jnp.where((mc < mr) & (mc // n_pages == mr // n_pages), 1.0, 0.0).astype(bf16)
    c = cs + sum(_dot(before, pc) for pc in _split3(tot))
    s_all = s_scr[...] - c
    yield

    qk_new = q * k_new
    lane_head = lax.broadcasted_iota(jnp.int32, (1, W_FOX), 1) // HEAD_DIM
    pn_exp = jnp.zeros((1, W_FOX), f32)
    l_exp = jnp.zeros((1, W_FOX), f32)
    for h in range(N_FOX_HEADS):
        rows = slice(h * n_pages, (h + 1) * n_pages)
        s_h = s_all[rows, :]
        c_new = c[(h + 1) * n_pages - 1:(h + 1) * n_pages, LANES - 1:] + lf_new[:, h:h + 1]
        s_new = jnp.sum(qk_new[:, HEAD_DIM * h:HEAD_DIM * (h + 1)], axis=1, keepdims=True) - c_new
        m = jnp.maximum(_full_max(s_h), s_new)
        p_h = jnp.exp(s_h - m)
        pn = jnp.exp(s_new - m)
        p_scr[rows, :] = p_h
        pn_exp = jnp.where(lane_head == h, pn, pn_exp)
        l_exp = jnp.where(lane_head == h, _full_sum(p_h) + pn, l_exp)
        acc = jnp.zeros((HEAD_DIM, LANES), f32)
        for p in range(n_pages):
            acc = acc + cv[p][h] * p_scr[h * n_pages + p:h * n_pages + p + 1, :]
        acc_scr[HEAD_DIM * h:HEAD_DIM * (h + 1), :] = acc
        yield
    fox_ref[...] = (_lane_sums(acc_scr[...]) + pn_exp * v_new) / l_exp
    yield

    qm = z[:, OFF_QM:OFF_QM + W_MEM] * QK_SCALE
    qmt = _col_bcast(qm)
    n_halves = mk_ref.shape[2] // LANES
    mem_head = lax.broadcasted_iota(jnp.int32, (1, W_MEM), 1) // HEAD_DIM
    lm_exp = jnp.zeros((1, W_MEM), f32)
    for h in range(N_MEM_HEADS):
        qh = qmt[HEAD_DIM * h:HEAD_DIM * (h + 1), :]
        ss = [jnp.sum(mk_ref[h, :, LANES * t:LANES * (t + 1)] * qh, axis=0, keepdims=True) for t in range(n_halves)]
        m = functools.reduce(jnp.maximum, [jnp.max(s, axis=1, keepdims=True) for s in ss])
        ps = [jnp.exp(s - m) for s in ss]
        lm = sum(jnp.sum(p, axis=1, keepdims=True) for p in ps)
        lm_exp = jnp.where(mem_head == h, lm, lm_exp)
        accm_scr[HEAD_DIM * h:HEAD_DIM * (h + 1), :] = sum(
            mv_ref[h, :, LANES * t:LANES * (t + 1)] * ps[t] for t in range(n_halves))
    mem_ref[...] = _lane_sums(accm_scr[...]) / lm_exp


def _outproj_decode(l, x, fox_o, y, lpow_all, cp, qm, mk_all, mv_all, sg, wglu_all, bglu_all, wout_all, gfin,
                    page_table, z, bf_all, ckt, cvt, lft, mkt, mvt, first_seq, tm, final):
    B, L, D = x.shape
    seg_len = L // N_SEG
    tiles_per_seg = seg_len // tm
    n_i = L // tm
    n_steps = B * n_i
    n_left = page_table.shape[0] - first_seq
    assert n_left % n_steps == 0
    guest = _DecodeGuest(l, n_left // n_steps, first_seq, n_steps, lambda b, i: b * n_i + i,
                         page_table, z, bf_all, ckt, cvt, lft, mkt, mvt)
    nb = N_SSM_BLOCKS
    row = lambda w: pl.BlockSpec((None, tm, w), lambda b, i, pt: (b, i, 0))
    mem_kv = pl.BlockSpec((None, None) + mk_all.shape[2:], lambda b, i, pt: (l, b, 0, 0))
    in_specs = [row(D), row(W_FOX), row(W_SSM), _layer(l, lpow_all.shape[1:]),
                pl.BlockSpec((None, None, nb, STATE_W, LANES), lambda b, i, pt: (b, i // tiles_per_seg, 0, 0, 0)),
                row(W_MEM), mem_kv, mem_kv, row(D),
                _layer(l, wglu_all.shape[1:]), _layer(l, (1, W_SSM)), _layer(l, wout_all.shape[1:]),
                pl.BlockSpec((1, D), lambda b, i, pt: (0, 0))]
    assert len(in_specs) == N_OUTPROJ_INPUTS
    xo, *dec = pl.pallas_call(
        functools.partial(_outproj_decode_kernel, guest=guest, final=final, tiles_per_seg=tiles_per_seg),
        grid_spec=pltpu.PrefetchScalarGridSpec(
            num_scalar_prefetch=1,
            grid=(B, n_i),
            in_specs=in_specs + guest.in_specs,
            out_specs=[row(D)] + guest.out_specs,
            scratch_shapes=guest.scratch_shapes),
        out_shape=[jax.ShapeDtypeStruct((B, L, D), f32)] + guest.out_shape,
        compiler_params=_cparams(("parallel", "parallel")),
        name="outproj_decode",
    )(page_table, x, fox_o, y, lpow_all, cp, qm, mk_all, mv_all, sg, wglu_all, bglu_all, wout_all, gfin,
      *guest.operands)
    return xo, dec


def _s_out_kernel(x_ref, z_ref, fox_ref, mem_ref, h0_ref, bmat_ref, cmat_ref, lamb_ref, dsk_ref,
                  wglu_ref, bglu_ref, wout_ref, gfin_ref, o_ref, h_ref, *, final):
    z = z_ref[...]
    u = z[:, OFF_U:OFF_U + W_SSM]
    ys = []
    for cb in range(N_SSM_BLOCKS):
        ucb = u[:, LANES * cb:LANES * (cb + 1)]
        bu = _dot3(ucb, bmat_ref[cb])
        lr = lamb_ref[cb, 0:1, :HALF]
        li = lamb_ref[cb, 0:1, HALF:]
        h0r = h0_ref[:, STATE_W * cb:STATE_W * cb + HALF]
        h0i = h0_ref[:, STATE_W * cb + HALF:STATE_W * (cb + 1)]
        hr = lr * h0r - li * h0i + bu[:, :HALF]
        hi = lr * h0i + li * h0r + bu[:, HALF:]
        h_ref[:, STATE_W * cb:STATE_W * cb + HALF] = hr
        h_ref[:, STATE_W * cb + HALF:STATE_W * (cb + 1)] = hi
        ys.append(_dot3(jnp.concatenate([hr, hi], axis=1), cmat_ref[cb]) + dsk_ref[cb] * ucb)
    s = _gelu(jnp.concatenate(ys, axis=1))
    s = s * jax.nn.sigmoid(_dot3(s, wglu_ref[...]) + bglu_ref[...])
    sg = _silu(z[:, OFF_G:W_MAIN])
    mix = jnp.concatenate([fox_ref[...] * sg[:, :W_FOX],
                           s * sg[:, W_FOX:W_FOX + W_SSM],
                           mem_ref[...] * sg[:, W_FOX + W_SSM:]], axis=1)
    xo = x_ref[...] + _dot3(mix, wout_ref[...])
    if final:
        xo = _rms(xo, gfin_ref[...])
    o_ref[...] = xo


def _s_out(l, xs, z, fox, mem, h0_all, bmat_all, cmat_all, lamb_all, dsk_all, wglu_all, bglu_all, wout_all,
           gfin, final):
    R, D = xs.shape
    whole = lambda a: pl.BlockSpec(a.shape, lambda i: (0,) * a.ndim)
    hspec = pl.BlockSpec((R, h0_all.shape[2]), lambda i: (0, 0))
    return pl.pallas_call(
        functools.partial(_s_out_kernel, final=final),
        grid=(1,),
        in_specs=[whole(xs), whole(z), whole(fox), whole(mem), _layer(l, h0_all.shape[1:]),
                  _layer(l, bmat_all.shape[1:]), _layer(l, cmat_all.shape[1:]), _layer(l, lamb_all.shape[1:]),
                  _layer(l, dsk_all.shape[1:]), _layer(l, wglu_all.shape[1:]), _layer(l, (1, W_SSM)),
                  _layer(l, wout_all.shape[1:]), whole(gfin)],
        out_specs=(whole(xs), hspec),
        out_shape=(jax.ShapeDtypeStruct((R, D), f32), jax.ShapeDtypeStruct(h0_all.shape[1:], f32)),
        compiler_params=_cparams(("arbitrary",)),
        name="sample_out",
    )(xs, z, fox, mem, h0_all, bmat_all, cmat_all, lamb_all, dsk_all, wglu_all, bglu_all, wout_all, gfin)


def _state_to_cols(h):
    lead = h.shape[:-3]
    n = len(lead)
    t = h.reshape(lead + (N_SSM_BLOCKS, GROUPS_PER_BLOCK, SSM_STATE, 2))
    t = t.transpose(tuple(range(n)) + (n, n + 3, n + 1, n + 2))
    return t.reshape(lead + (N_SSM_BLOCKS * STATE_W,))


def _cols_to_state(c):
    lead = c.shape[:-1]
    n = len(lead)
    t = c.reshape(lead + (N_SSM_BLOCKS, 2, GROUPS_PER_BLOCK, SSM_STATE))
    t = t.transpose(tuple(range(n)) + (n, n + 2, n + 3, n + 1))
    return t.reshape(lead + (N_SSM_GROUPS, SSM_STATE, 2))


def _tile(n, pref):
    t = min(n, pref)
    assert n % t == 0
    return t


def kernel(x_prompt, x_sample, mem_prompt, cache_k, cache_v, cache_logf, state_ssm, cache_mem_k, cache_mem_v,
           page_table, g_norm, w_in, b_f, lam_re, lam_im, log_dt, b_re, b_im, c_re, c_im, d_skip, w_glu, b_glu,
           g_mem, w_mem_kv, w_out, g_final):
    depth = w_in.shape[0]
    B, L, D = x_prompt.shape
    R = x_sample.shape[0]
    assert x_sample.shape[1] == 1 and L % N_SEG == 0
    seg_len = L // N_SEG
    tm = _tile(seg_len, TM_OUT)
    tq = _tile(L, TQ_FOX)
    tk = _tile(tq, TK_FOX)
    tj = _tile(seg_len, TJ_SCAN)

    wcat_all, wm_all, wf_all = _stage_w(w_in)
    bf_all = jnp.pad(b_f, ((0, 0), (0, LANES - N_FOX_HEADS))).reshape(depth, 1, LANES)
    gn_all = g_norm.reshape(depth, 1, D)
    gmem_all = g_mem.reshape(depth, 1, D)
    bglu_all = b_glu.reshape(depth, 1, W_SSM)
    gfin = g_final.reshape(1, D)
    dsk_all = d_skip.reshape(depth, N_SSM_BLOCKS, 1, LANES)
    wglu_bf = w_glu.astype(bf16)
    wout_bf = w_out.astype(bf16)
    ckt = jnp.transpose(cache_k, (0, 1, 3, 4, 2))
    cvt = jnp.transpose(cache_v, (0, 1, 3, 4, 2))
    lft = jnp.transpose(cache_logf, (0, 3, 1, 2))
    mkt = jnp.transpose(cache_mem_k, (0, 1, 3, 4, 2))
    mvt = jnp.transpose(cache_mem_v, (0, 1, 3, 4, 2))
    h0_all = _state_to_cols(state_ssm)

    lamb_all, bmat_all, cmat_all, lpow_all, lams_all = _s5prep(lam_re, lam_im, log_dt, b_re, b_im, c_re, c_im, seg_len)
    bmat_bf = bmat_all.astype(bf16)
    cmat_bf = cmat_all.astype(bf16)
    mk_all, mv_all = _memkv(mem_prompt, gmem_all, w_mem_kv.astype(bf16))

    xp = x_prompt
    xs = x_sample.reshape(R, D)
    outs = {n: [] for n in ("hp", "z", "lfs", "hs")}
    kv_prev = ()
    for l in range(depth):
        final = l == depth - 1
        qa, kt_all, ka, vt_all, va, kst, lfp_all, sg, u, qm = _inproj(l, xp, gn_all, wm_all, wf_all, bf_all,
                                                                      kv_prev, tk)
        kv_prev = (kt_all, vt_all, lfp_all)
        fox_o = _fox(qa, ka, va, kst, tq, tk)
        y, hend = _s5scan(l, u, bmat_bf, cmat_bf, lamb_all, dsk_all, tj)
        hlast, cp = _s5fix(l, hend, lams_all, cmat_all)
        outs["hp"].append(hlast.reshape(B, N_SSM_BLOCKS * STATE_W))

        z = _s_inproj(l, xs, gn_all, wcat_all, TN_SAMPLE)
        xp, dec = _outproj_decode(
            l, xp, fox_o, y, lpow_all, cp, qm, mk_all, mv_all, sg, wglu_bf, bglu_all, wout_bf, gfin,
            page_table, z, bf_all, ckt, cvt, lft, mkt, mvt, 0, tm, final)
        fox_s, mem_s, lf_s = (d.reshape(R, -1) for d in dec)
        xs, h_new = _s_out(l, xs, z, fox_s, mem_s, h0_all, bmat_all, cmat_all, lamb_all, dsk_all,
                           w_glu, bglu_all, w_out, gfin, final)
        outs["z"].append(z)
        outs["lfs"].append(lf_s)
        outs["hs"].append(h_new)

    st = lambda n: jnp.stack(outs[n])
    z_all = st("z")
    n_mem = mk_all.shape[2]
    return (xp, xs.reshape(R, 1, D),
            kt_all.reshape(depth, B, N_FOX_HEADS, HEAD_DIM, L).transpose(0, 1, 4, 2, 3),
            vt_all.reshape(depth, B, N_FOX_HEADS, HEAD_DIM, L).transpose(0, 1, 4, 2, 3),
            lfp_all.reshape(depth, N_FOX_HEADS, B, L).transpose(0, 2, 3, 1),
            _cols_to_state(st("hp")),
            mk_all.reshape(depth, B, n_mem, N_MEM_HEADS, HEAD_DIM),
            mv_all.reshape(depth, B, n_mem, N_MEM_HEADS, HEAD_DIM),
            z_all[:, :, OFF_K:OFF_K + W_FOX].reshape(depth, R, 1, N_FOX_HEADS, HEAD_DIM),
            z_all[:, :, OFF_V:OFF_V + W_FOX].reshape(depth, R, 1, N_FOX_HEADS, HEAD_DIM),
            st("lfs")[:, :, :N_FOX_HEADS].reshape(depth, R, 1, N_FOX_HEADS),
            _cols_to_state(st("hs")))
```

```python
import functools

import jax
import jax.numpy as jnp
from jax import lax
from jax.experimental import pallas as pl
from jax.experimental.pallas import tpu as pltpu

f32 = jnp.float32
bf16 = jnp.bfloat16

HEAD_DIM = 64
N_FOX_HEADS = 6
W_FOX = N_FOX_HEADS * HEAD_DIM
SSM_GROUP = 16
N_SSM_GROUPS = 24
W_SSM = N_SSM_GROUPS * SSM_GROUP
SSM_STATE = 64
N_MEM_HEADS = 4
W_MEM = N_MEM_HEADS * HEAD_DIM
W_MIX = W_FOX + W_SSM + W_MEM
RMS_EPS = 1e-6
QK_SCALE = HEAD_DIM ** -0.5

LANES = 128
SUBLANES = 8
GROUPS_PER_BLOCK = LANES // SSM_GROUP
N_SSM_BLOCKS = N_SSM_GROUPS // GROUPS_PER_BLOCK
HALF = GROUPS_PER_BLOCK * SSM_STATE
STATE_W = 2 * HALF
N_SEG = SUBLANES
PIECE_PITCH = SUBLANES

OFF_Q, OFF_K, OFF_V, OFF_U, OFF_QM, OFF_G = 0, 384, 768, 1152, 1536, 1792
W_MAIN = OFF_G + W_MIX
OFF_F = W_MAIN
W_CAT = 3072
IN_CUTS = (0, 384, 768, 1152, 1158, 1542, 1926, 2310, 2566, 2822)

VMEM_LIMIT = 56 * 1024 * 1024

TQ_FOX, TK_FOX = 1024, 512
TM_OUT = 256
TJ_SCAN = 128
TN_SAMPLE = 256

SKIP_GAP = 110.0
NORM_SLACK = 1.02
KST_KN, KST_QN, KST_CEND, KST_ROWS = 0, SUBLANES, 2 * SUBLANES, 3 * SUBLANES


def _cparams(sem):
    return pltpu.CompilerParams(dimension_semantics=sem, vmem_limit_bytes=VMEM_LIMIT)


def _dot(a, b):
    return jnp.dot(a, b, preferred_element_type=f32)


def _dot_nt(a, b):
    return lax.dot_general(a, b, (((1,), (1,)), ((), ())), preferred_element_type=f32)


def _split2(a):
    hi = a.astype(bf16)
    lo = (a - hi.astype(f32)).astype(bf16)
    return hi, lo


def _split3(a):
    hi = a.astype(bf16)
    r = a - hi.astype(f32)
    mid = r.astype(bf16)
    lo = (r - mid.astype(f32)).astype(bf16)
    return hi, mid, lo


def _dot3(a, b):
    ah, al = _split2(a)
    bh, bl = _split2(b)
    return _dot(ah, bh) + _dot(al, bh) + _dot(ah, bl)


def _rms(x, g):
    ms = jnp.mean(x * x, axis=-1, keepdims=True)
    return x * lax.rsqrt(ms + RMS_EPS) * g


def _log_sigmoid(x):
    return jnp.minimum(x, 0.0) - jnp.log1p(jnp.exp(-jnp.abs(x)))


def _silu(x):
    return x * jax.nn.sigmoid(x)


def _gelu(x):
    return 0.5 * x * (1.0 + lax.erf(x * (2.0 ** -0.5)))


def _tri(n):
    r = lax.broadcasted_iota(jnp.int32, (n, n), 0)
    c = lax.broadcasted_iota(jnp.int32, (n, n), 1)
    return jnp.where(c <= r, 1.0, 0.0).astype(bf16)


def _pack_pieces(x):
    packed = None
    for i, pc in enumerate(_split3(x)):
        pc = pc.astype(f32)
        if i:
            pc = pltpu.roll(pc, PIECE_PITCH * i, 1)
        packed = pc if packed is None else packed + pc
    return packed.astype(bf16)


def _unpack_pieces(y):
    return y + pltpu.roll(y, LANES - PIECE_PITCH, 1) + pltpu.roll(y, LANES - 2 * PIECE_PITCH, 1)


def _layer(l, shape):
    return pl.BlockSpec((None,) + tuple(shape), lambda *_: (l,) + (0,) * len(shape),
                        pipeline_mode=pl.Buffered(1))


def _stage_plan():
    cut = dict(zip(("q", "k", "v", "f", "ga", "u", "gs", "qm", "gm"), zip(IN_CUTS[:-1], IN_CUTS[1:])))
    src, valid = [], []
    for name in ("q", "k", "v", "u", "qm", "ga", "gs", "gm"):
        a, b = cut[name]
        assert (b - a) % LANES == 0
        src += list(range(a, b, LANES))
        valid += [LANES] * ((b - a) // LANES)
    assert len(src) * LANES == W_MAIN == OFF_F
    src.append(cut["f"][0])
    valid.append(N_FOX_HEADS)
    while len(src) * LANES < W_CAT:
        src.append(0)
        valid.append(0)
    return src, valid


def _stage_w_kernel(src_ref, valid_ref, w_ref, cat_ref, wm_ref, wf_ref):
    t = pl.program_id(0)
    for l in range(w_ref.shape[1]):
        w = w_ref[:, l, :]
        row = lax.broadcasted_iota(jnp.int32, w.shape, 0)
        wt = jnp.where(row < valid_ref[t], w, 0.0).T
        hi = wt.astype(bf16)
        cat_ref[l] = wt
        wm_ref[l] = hi

        @pl.when(t == OFF_F // LANES)
        def _():
            wf_ref[l, :, :LANES] = hi
            wf_ref[l, :, LANES:] = (wt - hi.astype(f32)).astype(bf16)


def _stage_w(w_in):
    depth, D, d_in = w_in.shape
    assert d_in == IN_CUTS[-1]
    src, valid = _stage_plan()
    whole = pl.Element
    out = pl.BlockSpec((depth, D, LANES), lambda t, src, valid: (0, 0, t))
    return pl.pallas_call(
        _stage_w_kernel,
        grid_spec=pltpu.PrefetchScalarGridSpec(
            num_scalar_prefetch=2,
            grid=(len(src),),
            in_specs=[pl.BlockSpec((pl.Element(LANES), whole(depth), whole(D)), lambda t, src, valid: (src[t], 0, 0))],
            out_specs=(out, out, pl.BlockSpec((depth, D, 2 * LANES), lambda t, src, valid: (0, 0, 0)))),
        out_shape=(jax.ShapeDtypeStruct((depth, D, W_CAT), f32), jax.ShapeDtypeStruct((depth, D, W_CAT), bf16),
                   jax.ShapeDtypeStruct((depth, D, 2 * LANES), bf16)),
        compiler_params=_cparams(("arbitrary",)),
        name="stage_weights",
    )(jnp.asarray(src, jnp.int32), jnp.asarray(valid, jnp.int32), jnp.transpose(w_in, (2, 0, 1)))


def _head_block(x, h):
    blk = x[:, LANES * (h // 2):LANES * (h // 2 + 1)]
    return blk if h % 2 == 0 else pltpu.roll(blk, HEAD_DIM, 1)


def _inproj_kernel(x_ref, g_ref, wm_ref, wf_ref, bf_ref, *rest):
    qa_ref, kt_ref, ka_ref, vt_ref, va_ref, kst_ref, lf_ref, sg_ref, u_ref, qm_ref, carry_ref = rest[-11:]
    i = pl.program_id(1)

    @pl.when(i == 0)
    def _():
        carry_ref[...] = jnp.zeros_like(carry_ref)

    xn = _rms(x_ref[...], g_ref[...])
    xh = xn.astype(bf16)
    tm = xn.shape[0]
    lane = lax.broadcasted_iota(jnp.int32, (tm, LANES), 1)
    low = lane < HEAD_DIM

    ff = _dot(xh, wf_ref[...])
    f = ff[:, :LANES] + ff[:, LANES:] + bf_ref[...]
    lf = jnp.where(lane < N_FOX_HEADS, _log_sigmoid(f), 0.0)
    lf_ref[...] = lf.T[:N_FOX_HEADS, :]
    cs = _unpack_pieces(_dot(_tri(tm), _pack_pieces(lf)))
    cs = jnp.where(lane < N_FOX_HEADS, cs, 0.0) + carry_ref[0:1, :]
    carry_ref[...] = jnp.broadcast_to(cs[tm - 1:, :], carry_ref.shape)
    prow = lax.broadcasted_iota(jnp.int32, (LANES, N_FOX_HEADS * LANES), 0)
    pcol = lax.broadcasted_iota(jnp.int32, (LANES, N_FOX_HEADS * LANES), 1)
    place = (pcol == (prow % PIECE_PITCH) * LANES + HEAD_DIM + prow // PIECE_PITCH) & (prow % PIECE_PITCH < N_FOX_HEADS)
    caug = _dot(_pack_pieces(-cs), jnp.where(place, 1.0, 0.0).astype(bf16))
    ones3 = jnp.where((lane >= HEAD_DIM) & (lane < HEAD_DIM + 3), 1.0, 0.0)

    qk = _dot(xh, wm_ref[:, OFF_Q:OFF_V])
    q = qk[:, :W_FOX] * QK_SCALE
    k = qk[:, W_FOX:]
    kt_ref[...] = k.T
    kb = k.astype(bf16).astype(f32)
    qb = q.astype(bf16).astype(f32)
    squares = jnp.concatenate([(kb * kb).astype(bf16), (qb * qb).astype(bf16)], axis=1)
    srow = lax.broadcasted_iota(jnp.int32, (2 * W_FOX, LANES), 0)
    scol = lax.broadcasted_iota(jnp.int32, (2 * W_FOX, LANES), 1)
    target = (srow // W_FOX) * SUBLANES + (srow % W_FOX) // HEAD_DIM
    sums = _dot(squares, jnp.where(scol == target, 1.0, 0.0).astype(bf16))
    norms = jnp.sqrt(jnp.max(sums, axis=0, keepdims=True)) * NORM_SLACK
    kst_ref[:KST_CEND, :] = _col_bcast(norms)[:KST_CEND, :]
    kst_ref[KST_CEND:, :] = _col_bcast(cs[tm - 1:, :])[:SUBLANES, :]
    vu = _dot(xh, wm_ref[:, OFF_V:OFF_QM])
    v = vu[:, :W_FOX]
    vt_ref[...] = v.T
    u_ref[...] = vu[:, W_FOX:]
    for h in range(N_FOX_HEADS):
        qa_ref[h] = jnp.where(low, _head_block(q, h), ones3).astype(bf16)
        ka_ref[h] = jnp.where(low, _head_block(k, h), caug[:, LANES * h:LANES * (h + 1)]).astype(bf16)
        va_ref[h] = jnp.where(low, _head_block(v, h), 1.0).astype(bf16)
    rest = _dot(xh, wm_ref[:, OFF_QM:W_MAIN])
    qm_ref[...] = (rest[:, :W_MEM] * QK_SCALE).astype(bf16)
    sg_ref[...] = _silu(rest[:, W_MEM:]).astype(bf16)


def _inproj(l, x, g_all, wm_all, wf_all, bf_all, kv_prev, tm):
    B, L, D = x.shape
    depth = wm_all.shape[0]
    row = lambda w: pl.BlockSpec((None, tm, w), lambda b, i: (b, i, 0))
    heads = pl.BlockSpec((None, N_FOX_HEADS, tm, LANES), lambda b, i: (b, 0, i, 0))
    slab = pl.BlockSpec((None, None, W_FOX, tm), lambda b, i: (l, b, 0, i))
    aug = jax.ShapeDtypeStruct((B, N_FOX_HEADS, L, LANES), bf16)
    kvt = jax.ShapeDtypeStruct((depth, B, W_FOX, L), f32)
    out_shape = (
        aug,
        kvt,
        aug,
        kvt,
        aug,
        jax.ShapeDtypeStruct((B, L // tm, KST_ROWS, LANES), f32),
        jax.ShapeDtypeStruct((depth, N_FOX_HEADS, B * L), f32),
        jax.ShapeDtypeStruct((B, L, W_MIX), bf16),
        jax.ShapeDtypeStruct((B, L, W_SSM), f32),
        jax.ShapeDtypeStruct((B, L, W_MEM), bf16),
    )
    kst = pl.BlockSpec((None, None, KST_ROWS, LANES), lambda b, i: (b, i, 0, 0))
    lft = pl.BlockSpec((None, N_FOX_HEADS, tm), lambda b, i: (l, 0, b * (L // tm) + i))
    out_specs = (heads, slab, heads, slab, heads, kst, lft, row(W_MIX), row(W_SSM), row(W_MEM))
    in_specs = [row(D), _layer(l, (1, D)), _layer(l, wm_all.shape[1:]), _layer(l, wf_all.shape[1:]),
                _layer(l, (1, LANES))]
    n_in = len(in_specs)
    return pl.pallas_call(
        _inproj_kernel,
        grid=(B, L // tm),
        in_specs=in_specs + [pl.BlockSpec(memory_space=pl.ANY)] * len(kv_prev),
        out_specs=out_specs,
        out_shape=out_shape,
        input_output_aliases={n_in + n: out for n, out in zip(range(len(kv_prev)), (1, 3, 6))},
        scratch_shapes=[pltpu.VMEM((SUBLANES, LANES), f32)],
        compiler_params=_cparams(("parallel", "arbitrary")),
        name="prompt_inproj",
    )(x, g_all, wm_all, wf_all, bf_all, *kv_prev)


def _memkv_kernel(m_ref, g_ref, w_ref, k_ref, v_ref):
    mn = _rms(m_ref[...], g_ref[...]).astype(bf16)
    kv = _dot(mn, w_ref[...])
    k_ref[...] = kv[:, :W_MEM]
    v_ref[...] = kv[:, W_MEM:]


def _memkv(mem, g_all, w_all):
    B, N, D = mem.shape
    depth = w_all.shape[0]
    out = pl.BlockSpec((None, None, N, W_MEM), lambda l, b: (l, b, 0, 0))
    return pl.pallas_call(
        _memkv_kernel,
        grid=(depth, B),
        in_specs=[pl.BlockSpec((None, N, D), lambda l, b: (b, 0, 0)),
                  pl.BlockSpec((None, 1, D), lambda l, b: (l, 0, 0)),
                  pl.BlockSpec((None,) + w_all.shape[1:], lambda l, b: (l, 0, 0))],
        out_specs=(out, out),
        out_shape=(jax.ShapeDtypeStruct((depth, B, N, W_MEM), f32),) * 2,
        compiler_params=_cparams(("parallel", "parallel")),
        name="prompt_memkv",
    )(mem, g_all, w_all)


def _first_needed_block(kst_ref, h, i, n_diag):
    n_full = i * n_diag
    mine = pl.ds(n_full, n_diag)
    qn = jnp.max(kst_ref[mine, KST_QN + h, :], axis=0, keepdims=True)
    kn_own = jnp.max(kst_ref[mine, KST_KN + h, :], axis=0, keepdims=True)
    lb = -qn * kn_own - kst_ref[pl.ds(jnp.maximum(n_full - 1, 0), 1), KST_CEND + h, :]
    ub = qn * kst_ref[:, KST_KN + h, :] - kst_ref[:, KST_CEND + h, :]
    kj = lax.broadcasted_iota(jnp.int32, ub.shape, 0)
    needed = (ub - lb >= -SKIP_GAP) & (kj < n_full)
    return _full_min(jnp.where(needed, kj, n_full).astype(f32))


def _fox_kernel(q_ref, k_ref, v_ref, kst_ref, o_ref, acc_ref, m_ref, *, tq, tk):
    j = pl.program_id(1)
    i = pl.program_id(2)
    acc_ref[...] = jnp.zeros_like(acc_ref)
    m_ref[...] = jnp.full_like(m_ref, -jnp.inf)

    def step(kj, r0):
        k0 = pl.multiple_of(kj * tk, tk)
        rows = slice(0, tq) if r0 is None else slice(r0, tq)
        n = rows.stop - rows.start
        for e in range(2):
            s = _dot_nt(q_ref[e, rows, :], k_ref[e, pl.ds(k0, tk), :])
            if r0 is not None:
                row = r0 + lax.broadcasted_iota(jnp.int32, (n, tk), 0)
                col = r0 + lax.broadcasted_iota(jnp.int32, (n, tk), 1)
                s = jnp.where(col <= row, s, -jnp.inf)
            m_old = m_ref[e, rows, :]
            m_new = jnp.maximum(m_old, jnp.max(s, axis=1, keepdims=True))
            p = jnp.exp((s - jnp.tile(m_new, (1, tk // LANES))).astype(bf16))
            acc_ref[e, rows, :] = (jnp.exp(m_old - m_new) * acc_ref[e, rows, :]
                                   + _dot(p, v_ref[e, pl.ds(k0, tk), :]))
            m_ref[e, rows, :] = m_new

    n_diag = tq // tk
    first = jnp.minimum(*[_first_needed_block(kst_ref, 2 * j + e, i, n_diag) for e in range(2)])
    kj_first = jnp.min(first).astype(jnp.int32)
    t_first = (kj_first + n_diag - 1) // n_diag
    for d in range(1, n_diag):
        @pl.when(t_first * n_diag - d >= kj_first)
        def _():
            step(t_first * n_diag - d, None)

    def body(t, carry):
        for d in range(n_diag):
            step(t * n_diag + d, None)
        return carry

    lax.fori_loop(t_first, i, body, 0)
    for d in range(n_diag):
        step(i * n_diag + d, d * tk)
    lane = lax.broadcasted_iota(jnp.int32, (tq, LANES), 1)
    a0 = acc_ref[0]
    a1 = acc_ref[1]
    o = jnp.where(lane < HEAD_DIM, a0 / pltpu.roll(a0, HEAD_DIM, 1), pltpu.roll(a1, HEAD_DIM, 1) / a1)
    o_ref[...] = o.astype(o_ref.dtype)


def _fox(qa, ka, va, kst, tq, tk):
    B, _, L, _ = qa.shape
    n_pairs = W_FOX // LANES
    assert kst.shape[1] == L // tk
    return pl.pallas_call(
        functools.partial(_fox_kernel, tq=tq, tk=tk),
        grid=(B, n_pairs, L // tq),
        in_specs=[pl.BlockSpec((None, 2, tq, LANES), lambda b, j, i: (b, j, i, 0)),
                  pl.BlockSpec((None, 2, L, LANES), lambda b, j, i: (b, j, 0, 0)),
                  pl.BlockSpec((None, 2, L, LANES), lambda b, j, i: (b, j, 0, 0)),
                  pl.BlockSpec((None,) + kst.shape[1:], lambda b, j, i: (b, 0, 0, 0))],
        out_specs=pl.BlockSpec((None, tq, LANES), lambda b, j, i: (b, i, j)),
        out_shape=jax.ShapeDtypeStruct((B, L, W_FOX), bf16),
        scratch_shapes=[pltpu.VMEM((2, tq, LANES), f32), pltpu.VMEM((2, tq, LANES), f32)],
        compiler_params=_cparams(("parallel", "parallel", "arbitrary")),
        name="prompt_fox",
    )(qa, ka, va, kst)


def _s5prep_kernel(lr_ref, li_ref, ldt_ref, bre_ref, bim_ref, cre_ref, cim_ref,
                   lamb_ref, bmat_ref, cmat_ref, lpow_ref, lams_ref, pw_ref, *, seg_len):
    lr = lr_ref[...]
    li = li_ref[...]
    dt = jnp.exp(ldt_ref[...])
    ar = lr * dt
    ai = li * dt
    mag = jnp.exp(ar)
    lbr = mag * jnp.cos(ai)
    lbi = mag * jnp.sin(ai)
    lamb_ref[:, :HALF] = jnp.broadcast_to(lbr, (SUBLANES, HALF))
    lamb_ref[:, HALF:] = jnp.broadcast_to(lbi, (SUBLANES, HALF))
    den = lr * lr + li * li
    zr = ((lbr - 1.0) * lr + lbi * li) / den
    zi = (lbi * lr - (lbr - 1.0) * li) / den
    bre = bre_ref[...]
    bim = bim_ref[...]
    bmat_ref[:, :HALF] = zr * bre - zi * bim
    bmat_ref[:, HALF:] = zr * bim + zi * bre
    cmat_ref[:HALF, :] = cre_ref[...]
    cmat_ref[HALF:, :] = -cim_ref[...]
    jj = (lax.broadcasted_iota(jnp.int32, (SUBLANES, HALF), 0) + 1).astype(f32)
    magp = jnp.exp(ar * jj)
    pw_ref[:SUBLANES, :HALF] = magp * jnp.cos(ai * jj)
    pw_ref[:SUBLANES, HALF:] = magp * jnp.sin(ai * jj)
    n = SUBLANES
    while n < seg_len:
        sr = pw_ref[n - 1:n, :HALF]
        si = pw_ref[n - 1:n, HALF:]
        pr = pw_ref[:n, :HALF]
        pi = pw_ref[:n, HALF:]
        pw_ref[n:2 * n, :HALF] = pr * sr - pi * si
        pw_ref[n:2 * n, HALF:] = pr * si + pi * sr
        n *= 2
    lpow_ref[...] = pw_ref[...].astype(bf16)
    lams_ref[...] = pw_ref[seg_len - SUBLANES:, :]


def _s5prep(lam_re, lam_im, log_dt, b_re, b_im, c_re, c_im, seg_len):
    depth = lam_re.shape[0]
    assert seg_len >= SUBLANES and seg_len & (seg_len - 1) == 0
    nb, gb = N_SSM_BLOCKS, GROUPS_PER_BLOCK
    eye = jnp.eye(gb, dtype=f32)
    half = lambda a: a.reshape(depth, nb, 1, HALF)
    ldt = jnp.broadcast_to(log_dt[:, :, None], (depth, N_SSM_GROUPS, SSM_STATE))

    def embed_b(b):
        t = b.reshape(depth, nb, gb, SSM_STATE, SSM_GROUP).transpose(0, 1, 2, 4, 3)
        return (t[:, :, :, :, None, :] * eye[None, None, :, None, :, None]).reshape(depth, nb, LANES, HALF)

    def embed_c(c):
        t = c.reshape(depth, nb, gb, SSM_GROUP, SSM_STATE).transpose(0, 1, 2, 4, 3)
        return (t[:, :, :, :, None, :] * eye[None, None, :, None, :, None]).reshape(depth, nb, HALF, LANES)

    blk = lambda shape: pl.BlockSpec((None, None) + shape, lambda l, cb: (l, cb) + (0,) * len(shape))
    arr = lambda shape, dt: jax.ShapeDtypeStruct((depth, nb) + shape, dt)
    return pl.pallas_call(
        functools.partial(_s5prep_kernel, seg_len=seg_len),
        grid=(depth, nb),
        in_specs=[blk((1, HALF))] * 3 + [blk((LANES, HALF))] * 2 + [blk((HALF, LANES))] * 2,
        out_specs=(blk((SUBLANES, STATE_W)), blk((LANES, STATE_W)), blk((STATE_W, LANES)),
                   blk((seg_len, STATE_W)), blk((SUBLANES, STATE_W))),
        out_shape=(arr((SUBLANES, STATE_W), f32),
                   arr((LANES, STATE_W), f32),
                   arr((STATE_W, LANES), f32),
                   arr((seg_len, STATE_W), bf16),
                   arr((SUBLANES, STATE_W), f32)),
        scratch_shapes=[pltpu.VMEM((seg_len, STATE_W), f32)],
        compiler_params=_cparams(("parallel", "parallel")),
        name="s5_tables",
    )(half(lam_re), half(lam_im), half(ldt), embed_b(b_re), embed_b(b_im), embed_c(c_re), embed_c(c_im))


def _s5scan_kernel(u_ref, bmat_ref, cmat_ref, lamb_ref, dsk_ref, y_ref, hend_ref,
                   up_ref, hb_ref, hq_ref, hcar_ref, *, tj):
    j = pl.program_id(1)
    nb = N_SSM_BLOCKS

    @pl.when(j == 0)
    def _():
        hcar_ref[...] = jnp.zeros_like(hcar_ref)

    for cb in range(nb):
        for s in range(N_SEG):
            up_ref[cb, pl.ds(s, tj, stride=N_SEG), :] = u_ref[s, :, LANES * cb:LANES * (cb + 1)]
        hb_ref[cb] = _dot(up_ref[cb].astype(bf16), bmat_ref[cb])

    def rec(t2, h):
        r0 = pl.multiple_of(t2 * 2 * N_SEG, 2 * N_SEG)
        new = []
        for cb in range(nb):
            hr, hi = h[2 * cb], h[2 * cb + 1]
            lr = lamb_ref[cb, :, :HALF]
            li = lamb_ref[cb, :, HALF:]
            rs, is_ = [], []
            for d in range(2):
                bu = hb_ref[cb, pl.ds(r0 + d * N_SEG, N_SEG), :]
                hr, hi = lr * hr - li * hi + bu[:, :HALF], lr * hi + li * hr + bu[:, HALF:]
                rs.append(hr)
                is_.append(hi)
            hq_ref[cb, pl.ds(r0, 2 * N_SEG), :HALF] = jnp.concatenate(rs, axis=0).astype(bf16)
            hq_ref[cb, pl.ds(r0, 2 * N_SEG), HALF:] = jnp.concatenate(is_, axis=0).astype(bf16)
            new += [hr, hi]
        return tuple(new)

    h0 = tuple(hcar_ref[cb, :, HALF * part:HALF * (part + 1)] for cb in range(nb) for part in range(2))
    h = lax.fori_loop(0, tj // 2, rec, h0, unroll=True)
    for cb in range(nb):
        hcar_ref[cb, :, :HALF] = h[2 * cb]
        hcar_ref[cb, :, HALF:] = h[2 * cb + 1]
        up_ref[cb] = _dot(hq_ref[cb], cmat_ref[cb]) + dsk_ref[cb] * up_ref[cb]
        for s in range(N_SEG):
            y_ref[s, :, LANES * cb:LANES * (cb + 1)] = up_ref[cb, pl.ds(s, tj, stride=N_SEG), :].astype(y_ref.dtype)

    @pl.when(j == pl.num_programs(1) - 1)
    def _():
        hend_ref[...] = hcar_ref[...]


def _s5scan(l, u, bmat_all, cmat_all, lamb_all, dsk_all, tj):
    B, L, _ = u.shape
    seg_len = L // N_SEG
    u4 = u.reshape(B, N_SEG, seg_len, W_SSM)
    nb = N_SSM_BLOCKS
    tile = pl.BlockSpec((None, N_SEG, tj, W_SSM), lambda b, j: (b, 0, j, 0))
    y4, hend = pl.pallas_call(
        functools.partial(_s5scan_kernel, tj=tj),
        grid=(B, seg_len // tj),
        in_specs=[tile, _layer(l, bmat_all.shape[1:]), _layer(l, cmat_all.shape[1:]),
                  _layer(l, lamb_all.shape[1:]), _layer(l, dsk_all.shape[1:])],
        out_specs=(tile, pl.BlockSpec((None, nb, N_SEG, STATE_W), lambda b, j: (b, 0, 0, 0))),
        out_shape=(jax.ShapeDtypeStruct((B, N_SEG, seg_len, W_SSM), bf16),
                   jax.ShapeDtypeStruct((B, nb, N_SEG, STATE_W), f32)),
        scratch_shapes=[pltpu.VMEM((nb, N_SEG * tj, LANES), f32), pltpu.VMEM((nb, N_SEG * tj, STATE_W), f32),
                        pltpu.VMEM((nb, N_SEG * tj, STATE_W), bf16), pltpu.VMEM((nb, N_SEG, STATE_W), f32)],
        compiler_params=_cparams(("parallel", "arbitrary")),
        name="prompt_s5_scan",
    )(u4, bmat_all, cmat_all, lamb_all, dsk_all)
    return y4.reshape(B, L, W_SSM), hend


def _s5fix_kernel(hend_ref, lams_ref, cmat_ref, hlast_ref, cp_ref, hs_ref):
    he = hend_ref[...]
    lsr = lams_ref[SUBLANES - 1:, :HALF]
    lsi = lams_ref[SUBLANES - 1:, HALF:]
    hs_ref[...] = jnp.zeros_like(hs_ref)
    hr = jnp.zeros((1, HALF), f32)
    hi = jnp.zeros((1, HALF), f32)
    for s in range(N_SEG):
        hs_ref[s:s + 1, :HALF] = hr
        hs_ref[s:s + 1, HALF:] = hi
        nr = lsr * hr - lsi * hi + he[s:s + 1, :HALF]
        ni = lsr * hi + lsi * hr + he[s:s + 1, HALF:]
        hr, hi = nr, ni
    hlast_ref[:, :HALF] = hr
    hlast_ref[:, HALF:] = hi
    hst_r = hs_ref[:, :HALF].T
    hst_i = hs_ref[:, HALF:].T
    cre = cmat_ref[:HALF, :]
    mci = cmat_ref[HALF:, :]
    for s in range(N_SEG):
        col_r = hst_r[:, s:s + 1]
        col_i = hst_i[:, s:s + 1]
        cp_ref[s, :HALF, :] = (cre * col_r + mci * col_i).astype(bf16)
        cp_ref[s, HALF:, :] = (mci * col_r - cre * col_i).astype(bf16)


def _s5fix(l, hend, lams_all, cmat_all):
    B, nb = hend.shape[:2]
    return pl.pallas_call(
        _s5fix_kernel,
        grid=(B, nb),
        in_specs=[pl.BlockSpec((None, None, N_SEG, STATE_W), lambda b, cb: (b, cb, 0, 0)),
                  pl.BlockSpec((None, None, SUBLANES, STATE_W), lambda b, cb: (l, cb, 0, 0)),
                  pl.BlockSpec((None, None, STATE_W, LANES), lambda b, cb: (l, cb, 0, 0))],
        out_specs=(pl.BlockSpec((None, None, 1, STATE_W), lambda b, cb: (b, cb, 0, 0)),
                   pl.BlockSpec((None, N_SEG, None, STATE_W, LANES), lambda b, cb: (b, 0, cb, 0, 0))),
        out_shape=(jax.ShapeDtypeStruct((B, nb, 1, STATE_W), f32),
                   jax.ShapeDtypeStruct((B, N_SEG, nb, STATE_W, LANES), bf16)),
        scratch_shapes=[pltpu.VMEM((LANES, STATE_W), f32)],
        compiler_params=_cparams(("parallel", "parallel")),
        name="prompt_s5_fix",
    )(hend, lams_all, cmat_all)


def _pair_attend(q, k, v):
    lane = lax.broadcasted_iota(jnp.int32, q.shape, 1)
    zero = jnp.zeros_like(q)
    outs = []
    for e in range(2):
        qe = jnp.where((lane < HEAD_DIM) == (e == 0), q, zero)
        s = _dot_nt(qe, k)
        p = jnp.exp(s - jnp.max(s, axis=1, keepdims=True))
        outs.append(_dot(p.astype(bf16), v) / jnp.sum(p, axis=1, keepdims=True))
    return jnp.where(lane < HEAD_DIM, outs[0], outs[1])


N_OUTPROJ_INPUTS = 13


def _outproj_stages(x_ref, fox_ref, y_ref, lp_ref, cp_ref, qm_ref, mk_ref, mv_ref, sg_ref,
                    wglu_ref, bglu_ref, wout_ref, gfin_ref, o_ref, *, final, tiles_per_seg):
    tm = x_ref.shape[0]
    j0 = pl.multiple_of((pl.program_id(1) % tiles_per_seg) * tm, tm)
    corr = [_dot(lp_ref[cb, pl.ds(j0, tm), :], cp_ref[cb]) for cb in range(N_SSM_BLOCKS)]
    ssm_y = y_ref[...] + jnp.concatenate(corr, axis=1)
    yield
    s = _gelu(ssm_y)
    s = s * jax.nn.sigmoid(_dot(s.astype(bf16), wglu_ref[...]) + bglu_ref[...])
    yield
    qm = qm_ref[...]
    mk = mk_ref[...].astype(bf16)
    mv = mv_ref[...].astype(bf16)
    mem = []
    for jp in range(W_MEM // LANES):
        mem.append(_pair_attend(qm[:, LANES * jp:LANES * (jp + 1)], mk[:, LANES * jp:LANES * (jp + 1)],
                                mv[:, LANES * jp:LANES * (jp + 1)]))
        yield
    sg = sg_ref[...].astype(f32)
    mix = jnp.concatenate([fox_ref[...] * sg[:, :W_FOX],
                           s * sg[:, W_FOX:W_FOX + W_SSM],
                           jnp.concatenate(mem, axis=1) * sg[:, W_FOX + W_SSM:]], axis=1)
    xo = x_ref[...] + _dot(mix.astype(bf16), wout_ref[...])
    yield
    if final:
        xo = _rms(xo, gfin_ref[...])
    o_ref[...] = xo


def _s_inproj_kernel(x_ref, g_ref, w_ref, z_ref):
    xn = _rms(x_ref[...], g_ref[...])
    z_ref[...] = _dot3(xn, w_ref[...])


def _s_inproj(l, xs, g_all, wcat_all, tn):
    R, D = xs.shape
    return pl.pallas_call(
        _s_inproj_kernel,
        grid=(W_CAT // tn,),
        in_specs=[pl.BlockSpec((R, D), lambda n: (0, 0)), _layer(l, (1, D)),
                  pl.BlockSpec((None, D, tn), lambda n: (l, 0, n))],
        out_specs=pl.BlockSpec((R, tn), lambda n: (0, n)),
        out_shape=jax.ShapeDtypeStruct((R, W_CAT), f32),
        compiler_params=_cparams(("parallel",)),
        name="sample_inproj",
    )(xs, g_all, wcat_all)


def _full_max(x):
    return jnp.max(jnp.max(x, axis=0, keepdims=True), axis=1, keepdims=True)


def _full_min(x):
    return jnp.min(jnp.min(x, axis=0, keepdims=True), axis=1, keepdims=True)


def _full_sum(x):
    return jnp.sum(jnp.sum(x, axis=0, keepdims=True), axis=1, keepdims=True)


def _lane_sums(acc):
    ones = jnp.ones((SUBLANES, LANES), bf16)
    hi, lo = _split2(acc)
    return (_dot_nt(ones, hi) + _dot_nt(ones, lo))[0:1, :]


def _col_bcast(row):
    return jnp.broadcast_to(row, (LANES, row.shape[1])).T


class _DecodeGuest:
    N_FIXED_IN, N_OUT, N_SCRATCH = 3, 3, 5

    def __init__(self, l, n_seq, first_seq, n_steps, step_of, page_table, z, bf_all, ckt, cvt, lft, mkt, mvt):
        R, n_pages = page_table.shape
        n_pool, page = ckt.shape[1], ckt.shape[4]
        n_mem = mkt.shape[4]
        assert page == LANES and n_mem % LANES == 0 and first_seq % n_seq == 0
        assert first_seq + n_steps * n_seq <= R
        self.n_seq, self.n_pages, self.first_seq, self.step_of = n_seq, n_pages, first_seq, step_of
        self.per_seq = 2 * n_pages + 2
        z_blk = lambda *g: (first_seq // n_seq + step_of(*g[:-1]), 0, 0)
        self.in_specs = [pl.BlockSpec((n_seq, 1, W_CAT), z_blk), _layer(l, (1, LANES)),
                         _layer(l, (N_FOX_HEADS, n_pool, page))]
        self.operands = [z.reshape(R, 1, W_CAT), bf_all, lft]
        for s in range(n_seq):
            seq = lambda g, s=s: first_seq + step_of(*g) * n_seq + s
            paged = [pl.BlockSpec((None, None, N_FOX_HEADS, HEAD_DIM, page),
                                  lambda *g, seq=seq, p=p: (l, g[-1][seq(g[:-1]), p], 0, 0, 0))
                     for p in range(n_pages)]
            mem_spec = pl.BlockSpec((None, None, N_MEM_HEADS, HEAD_DIM, n_mem),
                                    lambda *g, seq=seq: (l, seq(g[:-1]), 0, 0, 0))
            self.in_specs += paged + paged + [mem_spec, mem_spec]
            self.operands += [ckt] * n_pages + [cvt] * n_pages + [mkt, mvt]
        out = lambda w: pl.BlockSpec((n_seq, 1, w), lambda *g: (step_of(*g[:-1]), 0, 0))
        self.out_specs = [out(W_FOX), out(W_MEM), out(LANES)]
        n_out = n_steps * n_seq
        self.out_shape = [jax.ShapeDtypeStruct((n_out, 1, w), f32) for w in (W_FOX, W_MEM, LANES)]
        n_rows = N_FOX_HEADS * n_pages
        self.scratch_shapes = [pltpu.VMEM((n_seq, rows, LANES), f32)
                               for rows in (n_rows, n_rows, n_rows, W_FOX, W_MEM)]

    def chains(self, pt_ref, step, ins, outs, scratch):
        z_ref, bf_ref, lfc_ref = ins[:self.N_FIXED_IN]
        fox_ref, mem_ref, lf_ref = outs
        n_pages = self.n_pages
        result = []
        for i in range(self.n_seq):
            mine = ins[self.N_FIXED_IN + i * self.per_seq:self.N_FIXED_IN + (i + 1) * self.per_seq]
            result.append(_decode_one(
                pt_ref, self.first_seq + step * self.n_seq + i, z_ref.at[i], bf_ref, lfc_ref,
                mine[:n_pages], mine[n_pages:2 * n_pages], mine[2 * n_pages], mine[2 * n_pages + 1],
                fox_ref.at[i], mem_ref.at[i], lf_ref.at[i], *[s.at[i] for s in scratch]))
        return result

    @property
    def n_in(self):
        return self.N_FIXED_IN + self.n_seq * self.per_seq


def _run_chains(chains):
    while chains:
        chains = [c for c in chains if next(c, _DONE) is not _DONE]


def _outproj_decode_kernel(pt_ref, *refs, guest, final, tiles_per_seg):
    n_in = N_OUTPROJ_INPUTS + guest.n_in
    op_in, g_in = refs[:N_OUTPROJ_INPUTS], refs[N_OUTPROJ_INPUTS:n_in]
    o_ref = refs[n_in]
    g_out = refs[n_in + 1:n_in + 1 + guest.N_OUT]
    g_scr = refs[n_in + 1 + guest.N_OUT:]
    step = guest.step_of(pl.program_id(0), pl.program_id(1))
    _run_chains([_outproj_stages(*op_in, o_ref, final=final, tiles_per_seg=tiles_per_seg)]
                + guest.chains(pt_ref, step, g_in, g_out, g_scr))


_DONE = object()


def _decode_one(pt_ref, r, z_ref, bf_ref, lfc_ref, ck, cv, mk_ref, mv_ref, fox_ref, mem_ref, lf_ref,
                s_scr, lfs_scr, p_scr, acc_scr, accm_scr):
    n_pages = len(ck)
    z = z_ref[...]
    q = z[:, OFF_Q:OFF_Q + W_FOX] * QK_SCALE
    k_new = z[:, OFF_K:OFF_K + W_FOX]
    v_new = z[:, OFF_V:OFF_V + W_FOX]
    lf_new = _log_sigmoid(z[:, OFF_F:OFF_F + LANES] + bf_ref[...])
    lf_ref[...] = lf_new

    qt = _col_bcast(q)
    for h in range(N_FOX_HEADS):
        qh = qt[HEAD_DIM * h:HEAD_DIM * (h + 1), :]
        for p in range(n_pages):
            row = h * n_pages + p
            s_scr[row:row + 1, :] = jnp.sum(ck[p][h] * qh, axis=0, keepdims=True)
            lfs_scr[row:row + 1, :] = lfc_ref[h, pl.ds(pt_ref[r, p], 1), :]
        yield

    n_rows = N_FOX_HEADS * n_pages
    ur = lax.broadcasted_iota(jnp.int32, (LANES, LANES), 0)
    uc = lax.broadcasted_iota(jnp.int32, (LANES, LANES), 1)
    upper = jnp.where(ur <= uc, 1.0, 0.0).astype(bf16)
    cs = sum(_dot(pc, upper) for pc in _split3(lfs_scr[...]))
    tot = jnp.broadcast_to(cs[:, LANES - 1:], (n_rows, LANES))
    mr = lax.broadcasted_iota(jnp.int32, (n_rows, n_rows), 0)
    mc = lax.broadcasted_iota(jnp.int32, (n_rows, n_rows), 1)
    before = jnp.where((mc < mr) & (mc // n_pages == mr // n_pages), 1.0, 0.0).astype(bf16)
    c = cs + sum(_dot(before, pc) for pc in _split3(tot))
    s_all = s_scr[...] - c
    yield

    qk_new = q * k_new
    lane_head = lax.broadcasted_iota(jnp.int32, (1, W_FOX), 1) // HEAD_DIM
    pn_exp = jnp.zeros((1, W_FOX), f32)
    l_exp = jnp.zeros((1, W_FOX), f32)
    for h in range(N_FOX_HEADS):
        rows = slice(h * n_pages, (h + 1) * n_pages)
        s_h = s_all[rows, :]
        c_new = c[(h + 1) * n_pages - 1:(h + 1) * n_pages, LANES - 1:] + lf_new[:, h:h + 1]
        s_new = jnp.sum(qk_new[:, HEAD_DIM * h:HEAD_DIM * (h + 1)], axis=1, keepdims=True) - c_new
        m = jnp.maximum(_full_max(s_h), s_new)
        p_h = jnp.exp(s_h - m)
        pn = jnp.exp(s_new - m)
        p_scr[rows, :] = p_h
        pn_exp = jnp.where(lane_head == h, pn, pn_exp)
        l_exp = jnp.where(lane_head == h, _full_sum(p_h) + pn, l_exp)
        acc = jnp.zeros((HEAD_DIM, LANES), f32)
        for p in range(n_pages):
            acc = acc + cv[p][h] * p_scr[h * n_pages + p:h * n_pages + p + 1, :]
        acc_scr[HEAD_DIM * h:HEAD_DIM * (h + 1), :] = acc
        yield
    fox_ref[...] = (_lane_sums(acc_scr[...]) + pn_exp * v_new) / l_exp
    yield

    qm = z[:, OFF_QM:OFF_QM + W_MEM] * QK_SCALE
    qmt = _col_bcast(qm)
    n_halves = mk_ref.shape[2] // LANES
    mem_head = lax.broadcasted_iota(jnp.int32, (1, W_MEM), 1) // HEAD_DIM
    lm_exp = jnp.zeros((1, W_MEM), f32)
    for h in range(N_MEM_HEADS):
        qh = qmt[HEAD_DIM * h:HEAD_DIM * (h + 1), :]
        ss = [jnp.sum(mk_ref[h, :, LANES * t:LANES * (t + 1)] * qh, axis=0, keepdims=True) for t in range(n_halves)]
        m = functools.reduce(jnp.maximum, [jnp.max(s, axis=1, keepdims=True) for s in ss])
        ps = [jnp.exp(s - m) for s in ss]
        lm = sum(jnp.sum(p, axis=1, keepdims=True) for p in ps)
        lm_exp = jnp.where(mem_head == h, lm, lm_exp)
        accm_scr[HEAD_DIM * h:HEAD_DIM * (h + 1), :] = sum(
            mv_ref[h, :, LANES * t:LANES * (t + 1)] * ps[t] for t in range(n_halves))
    mem_ref[...] = _lane_sums(accm_scr[...]) / lm_exp


def _outproj_decode(l, x, fox_o, y, lpow_all, cp, qm, mk_all, mv_all, sg, wglu_all, bglu_all, wout_all, gfin,
                    page_table, z, bf_all, ckt, cvt, lft, mkt, mvt, first_seq, tm, final):
    B, L, D = x.shape
    seg_len = L // N_SEG
    tiles_per_seg = seg_len // tm
    n_i = L // tm
    n_steps = B * n_i
    n_left = page_table.shape[0] - first_seq
    assert n_left % n_steps == 0
    guest = _DecodeGuest(l, n_left // n_steps, first_seq, n_steps, lambda b, i: b * n_i + i,
                         page_table, z, bf_all, ckt, cvt, lft, mkt, mvt)
    nb = N_SSM_BLOCKS
    row = lambda w: pl.BlockSpec((None, tm, w), lambda b, i, pt: (b, i, 0))
    mem_kv = pl.BlockSpec((None, None) + mk_all.shape[2:], lambda b, i, pt: (l, b, 0, 0))
    in_specs = [row(D), row(W_FOX), row(W_SSM), _layer(l, lpow_all.shape[1:]),
                pl.BlockSpec((None, None, nb, STATE_W, LANES), lambda b, i, pt: (b, i // tiles_per_seg, 0, 0, 0)),
                row(W_MEM), mem_kv, mem_kv, row(D),
                _layer(l, wglu_all.shape[1:]), _layer(l, (1, W_SSM)), _layer(l, wout_all.shape[1:]),
                pl.BlockSpec((1, D), lambda b, i, pt: (0, 0))]
    assert len(in_specs) == N_OUTPROJ_INPUTS
    xo, *dec = pl.pallas_call(
        functools.partial(_outproj_decode_kernel, guest=guest, final=final, tiles_per_seg=tiles_per_seg),
        grid_spec=pltpu.PrefetchScalarGridSpec(
            num_scalar_prefetch=1,
            grid=(B, n_i),
            in_specs=in_specs + guest.in_specs,
            out_specs=[row(D)] + guest.out_specs,
            scratch_shapes=guest.scratch_shapes),
        out_shape=[jax.ShapeDtypeStruct((B, L, D), f32)] + guest.out_shape,
        compiler_params=_cparams(("parallel", "parallel")),
        name="outproj_decode",
    )(page_table, x, fox_o, y, lpow_all, cp, qm, mk_all, mv_all, sg, wglu_all, bglu_all, wout_all, gfin,
      *guest.operands)
    return xo, dec


def _s_out_kernel(x_ref, z_ref, fox_ref, mem_ref, h0_ref, bmat_ref, cmat_ref, lamb_ref, dsk_ref,
                  wglu_ref, bglu_ref, wout_ref, gfin_ref, o_ref, h_ref, *, final):
    z = z_ref[...]
    u = z[:, OFF_U:OFF_U + W_SSM]
    ys = []
    for cb in range(N_SSM_BLOCKS):
        ucb = u[:, LANES * cb:LANES * (cb + 1)]
        bu = _dot3(ucb, bmat_ref[cb])
        lr = lamb_ref[cb, 0:1, :HALF]
        li = lamb_ref[cb, 0:1, HALF:]
        h0r = h0_ref[:, STATE_W * cb:STATE_W * cb + HALF]
        h0i = h0_ref[:, STATE_W * cb + HALF:STATE_W * (cb + 1)]
        hr = lr * h0r - li * h0i + bu[:, :HALF]
        hi = lr * h0i + li * h0r + bu[:, HALF:]
        h_ref[:, STATE_W * cb:STATE_W * cb + HALF] = hr
        h_ref[:, STATE_W * cb + HALF:STATE_W * (cb + 1)] = hi
        ys.append(_dot3(jnp.concatenate([hr, hi], axis=1), cmat_ref[cb]) + dsk_ref[cb] * ucb)
    s = _gelu(jnp.concatenate(ys, axis=1))
    s = s * jax.nn.sigmoid(_dot3(s, wglu_ref[...]) + bglu_ref[...])
    sg = _silu(z[:, OFF_G:W_MAIN])
    mix = jnp.concatenate([fox_ref[...] * sg[:, :W_FOX],
                           s * sg[:, W_FOX:W_FOX + W_SSM],
                           mem_ref[...] * sg[:, W_FOX + W_SSM:]], axis=1)
    xo = x_ref[...] + _dot3(mix, wout_ref[...])
    if final:
        xo = _rms(xo, gfin_ref[...])
    o_ref[...] = xo


def _s_out(l, xs, z, fox, mem, h0_all, bmat_all, cmat_all, lamb_all, dsk_all, wglu_all, bglu_all, wout_all,
           gfin, final):
    R, D = xs.shape
    whole = lambda a: pl.BlockSpec(a.shape, lambda i: (0,) * a.ndim)
    hspec = pl.BlockSpec((R, h0_all.shape[2]), lambda i: (0, 0))
    return pl.pallas_call(
        functools.partial(_s_out_kernel, final=final),
        grid=(1,),
        in_specs=[whole(xs), whole(z), whole(fox), whole(mem), _layer(l, h0_all.shape[1:]),
                  _layer(l, bmat_all.shape[1:]), _layer(l, cmat_all.shape[1:]), _layer(l, lamb_all.shape[1:]),
                  _layer(l, dsk_all.shape[1:]), _layer(l, wglu_all.shape[1:]), _layer(l, (1, W_SSM)),
                  _layer(l, wout_all.shape[1:]), whole(gfin)],
        out_specs=(whole(xs), hspec),
        out_shape=(jax.ShapeDtypeStruct((R, D), f32), jax.ShapeDtypeStruct(h0_all.shape[1:], f32)),
        compiler_params=_cparams(("arbitrary",)),
        name="sample_out",
    )(xs, z, fox, mem, h0_all, bmat_all, cmat_all, lamb_all, dsk_all, wglu_all, bglu_all, wout_all, gfin)


def _state_to_cols(h):
    lead = h.shape[:-3]
    n = len(lead)
    t = h.reshape(lead + (N_SSM_BLOCKS, GROUPS_PER_BLOCK, SSM_STATE, 2))
    t = t.transpose(tuple(range(n)) + (n, n + 3, n + 1, n + 2))
    return t.reshape(lead + (N_SSM_BLOCKS * STATE_W,))


def _cols_to_state(c):
    lead = c.shape[:-1]
    n = len(lead)
    t = c.reshape(lead + (N_SSM_BLOCKS, 2, GROUPS_PER_BLOCK, SSM_STATE))
    t = t.transpose(tuple(range(n)) + (n, n + 2, n + 3, n + 1))
    return t.reshape(lead + (N_SSM_GROUPS, SSM_STATE, 2))


def _tile(n, pref):
    t = min(n, pref)
    assert n % t == 0
    return t


def kernel(x_prompt, x_sample, mem_prompt, cache_k, cache_v, cache_logf, state_ssm, cache_mem_k, cache_mem_v,
           page_table, g_norm, w_in, b_f, lam_re, lam_im, log_dt, b_re, b_im, c_re, c_im, d_skip, w_glu, b_glu,
           g_mem, w_mem_kv, w_out, g_final):
    depth = w_in.shape[0]
    B, L, D = x_prompt.shape
    R = x_sample.shape[0]
    assert x_sample.shape[1] == 1 and L % N_SEG == 0
    seg_len = L // N_SEG
    tm = _tile(seg_len, TM_OUT)
    tq = _tile(L, TQ_FOX)
    tk = _tile(tq, TK_FOX)
    tj = _tile(seg_len, TJ_SCAN)

    wcat_all, wm_all, wf_all = _stage_w(w_in)
    bf_all = jnp.pad(b_f, ((0, 0), (0, LANES - N_FOX_HEADS))).reshape(depth, 1, LANES)
    gn_all = g_norm.reshape(depth, 1, D)
    gmem_all = g_mem.reshape(depth, 1, D)
    bglu_all = b_glu.reshape(depth, 1, W_SSM)
    gfin = g_final.reshape(1, D)
    dsk_all = d_skip.reshape(depth, N_SSM_BLOCKS, 1, LANES)
    wglu_bf = w_glu.astype(bf16)
    wout_bf = w_out.astype(bf16)
    ckt = jnp.transpose(cache_k, (0, 1, 3, 4, 2))
    cvt = jnp.transpose(cache_v, (0, 1, 3, 4, 2))
    lft = jnp.transpose(cache_logf, (0, 3, 1, 2))
    mkt = jnp.transpose(cache_mem_k, (0, 1, 3, 4, 2))
    mvt = jnp.transpose(cache_mem_v, (0, 1, 3, 4, 2))
    h0_all = _state_to_cols(state_ssm)

    lamb_all, bmat_all, cmat_all, lpow_all, lams_all = _s5prep(lam_re, lam_im, log_dt, b_re, b_im, c_re, c_im, seg_len)
    bmat_bf = bmat_all.astype(bf16)
    cmat_bf = cmat_all.astype(bf16)
    mk_all, mv_all = _memkv(mem_prompt, gmem_all, w_mem_kv.astype(bf16))

    xp = x_prompt
    xs = x_sample.reshape(R, D)
    outs = {n: [] for n in ("hp", "z", "lfs", "hs")}
    kv_prev = ()
    for l in range(depth):
        final = l == depth - 1
        qa, kt_all, ka, vt_all, va, kst, lfp_all, sg, u, qm = _inproj(l, xp, gn_all, wm_all, wf_all, bf_all,
                                                                      kv_prev, tk)
        kv_prev = (kt_all, vt_all, lfp_all)
        fox_o = _fox(qa, ka, va, kst, tq, tk)
        y, hend = _s5scan(l, u, bmat_bf, cmat_bf, lamb_all, dsk_all, tj)
        hlast, cp = _s5fix(l, hend, lams_all, cmat_all)
        outs["hp"].append(hlast.reshape(B, N_SSM_BLOCKS * STATE_W))

        z = _s_inproj(l, xs, gn_all, wcat_all, TN_SAMPLE)
        xp, dec = _outproj_decode(
            l, xp, fox_o, y, lpow_all, cp, qm, mk_all, mv_all, sg, wglu_bf, bglu_all, wout_bf, gfin,
            page_table, z, bf_all, ckt, cvt, lft, mkt, mvt, 0, tm, final)
        fox_s, mem_s, lf_s = (d.reshape(R, -1) for d in dec)
        xs, h_new = _s_out(l, xs, z, fox_s, mem_s, h0_all, bmat_all, cmat_all, lamb_all, dsk_all,
                           w_glu, bglu_all, w_out, gfin, final)
        outs["z"].append(z)
        outs["lfs"].append(lf_s)
        outs["hs"].append(h_new)

    st = lambda n: jnp.stack(outs[n])
    z_all = st("z")
    n_mem = mk_all.shape[2]
    return (xp, xs.reshape(R, 1, D),
            kt_all.reshape(depth, B, N_FOX_HEADS, HEAD_DIM, L).transpose(0, 1, 4, 2, 3),
            vt_all.reshape(depth, B, N_FOX_HEADS, HEAD_DIM, L).transpose(0, 1, 4, 2, 3),
            lfp_all.reshape(depth, N_FOX_HEADS, B, L).transpose(0, 2, 3, 1),
            _cols_to_state(st("hp")),
            mk_all.reshape(depth, B, n_mem, N_MEM_HEADS, HEAD_DIM),
            mv_all.reshape(depth, B, n_mem, N_MEM_HEADS, HEAD_DIM),
            z_all[:, :, OFF_K:OFF_K + W_FOX].reshape(depth, R, 1, N_FOX_HEADS, HEAD_DIM),
            z_all[:, :, OFF_V:OFF_V + W_FOX].reshape(depth, R, 1, N_FOX_HEADS, HEAD_DIM),
            st("lfs")[:, :, :N_FOX_HEADS].reshape(depth, R, 1, N_FOX_HEADS),
            _cols_to_state(st("hs")))
```

```python
import functools

import jax
import jax.numpy as jnp
from jax import lax
from jax.experimental import pallas as pl
from jax.experimental.pallas import tpu as pltpu

f32 = jnp.float32
bf16 = jnp.bfloat16

HEAD_DIM = 64
N_FOX_HEADS = 6
W_FOX = N_FOX_HEADS * HEAD_DIM
SSM_GROUP = 16
N_SSM_GROUPS = 24
W_SSM = N_SSM_GROUPS * SSM_GROUP
SSM_STATE = 64
N_MEM_HEADS = 4
W_MEM = N_MEM_HEADS * HEAD_DIM
W_MIX = W_FOX + W_SSM + W_MEM
RMS_EPS = 1e-6
QK_SCALE = HEAD_DIM ** -0.5

LANES = 128
SUBLANES = 8
GROUPS_PER_BLOCK = LANES // SSM_GROUP
N_SSM_BLOCKS = N_SSM_GROUPS // GROUPS_PER_BLOCK
HALF = GROUPS_PER_BLOCK * SSM_STATE
STATE_W = 2 * HALF
N_SEG = SUBLANES
PIECE_PITCH = SUBLANES

OFF_Q, OFF_K, OFF_V, OFF_U, OFF_QM, OFF_G = 0, 384, 768, 1152, 1536, 1792
W_MAIN = OFF_G + W_MIX
OFF_F = W_MAIN
W_CAT = 3072
IN_CUTS = (0, 384, 768, 1152, 1158, 1542, 1926, 2310, 2566, 2822)

VMEM_LIMIT = 56 * 1024 * 1024

TQ_FOX, TK_FOX = 1024, 512
TM_OUT = 128
PAGE_SLOTS = 3
TJ_SCAN = 128
TN_SAMPLE = 256

SKIP_GAP = 110.0
NORM_SLACK = 1.02
KST_KN, KST_QN, KST_CEND, KST_ROWS = 0, SUBLANES, 2 * SUBLANES, 3 * SUBLANES


def _cparams(sem):
    return pltpu.CompilerParams(dimension_semantics=sem, vmem_limit_bytes=VMEM_LIMIT)


def _dot(a, b):
    return jnp.dot(a, b, preferred_element_type=f32)


def _dot_nt(a, b):
    return lax.dot_general(a, b, (((1,), (1,)), ((), ())), preferred_element_type=f32)


def _split2(a):
    hi = a.astype(bf16)
    lo = (a - hi.astype(f32)).astype(bf16)
    return hi, lo


def _split3(a):
    hi = a.astype(bf16)
    r = a - hi.astype(f32)
    mid = r.astype(bf16)
    lo = (r - mid.astype(f32)).astype(bf16)
    return hi, mid, lo


def _dot3(a, b):
    ah, al = _split2(a)
    bh, bl = _split2(b)
    return _dot(ah, bh) + _dot(al, bh) + _dot(ah, bl)


def _rms(x, g):
    ms = jnp.mean(x * x, axis=-1, keepdims=True)
    return x * lax.rsqrt(ms + RMS_EPS) * g


def _log_sigmoid(x):
    return jnp.minimum(x, 0.0) - jnp.log1p(jnp.exp(-jnp.abs(x)))


def _silu(x):
    return x * jax.nn.sigmoid(x)


def _gelu(x):
    return 0.5 * x * (1.0 + lax.erf(x * (2.0 ** -0.5)))


def _tri(n):
    r = lax.broadcasted_iota(jnp.int32, (n, n), 0)
    c = lax.broadcasted_iota(jnp.int32, (n, n), 1)
    return jnp.where(c <= r, 1.0, 0.0).astype(bf16)


def _pack_pieces(x):
    packed = None
    for i, pc in enumerate(_split3(x)):
        pc = pc.astype(f32)
        if i:
            pc = pltpu.roll(pc, PIECE_PITCH * i, 1)
        packed = pc if packed is None else packed + pc
    return packed.astype(bf16)


def _unpack_pieces(y):
    return y + pltpu.roll(y, LANES - PIECE_PITCH, 1) + pltpu.roll(y, LANES - 2 * PIECE_PITCH, 1)


def _layer(l, shape):
    return pl.BlockSpec((None,) + tuple(shape), lambda *_: (l,) + (0,) * len(shape),
                        pipeline_mode=pl.Buffered(1))


def _stage_plan():
    cut = dict(zip(("q", "k", "v", "f", "ga", "u", "gs", "qm", "gm"), zip(IN_CUTS[:-1], IN_CUTS[1:])))
    src, valid = [], []
    for name in ("q", "k", "v", "u", "qm", "ga", "gs", "gm"):
        a, b = cut[name]
        assert (b - a) % LANES == 0
        src += list(range(a, b, LANES))
        valid += [LANES] * ((b - a) // LANES)
    assert len(src) * LANES == W_MAIN == OFF_F
    src.append(cut["f"][0])
    valid.append(N_FOX_HEADS)
    while len(src) * LANES < W_CAT:
        src.append(0)
        valid.append(0)
    return src, valid


def _stage_w_kernel(src_ref, valid_ref, w_ref, cat_ref, wm_ref, wf_ref):
    t = pl.program_id(0)
    for l in range(w_ref.shape[1]):
        w = w_ref[:, l, :]
        row = lax.broadcasted_iota(jnp.int32, w.shape, 0)
        wt = jnp.where(row < valid_ref[t], w, 0.0).T
        hi = wt.astype(bf16)
        cat_ref[l] = wt
        wm_ref[l] = hi

        @pl.when(t == OFF_F // LANES)
        def _():
            wf_ref[l, :, :LANES] = hi
            wf_ref[l, :, LANES:] = (wt - hi.astype(f32)).astype(bf16)


def _stage_w(w_in):
    depth, D, d_in = w_in.shape
    assert d_in == IN_CUTS[-1]
    src, valid = _stage_plan()
    whole = pl.Element
    out = pl.BlockSpec((depth, D, LANES), lambda t, src, valid: (0, 0, t))
    return pl.pallas_call(
        _stage_w_kernel,
        grid_spec=pltpu.PrefetchScalarGridSpec(
            num_scalar_prefetch=2,
            grid=(len(src),),
            in_specs=[pl.BlockSpec((pl.Element(LANES), whole(depth), whole(D)), lambda t, src, valid: (src[t], 0, 0))],
            out_specs=(out, out, pl.BlockSpec((depth, D, 2 * LANES), lambda t, src, valid: (0, 0, 0)))),
        out_shape=(jax.ShapeDtypeStruct((depth, D, W_CAT), f32), jax.ShapeDtypeStruct((depth, D, W_CAT), bf16),
                   jax.ShapeDtypeStruct((depth, D, 2 * LANES), bf16)),
        compiler_params=_cparams(("arbitrary",)),
        name="stage_weights",
    )(jnp.asarray(src, jnp.int32), jnp.asarray(valid, jnp.int32), jnp.transpose(w_in, (2, 0, 1)))


def _head_block(x, h):
    blk = x[:, LANES * (h // 2):LANES * (h // 2 + 1)]
    return blk if h % 2 == 0 else pltpu.roll(blk, HEAD_DIM, 1)


def _inproj_kernel(x_ref, g_ref, wm_ref, wf_ref, bf_ref, *rest):
    qa_ref, kt_ref, ka_ref, vt_ref, va_ref, kst_ref, lf_ref, sg_ref, u_ref, qm_ref, carry_ref = rest[-11:]
    i = pl.program_id(1)

    @pl.when(i == 0)
    def _():
        carry_ref[...] = jnp.zeros_like(carry_ref)

    xn = _rms(x_ref[...], g_ref[...])
    xh = xn.astype(bf16)
    tm = xn.shape[0]
    lane = lax.broadcasted_iota(jnp.int32, (tm, LANES), 1)
    low = lane < HEAD_DIM

    ff = _dot(xh, wf_ref[...])
    f = ff[:, :LANES] + ff[:, LANES:] + bf_ref[...]
    lf = jnp.where(lane < N_FOX_HEADS, _log_sigmoid(f), 0.0)
    lf_ref[...] = lf.T[:N_FOX_HEADS, :]
    cs = _unpack_pieces(_dot(_tri(tm), _pack_pieces(lf)))
    cs = jnp.where(lane < N_FOX_HEADS, cs, 0.0) + carry_ref[0:1, :]
    carry_ref[...] = jnp.broadcast_to(cs[tm - 1:, :], carry_ref.shape)
    prow = lax.broadcasted_iota(jnp.int32, (LANES, N_FOX_HEADS * LANES), 0)
    pcol = lax.broadcasted_iota(jnp.int32, (LANES, N_FOX_HEADS * LANES), 1)
    place = (pcol == (prow % PIECE_PITCH) * LANES + HEAD_DIM + prow // PIECE_PITCH) & (prow % PIECE_PITCH < N_FOX_HEADS)
    caug = _dot(_pack_pieces(-cs), jnp.where(place, 1.0, 0.0).astype(bf16))
    ones3 = jnp.where((lane >= HEAD_DIM) & (lane < HEAD_DIM + 3), 1.0, 0.0)

    qk = _dot(xh, wm_ref[:, OFF_Q:OFF_V])
    q = qk[:, :W_FOX] * QK_SCALE
    k = qk[:, W_FOX:]
    kt_ref[...] = k.T
    kb = k.astype(bf16).astype(f32)
    qb = q.astype(bf16).astype(f32)
    squares = jnp.concatenate([(kb * kb).astype(bf16), (qb * qb).astype(bf16)], axis=1)
    srow = lax.broadcasted_iota(jnp.int32, (2 * W_FOX, LANES), 0)
    scol = lax.broadcasted_iota(jnp.int32, (2 * W_FOX, LANES), 1)
    target = (srow // W_FOX) * SUBLANES + (srow % W_FOX) // HEAD_DIM
    sums = _dot(squares, jnp.where(scol == target, 1.0, 0.0).astype(bf16))
    norms = jnp.sqrt(jnp.max(sums, axis=0, keepdims=True)) * NORM_SLACK
    kst_ref[:KST_CEND, :] = _col_bcast(norms)[:KST_CEND, :]
    kst_ref[KST_CEND:, :] = _col_bcast(cs[tm - 1:, :])[:SUBLANES, :]
    vu = _dot(xh, wm_ref[:, OFF_V:OFF_QM])
    v = vu[:, :W_FOX]
    vt_ref[...] = v.T
    u_ref[...] = vu[:, W_FOX:]
    for h in range(N_FOX_HEADS):
        qa_ref[h] = jnp.where(low, _head_block(q, h), ones3).astype(bf16)
        ka_ref[h] = jnp.where(low, _head_block(k, h), caug[:, LANES * h:LANES * (h + 1)]).astype(bf16)
        va_ref[h] = jnp.where(low, _head_block(v, h), 1.0).astype(bf16)
    rest = _dot(xh, wm_ref[:, OFF_QM:W_MAIN])
    qm_ref[...] = (rest[:, :W_MEM] * QK_SCALE).astype(bf16)
    sg_ref[...] = _silu(rest[:, W_MEM:]).astype(bf16)


def _inproj(l, x, g_all, wm_all, wf_all, bf_all, kv_prev, tm):
    B, L, D = x.shape
    depth = wm_all.shape[0]
    row = lambda w: pl.BlockSpec((None, tm, w), lambda b, i: (b, i, 0))
    heads = pl.BlockSpec((None, N_FOX_HEADS, tm, LANES), lambda b, i: (b, 0, i, 0))
    slab = pl.BlockSpec((None, None, W_FOX, tm), lambda b, i: (l, b, 0, i))
    aug = jax.ShapeDtypeStruct((B, N_FOX_HEADS, L, LANES), bf16)
    kvt = jax.ShapeDtypeStruct((depth, B, W_FOX, L), f32)
    out_shape = (
        aug,
        kvt,
        aug,
        kvt,
        aug,
        jax.ShapeDtypeStruct((B, L // tm, KST_ROWS, LANES), f32),
        jax.ShapeDtypeStruct((depth, N_FOX_HEADS, B * L), f32),
        jax.ShapeDtypeStruct((B, L, W_MIX), bf16),
        jax.ShapeDtypeStruct((B, L, W_SSM), f32),
        jax.ShapeDtypeStruct((B, L, W_MEM), bf16),
    )
    kst = pl.BlockSpec((None, None, KST_ROWS, LANES), lambda b, i: (b, i, 0, 0))
    lft = pl.BlockSpec((None, N_FOX_HEADS, tm), lambda b, i: (l, 0, b * (L // tm) + i))
    out_specs = (heads, slab, heads, slab, heads, kst, lft, row(W_MIX), row(W_SSM), row(W_MEM))
    in_specs = [row(D), _layer(l, (1, D)), _layer(l, wm_all.shape[1:]), _layer(l, wf_all.shape[1:]),
                _layer(l, (1, LANES))]
    n_in = len(in_specs)
    return pl.pallas_call(
        _inproj_kernel,
        grid=(B, L // tm),
        in_specs=in_specs + [pl.BlockSpec(memory_space=pl.ANY)] * len(kv_prev),
        out_specs=out_specs,
        out_shape=out_shape,
        input_output_aliases={n_in + n: out for n, out in zip(range(len(kv_prev)), (1, 3, 6))},
        scratch_shapes=[pltpu.VMEM((SUBLANES, LANES), f32)],
        compiler_params=_cparams(("parallel", "arbitrary")),
        name="prompt_inproj",
    )(x, g_all, wm_all, wf_all, bf_all, *kv_prev)


def _memkv_kernel(m_ref, g_ref, w_ref, k_ref, v_ref):
    mn = _rms(m_ref[...], g_ref[...]).astype(bf16)
    kv = _dot(mn, w_ref[...])
    k_ref[...] = kv[:, :W_MEM]
    v_ref[...] = kv[:, W_MEM:]


def _memkv(mem, g_all, w_all):
    B, N, D = mem.shape
    depth = w_all.shape[0]
    out = pl.BlockSpec((None, None, N, W_MEM), lambda l, b: (l, b, 0, 0))
    return pl.pallas_call(
        _memkv_kernel,
        grid=(depth, B),
        in_specs=[pl.BlockSpec((None, N, D), lambda l, b: (b, 0, 0)),
                  pl.BlockSpec((None, 1, D), lambda l, b: (l, 0, 0)),
                  pl.BlockSpec((None,) + w_all.shape[1:], lambda l, b: (l, 0, 0))],
        out_specs=(out, out),
        out_shape=(jax.ShapeDtypeStruct((depth, B, N, W_MEM), f32),) * 2,
        compiler_params=_cparams(("parallel", "parallel")),
        name="prompt_memkv",
    )(mem, g_all, w_all)


def _first_needed_block(kst_ref, h, i, n_diag):
    n_full = i * n_diag
    mine = pl.ds(n_full, n_diag)
    qn = jnp.max(kst_ref[mine, KST_QN + h, :], axis=0, keepdims=True)
    kn_own = jnp.max(kst_ref[mine, KST_KN + h, :], axis=0, keepdims=True)
    lb = -qn * kn_own - kst_ref[pl.ds(jnp.maximum(n_full - 1, 0), 1), KST_CEND + h, :]
    ub = qn * kst_ref[:, KST_KN + h, :] - kst_ref[:, KST_CEND + h, :]
    kj = lax.broadcasted_iota(jnp.int32, ub.shape, 0)
    needed = (ub - lb >= -SKIP_GAP) & (kj < n_full)
    return _full_min(jnp.where(needed, kj, n_full).astype(f32))


def _fox_kernel(q_ref, k_ref, v_ref, kst_ref, o_ref, acc_ref, m_ref, *, tq, tk):
    j = pl.program_id(1)
    i = pl.program_id(2)
    acc_ref[...] = jnp.zeros_like(acc_ref)
    m_ref[...] = jnp.full_like(m_ref, -jnp.inf)

    def step(kj, r0):
        k0 = pl.multiple_of(kj * tk, tk)
        rows = slice(0, tq) if r0 is None else slice(r0, tq)
        n = rows.stop - rows.start
        for e in range(2):
            s = _dot_nt(q_ref[e, rows, :], k_ref[e, pl.ds(k0, tk), :])
            if r0 is not None:
                row = r0 + lax.broadcasted_iota(jnp.int32, (n, tk), 0)
                col = r0 + lax.broadcasted_iota(jnp.int32, (n, tk), 1)
                s = jnp.where(col <= row, s, -jnp.inf)
            m_old = m_ref[e, rows, :]
            m_new = jnp.maximum(m_old, jnp.max(s, axis=1, keepdims=True))
            p = jnp.exp((s - jnp.tile(m_new, (1, tk // LANES))).astype(bf16))
            acc_ref[e, rows, :] = (jnp.exp(m_old - m_new) * acc_ref[e, rows, :]
                                   + _dot(p, v_ref[e, pl.ds(k0, tk), :]))
            m_ref[e, rows, :] = m_new

    n_diag = tq // tk
    first = jnp.minimum(*[_first_needed_block(kst_ref, 2 * j + e, i, n_diag) for e in range(2)])
    kj_first = jnp.min(first).astype(jnp.int32)
    t_first = (kj_first + n_diag - 1) // n_diag
    for d in range(1, n_diag):
        @pl.when(t_first * n_diag - d >= kj_first)
        def _():
            step(t_first * n_diag - d, None)

    def body(t, carry):
        for d in range(n_diag):
            step(t * n_diag + d, None)
        return carry

    lax.fori_loop(t_first, i, body, 0)
    for d in range(n_diag):
        step(i * n_diag + d, d * tk)
    lane = lax.broadcasted_iota(jnp.int32, (tq, LANES), 1)
    a0 = acc_ref[0]
    a1 = acc_ref[1]
    o = jnp.where(lane < HEAD_DIM, a0 / pltpu.roll(a0, HEAD_DIM, 1), pltpu.roll(a1, HEAD_DIM, 1) / a1)
    o_ref[...] = o.astype(o_ref.dtype)


def _fox(qa, ka, va, kst, tq, tk):
    B, _, L, _ = qa.shape
    n_pairs = W_FOX // LANES
    assert kst.shape[1] == L // tk
    return pl.pallas_call(
        functools.partial(_fox_kernel, tq=tq, tk=tk),
        grid=(B, n_pairs, L // tq),
        in_specs=[pl.BlockSpec((None, 2, tq, LANES), lambda b, j, i: (b, j, i, 0)),
                  pl.BlockSpec((None, 2, L, LANES), lambda b, j, i: (b, j, 0, 0)),
                  pl.BlockSpec((None, 2, L, LANES), lambda b, j, i: (b, j, 0, 0)),
                  pl.BlockSpec((None,) + kst.shape[1:], lambda b, j, i: (b, 0, 0, 0))],
        out_specs=pl.BlockSpec((None, tq, LANES), lambda b, j, i: (b, i, j)),
        out_shape=jax.ShapeDtypeStruct((B, L, W_FOX), bf16),
        scratch_shapes=[pltpu.VMEM((2, tq, LANES), f32), pltpu.VMEM((2, tq, LANES), f32)],
        compiler_params=_cparams(("parallel", "parallel", "arbitrary")),
        name="prompt_fox",
    )(qa, ka, va, kst)


def _s5prep_kernel(lr_ref, li_ref, ldt_ref, bre_ref, bim_ref, cre_ref, cim_ref,
                   lamb_ref, bmat_ref, cmat_ref, lpow_ref, lams_ref, pw_ref, *, seg_len):
    lr = lr_ref[...]
    li = li_ref[...]
    dt = jnp.exp(ldt_ref[...])
    ar = lr * dt
    ai = li * dt
    mag = jnp.exp(ar)
    lbr = mag * jnp.cos(ai)
    lbi = mag * jnp.sin(ai)
    lamb_ref[:, :HALF] = jnp.broadcast_to(lbr, (SUBLANES, HALF))
    lamb_ref[:, HALF:] = jnp.broadcast_to(lbi, (SUBLANES, HALF))
    den = lr * lr + li * li
    zr = ((lbr - 1.0) * lr + lbi * li) / den
    zi = (lbi * lr - (lbr - 1.0) * li) / den
    bre = bre_ref[...]
    bim = bim_ref[...]
    bmat_ref[:, :HALF] = zr * bre - zi * bim
    bmat_ref[:, HALF:] = zr * bim + zi * bre
    cmat_ref[:HALF, :] = cre_ref[...]
    cmat_ref[HALF:, :] = -cim_ref[...]
    jj = (lax.broadcasted_iota(jnp.int32, (SUBLANES, HALF), 0) + 1).astype(f32)
    magp = jnp.exp(ar * jj)
    pw_ref[:SUBLANES, :HALF] = magp * jnp.cos(ai * jj)
    pw_ref[:SUBLANES, HALF:] = magp * jnp.sin(ai * jj)
    n = SUBLANES
    while n < seg_len:
        sr = pw_ref[n - 1:n, :HALF]
        si = pw_ref[n - 1:n, HALF:]
        pr = pw_ref[:n, :HALF]
        pi = pw_ref[:n, HALF:]
        pw_ref[n:2 * n, :HALF] = pr * sr - pi * si
        pw_ref[n:2 * n, HALF:] = pr * si + pi * sr
        n *= 2
    lpow_ref[...] = pw_ref[...].astype(bf16)
    lams_ref[...] = pw_ref[seg_len - SUBLANES:, :]


def _s5prep(lam_re, lam_im, log_dt, b_re, b_im, c_re, c_im, seg_len):
    depth = lam_re.shape[0]
    assert seg_len >= SUBLANES and seg_len & (seg_len - 1) == 0
    nb, gb = N_SSM_BLOCKS, GROUPS_PER_BLOCK
    eye = jnp.eye(gb, dtype=f32)
    half = lambda a: a.reshape(depth, nb, 1, HALF)
    ldt = jnp.broadcast_to(log_dt[:, :, None], (depth, N_SSM_GROUPS, SSM_STATE))

    def embed_b(b):
        t = b.reshape(depth, nb, gb, SSM_STATE, SSM_GROUP).transpose(0, 1, 2, 4, 3)
        return (t[:, :, :, :, None, :] * eye[None, None, :, None, :, None]).reshape(depth, nb, LANES, HALF)

    def embed_c(c):
        t = c.reshape(depth, nb, gb, SSM_GROUP, SSM_STATE).transpose(0, 1, 2, 4, 3)
        return (t[:, :, :, :, None, :] * eye[None, None, :, None, :, None]).reshape(depth, nb, HALF, LANES)

    blk = lambda shape: pl.BlockSpec((None, None) + shape, lambda l, cb: (l, cb) + (0,) * len(shape))
    arr = lambda shape, dt: jax.ShapeDtypeStruct((depth, nb) + shape, dt)
    return pl.pallas_call(
        functools.partial(_s5prep_kernel, seg_len=seg_len),
        grid=(depth, nb),
        in_specs=[blk((1, HALF))] * 3 + [blk((LANES, HALF))] * 2 + [blk((HALF, LANES))] * 2,
        out_specs=(blk((SUBLANES, STATE_W)), blk((LANES, STATE_W)), blk((STATE_W, LANES)),
                   blk((seg_len, STATE_W)), blk((SUBLANES, STATE_W))),
        out_shape=(arr((SUBLANES, STATE_W), f32),
                   arr((LANES, STATE_W), f32),
                   arr((STATE_W, LANES), f32),
                   arr((seg_len, STATE_W), bf16),
                   arr((SUBLANES, STATE_W), f32)),
        scratch_shapes=[pltpu.VMEM((seg_len, STATE_W), f32)],
        compiler_params=_cparams(("parallel", "parallel")),
        name="s5_tables",
    )(half(lam_re), half(lam_im), half(ldt), embed_b(b_re), embed_b(b_im), embed_c(c_re), embed_c(c_im))


def _s5scan_kernel(u_ref, bmat_ref, cmat_ref, lamb_ref, dsk_ref, y_ref, hend_ref,
                   up_ref, hb_ref, hq_ref, hcar_ref, *, tj):
    j = pl.program_id(1)
    nb = N_SSM_BLOCKS

    @pl.when(j == 0)
    def _():
        hcar_ref[...] = jnp.zeros_like(hcar_ref)

    for cb in range(nb):
        for s in range(N_SEG):
            up_ref[cb, pl.ds(s, tj, stride=N_SEG), :] = u_ref[s, :, LANES * cb:LANES * (cb + 1)]
        hb_ref[cb] = _dot(up_ref[cb].astype(bf16), bmat_ref[cb])

    def rec(t2, h):
        r0 = pl.multiple_of(t2 * 2 * N_SEG, 2 * N_SEG)
        new = []
        for cb in range(nb):
            hr, hi = h[2 * cb], h[2 * cb + 1]
            lr = lamb_ref[cb, :, :HALF]
            li = lamb_ref[cb, :, HALF:]
            rs, is_ = [], []
            for d in range(2):
                bu = hb_ref[cb, pl.ds(r0 + d * N_SEG, N_SEG), :]
                hr, hi = lr * hr - li * hi + bu[:, :HALF], lr * hi + li * hr + bu[:, HALF:]
                rs.append(hr)
                is_.append(hi)
            hq_ref[cb, pl.ds(r0, 2 * N_SEG), :HALF] = jnp.concatenate(rs, axis=0).astype(bf16)
            hq_ref[cb, pl.ds(r0, 2 * N_SEG), HALF:] = jnp.concatenate(is_, axis=0).astype(bf16)
            new += [hr, hi]
        return tuple(new)

    h0 = tuple(hcar_ref[cb, :, HALF * part:HALF * (part + 1)] for cb in range(nb) for part in range(2))
    h = lax.fori_loop(0, tj // 2, rec, h0, unroll=True)
    for cb in range(nb):
        hcar_ref[cb, :, :HALF] = h[2 * cb]
        hcar_ref[cb, :, HALF:] = h[2 * cb + 1]
        up_ref[cb] = _dot(hq_ref[cb], cmat_ref[cb]) + dsk_ref[cb] * up_ref[cb]
        for s in range(N_SEG):
            y_ref[s, :, LANES * cb:LANES * (cb + 1)] = up_ref[cb, pl.ds(s, tj, stride=N_SEG), :].astype(y_ref.dtype)

    @pl.when(j == pl.num_programs(1) - 1)
    def _():
        hend_ref[...] = hcar_ref[...]


def _s5scan(l, u, bmat_all, cmat_all, lamb_all, dsk_all, tj):
    B, L, _ = u.shape
    seg_len = L // N_SEG
    u4 = u.reshape(B, N_SEG, seg_len, W_SSM)
    nb = N_SSM_BLOCKS
    tile = pl.BlockSpec((None, N_SEG, tj, W_SSM), lambda b, j: (b, 0, j, 0))
    y4, hend = pl.pallas_call(
        functools.partial(_s5scan_kernel, tj=tj),
        grid=(B, seg_len // tj),
        in_specs=[tile, _layer(l, bmat_all.shape[1:]), _layer(l, cmat_all.shape[1:]),
                  _layer(l, lamb_all.shape[1:]), _layer(l, dsk_all.shape[1:])],
        out_specs=(tile, pl.BlockSpec((None, nb, N_SEG, STATE_W), lambda b, j: (b, 0, 0, 0))),
        out_shape=(jax.ShapeDtypeStruct((B, N_SEG, seg_len, W_SSM), bf16),
                   jax.ShapeDtypeStruct((B, nb, N_SEG, STATE_W), f32)),
        scratch_shapes=[pltpu.VMEM((nb, N_SEG * tj, LANES), f32), pltpu.VMEM((nb, N_SEG * tj, STATE_W), f32),
                        pltpu.VMEM((nb, N_SEG * tj, STATE_W), bf16), pltpu.VMEM((nb, N_SEG, STATE_W), f32)],
        compiler_params=_cparams(("parallel", "arbitrary")),
        name="prompt_s5_scan",
    )(u4, bmat_all, cmat_all, lamb_all, dsk_all)
    return y4.reshape(B, L, W_SSM), hend


def _s5fix_kernel(hend_ref, lams_ref, cmat_ref, hlast_ref, cp_ref, hs_ref):
    he = hend_ref[...]
    lsr = lams_ref[SUBLANES - 1:, :HALF]
    lsi = lams_ref[SUBLANES - 1:, HALF:]
    hs_ref[...] = jnp.zeros_like(hs_ref)
    hr = jnp.zeros((1, HALF), f32)
    hi = jnp.zeros((1, HALF), f32)
    for s in range(N_SEG):
        hs_ref[s:s + 1, :HALF] = hr
        hs_ref[s:s + 1, HALF:] = hi
        nr = lsr * hr - lsi * hi + he[s:s + 1, :HALF]
        ni = lsr * hi + lsi * hr + he[s:s + 1, HALF:]
        hr, hi = nr, ni
    hlast_ref[:, :HALF] = hr
    hlast_ref[:, HALF:] = hi
    hst_r = hs_ref[:, :HALF].T
    hst_i = hs_ref[:, HALF:].T
    cre = cmat_ref[:HALF, :]
    mci = cmat_ref[HALF:, :]
    for s in range(N_SEG):
        col_r = hst_r[:, s:s + 1]
        col_i = hst_i[:, s:s + 1]
        cp_ref[s, :HALF, :] = (cre * col_r + mci * col_i).astype(bf16)
        cp_ref[s, HALF:, :] = (mci * col_r - cre * col_i).astype(bf16)


def _s5fix(l, hend, lams_all, cmat_all):
    B, nb = hend.shape[:2]
    return pl.pallas_call(
        _s5fix_kernel,
        grid=(B, nb),
        in_specs=[pl.BlockSpec((None, None, N_SEG, STATE_W), lambda b, cb: (b, cb, 0, 0)),
                  pl.BlockSpec((None, None, SUBLANES, STATE_W), lambda b, cb: (l, cb, 0, 0)),
                  pl.BlockSpec((None, None, STATE_W, LANES), lambda b, cb: (l, cb, 0, 0))],
        out_specs=(pl.BlockSpec((None, None, 1, STATE_W), lambda b, cb: (b, cb, 0, 0)),
                   pl.BlockSpec((None, N_SEG, None, STATE_W, LANES), lambda b, cb: (b, 0, cb, 0, 0))),
        out_shape=(jax.ShapeDtypeStruct((B, nb, 1, STATE_W), f32),
                   jax.ShapeDtypeStruct((B, N_SEG, nb, STATE_W, LANES), bf16)),
        scratch_shapes=[pltpu.VMEM((LANES, STATE_W), f32)],
        compiler_params=_cparams(("parallel", "parallel")),
        name="prompt_s5_fix",
    )(hend, lams_all, cmat_all)


def _pair_attend(q, k, v):
    lane = lax.broadcasted_iota(jnp.int32, q.shape, 1)
    zero = jnp.zeros_like(q)
    outs = []
    for e in range(2):
        qe = jnp.where((lane < HEAD_DIM) == (e == 0), q, zero)
        s = _dot_nt(qe, k)
        p = jnp.exp(s - jnp.max(s, axis=1, keepdims=True))
        outs.append(_dot(p.astype(bf16), v) / jnp.sum(p, axis=1, keepdims=True))
    return jnp.where(lane < HEAD_DIM, outs[0], outs[1])


N_OUTPROJ_INPUTS = 13


def _outproj_stages(x_ref, fox_ref, y_ref, lp_ref, cp_ref, qm_ref, mk_ref, mv_ref, sg_ref,
                    wglu_ref, bglu_ref, wout_ref, gfin_ref, o_ref, *, final, tiles_per_seg):
    tm = x_ref.shape[0]
    j0 = pl.multiple_of((pl.program_id(1) % tiles_per_seg) * tm, tm)
    corr = [_dot(lp_ref[cb, pl.ds(j0, tm), :], cp_ref[cb]) for cb in range(N_SSM_BLOCKS)]
    ssm_y = y_ref[...] + jnp.concatenate(corr, axis=1)
    yield
    s = _gelu(ssm_y)
    s = s * jax.nn.sigmoid(_dot(s.astype(bf16), wglu_ref[...]) + bglu_ref[...])
    yield
    qm = qm_ref[...]
    mk = mk_ref[...].astype(bf16)
    mv = mv_ref[...].astype(bf16)
    mem = []
    for jp in range(W_MEM // LANES):
        mem.append(_pair_attend(qm[:, LANES * jp:LANES * (jp + 1)], mk[:, LANES * jp:LANES * (jp + 1)],
                                mv[:, LANES * jp:LANES * (jp + 1)]))
        yield
    sg = sg_ref[...].astype(f32)
    mix = jnp.concatenate([fox_ref[...] * sg[:, :W_FOX],
                           s * sg[:, W_FOX:W_FOX + W_SSM],
                           jnp.concatenate(mem, axis=1) * sg[:, W_FOX + W_SSM:]], axis=1)
    xo = x_ref[...] + _dot(mix.astype(bf16), wout_ref[...])
    yield
    if final:
        xo = _rms(xo, gfin_ref[...])
    o_ref[...] = xo


def _s_inproj_kernel(x_ref, g_ref, w_ref, z_ref):
    xn = _rms(x_ref[...], g_ref[...])
    z_ref[...] = _dot3(xn, w_ref[...])


def _s_inproj(l, xs, g_all, wcat_all, tn):
    R, D = xs.shape
    return pl.pallas_call(
        _s_inproj_kernel,
        grid=(W_CAT // tn,),
        in_specs=[pl.BlockSpec((R, D), lambda n: (0, 0)), _layer(l, (1, D)),
                  pl.BlockSpec((None, D, tn), lambda n: (l, 0, n))],
        out_specs=pl.BlockSpec((R, tn), lambda n: (0, n)),
        out_shape=jax.ShapeDtypeStruct((R, W_CAT), f32),
        compiler_params=_cparams(("parallel",)),
        name="sample_inproj",
    )(xs, g_all, wcat_all)


def _full_max(x):
    return jnp.max(jnp.max(x, axis=0, keepdims=True), axis=1, keepdims=True)


def _full_min(x):
    return jnp.min(jnp.min(x, axis=0, keepdims=True), axis=1, keepdims=True)


def _full_sum(x):
    return jnp.sum(jnp.sum(x, axis=0, keepdims=True), axis=1, keepdims=True)


def _lane_sums(acc):
    ones = jnp.ones((SUBLANES, LANES), bf16)
    hi, lo = _split2(acc)
    return (_dot_nt(ones, hi) + _dot_nt(ones, lo))[0:1, :]


def _col_bcast(row):
    return jnp.broadcast_to(row, (LANES, row.shape[1])).T


class _DecodeGuest:
    N_FIXED_IN, N_OUT = 5, 3

    def __init__(self, l, n_seq, first_seq, n_steps, step_of, page_table, z, bf_all, ckt, cvt, lft, mkt, mvt):
        R, n_pages = page_table.shape
        n_pool, page = ckt.shape[1], ckt.shape[4]
        n_mem = mkt.shape[4]
        assert page == LANES and n_mem % LANES == 0 and first_seq % n_seq == 0
        assert first_seq + n_steps * n_seq <= R
        self.l, self.n_seq, self.n_pages, self.n_steps = l, n_seq, n_pages, n_steps
        self.first_seq, self.step_of = first_seq, step_of
        self.per_seq = 2
        z_blk = lambda *g: (first_seq // n_seq + step_of(*g[:-1]), 0, 0)
        self.in_specs = [pl.BlockSpec((n_seq, 1, W_CAT), z_blk), _layer(l, (1, LANES)),
                         _layer(l, (N_FOX_HEADS, n_pool, page)),
                         pl.BlockSpec(memory_space=pl.ANY), pl.BlockSpec(memory_space=pl.ANY)]
        self.operands = [z.reshape(R, 1, W_CAT), bf_all, lft, ckt, cvt]
        for s in range(n_seq):
            seq = lambda g, s=s: first_seq + step_of(*g) * n_seq + s
            mem_spec = pl.BlockSpec((None, None, N_MEM_HEADS, HEAD_DIM, n_mem),
                                    lambda *g, seq=seq: (l, seq(g[:-1]), 0, 0, 0))
            self.in_specs += [mem_spec, mem_spec]
            self.operands += [mkt, mvt]
        out = lambda w: pl.BlockSpec((n_seq, 1, w), lambda *g: (step_of(*g[:-1]), 0, 0))
        self.out_specs = [out(W_FOX), out(W_MEM), out(LANES)]
        n_out = n_steps * n_seq
        self.out_shape = [jax.ShapeDtypeStruct((n_out, 1, w), f32) for w in (W_FOX, W_MEM, LANES)]
        n_rows = N_FOX_HEADS * n_pages
        pages = pltpu.VMEM((PAGE_SLOTS, n_seq, n_pages, N_FOX_HEADS, HEAD_DIM, page), f32)
        self.scratch_shapes = [pltpu.VMEM((n_seq, rows, LANES), f32)
                               for rows in (n_rows, n_rows, n_rows, W_FOX, W_MEM)]
        self.scratch_shapes += [pages, pages, pltpu.SemaphoreType.DMA((PAGE_SLOTS, 2))]

    def _page_copies(self, pt_ref, step, slot, ck_hbm, cv_hbm, kbuf, vbuf, sems):
        copies = []
        for i in range(self.n_seq):
            seq = self.first_seq + step * self.n_seq + i
            for p in range(self.n_pages):
                page = pt_ref[seq, p]
                copies.append(pltpu.make_async_copy(ck_hbm.at[self.l, page], kbuf.at[slot, i, p], sems.at[slot, 0]))
                copies.append(pltpu.make_async_copy(cv_hbm.at[self.l, page], vbuf.at[slot, i, p], sems.at[slot, 1]))
        return copies

    def chains(self, pt_ref, step, ins, outs, scratch):
        z_ref, bf_ref, lfc_ref, ck_hbm, cv_hbm = ins[:self.N_FIXED_IN]
        fox_ref, mem_ref, lf_ref = outs
        *work, kbuf, vbuf, sems = scratch
        ring = (ck_hbm, cv_hbm, kbuf, vbuf, sems)
        ahead = PAGE_SLOTS - 1

        @pl.when(step == 0)
        def _():
            for s0 in range(min(ahead, self.n_steps)):
                for c in self._page_copies(pt_ref, s0, s0 % PAGE_SLOTS, *ring):
                    c.start()

        @pl.when(step + ahead < self.n_steps)
        def _():
            for c in self._page_copies(pt_ref, step + ahead, (step + ahead) % PAGE_SLOTS, *ring):
                c.start()

        slot = step % PAGE_SLOTS
        for c in self._page_copies(pt_ref, step, slot, *ring):
            c.wait()
        result = []
        for i in range(self.n_seq):
            mine = ins[self.N_FIXED_IN + i * self.per_seq:self.N_FIXED_IN + (i + 1) * self.per_seq]
            result.append(_decode_one(
                pt_ref, self.first_seq + step * self.n_seq + i, z_ref.at[i], bf_ref, lfc_ref,
                [kbuf.at[slot, i, p] for p in range(self.n_pages)],
                [vbuf.at[slot, i, p] for p in range(self.n_pages)], mine[0], mine[1],
                fox_ref.at[i], mem_ref.at[i], lf_ref.at[i], *[s.at[i] for s in work]))
        return result

    @property
    def n_in(self):
        return self.N_FIXED_IN + self.n_seq * self.per_seq


def _run_chains(chains):
    while chains:
        chains = [c for c in chains if next(c, _DONE) is not _DONE]


def _outproj_decode_kernel(pt_ref, *refs, guest, final, tiles_per_seg):
    n_in = N_OUTPROJ_INPUTS + guest.n_in
    op_in, g_in = refs[:N_OUTPROJ_INPUTS], refs[N_OUTPROJ_INPUTS:n_in]
    o_ref = refs[n_in]
    g_out = refs[n_in + 1:n_in + 1 + guest.N_OUT]
    g_scr = refs[n_in + 1 + guest.N_OUT:]
    step = guest.step_of(pl.program_id(0), pl.program_id(1))
    _run_chains([_outproj_stages(*op_in, o_ref, final=final, tiles_per_seg=tiles_per_seg)]
                + guest.chains(pt_ref, step, g_in, g_out, g_scr))


_DONE = object()


def _decode_one(pt_ref, r, z_ref, bf_ref, lfc_ref, ck, cv, mk_ref, mv_ref, fox_ref, mem_ref, lf_ref,
                s_scr, lfs_scr, p_scr, acc_scr, accm_scr):
    n_pages = len(ck)
    z = z_ref[...]
    q = z[:, OFF_Q:OFF_Q + W_FOX] * QK_SCALE
    k_new = z[:, OFF_K:OFF_K + W_FOX]
    v_new = z[:, OFF_V:OFF_V + W_FOX]
    lf_new = _log_sigmoid(z[:, OFF_F:OFF_F + LANES] + bf_ref[...])
    lf_ref[...] = lf_new

    qt = _col_bcast(q)
    for h in range(N_FOX_HEADS):
        qh = qt[HEAD_DIM * h:HEAD_DIM * (h + 1), :]
        for p in range(n_pages):
            row = h * n_pages + p
            s_scr[row:row + 1, :] = jnp.sum(ck[p][h] * qh, axis=0, keepdims=True)
            lfs_scr[row:row + 1, :] = lfc_ref[h, pl.ds(pt_ref[r, p], 1), :]
        yield

    n_rows = N_FOX_HEADS * n_pages
    ur = lax.broadcasted_iota(jnp.int32, (LANES, LANES), 0)
    uc = lax.broadcasted_iota(jnp.int32, (LANES, LANES), 1)
    upper = jnp.where(ur <= uc, 1.0, 0.0).astype(bf16)
    cs = sum(_dot(pc, upper) for pc in _split3(lfs_scr[...]))
    tot = jnp.broadcast_to(cs[:, LANES - 1:], (n_rows, LANES))
    mr = lax.broadcasted_iota(jnp.int32, (n_rows, n_rows), 0)
    mc = lax.broadcasted_iota(jnp.int32, (n_rows, n_rows), 1)
    before = jnp.where((mc < mr) & (mc // n_pages == mr // n_pages), 1.0, 0.0).astype(bf16)
    c = cs + sum(_dot(before, pc) for pc in _split3(tot))
    s_all = s_scr[...] - c
    yield

    qk_new = q * k_new
    lane_head = lax.broadcasted_iota(jnp.int32, (1, W_FOX), 1) // HEAD_DIM
    pn_exp = jnp.zeros((1, W_FOX), f32)
    l_exp = jnp.zeros((1, W_FOX), f32)
    for h in range(N_FOX_HEADS):
        rows = slice(h * n_pages, (h + 1) * n_pages)
        s_h = s_all[rows, :]
        c_new = c[(h + 1) * n_pages - 1:(h + 1) * n_pages, LANES - 1:] + lf_new[:, h:h + 1]
        s_new = jnp.sum(qk_new[:, HEAD_DIM * h:HEAD_DIM * (h + 1)], axis=1, keepdims=True) - c_new
        m = jnp.maximum(_full_max(s_h), s_new)
        p_h = jnp.exp(s_h - m)
        pn = jnp.exp(s_new - m)
        p_scr[rows, :] = p_h
        pn_exp = jnp.where(lane_head == h, pn, pn_exp)
        l_exp = jnp.where(lane_head == h, _full_sum(p_h) + pn, l_exp)
        acc = jnp.zeros((HEAD_DIM, LANES), f32)
        for p in range(n_pages):
            acc = acc + cv[p][h] * p_scr[h * n_pages + p:h * n_pages + p + 1, :]
        acc_scr[HEAD_DIM * h:HEAD_DIM * (h + 1), :] = acc
        yield
    fox_ref[...] = (_lane_sums(acc_scr[...]) + pn_exp * v_new) / l_exp
    yield

    qm = z[:, OFF_QM:OFF_QM + W_MEM] * QK_SCALE
    qmt = _col_bcast(qm)
    n_halves = mk_ref.shape[2] // LANES
    mem_head = lax.broadcasted_iota(jnp.int32, (1, W_MEM), 1) // HEAD_DIM
    lm_exp = jnp.zeros((1, W_MEM), f32)
    for h in range(N_MEM_HEADS):
        qh = qmt[HEAD_DIM * h:HEAD_DIM * (h + 1), :]
        ss = [jnp.sum(mk_ref[h, :, LANES * t:LANES * (t + 1)] * qh, axis=0, keepdims=True) for t in range(n_halves)]
        m = functools.reduce(jnp.maximum, [jnp.max(s, axis=1, keepdims=True) for s in ss])
        ps = [jnp.exp(s - m) for s in ss]
        lm = sum(jnp.sum(p, axis=1, keepdims=True) for p in ps)
        lm_exp = jnp.where(mem_head == h, lm, lm_exp)
        accm_scr[HEAD_DIM * h:HEAD_DIM * (h + 1), :] = sum(
            mv_ref[h, :, LANES * t:LANES * (t + 1)] * ps[t] for t in range(n_halves))
    mem_ref[...] = _lane_sums(accm_scr[...]) / lm_exp


def _outproj_decode(l, x, fox_o, y, lpow_all, cp, qm, mk_all, mv_all, sg, wglu_all, bglu_all, wout_all, gfin,
                    page_table, z, bf_all, ckt, cvt, lft, mkt, mvt, first_seq, tm, final):
    B, L, D = x.shape
    seg_len = L // N_SEG
    tiles_per_seg = seg_len // tm
    n_i = L // tm
    n_steps = B * n_i
    n_left = page_table.shape[0] - first_seq
    assert n_left % n_steps == 0
    guest = _DecodeGuest(l, n_left // n_steps, first_seq, n_steps, lambda b, i: b * n_i + i,
                         page_table, z, bf_all, ckt, cvt, lft, mkt, mvt)
    nb = N_SSM_BLOCKS
    row = lambda w: pl.BlockSpec((None, tm, w), lambda b, i, pt: (b, i, 0))
    mem_kv = pl.BlockSpec((None, None) + mk_all.shape[2:], lambda b, i, pt: (l, b, 0, 0))
    in_specs = [row(D), row(W_FOX), row(W_SSM), _layer(l, lpow_all.shape[1:]),
                pl.BlockSpec((None, None, nb, STATE_W, LANES), lambda b, i, pt: (b, i // tiles_per_seg, 0, 0, 0)),
                row(W_MEM), mem_kv, mem_kv, row(D),
                _layer(l, wglu_all.shape[1:]), _layer(l, (1, W_SSM)), _layer(l, wout_all.shape[1:]),
                pl.BlockSpec((1, D), lambda b, i, pt: (0, 0))]
    assert len(in_specs) == N_OUTPROJ_INPUTS
    xo, *dec = pl.pallas_call(
        functools.partial(_outproj_decode_kernel, guest=guest, final=final, tiles_per_seg=tiles_per_seg),
        grid_spec=pltpu.PrefetchScalarGridSpec(
            num_scalar_prefetch=1,
            grid=(B, n_i),
            in_specs=in_specs + guest.in_specs,
            out_specs=[row(D)] + guest.out_specs,
            scratch_shapes=guest.scratch_shapes),
        out_shape=[jax.ShapeDtypeStruct((B, L, D), f32)] + guest.out_shape,
        compiler_params=_cparams(("arbitrary", "arbitrary")),
        name="outproj_decode",
    )(page_table, x, fox_o, y, lpow_all, cp, qm, mk_all, mv_all, sg, wglu_all, bglu_all, wout_all, gfin,
      *guest.operands)
    return xo, dec


def _s_out_kernel(x_ref, z_ref, fox_ref, mem_ref, h0_ref, bmat_ref, cmat_ref, lamb_ref, dsk_ref,
                  wglu_ref, bglu_ref, wout_ref, gfin_ref, o_ref, h_ref, *, final):
    z = z_ref[...]
    u = z[:, OFF_U:OFF_U + W_SSM]
    ys = []
    for cb in range(N_SSM_BLOCKS):
        ucb = u[:, LANES * cb:LANES * (cb + 1)]
        bu = _dot3(ucb, bmat_ref[cb])
        lr = lamb_ref[cb, 0:1, :HALF]
        li = lamb_ref[cb, 0:1, HALF:]
        h0r = h0_ref[:, STATE_W * cb:STATE_W * cb + HALF]
        h0i = h0_ref[:, STATE_W * cb + HALF:STATE_W * (cb + 1)]
        hr = lr * h0r - li * h0i + bu[:, :HALF]
        hi = lr * h0i + li * h0r + bu[:, HALF:]
        h_ref[:, STATE_W * cb:STATE_W * cb + HALF] = hr
        h_ref[:, STATE_W * cb + HALF:STATE_W * (cb + 1)] = hi
        ys.append(_dot3(jnp.concatenate([hr, hi], axis=1), cmat_ref[cb]) + dsk_ref[cb] * ucb)
    s = _gelu(jnp.concatenate(ys, axis=1))
    s = s * jax.nn.sigmoid(_dot3(s, wglu_ref[...]) + bglu_ref[...])
    sg = _silu(z[:, OFF_G:W_MAIN])
    mix = jnp.concatenate([fox_ref[...] * sg[:, :W_FOX],
                           s * sg[:, W_FOX:W_FOX + W_SSM],
                           mem_ref[...] * sg[:, W_FOX + W_SSM:]], axis=1)
    xo = x_ref[...] + _dot3(mix, wout_ref[...])
    if final:
        xo = _rms(xo, gfin_ref[...])
    o_ref[...] = xo


def _s_out(l, xs, z, fox, mem, h0_all, bmat_all, cmat_all, lamb_all, dsk_all, wglu_all, bglu_all, wout_all,
           gfin, final):
    R, D = xs.shape
    whole = lambda a: pl.BlockSpec(a.shape, lambda i: (0,) * a.ndim)
    hspec = pl.BlockSpec((R, h0_all.shape[2]), lambda i: (0, 0))
    return pl.pallas_call(
        functools.partial(_s_out_kernel, final=final),
        grid=(1,),
        in_specs=[whole(xs), whole(z), whole(fox), whole(mem), _layer(l, h0_all.shape[1:]),
                  _layer(l, bmat_all.shape[1:]), _layer(l, cmat_all.shape[1:]), _layer(l, lamb_all.shape[1:]),
                  _layer(l, dsk_all.shape[1:]), _layer(l, wglu_all.shape[1:]), _layer(l, (1, W_SSM)),
                  _layer(l, wout_all.shape[1:]), whole(gfin)],
        out_specs=(whole(xs), hspec),
        out_shape=(jax.ShapeDtypeStruct((R, D), f32), jax.ShapeDtypeStruct(h0_all.shape[1:], f32)),
        compiler_params=_cparams(("arbitrary",)),
        name="sample_out",
    )(xs, z, fox, mem, h0_all, bmat_all, cmat_all, lamb_all, dsk_all, wglu_all, bglu_all, wout_all, gfin)


def _state_to_cols(h):
    lead = h.shape[:-3]
    n = len(lead)
    t = h.reshape(lead + (N_SSM_BLOCKS, GROUPS_PER_BLOCK, SSM_STATE, 2))
    t = t.transpose(tuple(range(n)) + (n, n + 3, n + 1, n + 2))
    return t.reshape(lead + (N_SSM_BLOCKS * STATE_W,))


def _cols_to_state(c):
    lead = c.shape[:-1]
    n = len(lead)
    t = c.reshape(lead + (N_SSM_BLOCKS, 2, GROUPS_PER_BLOCK, SSM_STATE))
    t = t.transpose(tuple(range(n)) + (n, n + 2, n + 3, n + 1))
    return t.reshape(lead + (N_SSM_GROUPS, SSM_STATE, 2))


def _tile(n, pref):
    t = min(n, pref)
    assert n % t == 0
    return t


def kernel(x_prompt, x_sample, mem_prompt, cache_k, cache_v, cache_logf, state_ssm, cache_mem_k, cache_mem_v,
           page_table, g_norm, w_in, b_f, lam_re, lam_im, log_dt, b_re, b_im, c_re, c_im, d_skip, w_glu, b_glu,
           g_mem, w_mem_kv, w_out, g_final):
    depth = w_in.shape[0]
    B, L, D = x_prompt.shape
    R = x_sample.shape[0]
    assert x_sample.shape[1] == 1 and L % N_SEG == 0
    seg_len = L // N_SEG
    tm = _tile(seg_len, TM_OUT)
    tq = _tile(L, TQ_FOX)
    tk = _tile(tq, TK_FOX)
    tj = _tile(seg_len, TJ_SCAN)

    wcat_all, wm_all, wf_all = _stage_w(w_in)
    bf_all = jnp.pad(b_f, ((0, 0), (0, LANES - N_FOX_HEADS))).reshape(depth, 1, LANES)
    gn_all = g_norm.reshape(depth, 1, D)
    gmem_all = g_mem.reshape(depth, 1, D)
    bglu_all = b_glu.reshape(depth, 1, W_SSM)
    gfin = g_final.reshape(1, D)
    dsk_all = d_skip.reshape(depth, N_SSM_BLOCKS, 1, LANES)
    wglu_bf = w_glu.astype(bf16)
    wout_bf = w_out.astype(bf16)
    ckt = jnp.transpose(cache_k, (0, 1, 3, 4, 2))
    cvt = jnp.transpose(cache_v, (0, 1, 3, 4, 2))
    lft = jnp.transpose(cache_logf, (0, 3, 1, 2))
    mkt = jnp.transpose(cache_mem_k, (0, 1, 3, 4, 2))
    mvt = jnp.transpose(cache_mem_v, (0, 1, 3, 4, 2))
    h0_all = _state_to_cols(state_ssm)

    lamb_all, bmat_all, cmat_all, lpow_all, lams_all = _s5prep(lam_re, lam_im, log_dt, b_re, b_im, c_re, c_im, seg_len)
    bmat_bf = bmat_all.astype(bf16)
    cmat_bf = cmat_all.astype(bf16)
    mk_all, mv_all = _memkv(mem_prompt, gmem_all, w_mem_kv.astype(bf16))

    xp = x_prompt
    xs = x_sample.reshape(R, D)
    outs = {n: [] for n in ("hp", "z", "lfs", "hs")}
    kv_prev = ()
    for l in range(depth):
        final = l == depth - 1
        qa, kt_all, ka, vt_all, va, kst, lfp_all, sg, u, qm = _inproj(l, xp, gn_all, wm_all, wf_all, bf_all,
                                                                      kv_prev, tk)
        kv_prev = (kt_all, vt_all, lfp_all)
        fox_o = _fox(qa, ka, va, kst, tq, tk)
        y, hend = _s5scan(l, u, bmat_bf, cmat_bf, lamb_all, dsk_all, tj)
        hlast, cp = _s5fix(l, hend, lams_all, cmat_all)
        outs["hp"].append(hlast.reshape(B, N_SSM_BLOCKS * STATE_W))

        z = _s_inproj(l, xs, gn_all, wcat_all, TN_SAMPLE)
        xp, dec = _outproj_decode(
            l, xp, fox_o, y, lpow_all, cp, qm, mk_all, mv_all, sg, wglu_bf, bglu_all, wout_bf, gfin,
            page_table, z, bf_all, ckt, cvt, lft, mkt, mvt, 0, tm, final)
        fox_s, mem_s, lf_s = (d.reshape(R, -1) for d in dec)
        xs, h_new = _s_out(l, xs, z, fox_s, mem_s, h0_all, bmat_all, cmat_all, lamb_all, dsk_all,
                           w_glu, bglu_all, w_out, gfin, final)
        outs["z"].append(z)
        outs["lfs"].append(lf_s)
        outs["hs"].append(h_new)

    st = lambda n: jnp.stack(outs[n])
    z_all = st("z")
    n_mem = mk_all.shape[2]
    return (xp, xs.reshape(R, 1, D),
            kt_all.reshape(depth, B, N_FOX_HEADS, HEAD_DIM, L).transpose(0, 1, 4, 2, 3),
            vt_all.reshape(depth, B, N_FOX_HEADS, HEAD_DIM, L).transpose(0, 1, 4, 2, 3),
            lfp_all.reshape(depth, N_FOX_HEADS, B, L).transpose(0, 2, 3, 1),
            _cols_to_state(st("hp")),
            mk_all.reshape(depth, B, n_mem, N_MEM_HEADS, HEAD_DIM),
            mv_all.reshape(depth, B, n_mem, N_MEM_HEADS, HEAD_DIM),
            z_all[:, :, OFF_K:OFF_K + W_FOX].reshape(depth, R, 1, N_FOX_HEADS, HEAD_DIM),
            z_all[:, :, OFF_V:OFF_V + W_FOX].reshape(depth, R, 1, N_FOX_HEADS, HEAD_DIM),
            st("lfs")[:, :, :N_FOX_HEADS].reshape(depth, R, 1, N_FOX_HEADS),
            _cols_to_state(st("hs")))
```
